```python
import math
import jax
import jax.numpy as jnp
from jax import lax
import numpy as np

D_MODEL = 1024
BATCH = 8
SEQ = 2048
DEPTH = 1
DEC_BATCH = 8
DEC_SEQ = 64
PAST_LEN = 1024

CHUNK = 64
SSM_HEADS = 16
SSM_HEAD_DIM = 64
D_SSM = SSM_HEADS * SSM_HEAD_DIM
SSM_GROUPS = 2
D_STATE = 128
CONV_W = 4
D_CONV = D_SSM + 2 * SSM_GROUPS * D_STATE
SSD_CHUNK = CHUNK
ATT_HEADS = 8
ATT_HEAD_DIM = 64
D_ATT = ATT_HEADS * ATT_HEAD_DIM
LEFT_CHUNKS = 8
ATT_LEFT = LEFT_CHUNKS * CHUNK
BAND = ATT_LEFT + CHUNK
REL_CLIP = 128
ATT_SCALE = ATT_HEAD_DIM ** -0.5
D_MIX = D_SSM + D_ATT
D_IN_PROJ = D_SSM + D_CONV + SSM_HEADS + 3 * D_ATT
N_EXPERTS = 32
TOP_K = 4
D_FF = D_MODEL
SWIGLU_ALPHA = 1.702
SWIGLU_LIMIT = 7.0
MOE_BLOCK = 128
EPS = 1e-5

kernel_name = "hybrid_ssd_chunkattn_moe_stream_step"


def rms_norm(x, g):
    xf = x.astype(jnp.float32)
    y = xf * lax.rsqrt(jnp.mean(xf * xf, axis=-1, keepdims=True) + EPS)
    return (y * g.astype(jnp.float32)).astype(x.dtype)


def split_proj(h, w_in):
    u = h @ w_in
    cuts = [int(c) for c in np.cumsum([D_SSM, D_CONV, SSM_HEADS, D_ATT, D_ATT])]
    return jnp.split(u, cuts, axis=-1)


def causal_dwconv(u, prev, w, b):
    s = u.shape[1]
    up = jnp.concatenate([prev.astype(u.dtype), u], axis=1)
    y = b
    for tap in range(CONV_W):
        y = y + w[tap] * up[:, tap:tap + s]
    return y, up[:, -(CONV_W - 1):]


def ssd_scan(x, dt, a, bm, cm, h0):
    bsz, s = x.shape[:2]
    pad = (-s) % SSD_CHUNK
    if pad:
        padw = lambda t: jnp.pad(t, [(0, 0), (0, pad)] + [(0, 0)] * (t.ndim - 2))
        x, dt, bm, cm = padw(x), padw(dt), padw(bm), padw(cm)
    nc = (s + pad) // SSD_CHUNK
    L = SSD_CHUNK
    x = x.reshape(bsz, nc, L, SSM_HEADS, SSM_HEAD_DIM)
    dt = dt.reshape(bsz, nc, L, SSM_HEADS)
    bm = bm.reshape(bsz, nc, L, SSM_HEADS, D_STATE)
    cm = cm.reshape(bsz, nc, L, SSM_HEADS, D_STATE)
    a_cum = jnp.cumsum(dt * a, axis=2)
    xdt = x * dt[..., None]
    seg = a_cum[:, :, :, None, :] - a_cum[:, :, None, :, :]
    causal = jnp.tril(jnp.ones((L, L), dtype=bool))[None, None, :, :, None]
    decay = jnp.exp(jnp.where(causal, seg, -jnp.inf))
    scores = jnp.einsum('bclhn,bcshn->bclsh', cm, bm) * decay
    y_diag = jnp.einsum('bclsh,bcshp->bclhp', scores, xdt)
    decay_to_end = jnp.exp(a_cum[:, :, -1:, :] - a_cum)
    chunk_states = jnp.einsum('bclhn,bclhp->bchpn', bm * decay_to_end[..., None], xdt)
    chunk_decay = jnp.exp(a_cum[:, :, -1, :])

    def step(h_prev, inp):
        st, dc = inp
        return h_prev * dc[:, :, None, None] + st, h_prev

    h_final, h_enter = lax.scan(step, h0, (jnp.swapaxes(chunk_states, 0, 1), jnp.swapaxes(chunk_decay, 0, 1)))
    h_enter = jnp.swapaxes(h_enter, 0, 1)
    y_off = jnp.einsum('bclhn,bchpn->bclhp', cm, h_enter) * jnp.exp(a_cum)[..., None]
    y = (y_diag + y_off).reshape(bsz, nc * L, SSM_HEADS, SSM_HEAD_DIM)[:, :s]
    return y, h_final


def ssd_mixer(z, xbc, dt_raw, conv_prev, h0, conv_w, conv_b, dt_bias, a_log, d_skip, norm_g):
    bsz, s = z.shape[:2]
    xbc, conv_state = causal_dwconv(xbc, conv_prev, conv_w, conv_b)
    xbc = jax.nn.silu(xbc).astype(jnp.float32)
    xs, bm, cm = jnp.split(xbc, [D_SSM, D_SSM + SSM_GROUPS * D_STATE], axis=-1)
    xs = xs.reshape(bsz, s, SSM_HEADS, SSM_HEAD_DIM)
    rep = SSM_HEADS // SSM_GROUPS
    bm = jnp.repeat(bm.reshape(bsz, s, SSM_GROUPS, D_STATE), rep, axis=2)
    cm = jnp.repeat(cm.reshape(bsz, s, SSM_GROUPS, D_STATE), rep, axis=2)
    dt = jax.nn.softplus(dt_raw.astype(jnp.float32) + dt_bias.astype(jnp.float32))
    a = -jnp.exp(a_log.astype(jnp.float32))
    y, h_final = ssd_scan(xs, dt, a, bm, cm, h0.astype(jnp.float32))
    y = y + d_skip.astype(jnp.float32)[:, None] * xs
    y = y.reshape(bsz, s, D_SSM).astype(z.dtype) * jax.nn.silu(z)
    y = rms_norm(y.reshape(bsz, s, SSM_GROUPS, D_SSM // SSM_GROUPS), norm_g.reshape(SSM_GROUPS, -1))
    return y.reshape(bsz, s, D_SSM), conv_state, h_final


def rel_bias(table, rel):
    return table[:, jnp.clip(rel, -REL_CLIP, REL_CLIP) + REL_CLIP]


def band_attention_prompt(q, k, v, table):
    bsz, s = q.shape[:2]
    nc = s // CHUNK
    qc = q.reshape(bsz, nc, CHUNK, ATT_HEADS, ATT_HEAD_DIM)
    padc = ((0, 0), (LEFT_CHUNKS, 0), (0, 0), (0, 0), (0, 0))
    kc = jnp.pad(k.reshape(bsz, nc, CHUNK, ATT_HEADS, ATT_HEAD_DIM), padc)
    vc = jnp.pad(v.reshape(bsz, nc, CHUNK, ATT_HEADS, ATT_HEAD_DIM), padc)
    kb = jnp.stack([kc[:, j:j + nc] for j in range(LEFT_CHUNKS + 1)], axis=2).reshape(bsz, nc, BAND, ATT_HEADS, ATT_HEAD_DIM)
    vb = jnp.stack([vc[:, j:j + nc] for j in range(LEFT_CHUNKS + 1)], axis=2).reshape(bsz, nc, BAND, ATT_HEADS, ATT_HEAD_DIM)
    qi = jnp.arange(CHUNK)
    kj = jnp.arange(BAND)
    bias = rel_bias(table, ATT_LEFT + qi[:, None] - kj[None, :])
    key_pos = jnp.arange(nc)[:, None] * CHUNK - ATT_LEFT + kj[None, :]
    valid = (key_pos >= 0)[None, :, None, None, :]
    scores = jnp.einsum('bcqhd,bckhd->bchqk', qc, kb).astype(jnp.float32) * ATT_SCALE
    scores = jnp.where(valid, scores + bias[None, None].astype(jnp.float32), -jnp.inf)
    p = jax.nn.softmax(scores, axis=-1)
    o = jnp.einsum('bchqk,bckhd->bcqhd', p.astype(v.dtype), vb)
    return o.reshape(bsz, s, ATT_HEADS, ATT_HEAD_DIM)


def band_attention_step(q, k_new, v_new, k_cache, v_cache, table):
    t = q.shape[1]
    w = k_cache.shape[1]
    keys = jnp.concatenate([k_cache.astype(k_new.dtype), k_new], axis=1)
    vals = jnp.concatenate([v_cache.astype(v_new.dtype), v_new], axis=1)
    rel = (w + jnp.arange(t))[:, None] - jnp.arange(w + t)[None, :]
    scores = jnp.einsum('bqhd,bkhd->bhqk', q, keys).astype(jnp.float32) * ATT_SCALE
    scores = scores + rel_bias(table, rel)[None].astype(jnp.float32)
    p = jax.nn.softmax(scores, axis=-1)
    return jnp.einsum('bhqk,bkhd->bqhd', p.astype(vals.dtype), vals)


def merge_heads(y_ssm, o_att, att_norm_g, w_out):
    o = rms_norm(o_att.reshape(o_att.shape[0], o_att.shape[1], D_ATT), att_norm_g)
    return jnp.concatenate([y_ssm, o], axis=-1) @ w_out


def moe_ffn(h, w_router, b_router, w_gate_up, b_gate_up, w_down, b_down):
    xt = h.reshape(-1, D_MODEL)
    n_tok = xt.shape[0]
    logits = (xt @ w_router + b_router).astype(jnp.float32)
    top_val, top_idx = lax.top_k(logits, TOP_K)
    gates = jax.nn.softmax(top_val, axis=-1)
    n_assign = n_tok * TOP_K
    flat_e = top_idx.reshape(-1)
    order = jnp.argsort(flat_e)
    sorted_e = flat_e[order]
    sorted_tok = order // TOP_K
    sorted_gate = gates.reshape(-1)[order]
    counts = jnp.bincount(flat_e, length=N_EXPERTS)
    padded = (counts + MOE_BLOCK - 1) // MOE_BLOCK * MOE_BLOCK
    pad_end = jnp.cumsum(padded)
    pad_start = pad_end - padded
    start = jnp.cumsum(counts) - counts
    dest = pad_start[sorted_e] + jnp.arange(n_assign) - start[sorted_e]
    n_blocks = -(-n_assign // MOE_BLOCK) + N_EXPERTS
    rows = n_blocks * MOE_BLOCK
    row_tok = jnp.full((rows,), n_tok, jnp.int32).at[dest].set(sorted_tok.astype(jnp.int32))
    x_rows = jnp.concatenate([xt, jnp.zeros((1, D_MODEL), xt.dtype)], axis=0)[row_tok]
    x_rows = x_rows.reshape(n_blocks, MOE_BLOCK, D_MODEL)
    block_expert = jnp.minimum(jnp.searchsorted(pad_end, jnp.arange(n_blocks) * MOE_BLOCK, side='right'), N_EXPERTS - 1)

    def expert_block(args):
        xb, e = args
        gu = xb @ w_gate_up[e] + b_gate_up[e]
        gate = jnp.minimum(gu[:, :D_FF], SWIGLU_LIMIT)
        up = jnp.clip(gu[:, D_FF:], -SWIGLU_LIMIT, SWIGLU_LIMIT)
        act = (up + 1) * gate * jax.nn.sigmoid(gate * SWIGLU_ALPHA)
        return act @ w_down[e] + b_down[e]

    y_rows = lax.map(expert_block, (x_rows, block_expert)).reshape(rows, D_MODEL)
    contrib = y_rows[dest] * sorted_gate[:, None].astype(y_rows.dtype)
    out = jnp.zeros_like(xt).at[sorted_tok].add(contrib)
    return out.reshape(h.shape)


def setup_inputs(seed: int = 0) -> dict:
    key = jax.random.key(seed)
    ks = jax.random.split(key, 25)
    f32 = jnp.float32
    nrm = lambda k, shape, s: jax.random.normal(k, shape, f32) * s
    att_rows = min(ATT_LEFT, PAST_LEN)
    dt0 = jnp.exp(jax.random.uniform(ks[10], (DEPTH, SSM_HEADS), f32, math.log(1e-3), math.log(1e-1)))
    return {
        'x_prompt': nrm(ks[0], (BATCH, SEQ, D_MODEL), 1.0),
        'x_sample': nrm(ks[1], (DEC_BATCH, DEC_SEQ, D_MODEL), 1.0),
        'cache_k': nrm(ks[2], (DEPTH, DEC_BATCH, att_rows, ATT_HEADS, ATT_HEAD_DIM), 1.0),
        'cache_v': nrm(ks[3], (DEPTH, DEC_BATCH, att_rows, ATT_HEADS, ATT_HEAD_DIM), 1.0),
        'state_conv': nrm(ks[4], (DEPTH, DEC_BATCH, CONV_W - 1, D_CONV), 1.0),
        'state_ssm': nrm(ks[5], (DEPTH, DEC_BATCH, SSM_HEADS, SSM_HEAD_DIM, D_STATE), 0.1),
        'norm_mix_g': 1.0 + nrm(ks[6], (DEPTH, D_MODEL), 0.02),
        'w_in': nrm(ks[7], (DEPTH, D_MODEL, D_IN_PROJ), D_MODEL ** -0.5),
        'conv_w': nrm(ks[8], (DEPTH, CONV_W, D_CONV), CONV_W ** -0.5),
        'conv_b': nrm(ks[9], (DEPTH, D_CONV), 0.02),
        'dt_bias': dt0 + jnp.log(-jnp.expm1(-dt0)),
        'a_log': jnp.log(jax.random.uniform(ks[11], (DEPTH, SSM_HEADS), f32, 1.0, 16.0)),
        'd_skip': 1.0 + nrm(ks[12], (DEPTH, SSM_HEADS), 0.1),
        'ssm_norm_g': 1.0 + nrm(ks[13], (DEPTH, D_SSM), 0.02),
        'att_norm_g': 1.0 + nrm(ks[14], (DEPTH, D_ATT), 0.02),
        'rel_bias_table': nrm(ks[15], (DEPTH, ATT_HEADS, 2 * REL_CLIP + 1), 0.1),
        'w_out': nrm(ks[16], (DEPTH, D_MIX, D_MODEL), D_MIX ** -0.5),
        'norm_ffn_g': 1.0 + nrm(ks[17], (DEPTH, D_MODEL), 0.02),
        'w_router': nrm(ks[18], (DEPTH, D_MODEL, N_EXPERTS), D_MODEL ** -0.5),
        'b_router': nrm(ks[19], (DEPTH, N_EXPERTS), 0.01),
        'w_gate_up': nrm(ks[20], (DEPTH, N_EXPERTS, D_MODEL, 2 * D_FF), D_MODEL ** -0.5),
        'b_gate_up': nrm(ks[21], (DEPTH, N_EXPERTS, 2 * D_FF), 0.02),
        'w_down': nrm(ks[22], (DEPTH, N_EXPERTS, D_FF, D_MODEL), D_FF ** -0.5),
        'b_down': nrm(ks[23], (DEPTH, N_EXPERTS, D_MODEL), 0.02),
        'norm_final_g': 1.0 + nrm(ks[24], (D_MODEL,), 0.02),
    }


def reference(x_prompt, x_sample, cache_k, cache_v, state_conv, state_ssm,
              norm_mix_g, w_in, conv_w, conv_b, dt_bias, a_log, d_skip, ssm_norm_g,
              att_norm_g, rel_bias_table, w_out, norm_ffn_g, w_router, b_router,
              w_gate_up, b_gate_up, w_down, b_down, norm_final_g):
    xp, xs = x_prompt, x_sample
    bp, sp = x_prompt.shape[:2]
    bd, sd = x_sample.shape[:2]
    keep = min(ATT_LEFT, sp)
    k_p, v_p, conv_p, ssm_p = [], [], [], []
    k_s, v_s, conv_s, ssm_s = [], [], [], []
    for l in range(DEPTH):
        hp = rms_norm(xp, norm_mix_g[l])
        z, xbc, dtr, q, k, v = split_proj(hp, w_in[l])
        conv0 = jnp.zeros((bp, CONV_W - 1, D_CONV), xp.dtype)
        h0 = jnp.zeros((bp, SSM_HEADS, SSM_HEAD_DIM, D_STATE), jnp.float32)
        y_ssm, cst, hst = ssd_mixer(z, xbc, dtr, conv0, h0, conv_w[l], conv_b[l], dt_bias[l],
                                    a_log[l], d_skip[l], ssm_norm_g[l])
        q, k, v = [t.reshape(bp, sp, ATT_HEADS, ATT_HEAD_DIM) for t in (q, k, v)]
        o = band_attention_prompt(q, k, v, rel_bias_table[l])
        xp = xp + merge_heads(y_ssm, o, att_norm_g[l], w_out[l])
        xp = xp + moe_ffn(rms_norm(xp, norm_ffn_g[l]), w_router[l], b_router[l],
                          w_gate_up[l], b_gate_up[l], w_down[l], b_down[l])
        k_p.append(k[:, sp - keep:])
        v_p.append(v[:, sp - keep:])
        conv_p.append(cst)
        ssm_p.append(hst.astype(state_ssm.dtype))
        hs = rms_norm(xs, norm_mix_g[l])
        z, xbc, dtr, q, k, v = split_proj(hs, w_in[l])
        y_ssm, cst, hst = ssd_mixer(z, xbc, dtr, state_conv[l], state_ssm[l], conv_w[l], conv_b[l],
                                    dt_bias[l], a_log[l], d_skip[l], ssm_norm_g[l])
        q, k, v = [t.reshape(bd, sd, ATT_HEADS, ATT_HEAD_DIM) for t in (q, k, v)]
        o = band_attention_step(q, k, v, cache_k[l], cache_v[l], rel_bias_table[l])
        xs = xs + merge_heads(y_ssm, o, att_norm_g[l], w_out[l])
        xs = xs + moe_ffn(rms_norm(xs, norm_ffn_g[l]), w_router[l], b_router[l],
                          w_gate_up[l], b_gate_up[l], w_down[l], b_down[l])
        k_s.append(k)
        v_s.append(v)
        conv_s.append(cst)
        ssm_s.append(hst.astype(state_ssm.dtype))
    y_prompt = rms_norm(xp, norm_final_g)
    y_sample = rms_norm(xs, norm_final_g)
    return (y_prompt, y_sample,
            jnp.stack(k_p), jnp.stack(v_p), jnp.stack(conv_p), jnp.stack(ssm_p),
            jnp.stack(k_s), jnp.stack(v_s), jnp.stack(conv_s), jnp.stack(ssm_s))
```

```python
import functools

import jax
import jax.numpy as jnp
from jax import lax
from jax.experimental import pallas as pl
from jax.experimental.pallas import tpu as pltpu

D_MODEL = 1024
BATCH = 8
SEQ = 2048
DEC_BATCH = 8
DEC_SEQ = 64
CHUNK = 64
SSM_HEADS = 16
SSM_HEAD_DIM = 64
D_SSM = SSM_HEADS * SSM_HEAD_DIM
SSM_GROUPS = 2
D_STATE = 128
CONV_W = 4
D_BC = SSM_GROUPS * D_STATE
D_CONV = D_SSM + 2 * D_BC
ATT_HEADS = 8
ATT_HEAD_DIM = 64
D_ATT = ATT_HEADS * ATT_HEAD_DIM
LEFT_CHUNKS = 8
ATT_LEFT = LEFT_CHUNKS * CHUNK
BAND = ATT_LEFT + CHUNK
REL_CLIP = 128
ATT_SCALE = ATT_HEAD_DIM ** -0.5
N_EXPERTS = 32
TOP_K = 4
D_FF = D_MODEL
SWIGLU_ALPHA = 1.702
SWIGLU_LIMIT = 7.0
EPS = 1e-5

F32 = jnp.float32
BF16 = jnp.bfloat16

N_PROMPT = BATCH * SEQ
N_SAMPLE = DEC_BATCH * DEC_SEQ
N_TOK = N_PROMPT + N_SAMPLE
TM = 512
N_PROMPT_TILES = N_PROMPT // TM
N_TILES = N_TOK // TM
TILES_PER_SEQ = SEQ // TM
CHUNKS_PER_TILE = TM // CHUNK
N_CHUNKS_SEQ = SEQ // CHUNK
HEAD_PAIRS = SSM_HEADS // 2
PAIRS_PER_GROUP = HEAD_PAIRS // SSM_GROUPS
ATT_PAIRS = ATT_HEADS // 2
CONV_HEAD = 8
MOE_BM = 256
N_ASSIGN = N_TOK * TOP_K
MOE_BLOCKS = N_ASSIGN // MOE_BM + N_EXPERTS
MOE_ROWS = MOE_BLOCKS * MOE_BM
VMEM_LIMIT = 56 * 1024 * 1024


def _dot(a, b):
    return jnp.dot(a, b, preferred_element_type=F32)


def _dot_nt(a, b):
    return lax.dot_general(a, b, (((1,), (1,)), ((), ())), preferred_element_type=F32)


def _dot_tn(a, b):
    return lax.dot_general(a, b, (((0,), (0,)), ((), ())), preferred_element_type=F32)


def _split3(x):
    x1 = x.astype(BF16)
    r1 = x - x1.astype(F32)
    x2 = r1.astype(BF16)
    r2 = r1 - x2.astype(F32)
    return x1, x2, r2.astype(BF16)


def _dot_exact_rhs(x, m):
    x1, x2, x3 = _split3(x)
    return _dot(x1, m) + _dot(x2, m) + _dot(x3, m)


def _dot_exact_lhs(m, x):
    x1, x2, x3 = _split3(x)
    return _dot(m, x1) + _dot(m, x2) + _dot(m, x3)


def _rms(x, g):
    return x * lax.rsqrt(jnp.mean(x * x, axis=-1, keepdims=True) + EPS) * g


def _sigmoid(x):
    return 1.0 / (1.0 + jnp.exp(-x))


def _softplus(x):
    return jnp.maximum(x, 0.0) + jnp.log(1.0 + jnp.exp(-jnp.abs(x)))


def _const_spec(shape):
    nd = len(shape)
    return pl.BlockSpec(shape, lambda *_: (0,) * nd)


def _params(n_axes=1):
    return pltpu.CompilerParams(dimension_semantics=("arbitrary",) * n_axes,
                                vmem_limit_bytes=VMEM_LIMIT)


def _inproj_kernel(xp_ref, xs_ref, g_ref, wz_ref, wxbc_ref, wdt_ref, wdtt_ref, wq_ref, wk_ref, wv_ref,
                   z_ref, xbc_ref, dt_ref, dtt_ref, q_ref, k_ref, v_ref, kf_ref, vf_ref, ctail_ref):
    i = pl.program_id(0)
    x = jnp.where(i == N_PROMPT_TILES, xs_ref[...], xp_ref[...])
    h = _rms(x, g_ref[...]).astype(BF16)
    z_ref[...] = _dot(h, wz_ref[...]).astype(BF16)
    xbc = _dot(h, wxbc_ref[...])
    xbc_ref[...] = xbc.astype(BF16)
    for c in range(CHUNKS_PER_TILE):
        ctail_ref[0, c * 8:(c + 1) * 8, :] = xbc[c * CHUNK + CHUNK - 8:(c + 1) * CHUNK, :]
    dt_ref[...] = _dot(h, wdt_ref[...])
    dtt_ref[...] = _dot_nt(wdtt_ref[...], h)
    q_ref[...] = (_dot(h, wq_ref[...]) * ATT_SCALE).astype(BF16)
    k = _dot(h, wk_ref[...])
    v = _dot(h, wv_ref[...])
    k_ref[...] = k.astype(BF16)
    v_ref[...] = v.astype(BF16)
    kf_ref[0] = k
    vf_ref[0] = v


def _inproj(xp, xs, g, wz, wxbc, wdt, wdtt, wq, wk, wv):
    tok = lambda n: pl.BlockSpec((TM, n), lambda i: (i, 0))
    tail_idx = lambda i: (i // TILES_PER_SEQ, 0, 0)
    n_tail = BATCH + 1
    return pl.pallas_call(
        _inproj_kernel,
        grid=(N_TILES,),
        in_specs=[
            pl.BlockSpec((TM, D_MODEL), lambda i: (jnp.minimum(i, N_PROMPT_TILES - 1), 0)),
            pl.BlockSpec((TM, D_MODEL), lambda i: (0, 0)),
            _const_spec((1, D_MODEL)),
            _const_spec((D_MODEL, D_SSM)),
            _const_spec((D_MODEL, D_CONV)),
            _const_spec((D_MODEL, SSM_HEADS)),
            _const_spec((SSM_HEADS, D_MODEL)),
            _const_spec((D_MODEL, D_ATT)),
            _const_spec((D_MODEL, D_ATT)),
            _const_spec((D_MODEL, D_ATT)),
        ],
        out_specs=[
            tok(D_SSM), tok(D_CONV), tok(SSM_HEADS),
            pl.BlockSpec((SSM_HEADS, TM), lambda i: (0, i)),
            tok(D_ATT), tok(D_ATT), tok(D_ATT),
            pl.BlockSpec((1, TM, D_ATT), tail_idx),
            pl.BlockSpec((1, TM, D_ATT), tail_idx),
            pl.BlockSpec((1, CHUNKS_PER_TILE * 8, D_CONV), tail_idx),
        ],
        out_shape=[
            jax.ShapeDtypeStruct((N_TOK, D_SSM), BF16),
            jax.ShapeDtypeStruct((N_TOK, D_CONV), BF16),
            jax.ShapeDtypeStruct((N_TOK, SSM_HEADS), F32),
            jax.ShapeDtypeStruct((SSM_HEADS, N_TOK), F32),
            jax.ShapeDtypeStruct((N_TOK, D_ATT), BF16),
            jax.ShapeDtypeStruct((N_TOK, D_ATT), BF16),
            jax.ShapeDtypeStruct((N_TOK, D_ATT), BF16),
            jax.ShapeDtypeStruct((n_tail, TM, D_ATT), F32),
            jax.ShapeDtypeStruct((n_tail, TM, D_ATT), F32),
            jax.ShapeDtypeStruct((n_tail, CHUNKS_PER_TILE * 8, D_CONV), F32),
        ],
        compiler_params=_params(),
        name="inproj",
    )(xp, xs, g, wz, wxbc, wdt, wdtt, wq, wk, wv)


def _ssd_tile(n_chunks, z_ref, xbc_ref, dt_ref, dtp_ref, cw_ref, cb_ref, dtb_ref, dtbp_ref,
              alog_e_ref, alog_p_ref, dskip_e_ref, ng_ref, expand_ref, tril_ref, triu2_ref,
              y_ref, xw_ref, xa_ref, dte_ref, state_ref):
    rows = n_chunks * CHUNK
    xw_ref[CONV_HEAD:CONV_HEAD + rows, :] = xbc_ref[...].astype(F32)
    acc = cb_ref[...]
    for tap in range(CONV_W):
        off = CONV_HEAD - (CONV_W - 1) + tap
        acc = acc + cw_ref[tap:tap + 1, :] * xw_ref[off:off + rows, :]
    xa_ref[...] = acc * _sigmoid(acc)
    dt = _softplus(dt_ref[...] + dtb_ref[...])
    dte_ref[...] = _dot_exact_rhs(dt, expand_ref[...])

    a_e = -jnp.exp(alog_e_ref[...])
    a_p = -jnp.exp(alog_p_ref[...])
    row_i = lax.broadcasted_iota(jnp.int32, (CHUNK, 128), 0)
    col_i = lax.broadcasted_iota(jnp.int32, (CHUNK, 128), 1)
    causal2 = row_i >= (col_i % CHUNK)
    bd_r = lax.broadcasted_iota(jnp.int32, (128, 128), 0) // CHUNK
    bd_c = lax.broadcasted_iota(jnp.int32, (128, 128), 1) // CHUNK
    blockdiag = bd_r == bd_c

    def chunk(c, carry):
        r0 = pl.multiple_of(c * CHUNK, CHUNK)
        xs = xa_ref[pl.ds(r0, CHUNK), 0:D_SSM]
        bm = xa_ref[pl.ds(r0, CHUNK), D_SSM:D_SSM + D_BC].astype(BF16)
        cm = xa_ref[pl.ds(r0, CHUNK), D_SSM + D_BC:D_CONV].astype(BF16)
        dt_e = dte_ref[pl.ds(r0, CHUNK), :]
        acum = _dot_exact_lhs(tril_ref[...], dt_e * a_e)
        dtp = _softplus(dtp_ref[c] + dtbp_ref[...])
        acum_p = _dot_exact_rhs(dtp * a_p, triu2_ref[...])
        xdt = xs * dt_e
        a_last = acum[CHUNK - 1:CHUNK, :]
        xdt_end = (xdt * jnp.exp(a_last - acum)).astype(BF16)
        y_parts = []
        for g in range(SSM_GROUPS):
            bg = bm[:, g * D_STATE:(g + 1) * D_STATE]
            cg = cm[:, g * D_STATE:(g + 1) * D_STATE]
            cb2 = _dot_nt(cg, jnp.concatenate([bg, bg], axis=0))
            for jj in range(PAIRS_PER_GROUP):
                j = g * PAIRS_PER_GROUP + jj
                seg = acum[:, j * 128:(j + 1) * 128] - acum_p[j:j + 1, :]
                decay = jnp.exp(jnp.where(causal2, seg, -jnp.inf))
                s_pair = (cb2 * decay).astype(BF16)
                x2 = xdt[:, j * 128:(j + 1) * 128]
                rhs = jnp.where(blockdiag, jnp.concatenate([x2, x2], axis=0), 0.0).astype(BF16)
                y_parts.append(_dot(s_pair, rhs))
        y_diag = jnp.concatenate(y_parts, axis=1)
        half = D_SSM // SSM_GROUPS
        y_off = jnp.concatenate(
            [_dot(cm[:, g * D_STATE:(g + 1) * D_STATE], state_ref[:, g * half:(g + 1) * half].astype(BF16))
             for g in range(SSM_GROUPS)], axis=1)
        new_s = jnp.concatenate(
            [_dot_tn(bm[:, g * D_STATE:(g + 1) * D_STATE], xdt_end[:, g * half:(g + 1) * half])
             for g in range(SSM_GROUPS)], axis=1)
        state_ref[...] = state_ref[...] * jnp.exp(a_last) + new_s
        y = y_diag + y_off * jnp.exp(acum) + dskip_e_ref[...] * xs
        zc = z_ref[pl.ds(r0, CHUNK), :].astype(F32)
        y = y * (zc * _sigmoid(zc))
        yn = jnp.concatenate(
            [y[:, g * half:(g + 1) * half]
             * lax.rsqrt(jnp.mean(jnp.square(y[:, g * half:(g + 1) * half]), axis=-1, keepdims=True) + EPS)
             for g in range(SSM_GROUPS)], axis=1)
        y_ref[pl.ds(r0, CHUNK), :] = (yn * ng_ref[...]).astype(BF16)
        return carry

    lax.fori_loop(0, n_chunks, chunk, 0)


def _state_store(state_ref, out_ref):
    for j in range(HEAD_PAIRS):
        out_ref[0, j * 128:(j + 1) * 128, :] = state_ref[:, j * 128:(j + 1) * 128].T


def _ssd_prompt_kernel(z_ref, xbc_ref, dt_ref, dtp_ref, cw_ref, cb_ref, dtb_ref, dtbp_ref,
                       alog_e_ref, alog_p_ref, dskip_e_ref, ng_ref, expand_ref, tril_ref, triu2_ref,
                       y_ref, ssm_ref, xw_ref, xa_ref, dte_ref, state_ref, tail_ref):
    t = pl.program_id(1)

    @pl.when(t == 0)
    def _():
        state_ref[...] = jnp.zeros_like(state_ref)
        xw_ref[0:CONV_HEAD, :] = jnp.zeros((CONV_HEAD, D_CONV), F32)

    @pl.when(t > 0)
    def _():
        xw_ref[0:CONV_HEAD, :] = tail_ref[...]

    _ssd_tile(CHUNKS_PER_TILE, z_ref, xbc_ref, dt_ref, dtp_ref, cw_ref, cb_ref, dtb_ref, dtbp_ref,
              alog_e_ref, alog_p_ref, dskip_e_ref, ng_ref, expand_ref, tril_ref, triu2_ref,
              y_ref, xw_ref, xa_ref, dte_ref, state_ref)
    tail_ref[...] = xw_ref[TM:TM + CONV_HEAD, :]

    @pl.when(t == TILES_PER_SEQ - 1)
    def _():
        _state_store(state_ref, ssm_ref)


def _ssd_sample_kernel(z_ref, xbc_ref, dt_ref, dtp_ref, cprev_ref, sprev_ref,
                       cw_ref, cb_ref, dtb_ref, dtbp_ref,
                       alog_e_ref, alog_p_ref, dskip_e_ref, ng_ref, expand_ref, tril_ref, triu2_ref,
                       y_ref, ssm_ref, xw_ref, xa_ref, dte_ref, state_ref):
    xw_ref[0:CONV_HEAD, :] = jnp.zeros((CONV_HEAD, D_CONV), F32)
    xw_ref[CONV_HEAD - (CONV_W - 1):CONV_HEAD, :] = cprev_ref[0]
    for j in range(HEAD_PAIRS):
        state_ref[:, j * 128:(j + 1) * 128] = sprev_ref[0, j * 128:(j + 1) * 128, :].T
    _ssd_tile(1, z_ref, xbc_ref, dt_ref, dtp_ref, cw_ref, cb_ref, dtb_ref, dtbp_ref,
              alog_e_ref, alog_p_ref, dskip_e_ref, ng_ref, expand_ref, tril_ref, triu2_ref,
              y_ref, xw_ref, xa_ref, dte_ref, state_ref)
    _state_store(state_ref, ssm_ref)


def _ssd_const_specs():
    return [
        _const_spec((CONV_W, D_CONV)), _const_spec((1, D_CONV)),
        _const_spec((1, SSM_HEADS)), _const_spec((HEAD_PAIRS, 128)),
        _const_spec((1, D_SSM)), _const_spec((HEAD_PAIRS, 128)),
        _const_spec((1, D_SSM)), _const_spec((1, D_SSM)),
        _const_spec((SSM_HEADS, D_SSM)), _const_spec((CHUNK, CHUNK)), _const_spec((128, 128)),
    ]


def _ssd_prompt(z, xbc, dt, dtp, consts):
    tile = lambda b, t: (b * TILES_PER_SEQ + t, 0)
    return pl.pallas_call(
        _ssd_prompt_kernel,
        grid=(BATCH, TILES_PER_SEQ),
        in_specs=[
            pl.BlockSpec((TM, D_SSM), tile),
            pl.BlockSpec((TM, D_CONV), tile),
            pl.BlockSpec((TM, SSM_HEADS), tile),
            pl.BlockSpec((CHUNKS_PER_TILE, HEAD_PAIRS, 128), lambda b, t: (b * TILES_PER_SEQ + t, 0, 0)),
        ] + _ssd_const_specs(),
        out_specs=[
            pl.BlockSpec((TM, D_SSM), tile),
            pl.BlockSpec((1, D_SSM, D_STATE), lambda b, t: (b, 0, 0)),
        ],
        out_shape=[
            jax.ShapeDtypeStruct((N_PROMPT, D_SSM), BF16),
            jax.ShapeDtypeStruct((BATCH, D_SSM, D_STATE), F32),
        ],
        scratch_shapes=[
            pltpu.VMEM((CONV_HEAD + TM, D_CONV), F32),
            pltpu.VMEM((TM, D_CONV), F32),
            pltpu.VMEM((TM, D_SSM), F32),
            pltpu.VMEM((D_STATE, D_SSM), F32),
            pltpu.VMEM((CONV_HEAD, D_CONV), F32),
        ],
        compiler_params=_params(2),
        name="ssd_prompt",
    )(z, xbc, dt, dtp, *consts)


def _ssd_sample(z, xbc, dt, dtp, conv_prev, ssm_prev, consts):
    first = N_PROMPT // CHUNK
    row = lambda b: (first + b, 0)
    return pl.pallas_call(
        _ssd_sample_kernel,
        grid=(DEC_BATCH,),
        in_specs=[
            pl.BlockSpec((CHUNK, D_SSM), row),
            pl.BlockSpec((CHUNK, D_CONV), row),
            pl.BlockSpec((CHUNK, SSM_HEADS), row),
            pl.BlockSpec((1, HEAD_PAIRS, 128), lambda b: (first + b, 0, 0)),
            pl.BlockSpec((1, CONV_W - 1, D_CONV), lambda b: (b, 0, 0)),
            pl.BlockSpec((1, D_SSM, D_STATE), lambda b: (b, 0, 0)),
        ] + _ssd_const_specs(),
        out_specs=[
            pl.BlockSpec((CHUNK, D_SSM), lambda b: (b, 0)),
            pl.BlockSpec((1, D_SSM, D_STATE), lambda b: (b, 0, 0)),
        ],
        out_shape=[
            jax.ShapeDtypeStruct((N_SAMPLE, D_SSM), BF16),
            jax.ShapeDtypeStruct((DEC_BATCH, D_SSM, D_STATE), F32),
        ],
        scratch_shapes=[
            pltpu.VMEM((CONV_HEAD + CHUNK, D_CONV), F32),
            pltpu.VMEM((CHUNK, D_CONV), F32),
            pltpu.VMEM((CHUNK, D_SSM), F32),
            pltpu.VMEM((D_STATE, D_SSM), F32),
        ],
        compiler_params=_params(),
        name="ssd_sample",
    )(z, xbc, dt, dtp, conv_prev, ssm_prev, *consts)


def _attn_chunks(n_chunks, first_chunk, q_ref, kpad_ref, vpad_ref, bias_ref, o_ref):
    lane = lax.broadcasted_iota(jnp.int32, (CHUNK, 128), 1)
    low = lane < ATT_HEAD_DIM
    kj = lax.broadcasted_iota(jnp.int32, (2 * CHUNK, BAND), 1)

    def chunk(c, carry):
        r0 = pl.multiple_of(c * CHUNK, CHUNK)
        first_valid = jnp.maximum(LEFT_CHUNKS - (first_chunk + c), 0) * CHUNK
        valid = kj >= first_valid
        outs = []
        for j in range(ATT_PAIRS):
            qp = q_ref[pl.ds(r0, CHUNK), j * 128:(j + 1) * 128]
            zero = jnp.zeros_like(qp)
            q2 = jnp.concatenate([jnp.where(low, qp, zero), jnp.where(low, zero, qp)], axis=0)
            kb = kpad_ref[pl.ds(r0, BAND), j * 128:(j + 1) * 128]
            vb = vpad_ref[pl.ds(r0, BAND), j * 128:(j + 1) * 128]
            s = _dot_nt(q2, kb) + bias_ref[j]
            s = jnp.where(valid, s, -jnp.inf)
            m = jnp.max(s, axis=-1, keepdims=True)
            e = jnp.exp(s - m)
            denom = jnp.sum(e, axis=-1, keepdims=True)
            r = _dot(e.astype(BF16), vb) / denom
            outs.append(jnp.where(low, r[0:CHUNK], r[CHUNK:2 * CHUNK]))
        o_ref[pl.ds(r0, CHUNK), :] = jnp.concatenate(outs, axis=1).astype(BF16)
        return carry

    lax.fori_loop(0, n_chunks, chunk, 0)


def _attn_prompt_kernel(q_ref, k_ref, v_ref, bias_ref, o_ref, kpad_ref, vpad_ref):
    kpad_ref[0:ATT_LEFT, :] = jnp.zeros((ATT_LEFT, D_ATT), BF16)
    vpad_ref[0:ATT_LEFT, :] = jnp.zeros((ATT_LEFT, D_ATT), BF16)
    kpad_ref[ATT_LEFT:ATT_LEFT + SEQ, :] = k_ref[...]
    vpad_ref[ATT_LEFT:ATT_LEFT + SEQ, :] = v_ref[...]
    _attn_chunks(N_CHUNKS_SEQ, 0, q_ref, kpad_ref, vpad_ref, bias_ref, o_ref)


def _attn_sample_kernel(q_ref, k_ref, v_ref, ck_ref, cv_ref, bias_ref, o_ref, kpad_ref, vpad_ref):
    kpad_ref[0:ATT_LEFT, :] = ck_ref[0].astype(BF16)
    vpad_ref[0:ATT_LEFT, :] = cv_ref[0].astype(BF16)
    kpad_ref[ATT_LEFT:BAND, :] = k_ref[...]
    vpad_ref[ATT_LEFT:BAND, :] = v_ref[...]
    _attn_chunks(1, LEFT_CHUNKS, q_ref, kpad_ref, vpad_ref, bias_ref, o_ref)


def _attn_prompt(q, k, v, bias2):
    seq = pl.BlockSpec((SEQ, D_ATT), lambda b: (b, 0))
    return pl.pallas_call(
        _attn_prompt_kernel,
        grid=(BATCH,),
        in_specs=[seq, seq, seq, _const_spec((ATT_PAIRS, 2 * CHUNK, BAND))],
        out_specs=seq,
        out_shape=jax.ShapeDtypeStruct((N_PROMPT, D_ATT), BF16),
        scratch_shapes=[pltpu.VMEM((ATT_LEFT + SEQ, D_ATT), BF16),
                        pltpu.VMEM((ATT_LEFT + SEQ, D_ATT), BF16)],
        compiler_params=_params(),
        name="attn_prompt",
    )(q, k, v, bias2)


def _attn_sample(q, k, v, cache_k, cache_v, bias2):
    first = N_PROMPT // CHUNK
    row = pl.BlockSpec((CHUNK, D_ATT), lambda b: (first + b, 0))
    cache = pl.BlockSpec((1, ATT_LEFT, D_ATT), lambda b: (b, 0, 0))
    return pl.pallas_call(
        _attn_sample_kernel,
        grid=(DEC_BATCH,),
        in_specs=[row, row, row, cache, cache, _const_spec((ATT_PAIRS, 2 * CHUNK, BAND))],
        out_specs=pl.BlockSpec((CHUNK, D_ATT), lambda b: (b, 0)),
        out_shape=jax.ShapeDtypeStruct((N_SAMPLE, D_ATT), BF16),
        scratch_shapes=[pltpu.VMEM((BAND, D_ATT), BF16), pltpu.VMEM((BAND, D_ATT), BF16)],
        compiler_params=_params(),
        name="attn_sample",
    )(q, k, v, cache_k, cache_v, bias2)


def _outproj_kernel(xp_ref, xs_ref, yp_ref, ys_ref, op_ref, os_ref, ag_ref, wos_ref, woa_ref, fg_ref,
                    wr_ref, br_ref, ltri_ref,
                    xmid_ref, h_ref, idx_ref, gate_ref, rank_ref, cnt_ref, carry_ref):
    i = pl.program_id(0)

    @pl.when(i == 0)
    def _():
        carry_ref[...] = jnp.zeros_like(carry_ref)

    is_sample = i == N_PROMPT_TILES
    x = jnp.where(is_sample, xs_ref[...], xp_ref[...])
    y = jnp.where(is_sample, ys_ref[...], yp_ref[...])
    o = jnp.where(is_sample, os_ref[...], op_ref[...])
    o = _rms(o.astype(F32), ag_ref[...]).astype(BF16)
    xm = x + _dot(y, wos_ref[...]) + _dot(o, woa_ref[...])
    xmid_ref[...] = xm
    h = _rms(xm, fg_ref[...])
    h_ref[...] = h
    h1 = h.astype(BF16)
    h2 = (h - h1.astype(F32)).astype(BF16)
    w1 = wr_ref[0]
    w2 = wr_ref[1]
    logits = _dot(h1, w1) + (_dot(h1, w2) + _dot(h2, w1)) + br_ref[...]
    eidx = lax.broadcasted_iota(jnp.int32, (TM, N_EXPERTS), 1)
    lane = lax.broadcasted_iota(jnp.int32, (TM, 128), 1)
    work = logits
    vals, sels = [], []
    idx_out = jnp.zeros((TM, 128), jnp.int32)
    for k in range(TOP_K):
        m = jnp.max(work, axis=-1, keepdims=True)
        idx = jnp.min(jnp.where(work == m, eidx, N_EXPERTS), axis=-1, keepdims=True)
        sel = eidx == idx
        vals.append(m)
        sels.append(sel)
        idx_out = jnp.where(lane == k, idx, idx_out)
        work = jnp.where(sel, -jnp.inf, work)
    es = [jnp.exp(v - vals[0]) for v in vals]
    tot = es[0] + es[1] + es[2] + es[3]
    gate_out = jnp.zeros((TM, 128), F32)
    for k in range(TOP_K):
        gate_out = jnp.where(lane == k, es[k] / tot, gate_out)
    idx_ref[...] = idx_out
    gate_ref[...] = gate_out
    multi = jnp.zeros((TM, N_EXPERTS), F32)
    for sel in sels:
        multi = jnp.where(sel, 1.0, multi)
    before = _dot(ltri_ref[...], multi.astype(BF16)) + carry_ref[...]
    rank_out = jnp.zeros((TM, 128), jnp.int32)
    for k in range(TOP_K):
        rk = jnp.sum(jnp.where(sels[k], before, 0.0), axis=-1, keepdims=True).astype(jnp.int32)
        rank_out = jnp.where(lane == k, rk, rank_out)
    rank_ref[...] = rank_out
    carry_ref[...] = carry_ref[...] + jnp.sum(multi, axis=0, keepdims=True)
    cnt_ref[...] = carry_ref[...]


def _outproj(xp, xs, yp, ys, op, os_, ag, wos, woa, fg, wr, br, ltri):
    tok = lambda n: pl.BlockSpec((TM, n), lambda i: (i, 0))
    prompt = lambda n: pl.BlockSpec((TM, n), lambda i: (jnp.minimum(i, N_PROMPT_TILES - 1), 0))
    sample = lambda n: pl.BlockSpec((TM, n), lambda i: (0, 0))
    return pl.pallas_call(
        _outproj_kernel,
        grid=(N_TILES,),
        in_specs=[
            prompt(D_MODEL), sample(D_MODEL), prompt(D_SSM), sample(D_SSM), prompt(D_ATT), sample(D_ATT),
            _const_spec((1, D_ATT)),
            _const_spec((D_SSM, D_MODEL)), _const_spec((D_ATT, D_MODEL)),
            _const_spec((1, D_MODEL)),
            _const_spec((2, D_MODEL, N_EXPERTS)), _const_spec((1, N_EXPERTS)),
            _const_spec((TM, TM)),
        ],
        out_specs=[tok(D_MODEL), tok(D_MODEL), tok(128), tok(128), tok(128),
                   _const_spec((1, N_EXPERTS))],
        out_shape=[
            jax.ShapeDtypeStruct((N_TOK, D_MODEL), F32),
            jax.ShapeDtypeStruct((N_TOK, D_MODEL), F32),
            jax.ShapeDtypeStruct((N_TOK, 128), jnp.int32),
            jax.ShapeDtypeStruct((N_TOK, 128), F32),
            jax.ShapeDtypeStruct((N_TOK, 128), jnp.int32),
            jax.ShapeDtypeStruct((1, N_EXPERTS), F32),
        ],
        scratch_shapes=[pltpu.VMEM((1, N_EXPERTS), F32)],
        compiler_params=_params(),
        name="outproj_router",
    )(xp, xs, yp, ys, op, os_, ag, wos, woa, fg, wr, br, ltri)


def _scatter_kernel(dest_ref, h_ref, rows_in_ref, rows_ref, sem):
    del rows_in_ref

    def issue(r, carry):
        for k in range(TOP_K):
            d = dest_ref[r * TOP_K + k]
            pltpu.make_async_copy(h_ref.at[pl.ds(r, 1)], rows_ref.at[pl.ds(d, 1)], sem).start()
        return carry

    lax.fori_loop(0, TM, issue, 0)
    for _ in range(TOP_K):
        pltpu.make_async_copy(h_ref, rows_ref.at[pl.ds(0, TM)], sem).wait()


def _scatter_rows(dest_flat, h, rows_init):
    return pl.pallas_call(
        _scatter_kernel,
        grid=(N_TILES,),
        in_specs=[
            pl.BlockSpec((TM * TOP_K,), lambda i: (i,), memory_space=pltpu.SMEM),
            pl.BlockSpec((TM, D_MODEL), lambda i: (i, 0)),
            pl.BlockSpec(memory_space=pl.ANY),
        ],
        out_specs=pl.BlockSpec(memory_space=pl.ANY),
        out_shape=jax.ShapeDtypeStruct((MOE_ROWS, D_MODEL), F32),
        scratch_shapes=[pltpu.SemaphoreType.DMA(())],
        input_output_aliases={2: 0},
        compiler_params=_params(),
        name="moe_scatter",
    )(dest_flat, h, rows_init)


def _expert_kernel(be_ref, nu_ref, x_ref, wgu_ref, bgu_ref, wd_ref, bd_ref, y_ref, wgu_s, wd_s):
    i = pl.program_id(0)
    prev = be_ref[jnp.maximum(i - 1, 0)]

    @pl.when((i == 0) | (be_ref[i] != prev))
    def _():
        wgu_s[...] = wgu_ref[0].astype(BF16)
        wd_s[...] = wd_ref[0].astype(BF16)

    @pl.when(i < nu_ref[0])
    def _():
        gu = _dot(x_ref[...].astype(BF16), wgu_s[...]) + bgu_ref[0]
        gate = jnp.minimum(gu[:, :D_FF], SWIGLU_LIMIT)
        up = jnp.clip(gu[:, D_FF:], -SWIGLU_LIMIT, SWIGLU_LIMIT)
        act = (up + 1.0) * gate * _sigmoid(gate * SWIGLU_ALPHA)
        y_ref[...] = _dot(act.astype(BF16), wd_s[...]) + bd_ref[0]

    @pl.when(i >= nu_ref[0])
    def _():
        y_ref[...] = jnp.zeros_like(y_ref)


def _experts(block_expert, n_used, rows, wgu, bgu, wd, bd):
    grid_spec = pltpu.PrefetchScalarGridSpec(
        num_scalar_prefetch=2,
        grid=(MOE_BLOCKS,),
        in_specs=[
            pl.BlockSpec((MOE_BM, D_MODEL), lambda i, be, nu: (jnp.minimum(i, nu[0] - 1), 0)),
            pl.BlockSpec((1, D_MODEL, 2 * D_FF), lambda i, be, nu: (be[i], 0, 0)),
            pl.BlockSpec((1, 1, 2 * D_FF), lambda i, be, nu: (be[i], 0, 0)),
            pl.BlockSpec((1, D_FF, D_MODEL), lambda i, be, nu: (be[i], 0, 0)),
            pl.BlockSpec((1, 1, D_MODEL), lambda i, be, nu: (be[i], 0, 0)),
        ],
        out_specs=pl.BlockSpec((MOE_BM, D_MODEL), lambda i, be, nu: (i, 0)),
        scratch_shapes=[pltpu.VMEM((D_MODEL, 2 * D_FF), BF16), pltpu.VMEM((D_FF, D_MODEL), BF16)],
    )
    return pl.pallas_call(
        _expert_kernel,
        grid_spec=grid_spec,
        out_shape=jax.ShapeDtypeStruct((MOE_ROWS, D_MODEL), F32),
        compiler_params=_params(),
        name="moe_experts",
    )(block_expert, n_used, rows, wgu, bgu, wd, bd)


def _combine_kernel(dest_ref, gate_ref, xmid_ref, g_ref, rows_ref, yp_ref, ys_ref, buf_ref, sem):
    i = pl.program_id(0)

    def issue(r, carry):
        for k in range(TOP_K):
            d = dest_ref[r * TOP_K + k]
            pltpu.make_async_copy(rows_ref.at[pl.ds(d, 1)], buf_ref.at[k, pl.ds(r, 1)], sem).start()
        return carry

    lax.fori_loop(0, TM, issue, 0)
    for k in range(TOP_K):
        pltpu.make_async_copy(rows_ref.at[pl.ds(0, TM)], buf_ref.at[k], sem).wait()
    acc = xmid_ref[...]
    for k in range(TOP_K):
        acc = acc + buf_ref[k] * gate_ref[:, k:k + 1]
    y = _rms(acc, g_ref[...])

    @pl.when(i < N_PROMPT_TILES)
    def _():
        yp_ref[...] = y

    @pl.when(i == N_PROMPT_TILES)
    def _():
        ys_ref[...] = y


def _combine(dest_flat, gates, xmid, g, y_rows):
    return pl.pallas_call(
        _combine_kernel,
        grid=(N_TILES,),
        in_specs=[
            pl.BlockSpec((TM * TOP_K,), lambda i: (i,), memory_space=pltpu.SMEM),
            pl.BlockSpec((TM, 128), lambda i: (i, 0)),
            pl.BlockSpec((TM, D_MODEL), lambda i: (i, 0)),
            _const_spec((1, D_MODEL)),
            pl.BlockSpec(memory_space=pl.ANY),
        ],
        out_specs=[
            pl.BlockSpec((TM, D_MODEL), lambda i: (jnp.minimum(i, N_PROMPT_TILES - 1), 0)),
            pl.BlockSpec((TM, D_MODEL), lambda i: (0, 0)),
        ],
        out_shape=[
            jax.ShapeDtypeStruct((N_PROMPT, D_MODEL), F32),
            jax.ShapeDtypeStruct((N_SAMPLE, D_MODEL), F32),
        ],
        scratch_shapes=[pltpu.VMEM((TOP_K, TM, D_MODEL), F32), pltpu.SemaphoreType.DMA(())],
        compiler_params=_params(),
        name="moe_combine",
    )(dest_flat, gates, xmid, g, y_rows)


def _pair_rows(v):
    return jnp.repeat(v.reshape(HEAD_PAIRS, 2), CHUNK, axis=1)


def _layer(l, xp, xs, cache_k, cache_v, state_conv, state_ssm,
           norm_mix_g, w_in, conv_w, conv_b, dt_bias, a_log, d_skip, ssm_norm_g,
           att_norm_g, rel_bias_table, w_out, norm_ffn_g, w_router, b_router,
           w_gate_up, b_gate_up, w_down, b_down, norm_final_g):
    wb = w_in[l].astype(BF16)
    c0 = D_SSM
    c1 = c0 + D_CONV
    c2 = c1 + SSM_HEADS
    c3 = c2 + D_ATT
    c4 = c3 + D_ATT
    z, xbc, dt, dtt, q, k, v, kf, vf, ctail = _inproj(
        xp, xs, norm_mix_g[l][None], wb[:, :c0], wb[:, c0:c1], wb[:, c1:c2], wb[:, c1:c2].T,
        wb[:, c2:c3], wb[:, c3:c4], wb[:, c4:])

    n_chunks = N_TOK // CHUNK
    dtp = dtt.reshape(HEAD_PAIRS, 2, n_chunks, CHUNK).transpose(2, 0, 1, 3).reshape(n_chunks, HEAD_PAIRS, 128)
    hp = jnp.arange(D_SSM) // SSM_HEAD_DIM
    expand = (hp[None, :] == jnp.arange(SSM_HEADS)[:, None]).astype(BF16)
    tril = jnp.tril(jnp.ones((CHUNK, CHUNK), BF16))
    r128 = jnp.arange(128)
    triu2 = ((r128[:, None] // CHUNK == r128[None, :] // CHUNK) & (r128[:, None] <= r128[None, :])).astype(BF16)
    consts = (conv_w[l], conv_b[l][None], dt_bias[l][None], _pair_rows(dt_bias[l]),
              jnp.repeat(a_log[l], SSM_HEAD_DIM)[None], _pair_rows(a_log[l]),
              jnp.repeat(d_skip[l], SSM_HEAD_DIM)[None], ssm_norm_g[l][None],
              expand, tril, triu2)
    y_ssm_p, ssm_p = _ssd_prompt(z, xbc, dt, dtp, consts)
    y_ssm_s, ssm_s = _ssd_sample(z, xbc, dt, dtp, state_conv[l],
                                 state_ssm[l].reshape(DEC_BATCH, D_SSM, D_STATE), consts)

    rel = ATT_LEFT + jnp.arange(CHUNK)[:, None] - jnp.arange(BAND)[None, :]
    bias = rel_bias_table[l][:, jnp.clip(rel, -REL_CLIP, REL_CLIP) + REL_CLIP]
    bias2 = bias.reshape(ATT_PAIRS, 2 * CHUNK, BAND)
    o_att_p = _attn_prompt(q, k, v, bias2)
    o_att_s = _attn_sample(q, k, v, cache_k[l].reshape(DEC_BATCH, ATT_LEFT, D_ATT),
                           cache_v[l].reshape(DEC_BATCH, ATT_LEFT, D_ATT), bias2)

    wo = w_out[l].astype(BF16)
    wr = w_router[l]
    wr1 = wr.astype(BF16)
    wr2 = (wr - wr1.astype(F32)).astype(BF16)
    ltri = jnp.tril(jnp.ones((TM, TM), BF16), -1)
    xmid, h, top_idx, gates, rank, counts = _outproj(
        xp, xs, y_ssm_p, y_ssm_s, o_att_p, o_att_s, att_norm_g[l][None], wo[:D_SSM], wo[D_SSM:], norm_ffn_g[l][None],
        jnp.stack([wr1, wr2]), b_router[l][None], ltri)

    counts = counts[0].astype(jnp.int32)
    padded = (counts + MOE_BM - 1) // MOE_BM * MOE_BM
    pad_end = jnp.cumsum(padded)
    pad_start = pad_end - padded
    dest = (pad_start[top_idx[:, :TOP_K]] + rank[:, :TOP_K]).reshape(-1).astype(jnp.int32)
    block_expert = jnp.minimum(
        jnp.searchsorted(pad_end, jnp.arange(MOE_BLOCKS, dtype=jnp.int32) * MOE_BM, side='right'),
        N_EXPERTS - 1).astype(jnp.int32)
    n_used = (pad_end[-1:] // MOE_BM).astype(jnp.int32)

    rows = _scatter_rows(dest, h, jnp.zeros((MOE_ROWS, D_MODEL), F32))
    y_rows = _experts(block_expert, n_used, rows, w_gate_up[l], b_gate_up[l][:, None, :],
                      w_down[l], b_down[l][:, None, :])
    y_p, y_s = _combine(dest, gates, xmid, norm_final_g[None], y_rows)

    keep = min(ATT_LEFT, SEQ)
    k_p = kf[:BATCH, TM - keep:].reshape(BATCH, keep, ATT_HEADS, ATT_HEAD_DIM)
    v_p = vf[:BATCH, TM - keep:].reshape(BATCH, keep, ATT_HEADS, ATT_HEAD_DIM)
    k_s = kf[BATCH].reshape(DEC_BATCH, DEC_SEQ, ATT_HEADS, ATT_HEAD_DIM)
    v_s = vf[BATCH].reshape(DEC_BATCH, DEC_SEQ, ATT_HEADS, ATT_HEAD_DIM)
    conv_p = ctail[:BATCH, -(CONV_W - 1):]
    conv_s = ctail[BATCH].reshape(DEC_BATCH, 8, D_CONV)[:, -(CONV_W - 1):]
    ssm_p = ssm_p.reshape(BATCH, SSM_HEADS, SSM_HEAD_DIM, D_STATE)
    ssm_s = ssm_s.reshape(DEC_BATCH, SSM_HEADS, SSM_HEAD_DIM, D_STATE)
    return (y_p.reshape(BATCH, SEQ, D_MODEL), y_s.reshape(DEC_BATCH, DEC_SEQ, D_MODEL),
            k_p, v_p, conv_p, ssm_p, k_s, v_s, conv_s, ssm_s)


def kernel(x_prompt, x_sample, cache_k, cache_v, state_conv, state_ssm, norm_mix_g, w_in, conv_w, conv_b,
           dt_bias, a_log, d_skip, ssm_norm_g, att_norm_g, rel_bias_table, w_out, norm_ffn_g, w_router,
           b_router, w_gate_up, b_gate_up, w_down, b_down, norm_final_g):
    assert w_in.shape[0] == 1, "single trunk layer"
    xp = x_prompt.reshape(N_PROMPT, D_MODEL)
    xs = x_sample.reshape(N_SAMPLE, D_MODEL)
    outs = _layer(0, xp, xs, cache_k, cache_v, state_conv, state_ssm,
                  norm_mix_g, w_in, conv_w, conv_b, dt_bias, a_log, d_skip, ssm_norm_g,
                  att_norm_g, rel_bias_table, w_out, norm_ffn_g, w_router, b_router,
                  w_gate_up, b_gate_up, w_down, b_down, norm_final_g)
    y_p, y_s, k_p, v_p, conv_p, ssm_p, k_s, v_s, conv_s, ssm_s = outs
    return (y_p, y_s, k_p[None], v_p[None], conv_p[None], ssm_p[None],
            k_s[None], v_s[None], conv_s[None], ssm_s[None])
```

```python
import functools

import jax
import jax.numpy as jnp
from jax import lax
from jax.experimental import pallas as pl
from jax.experimental.pallas import tpu as pltpu

D_MODEL = 1024
BATCH = 8
SEQ = 2048
DEC_BATCH = 8
DEC_SEQ = 64
CHUNK = 64
SSM_HEADS = 16
SSM_HEAD_DIM = 64
D_SSM = SSM_HEADS * SSM_HEAD_DIM
SSM_GROUPS = 2
D_STATE = 128
CONV_W = 4
D_BC = SSM_GROUPS * D_STATE
D_CONV = D_SSM + 2 * D_BC
ATT_HEADS = 8
ATT_HEAD_DIM = 64
D_ATT = ATT_HEADS * ATT_HEAD_DIM
LEFT_CHUNKS = 8
ATT_LEFT = LEFT_CHUNKS * CHUNK
BAND = ATT_LEFT + CHUNK
REL_CLIP = 128
ATT_SCALE = ATT_HEAD_DIM ** -0.5
N_EXPERTS = 32
TOP_K = 4
D_FF = D_MODEL
SWIGLU_ALPHA = 1.702
SWIGLU_LIMIT = 7.0
EPS = 1e-5

F32 = jnp.float32
BF16 = jnp.bfloat16

N_PROMPT = BATCH * SEQ
N_SAMPLE = DEC_BATCH * DEC_SEQ
N_TOK = N_PROMPT + N_SAMPLE
TM = 512
N_PROMPT_TILES = N_PROMPT // TM
N_TILES = N_TOK // TM
TILES_PER_SEQ = SEQ // TM
CHUNKS_PER_TILE = TM // CHUNK
N_CHUNKS_SEQ = SEQ // CHUNK
HEAD_PAIRS = SSM_HEADS // 2
PAIRS_PER_GROUP = HEAD_PAIRS // SSM_GROUPS
ATT_PAIRS = ATT_HEADS // 2
CONV_HEAD = 8
MOE_BM = 256
N_ASSIGN = N_TOK * TOP_K
MOE_BLOCKS = N_ASSIGN // MOE_BM + N_EXPERTS
MOE_ROWS = MOE_BLOCKS * MOE_BM
VMEM_LIMIT = 56 * 1024 * 1024


def _dot(a, b):
    return jnp.dot(a, b, preferred_element_type=F32)


def _dot_nt(a, b):
    return lax.dot_general(a, b, (((1,), (1,)), ((), ())), preferred_element_type=F32)


def _dot_tn(a, b):
    return lax.dot_general(a, b, (((0,), (0,)), ((), ())), preferred_element_type=F32)


def _split3(x):
    x1 = x.astype(BF16)
    r1 = x - x1.astype(F32)
    x2 = r1.astype(BF16)
    r2 = r1 - x2.astype(F32)
    return x1, x2, r2.astype(BF16)


def _dot_exact_rhs(x, m):
    x1, x2, x3 = _split3(x)
    return _dot(x1, m) + _dot(x2, m) + _dot(x3, m)


def _dot_exact_lhs(m, x):
    x1, x2, x3 = _split3(x)
    return _dot(m, x1) + _dot(m, x2) + _dot(m, x3)


def _rms(x, g):
    return x * lax.rsqrt(jnp.mean(x * x, axis=-1, keepdims=True) + EPS) * g


def _sigmoid(x):
    return 1.0 / (1.0 + jnp.exp(-x))


def _softplus(x):
    return jnp.maximum(x, 0.0) + jnp.log(1.0 + jnp.exp(-jnp.abs(x)))


def _const_spec(shape):
    nd = len(shape)
    return pl.BlockSpec(shape, lambda *_: (0,) * nd)


def _params(n_axes=1):
    return pltpu.CompilerParams(dimension_semantics=("arbitrary",) * n_axes,
                                vmem_limit_bytes=VMEM_LIMIT)


def _inproj_kernel(xp_ref, xs_ref, g_ref, wz_ref, wxbc_ref, wdt_ref, wdtt_ref, wq_ref, wk_ref, wv_ref,
                   z_ref, xbc_ref, dt_ref, dtt_ref, q_ref, k_ref, v_ref, kf_ref, vf_ref, ctail_ref):
    i = pl.program_id(0)
    x = jnp.where(i == N_PROMPT_TILES, xs_ref[...], xp_ref[...])
    h = _rms(x, g_ref[...]).astype(BF16)
    z_ref[...] = _dot(h, wz_ref[...]).astype(BF16)
    xbc = _dot(h, wxbc_ref[...])
    xbc_ref[...] = xbc.astype(BF16)
    for c in range(CHUNKS_PER_TILE):
        ctail_ref[0, c * 8:(c + 1) * 8, :] = xbc[c * CHUNK + CHUNK - 8:(c + 1) * CHUNK, :]
    dt_ref[...] = _dot(h, wdt_ref[...])
    dtt_ref[...] = _dot_nt(wdtt_ref[...], h)
    q_ref[...] = (_dot(h, wq_ref[...]) * ATT_SCALE).astype(BF16)
    k = _dot(h, wk_ref[...])
    v = _dot(h, wv_ref[...])
    k_ref[...] = k.astype(BF16)
    v_ref[...] = v.astype(BF16)
    kf_ref[0] = k
    vf_ref[0] = v


def _inproj(xp, xs, g, wz, wxbc, wdt, wdtt, wq, wk, wv):
    tok = lambda n: pl.BlockSpec((TM, n), lambda i: (i, 0))
    tail_idx = lambda i: (i // TILES_PER_SEQ, 0, 0)
    n_tail = BATCH + 1
    return pl.pallas_call(
        _inproj_kernel,
        grid=(N_TILES,),
        in_specs=[
            pl.BlockSpec((TM, D_MODEL), lambda i: (jnp.minimum(i, N_PROMPT_TILES - 1), 0)),
            pl.BlockSpec((TM, D_MODEL), lambda i: (0, 0)),
            _const_spec((1, D_MODEL)),
            _const_spec((D_MODEL, D_SSM)),
            _const_spec((D_MODEL, D_CONV)),
            _const_spec((D_MODEL, SSM_HEADS)),
            _const_spec((SSM_HEADS, D_MODEL)),
            _const_spec((D_MODEL, D_ATT)),
            _const_spec((D_MODEL, D_ATT)),
            _const_spec((D_MODEL, D_ATT)),
        ],
        out_specs=[
            tok(D_SSM), tok(D_CONV), tok(SSM_HEADS),
            pl.BlockSpec((SSM_HEADS, TM), lambda i: (0, i)),
            tok(D_ATT), tok(D_ATT), tok(D_ATT),
            pl.BlockSpec((1, TM, D_ATT), tail_idx),
            pl.BlockSpec((1, TM, D_ATT), tail_idx),
            pl.BlockSpec((1, CHUNKS_PER_TILE * 8, D_CONV), tail_idx),
        ],
        out_shape=[
            jax.ShapeDtypeStruct((N_TOK, D_SSM), BF16),
            jax.ShapeDtypeStruct((N_TOK, D_CONV), BF16),
            jax.ShapeDtypeStruct((N_TOK, SSM_HEADS), F32),
            jax.ShapeDtypeStruct((SSM_HEADS, N_TOK), F32),
            jax.ShapeDtypeStruct((N_TOK, D_ATT), BF16),
            jax.ShapeDtypeStruct((N_TOK, D_ATT), BF16),
            jax.ShapeDtypeStruct((N_TOK, D_ATT), BF16),
            jax.ShapeDtypeStruct((n_tail, TM, D_ATT), F32),
            jax.ShapeDtypeStruct((n_tail, TM, D_ATT), F32),
            jax.ShapeDtypeStruct((n_tail, CHUNKS_PER_TILE * 8, D_CONV), F32),
        ],
        compiler_params=_params(),
        name="inproj",
    )(xp, xs, g, wz, wxbc, wdt, wdtt, wq, wk, wv)


def _ssd_tile(n_chunks, z_ref, xbc_ref, dt_ref, dtp_ref, cw_ref, cb_ref, dtb_ref, dtbp_ref,
              alog_e_ref, alog_p_ref, dskip_e_ref, ng_ref, expand_ref, tril_ref, triu2_ref,
              y_ref, xw_ref, xa_ref, dte_ref, state_ref):
    rows = n_chunks * CHUNK
    xw_ref[CONV_HEAD:CONV_HEAD + rows, :] = xbc_ref[...].astype(F32)
    acc = cb_ref[...]
    for tap in range(CONV_W):
        off = CONV_HEAD - (CONV_W - 1) + tap
        acc = acc + cw_ref[tap:tap + 1, :] * xw_ref[off:off + rows, :]
    xa_ref[...] = acc * _sigmoid(acc)
    dt = _softplus(dt_ref[...] + dtb_ref[...])
    dte_ref[...] = _dot_exact_rhs(dt, expand_ref[...])

    a_e = -jnp.exp(alog_e_ref[...])
    a_p = -jnp.exp(alog_p_ref[...])
    row_i = lax.broadcasted_iota(jnp.int32, (CHUNK, 128), 0)
    col_i = lax.broadcasted_iota(jnp.int32, (CHUNK, 128), 1)
    causal2 = row_i >= (col_i % CHUNK)
    bd_r = lax.broadcasted_iota(jnp.int32, (128, 128), 0) // CHUNK
    bd_c = lax.broadcasted_iota(jnp.int32, (128, 128), 1) // CHUNK
    blockdiag = bd_r == bd_c

    def chunk(c, carry):
        r0 = pl.multiple_of(c * CHUNK, CHUNK)
        xs = xa_ref[pl.ds(r0, CHUNK), 0:D_SSM]
        bm = xa_ref[pl.ds(r0, CHUNK), D_SSM:D_SSM + D_BC].astype(BF16)
        cm = xa_ref[pl.ds(r0, CHUNK), D_SSM + D_BC:D_CONV].astype(BF16)
        dt_e = dte_ref[pl.ds(r0, CHUNK), :]
        acum = _dot_exact_lhs(tril_ref[...], dt_e * a_e)
        dtp = _softplus(dtp_ref[c] + dtbp_ref[...])
        acum_p = _dot_exact_rhs(dtp * a_p, triu2_ref[...])
        xdt = xs * dt_e
        a_last = acum[CHUNK - 1:CHUNK, :]
        xdt_end = (xdt * jnp.exp(a_last - acum)).astype(BF16)
        y_parts = []
        for g in range(SSM_GROUPS):
            bg = bm[:, g * D_STATE:(g + 1) * D_STATE]
            cg = cm[:, g * D_STATE:(g + 1) * D_STATE]
            cb2 = _dot_nt(cg, jnp.concatenate([bg, bg], axis=0))
            for jj in range(PAIRS_PER_GROUP):
                j = g * PAIRS_PER_GROUP + jj
                seg = acum[:, j * 128:(j + 1) * 128] - acum_p[j:j + 1, :]
                decay = jnp.exp(jnp.where(causal2, seg, -jnp.inf))
                s_pair = (cb2 * decay).astype(BF16)
                x2 = xdt[:, j * 128:(j + 1) * 128]
                rhs = jnp.where(blockdiag, jnp.concatenate([x2, x2], axis=0), 0.0).astype(BF16)
                y_parts.append(_dot(s_pair, rhs))
        y_diag = jnp.concatenate(y_parts, axis=1)
        half = D_SSM // SSM_GROUPS
        y_off = jnp.concatenate(
            [_dot(cm[:, g * D_STATE:(g + 1) * D_STATE], state_ref[:, g * half:(g + 1) * half].astype(BF16))
             for g in range(SSM_GROUPS)], axis=1)
        new_s = jnp.concatenate(
            [_dot_tn(bm[:, g * D_STATE:(g + 1) * D_STATE], xdt_end[:, g * half:(g + 1) * half])
             for g in range(SSM_GROUPS)], axis=1)
        state_ref[...] = state_ref[...] * jnp.exp(a_last) + new_s
        y = y_diag + y_off * jnp.exp(acum) + dskip_e_ref[...] * xs
        zc = z_ref[pl.ds(r0, CHUNK), :].astype(F32)
        y = y * (zc * _sigmoid(zc))
        yn = jnp.concatenate(
            [y[:, g * half:(g + 1) * half]
             * lax.rsqrt(jnp.mean(jnp.square(y[:, g * half:(g + 1) * half]), axis=-1, keepdims=True) + EPS)
             for g in range(SSM_GROUPS)], axis=1)
        y_ref[pl.ds(r0, CHUNK), :] = (yn * ng_ref[...]).astype(BF16)
        return carry

    lax.fori_loop(0, n_chunks, chunk, 0)


def _state_store(state_ref, out_ref):
    for j in range(HEAD_PAIRS):
        out_ref[0, j * 128:(j + 1) * 128, :] = state_ref[:, j * 128:(j + 1) * 128].T


def _ssd_prompt_kernel(z_ref, xbc_ref, dt_ref, dtp_ref, cw_ref, cb_ref, dtb_ref, dtbp_ref,
                       alog_e_ref, alog_p_ref, dskip_e_ref, ng_ref, expand_ref, tril_ref, triu2_ref,
                       y_ref, ssm_ref, xw_ref, xa_ref, dte_ref, state_ref, tail_ref):
    t = pl.program_id(1)

    @pl.when(t == 0)
    def _():
        state_ref[...] = jnp.zeros_like(state_ref)
        xw_ref[0:CONV_HEAD, :] = jnp.zeros((CONV_HEAD, D_CONV), F32)

    @pl.when(t > 0)
    def _():
        xw_ref[0:CONV_HEAD, :] = tail_ref[...]

    _ssd_tile(CHUNKS_PER_TILE, z_ref, xbc_ref, dt_ref, dtp_ref, cw_ref, cb_ref, dtb_ref, dtbp_ref,
              alog_e_ref, alog_p_ref, dskip_e_ref, ng_ref, expand_ref, tril_ref, triu2_ref,
              y_ref, xw_ref, xa_ref, dte_ref, state_ref)
    tail_ref[...] = xw_ref[TM:TM + CONV_HEAD, :]

    @pl.when(t == TILES_PER_SEQ - 1)
    def _():
        _state_store(state_ref, ssm_ref)


def _ssd_sample_kernel(z_ref, xbc_ref, dt_ref, dtp_ref, cprev_ref, sprev_ref,
                       cw_ref, cb_ref, dtb_ref, dtbp_ref,
                       alog_e_ref, alog_p_ref, dskip_e_ref, ng_ref, expand_ref, tril_ref, triu2_ref,
                       y_ref, ssm_ref, xw_ref, xa_ref, dte_ref, state_ref):
    xw_ref[0:CONV_HEAD, :] = jnp.zeros((CONV_HEAD, D_CONV), F32)
    xw_ref[CONV_HEAD - (CONV_W - 1):CONV_HEAD, :] = cprev_ref[0]
    for j in range(HEAD_PAIRS):
        state_ref[:, j * 128:(j + 1) * 128] = sprev_ref[0, j * 128:(j + 1) * 128, :].T
    _ssd_tile(1, z_ref, xbc_ref, dt_ref, dtp_ref, cw_ref, cb_ref, dtb_ref, dtbp_ref,
              alog_e_ref, alog_p_ref, dskip_e_ref, ng_ref, expand_ref, tril_ref, triu2_ref,
              y_ref, xw_ref, xa_ref, dte_ref, state_ref)
    _state_store(state_ref, ssm_ref)


def _ssd_const_specs():
    return [
        _const_spec((CONV_W, D_CONV)), _const_spec((1, D_CONV)),
        _const_spec((1, SSM_HEADS)), _const_spec((HEAD_PAIRS, 128)),
        _const_spec((1, D_SSM)), _const_spec((HEAD_PAIRS, 128)),
        _const_spec((1, D_SSM)), _const_spec((1, D_SSM)),
        _const_spec((SSM_HEADS, D_SSM)), _const_spec((CHUNK, CHUNK)), _const_spec((128, 128)),
    ]


def _ssd_prompt(z, xbc, dt, dtp, consts):
    tile = lambda b, t: (b * TILES_PER_SEQ + t, 0)
    return pl.pallas_call(
        _ssd_prompt_kernel,
        grid=(BATCH, TILES_PER_SEQ),
        in_specs=[
            pl.BlockSpec((TM, D_SSM), tile),
            pl.BlockSpec((TM, D_CONV), tile),
            pl.BlockSpec((TM, SSM_HEADS), tile),
            pl.BlockSpec((CHUNKS_PER_TILE, HEAD_PAIRS, 128), lambda b, t: (b * TILES_PER_SEQ + t, 0, 0)),
        ] + _ssd_const_specs(),
        out_specs=[
            pl.BlockSpec((TM, D_SSM), tile),
            pl.BlockSpec((1, D_SSM, D_STATE), lambda b, t: (b, 0, 0)),
        ],
        out_shape=[
            jax.ShapeDtypeStruct((N_PROMPT, D_SSM), BF16),
            jax.ShapeDtypeStruct((BATCH, D_SSM, D_STATE), F32),
        ],
        scratch_shapes=[
            pltpu.VMEM((CONV_HEAD + TM, D_CONV), F32),
            pltpu.VMEM((TM, D_CONV), F32),
            pltpu.VMEM((TM, D_SSM), F32),
            pltpu.VMEM((D_STATE, D_SSM), F32),
            pltpu.VMEM((CONV_HEAD, D_CONV), F32),
        ],
        compiler_params=_params(2),
        name="ssd_prompt",
    )(z, xbc, dt, dtp, *consts)


def _ssd_sample(z, xbc, dt, dtp, conv_prev, ssm_prev, consts):
    first = N_PROMPT // CHUNK
    row = lambda b: (first + b, 0)
    return pl.pallas_call(
        _ssd_sample_kernel,
        grid=(DEC_BATCH,),
        in_specs=[
            pl.BlockSpec((CHUNK, D_SSM), row),
            pl.BlockSpec((CHUNK, D_CONV), row),
            pl.BlockSpec((CHUNK, SSM_HEADS), row),
            pl.BlockSpec((1, HEAD_PAIRS, 128), lambda b: (first + b, 0, 0)),
            pl.BlockSpec((1, CONV_W - 1, D_CONV), lambda b: (b, 0, 0)),
            pl.BlockSpec((1, D_SSM, D_STATE), lambda b: (b, 0, 0)),
        ] + _ssd_const_specs(),
        out_specs=[
            pl.BlockSpec((CHUNK, D_SSM), lambda b: (b, 0)),
            pl.BlockSpec((1, D_SSM, D_STATE), lambda b: (b, 0, 0)),
        ],
        out_shape=[
            jax.ShapeDtypeStruct((N_SAMPLE, D_SSM), BF16),
            jax.ShapeDtypeStruct((DEC_BATCH, D_SSM, D_STATE), F32),
        ],
        scratch_shapes=[
            pltpu.VMEM((CONV_HEAD + CHUNK, D_CONV), F32),
            pltpu.VMEM((CHUNK, D_CONV), F32),
            pltpu.VMEM((CHUNK, D_SSM), F32),
            pltpu.VMEM((D_STATE, D_SSM), F32),
        ],
        compiler_params=_params(),
        name="ssd_sample",
    )(z, xbc, dt, dtp, conv_prev, ssm_prev, *consts)


def _attn_chunks(n_chunks, first_chunk, q_ref, kpad_ref, vpad_ref, bias_ref, o_ref):
    lane = lax.broadcasted_iota(jnp.int32, (CHUNK, 128), 1)
    low = lane < ATT_HEAD_DIM
    kj = lax.broadcasted_iota(jnp.int32, (2 * CHUNK, BAND), 1)

    def chunk(c, carry):
        r0 = pl.multiple_of(c * CHUNK, CHUNK)
        first_valid = jnp.maximum(LEFT_CHUNKS - (first_chunk + c), 0) * CHUNK
        valid = kj >= first_valid
        outs = []
        for j in range(ATT_PAIRS):
            qp = q_ref[pl.ds(r0, CHUNK), j * 128:(j + 1) * 128]
            zero = jnp.zeros_like(qp)
            q2 = jnp.concatenate([jnp.where(low, qp, zero), jnp.where(low, zero, qp)], axis=0)
            kb = kpad_ref[pl.ds(r0, BAND), j * 128:(j + 1) * 128]
            vb = vpad_ref[pl.ds(r0, BAND), j * 128:(j + 1) * 128]
            s = _dot_nt(q2, kb) + bias_ref[j]
            s = jnp.where(valid, s, -jnp.inf)
            m = jnp.max(s, axis=-1, keepdims=True)
            e = jnp.exp(s - m)
            denom = jnp.sum(e, axis=-1, keepdims=True)
            r = _dot(e.astype(BF16), vb) / denom
            outs.append(jnp.where(low, r[0:CHUNK], r[CHUNK:2 * CHUNK]))
        o_ref[pl.ds(r0, CHUNK), :] = jnp.concatenate(outs, axis=1).astype(BF16)
        return carry

    lax.fori_loop(0, n_chunks, chunk, 0)


def _attn_prompt_kernel(q_ref, k_ref, v_ref, bias_ref, o_ref, kpad_ref, vpad_ref):
    kpad_ref[0:ATT_LEFT, :] = jnp.zeros((ATT_LEFT, D_ATT), BF16)
    vpad_ref[0:ATT_LEFT, :] = jnp.zeros((ATT_LEFT, D_ATT), BF16)
    kpad_ref[ATT_LEFT:ATT_LEFT + SEQ, :] = k_ref[...]
    vpad_ref[ATT_LEFT:ATT_LEFT + SEQ, :] = v_ref[...]
    _attn_chunks(N_CHUNKS_SEQ, 0, q_ref, kpad_ref, vpad_ref, bias_ref, o_ref)


def _attn_sample_kernel(q_ref, k_ref, v_ref, ck_ref, cv_ref, bias_ref, o_ref, kpad_ref, vpad_ref):
    kpad_ref[0:ATT_LEFT, :] = ck_ref[0].astype(BF16)
    vpad_ref[0:ATT_LEFT, :] = cv_ref[0].astype(BF16)
    kpad_ref[ATT_LEFT:BAND, :] = k_ref[...]
    vpad_ref[ATT_LEFT:BAND, :] = v_ref[...]
    _attn_chunks(1, LEFT_CHUNKS, q_ref, kpad_ref, vpad_ref, bias_ref, o_ref)


def _attn_prompt(q, k, v, bias2):
    seq = pl.BlockSpec((SEQ, D_ATT), lambda b: (b, 0))
    return pl.pallas_call(
        _attn_prompt_kernel,
        grid=(BATCH,),
        in_specs=[seq, seq, seq, _const_spec((ATT_PAIRS, 2 * CHUNK, BAND))],
        out_specs=seq,
        out_shape=jax.ShapeDtypeStruct((N_PROMPT, D_ATT), BF16),
        scratch_shapes=[pltpu.VMEM((ATT_LEFT + SEQ, D_ATT), BF16),
                        pltpu.VMEM((ATT_LEFT + SEQ, D_ATT), BF16)],
        compiler_params=_params(),
        name="attn_prompt",
    )(q, k, v, bias2)


def _attn_sample(q, k, v, cache_k, cache_v, bias2):
    first = N_PROMPT // CHUNK
    row = pl.BlockSpec((CHUNK, D_ATT), lambda b: (first + b, 0))
    cache = pl.BlockSpec((1, ATT_LEFT, D_ATT), lambda b: (b, 0, 0))
    return pl.pallas_call(
        _attn_sample_kernel,
        grid=(DEC_BATCH,),
        in_specs=[row, row, row, cache, cache, _const_spec((ATT_PAIRS, 2 * CHUNK, BAND))],
        out_specs=pl.BlockSpec((CHUNK, D_ATT), lambda b: (b, 0)),
        out_shape=jax.ShapeDtypeStruct((N_SAMPLE, D_ATT), BF16),
        scratch_shapes=[pltpu.VMEM((BAND, D_ATT), BF16), pltpu.VMEM((BAND, D_ATT), BF16)],
        compiler_params=_params(),
        name="attn_sample",
    )(q, k, v, cache_k, cache_v, bias2)


def _outproj_kernel(xp_ref, xs_ref, yp_ref, ys_ref, op_ref, os_ref, ag_ref, wos_ref, woa_ref, fg_ref,
                    wr_ref, br_ref, ltri_ref,
                    xmid_ref, h_ref, idx_ref, gate_ref, rank_ref, cnt_ref, carry_ref):
    i = pl.program_id(0)

    @pl.when(i == 0)
    def _():
        carry_ref[...] = jnp.zeros_like(carry_ref)

    is_sample = i == N_PROMPT_TILES
    x = jnp.where(is_sample, xs_ref[...], xp_ref[...])
    y = jnp.where(is_sample, ys_ref[...], yp_ref[...])
    o = jnp.where(is_sample, os_ref[...], op_ref[...])
    o = _rms(o.astype(F32), ag_ref[...]).astype(BF16)
    xm = x + _dot(y, wos_ref[...]) + _dot(o, woa_ref[...])
    xmid_ref[...] = xm
    h = _rms(xm, fg_ref[...])
    h_ref[...] = h
    h1 = h.astype(BF16)
    h2 = (h - h1.astype(F32)).astype(BF16)
    w1 = wr_ref[0]
    w2 = wr_ref[1]
    logits = _dot(h1, w1) + (_dot(h1, w2) + _dot(h2, w1)) + br_ref[...]
    eidx = lax.broadcasted_iota(jnp.int32, (TM, N_EXPERTS), 1)
    lane = lax.broadcasted_iota(jnp.int32, (TM, 128), 1)
    work = logits
    vals, sels = [], []
    idx_out = jnp.zeros((TM, 128), jnp.int32)
    for k in range(TOP_K):
        m = jnp.max(work, axis=-1, keepdims=True)
        idx = jnp.min(jnp.where(work == m, eidx, N_EXPERTS), axis=-1, keepdims=True)
        sel = eidx == idx
        vals.append(m)
        sels.append(sel)
        idx_out = jnp.where(lane == k, idx, idx_out)
        work = jnp.where(sel, -jnp.inf, work)
    es = [jnp.exp(v - vals[0]) for v in vals]
    tot = es[0] + es[1] + es[2] + es[3]
    gate_out = jnp.zeros((TM, 128), F32)
    for k in range(TOP_K):
        gate_out = jnp.where(lane == k, es[k] / tot, gate_out)
    idx_ref[...] = idx_out
    gate_ref[...] = gate_out
    multi = jnp.zeros((TM, N_EXPERTS), F32)
    for sel in sels:
        multi = jnp.where(sel, 1.0, multi)
    before = _dot(ltri_ref[...], multi.astype(BF16)) + carry_ref[...]
    rank_out = jnp.zeros((TM, 128), jnp.int32)
    for k in range(TOP_K):
        rk = jnp.sum(jnp.where(sels[k], before, 0.0), axis=-1, keepdims=True).astype(jnp.int32)
        rank_out = jnp.where(lane == k, rk, rank_out)
    rank_ref[...] = rank_out
    carry_ref[...] = carry_ref[...] + jnp.sum(multi, axis=0, keepdims=True)
    cnt_ref[...] = carry_ref[...]


def _outproj(xp, xs, yp, ys, op, os_, ag, wos, woa, fg, wr, br, ltri):
    tok = lambda n: pl.BlockSpec((TM, n), lambda i: (i, 0))
    prompt = lambda n: pl.BlockSpec((TM, n), lambda i: (jnp.minimum(i, N_PROMPT_TILES - 1), 0))
    sample = lambda n: pl.BlockSpec((TM, n), lambda i: (0, 0))
    return pl.pallas_call(
        _outproj_kernel,
        grid=(N_TILES,),
        in_specs=[
            prompt(D_MODEL), sample(D_MODEL), prompt(D_SSM), sample(D_SSM), prompt(D_ATT), sample(D_ATT),
            _const_spec((1, D_ATT)),
            _const_spec((D_SSM, D_MODEL)), _const_spec((D_ATT, D_MODEL)),
            _const_spec((1, D_MODEL)),
            _const_spec((2, D_MODEL, N_EXPERTS)), _const_spec((1, N_EXPERTS)),
            _const_spec((TM, TM)),
        ],
        out_specs=[tok(D_MODEL), tok(D_MODEL), tok(128), tok(128), tok(128),
                   _const_spec((1, N_EXPERTS))],
        out_shape=[
            jax.ShapeDtypeStruct((N_TOK, D_MODEL), F32),
            jax.ShapeDtypeStruct((N_TOK, D_MODEL), F32),
            jax.ShapeDtypeStruct((N_TOK, 128), jnp.int32),
            jax.ShapeDtypeStruct((N_TOK, 128), F32),
            jax.ShapeDtypeStruct((N_TOK, 128), jnp.int32),
            jax.ShapeDtypeStruct((1, N_EXPERTS), F32),
        ],
        scratch_shapes=[pltpu.VMEM((1, N_EXPERTS), F32)],
        compiler_params=_params(),
        name="outproj_router",
    )(xp, xs, yp, ys, op, os_, ag, wos, woa, fg, wr, br, ltri)


def _scatter_kernel(dest_ref, pend_ref, h_ref, rows_ref, zero_ref, sem, zsem):
    i = pl.program_id(0)

    @pl.when(i == 0)
    def _():
        zero_ref[...] = jnp.zeros_like(zero_ref)

        def last_block(e):
            end = pend_ref[e]
            start = pl.multiple_of(jnp.maximum(end - MOE_BM, 0), MOE_BM)
            nonempty = end > (pend_ref[e - 1] if e > 0 else 0)
            return nonempty, pltpu.make_async_copy(zero_ref, rows_ref.at[pl.ds(start, MOE_BM)], zsem)

        for e in range(N_EXPERTS):
            nonempty, cp = last_block(e)
            pl.when(nonempty)(cp.start)
        for e in range(N_EXPERTS):
            nonempty, cp = last_block(e)
            pl.when(nonempty)(cp.wait)

        def unused_block(b):
            start = pl.multiple_of(b * MOE_BM, MOE_BM)
            return pltpu.make_async_copy(zero_ref, rows_ref.at[pl.ds(start, MOE_BM)], zsem)

        first_unused = pend_ref[N_EXPERTS - 1] // MOE_BM
        lax.fori_loop(first_unused, MOE_BLOCKS, lambda b, c: (unused_block(b).start(), c)[1], 0)
        lax.fori_loop(first_unused, MOE_BLOCKS, lambda b, c: (unused_block(b).wait(), c)[1], 0)

    def issue(r, carry):
        for k in range(TOP_K):
            d = dest_ref[r * TOP_K + k]
            pltpu.make_async_copy(h_ref.at[pl.ds(r, 1)], rows_ref.at[pl.ds(d, 1)], sem).start(priority=k % 2)
        return carry

    lax.fori_loop(0, TM, issue, 0)
    for _ in range(TOP_K):
        pltpu.make_async_copy(h_ref, rows_ref.at[pl.ds(0, TM)], sem).wait()


def _scatter_rows(dest_flat, pad_end, h):
    return pl.pallas_call(
        _scatter_kernel,
        grid=(N_TILES,),
        in_specs=[
            pl.BlockSpec((TM * TOP_K,), lambda i: (i,), memory_space=pltpu.SMEM),
            pl.BlockSpec((N_EXPERTS,), lambda i: (0,), memory_space=pltpu.SMEM),
            pl.BlockSpec((TM, D_MODEL), lambda i: (i, 0)),
        ],
        out_specs=pl.BlockSpec(memory_space=pl.ANY),
        out_shape=jax.ShapeDtypeStruct((MOE_ROWS, D_MODEL), F32),
        scratch_shapes=[pltpu.VMEM((MOE_BM, D_MODEL), F32), pltpu.SemaphoreType.DMA(()),
                        pltpu.SemaphoreType.DMA(())],
        compiler_params=_params(),
        name="moe_scatter",
    )(dest_flat, pad_end, h)


def _expert_kernel(be_ref, nu_ref, x_ref, wgu_ref, bgu_ref, wd_ref, bd_ref, y_ref, wgu_s, wd_s):
    i = pl.program_id(0)
    prev = be_ref[jnp.maximum(i - 1, 0)]

    @pl.when((i == 0) | (be_ref[i] != prev))
    def _():
        wgu_s[...] = wgu_ref[0].astype(BF16)
        wd_s[...] = wd_ref[0].astype(BF16)

    @pl.when(i < nu_ref[0])
    def _():
        gu = _dot(x_ref[...].astype(BF16), wgu_s[...]) + bgu_ref[0]
        gate = jnp.minimum(gu[:, :D_FF], SWIGLU_LIMIT)
        up = jnp.clip(gu[:, D_FF:], -SWIGLU_LIMIT, SWIGLU_LIMIT)
        act = (up + 1.0) * gate * _sigmoid(gate * SWIGLU_ALPHA)
        y_ref[...] = _dot(act.astype(BF16), wd_s[...]) + bd_ref[0]

    @pl.when(i >= nu_ref[0])
    def _():
        y_ref[...] = jnp.zeros_like(y_ref)


def _experts(block_expert, n_used, rows, wgu, bgu, wd, bd):
    grid_spec = pltpu.PrefetchScalarGridSpec(
        num_scalar_prefetch=2,
        grid=(MOE_BLOCKS,),
        in_specs=[
            pl.BlockSpec((MOE_BM, D_MODEL), lambda i, be, nu: (jnp.minimum(i, nu[0] - 1), 0)),
            pl.BlockSpec((1, D_MODEL, 2 * D_FF), lambda i, be, nu: (be[i], 0, 0)),
            pl.BlockSpec((1, 1, 2 * D_FF), lambda i, be, nu: (be[i], 0, 0)),
            pl.BlockSpec((1, D_FF, D_MODEL), lambda i, be, nu: (be[i], 0, 0)),
            pl.BlockSpec((1, 1, D_MODEL), lambda i, be, nu: (be[i], 0, 0)),
        ],
        out_specs=pl.BlockSpec((MOE_BM, D_MODEL), lambda i, be, nu: (i, 0)),
        scratch_shapes=[pltpu.VMEM((D_MODEL, 2 * D_FF), BF16), pltpu.VMEM((D_FF, D_MODEL), BF16)],
    )
    return pl.pallas_call(
        _expert_kernel,
        grid_spec=grid_spec,
        out_shape=jax.ShapeDtypeStruct((MOE_ROWS, D_MODEL), F32),
        compiler_params=_params(),
        name="moe_experts",
    )(block_expert, n_used, rows, wgu, bgu, wd, bd)


def _combine_kernel(dest_ref, gate_ref, xmid_ref, g_ref, rows_ref, yp_ref, ys_ref, buf_ref, sem):
    i = pl.program_id(0)

    def issue(r, carry):
        for k in range(TOP_K):
            d = dest_ref[r * TOP_K + k]
            pltpu.make_async_copy(rows_ref.at[pl.ds(d, 1)], buf_ref.at[k, pl.ds(r, 1)], sem).start(priority=k % 2)
        return carry

    lax.fori_loop(0, TM, issue, 0)
    for k in range(TOP_K):
        pltpu.make_async_copy(rows_ref.at[pl.ds(0, TM)], buf_ref.at[k], sem).wait()
    acc = xmid_ref[...]
    for k in range(TOP_K):
        acc = acc + buf_ref[k] * gate_ref[:, k:k + 1]
    y = _rms(acc, g_ref[...])

    @pl.when(i < N_PROMPT_TILES)
    def _():
        yp_ref[...] = y

    @pl.when(i == N_PROMPT_TILES)
    def _():
        ys_ref[...] = y


def _combine(dest_flat, gates, xmid, g, y_rows):
    return pl.pallas_call(
        _combine_kernel,
        grid=(N_TILES,),
        in_specs=[
            pl.BlockSpec((TM * TOP_K,), lambda i: (i,), memory_space=pltpu.SMEM),
            pl.BlockSpec((TM, 128), lambda i: (i, 0)),
            pl.BlockSpec((TM, D_MODEL), lambda i: (i, 0)),
            _const_spec((1, D_MODEL)),
            pl.BlockSpec(memory_space=pl.ANY),
        ],
        out_specs=[
            pl.BlockSpec((TM, D_MODEL), lambda i: (jnp.minimum(i, N_PROMPT_TILES - 1), 0)),
            pl.BlockSpec((TM, D_MODEL), lambda i: (0, 0)),
        ],
        out_shape=[
            jax.ShapeDtypeStruct((N_PROMPT, D_MODEL), F32),
            jax.ShapeDtypeStruct((N_SAMPLE, D_MODEL), F32),
        ],
        scratch_shapes=[pltpu.VMEM((TOP_K, TM, D_MODEL), F32), pltpu.SemaphoreType.DMA(())],
        compiler_params=_params(),
        name="moe_combine",
    )(dest_flat, gates, xmid, g, y_rows)


def _pair_rows(v):
    return jnp.repeat(v.reshape(HEAD_PAIRS, 2), CHUNK, axis=1)


def _layer(l, xp, xs, cache_k, cache_v, state_conv, state_ssm,
           norm_mix_g, w_in, conv_w, conv_b, dt_bias, a_log, d_skip, ssm_norm_g,
           att_norm_g, rel_bias_table, w_out, norm_ffn_g, w_router, b_router,
           w_gate_up, b_gate_up, w_down, b_down, norm_final_g):
    wb = w_in[l].astype(BF16)
    c0 = D_SSM
    c1 = c0 + D_CONV
    c2 = c1 + SSM_HEADS
    c3 = c2 + D_ATT
    c4 = c3 + D_ATT
    z, xbc, dt, dtt, q, k, v, kf, vf, ctail = _inproj(
        xp, xs, norm_mix_g[l][None], wb[:, :c0], wb[:, c0:c1], wb[:, c1:c2], wb[:, c1:c2].T,
        wb[:, c2:c3], wb[:, c3:c4], wb[:, c4:])

    n_chunks = N_TOK // CHUNK
    dtp = dtt.reshape(HEAD_PAIRS, 2, n_chunks, CHUNK).transpose(2, 0, 1, 3).reshape(n_chunks, HEAD_PAIRS, 128)
    hp = jnp.arange(D_SSM) // SSM_HEAD_DIM
    expand = (hp[None, :] == jnp.arange(SSM_HEADS)[:, None]).astype(BF16)
    tril = jnp.tril(jnp.ones((CHUNK, CHUNK), BF16))
    r128 = jnp.arange(128)
    triu2 = ((r128[:, None] // CHUNK == r128[None, :] // CHUNK) & (r128[:, None] <= r128[None, :])).astype(BF16)
    consts = (conv_w[l], conv_b[l][None], dt_bias[l][None], _pair_rows(dt_bias[l]),
              jnp.repeat(a_log[l], SSM_HEAD_DIM)[None], _pair_rows(a_log[l]),
              jnp.repeat(d_skip[l], SSM_HEAD_DIM)[None], ssm_norm_g[l][None],
              expand, tril, triu2)
    y_ssm_p, ssm_p = _ssd_prompt(z, xbc, dt, dtp, consts)
    y_ssm_s, ssm_s = _ssd_sample(z, xbc, dt, dtp, state_conv[l],
                                 state_ssm[l].reshape(DEC_BATCH, D_SSM, D_STATE), consts)

    rel = ATT_LEFT + jnp.arange(CHUNK)[:, None] - jnp.arange(BAND)[None, :]
    bias = rel_bias_table[l][:, jnp.clip(rel, -REL_CLIP, REL_CLIP) + REL_CLIP]
    bias2 = bias.reshape(ATT_PAIRS, 2 * CHUNK, BAND)
    o_att_p = _attn_prompt(q, k, v, bias2)
    o_att_s = _attn_sample(q, k, v, cache_k[l].reshape(DEC_BATCH, ATT_LEFT, D_ATT),
                           cache_v[l].reshape(DEC_BATCH, ATT_LEFT, D_ATT), bias2)

    wo = w_out[l].astype(BF16)
    wr = w_router[l]
    wr1 = wr.astype(BF16)
    wr2 = (wr - wr1.astype(F32)).astype(BF16)
    ltri = jnp.tril(jnp.ones((TM, TM), BF16), -1)
    xmid, h, top_idx, gates, rank, counts = _outproj(
        xp, xs, y_ssm_p, y_ssm_s, o_att_p, o_att_s, att_norm_g[l][None], wo[:D_SSM], wo[D_SSM:], norm_ffn_g[l][None],
        jnp.stack([wr1, wr2]), b_router[l][None], ltri)

    counts = counts[0].astype(jnp.int32)
    padded = (counts + MOE_BM - 1) // MOE_BM * MOE_BM
    pad_end = jnp.cumsum(padded)
    pad_start = pad_end - padded
    experts = jnp.arange(N_EXPERTS, dtype=jnp.int32)
    start_of = jnp.sum(jnp.where(top_idx[:, :TOP_K, None] == experts, pad_start, 0), axis=-1)
    dest = (start_of + rank[:, :TOP_K]).reshape(-1).astype(jnp.int32)
    block_start = jnp.arange(MOE_BLOCKS, dtype=jnp.int32) * MOE_BM
    block_expert = jnp.minimum(jnp.sum((pad_end[None, :] <= block_start[:, None]).astype(jnp.int32), axis=1),
                               N_EXPERTS - 1).astype(jnp.int32)
    n_used = (pad_end[-1:] // MOE_BM).astype(jnp.int32)

    rows = _scatter_rows(dest, pad_end.astype(jnp.int32), h)
    y_rows = _experts(block_expert, n_used, rows, w_gate_up[l], b_gate_up[l][:, None, :],
                      w_down[l], b_down[l][:, None, :])
    y_p, y_s = _combine(dest, gates, xmid, norm_final_g[None], y_rows)

    keep = min(ATT_LEFT, SEQ)
    k_p = kf[:BATCH, TM - keep:].reshape(BATCH, keep, ATT_HEADS, ATT_HEAD_DIM)
    v_p = vf[:BATCH, TM - keep:].reshape(BATCH, keep, ATT_HEADS, ATT_HEAD_DIM)
    k_s = kf[BATCH].reshape(DEC_BATCH, DEC_SEQ, ATT_HEADS, ATT_HEAD_DIM)
    v_s = vf[BATCH].reshape(DEC_BATCH, DEC_SEQ, ATT_HEADS, ATT_HEAD_DIM)
    conv_p = ctail[:BATCH, -(CONV_W - 1):]
    conv_s = ctail[BATCH].reshape(DEC_BATCH, 8, D_CONV)[:, -(CONV_W - 1):]
    ssm_p = ssm_p.reshape(BATCH, SSM_HEADS, SSM_HEAD_DIM, D_STATE)
    ssm_s = ssm_s.reshape(DEC_BATCH, SSM_HEADS, SSM_HEAD_DIM, D_STATE)
    return (y_p.reshape(BATCH, SEQ, D_MODEL), y_s.reshape(DEC_BATCH, DEC_SEQ, D_MODEL),
            k_p, v_p, conv_p, ssm_p, k_s, v_s, conv_s, ssm_s)


def kernel(x_prompt, x_sample, cache_k, cache_v, state_conv, state_ssm, norm_mix_g, w_in, conv_w, conv_b,
           dt_bias, a_log, d_skip, ssm_norm_g, att_norm_g, rel_bias_table, w_out, norm_ffn_g, w_router,
           b_router, w_gate_up, b_gate_up, w_down, b_down, norm_final_g):
    assert w_in.shape[0] == 1, "single trunk layer"
    xp = x_prompt.reshape(N_PROMPT, D_MODEL)
    xs = x_sample.reshape(N_SAMPLE, D_MODEL)
    outs = _layer(0, xp, xs, cache_k, cache_v, state_conv, state_ssm,
                  norm_mix_g, w_in, conv_w, conv_b, dt_bias, a_log, d_skip, ssm_norm_g,
                  att_norm_g, rel_bias_table, w_out, norm_ffn_g, w_router, b_router,
                  w_gate_up, b_gate_up, w_down, b_down, norm_final_g)
    y_p, y_s, k_p, v_p, conv_p, ssm_p, k_s, v_s, conv_s, ssm_s = outs
    return (y_p, y_s, k_p[None], v_p[None], conv_p[None], ssm_p[None],
            k_s[None], v_s[None], conv_s[None], ssm_s[None])
```

```python
import jax
import jax.numpy as jnp
import numpy as np
from jax import lax
from jax.experimental import pallas as pl
from jax.experimental.pallas import tpu as pltpu

D_MODEL = 1024
BATCH = 8
SEQ = 2048
DEC_BATCH = 8
DEC_SEQ = 64
CHUNK = 64
SSM_HEADS = 16
SSM_HEAD_DIM = 64
D_SSM = SSM_HEADS * SSM_HEAD_DIM
SSM_GROUPS = 2
D_STATE = 128
CONV_W = 4
D_BC = SSM_GROUPS * D_STATE
D_CONV = D_SSM + 2 * D_BC
ATT_HEADS = 8
ATT_HEAD_DIM = 64
D_ATT = ATT_HEADS * ATT_HEAD_DIM
LEFT_CHUNKS = 8
ATT_LEFT = LEFT_CHUNKS * CHUNK
BAND = ATT_LEFT + CHUNK
REL_CLIP = 128
ATT_SCALE = ATT_HEAD_DIM ** -0.5
N_EXPERTS = 32
TOP_K = 4
D_FF = D_MODEL
SWIGLU_ALPHA = 1.702
SWIGLU_LIMIT = 7.0
EPS = 1e-5

F32 = jnp.float32
BF16 = jnp.bfloat16

N_PROMPT = BATCH * SEQ
N_SAMPLE = DEC_BATCH * DEC_SEQ
N_TOK = N_PROMPT + N_SAMPLE
TM = 512
N_PROMPT_TILES = N_PROMPT // TM
N_TILES = N_TOK // TM
TILES_PER_SEQ = SEQ // TM
CHUNKS_PER_TILE = TM // CHUNK
N_CHUNKS_SEQ = SEQ // CHUNK
HEAD_PAIRS = SSM_HEADS // 2
PAIRS_PER_GROUP = HEAD_PAIRS // SSM_GROUPS
ATT_PAIRS = ATT_HEADS // 2
CONV_HEAD = 8
MOE_BM = 256
N_ASSIGN = N_TOK * TOP_K
MOE_BLOCKS = N_ASSIGN // MOE_BM + N_EXPERTS
MOE_ROWS = MOE_BLOCKS * MOE_BM
VMEM_LIMIT = 56 * 1024 * 1024


def _dot(a, b):
    return jnp.dot(a, b, preferred_element_type=F32)


def _dot_nt(a, b):
    return lax.dot_general(a, b, (((1,), (1,)), ((), ())), preferred_element_type=F32)


def _dot_tn(a, b):
    return lax.dot_general(a, b, (((0,), (0,)), ((), ())), preferred_element_type=F32)


def _split3(x):
    x1 = x.astype(BF16)
    r1 = x - x1.astype(F32)
    x2 = r1.astype(BF16)
    r2 = r1 - x2.astype(F32)
    return x1, x2, r2.astype(BF16)


def _dot_exact_rhs(x, m):
    x1, x2, x3 = _split3(x)
    return _dot(x1, m) + _dot(x2, m) + _dot(x3, m)


def _dot_exact_lhs(m, x):
    x1, x2, x3 = _split3(x)
    return _dot(m, x1) + _dot(m, x2) + _dot(m, x3)


def _rms(x, g):
    return x * lax.rsqrt(jnp.mean(x * x, axis=-1, keepdims=True) + EPS) * g


def _sigmoid(x):
    return 1.0 / (1.0 + jnp.exp(-x))


def _softplus(x):
    return jnp.maximum(x, 0.0) + jnp.log(1.0 + jnp.exp(-jnp.abs(x)))


def _const_spec(shape):
    nd = len(shape)
    return pl.BlockSpec(shape, lambda *_: (0,) * nd)


def _params(n_axes=1):
    return pltpu.CompilerParams(dimension_semantics=("arbitrary",) * n_axes,
                                vmem_limit_bytes=VMEM_LIMIT)


def _inproj_kernel(xp_ref, xs_ref, g_ref, wz_ref, wxbc_ref, wdt_ref, wdtt_ref, wq_ref, wk_ref, wv_ref,
                   z_ref, xbc_ref, dt_ref, dtt_ref, q_ref, k_ref, v_ref, kf_ref, vf_ref, ctail_ref):
    i = pl.program_id(0)
    x = jnp.where(i == N_PROMPT_TILES, xs_ref[...], xp_ref[...])
    h = _rms(x, g_ref[...]).astype(BF16)
    z_ref[...] = _dot(h, wz_ref[...]).astype(BF16)
    xbc = _dot(h, wxbc_ref[...])
    xbc_ref[...] = xbc.astype(BF16)
    for c in range(CHUNKS_PER_TILE):
        ctail_ref[0, c * 8:(c + 1) * 8, :] = xbc[c * CHUNK + CHUNK - 8:(c + 1) * CHUNK, :]
    dt_ref[...] = _dot(h, wdt_ref[...])
    dtt_ref[...] = _dot_nt(wdtt_ref[...], h)
    q_ref[...] = (_dot(h, wq_ref[...]) * ATT_SCALE).astype(BF16)
    k = _dot(h, wk_ref[...])
    v = _dot(h, wv_ref[...])
    k_ref[...] = k.astype(BF16)
    v_ref[...] = v.astype(BF16)
    kf_ref[0] = k
    vf_ref[0] = v


def _inproj(xp, xs, g, wz, wxbc, wdt, wdtt, wq, wk, wv):
    tok = lambda n: pl.BlockSpec((TM, n), lambda i: (i, 0))
    tail_idx = lambda i: (i // TILES_PER_SEQ, 0, 0)
    n_tail = BATCH + 1
    return pl.pallas_call(
        _inproj_kernel,
        grid=(N_TILES,),
        in_specs=[
            pl.BlockSpec((TM, D_MODEL), lambda i: (jnp.minimum(i, N_PROMPT_TILES - 1), 0)),
            pl.BlockSpec((TM, D_MODEL), lambda i: (0, 0)),
            _const_spec((1, D_MODEL)),
            _const_spec((D_MODEL, D_SSM)),
            _const_spec((D_MODEL, D_CONV)),
            _const_spec((D_MODEL, SSM_HEADS)),
            _const_spec((SSM_HEADS, D_MODEL)),
            _const_spec((D_MODEL, D_ATT)),
            _const_spec((D_MODEL, D_ATT)),
            _const_spec((D_MODEL, D_ATT)),
        ],
        out_specs=[
            tok(D_SSM), tok(D_CONV), tok(SSM_HEADS),
            pl.BlockSpec((SSM_HEADS, TM), lambda i: (0, i)),
            tok(D_ATT), tok(D_ATT), tok(D_ATT),
            pl.BlockSpec((1, TM, D_ATT), tail_idx),
            pl.BlockSpec((1, TM, D_ATT), tail_idx),
            pl.BlockSpec((1, CHUNKS_PER_TILE * 8, D_CONV), tail_idx),
        ],
        out_shape=[
            jax.ShapeDtypeStruct((N_TOK, D_SSM), BF16),
            jax.ShapeDtypeStruct((N_TOK, D_CONV), BF16),
            jax.ShapeDtypeStruct((N_TOK, SSM_HEADS), F32),
            jax.ShapeDtypeStruct((SSM_HEADS, N_TOK), F32),
            jax.ShapeDtypeStruct((N_TOK, D_ATT), BF16),
            jax.ShapeDtypeStruct((N_TOK, D_ATT), BF16),
            jax.ShapeDtypeStruct((N_TOK, D_ATT), BF16),
            jax.ShapeDtypeStruct((n_tail, TM, D_ATT), F32),
            jax.ShapeDtypeStruct((n_tail, TM, D_ATT), F32),
            jax.ShapeDtypeStruct((n_tail, CHUNKS_PER_TILE * 8, D_CONV), F32),
        ],
        compiler_params=_params(),
        name="inproj",
    )(xp, xs, g, wz, wxbc, wdt, wdtt, wq, wk, wv)


def _ssd_tile(n_chunks, z_ref, xbc_ref, dt_ref, dtp_ref, cw_ref, cb_ref, dtb_ref, dtbp_ref,
              alog_e_ref, alog_p_ref, dskip_e_ref, ng_ref, expand_ref, tril_ref, triu2_ref,
              y_ref, xw_ref, state_ref):
    rows = n_chunks * CHUNK
    xw_ref[CONV_HEAD:CONV_HEAD + rows, :] = xbc_ref[...].astype(F32)

    a_e = -jnp.exp(alog_e_ref[...])
    a_p = -jnp.exp(alog_p_ref[...])
    row_i = lax.broadcasted_iota(jnp.int32, (CHUNK, 128), 0)
    col_i = lax.broadcasted_iota(jnp.int32, (CHUNK, 128), 1)
    causal2 = row_i >= (col_i % CHUNK)
    bd_r = lax.broadcasted_iota(jnp.int32, (128, 128), 0) // CHUNK
    bd_c = lax.broadcasted_iota(jnp.int32, (128, 128), 1) // CHUNK
    blockdiag = bd_r == bd_c

    def chunk(c, carry):
        r0 = pl.multiple_of(c * CHUNK, CHUNK)
        win = xw_ref.at[pl.ds(r0, CONV_HEAD + CHUNK), :]
        acc = cb_ref[...]
        for tap in range(CONV_W):
            off = CONV_HEAD - (CONV_W - 1) + tap
            acc = acc + cw_ref[tap:tap + 1, :] * win[off:off + CHUNK, :]
        xa = acc * _sigmoid(acc)
        xs = xa[:, 0:D_SSM]
        bm = xa[:, D_SSM:D_SSM + D_BC].astype(BF16)
        cm = xa[:, D_SSM + D_BC:D_CONV].astype(BF16)
        dt = _softplus(dt_ref[pl.ds(r0, CHUNK), :] + dtb_ref[...])
        dt_e = _dot_exact_rhs(dt, expand_ref[...])
        acum = _dot_exact_lhs(tril_ref[...], dt_e * a_e)
        dtp = _softplus(dtp_ref[c] + dtbp_ref[...])
        acum_p = _dot_exact_rhs(dtp * a_p, triu2_ref[...])
        xdt = xs * dt_e
        a_last = acum[CHUNK - 1:CHUNK, :]
        xdt_end = (xdt * jnp.exp(a_last - acum)).astype(BF16)
        y_parts = []
        for g in range(SSM_GROUPS):
            bg = bm[:, g * D_STATE:(g + 1) * D_STATE]
            cg = cm[:, g * D_STATE:(g + 1) * D_STATE]
            cb2 = _dot_nt(cg, jnp.concatenate([bg, bg], axis=0))
            for jj in range(PAIRS_PER_GROUP):
                j = g * PAIRS_PER_GROUP + jj
                seg = acum[:, j * 128:(j + 1) * 128] - acum_p[j:j + 1, :]
                decay = jnp.exp(jnp.where(causal2, seg, -jnp.inf))
                s_pair = (cb2 * decay).astype(BF16)
                x2 = xdt[:, j * 128:(j + 1) * 128]
                rhs = jnp.where(blockdiag, jnp.concatenate([x2, x2], axis=0), 0.0).astype(BF16)
                y_parts.append(_dot(s_pair, rhs))
        y_diag = jnp.concatenate(y_parts, axis=1)
        half = D_SSM // SSM_GROUPS
        y_off = jnp.concatenate(
            [_dot(cm[:, g * D_STATE:(g + 1) * D_STATE], state_ref[:, g * half:(g + 1) * half].astype(BF16))
             for g in range(SSM_GROUPS)], axis=1)
        new_s = jnp.concatenate(
            [_dot_tn(bm[:, g * D_STATE:(g + 1) * D_STATE], xdt_end[:, g * half:(g + 1) * half])
             for g in range(SSM_GROUPS)], axis=1)
        state_ref[...] = state_ref[...] * jnp.exp(a_last) + new_s
        y = y_diag + y_off * jnp.exp(acum) + dskip_e_ref[...] * xs
        zc = z_ref[pl.ds(r0, CHUNK), :].astype(F32)
        y = y * (zc * _sigmoid(zc))
        yn = jnp.concatenate(
            [y[:, g * half:(g + 1) * half]
             * lax.rsqrt(jnp.mean(jnp.square(y[:, g * half:(g + 1) * half]), axis=-1, keepdims=True) + EPS)
             for g in range(SSM_GROUPS)], axis=1)
        y_ref[pl.ds(r0, CHUNK), :] = (yn * ng_ref[...]).astype(BF16)
        return carry

    lax.fori_loop(0, n_chunks, chunk, 0)


def _state_store(state_ref, out_ref):
    for j in range(HEAD_PAIRS):
        out_ref[0, j * 128:(j + 1) * 128, :] = state_ref[:, j * 128:(j + 1) * 128].T


def _ssd_prompt_kernel(z_ref, xbc_ref, dt_ref, dtp_ref, cw_ref, cb_ref, dtb_ref, dtbp_ref,
                       alog_e_ref, alog_p_ref, dskip_e_ref, ng_ref, expand_ref, tril_ref, triu2_ref,
                       y_ref, ssm_ref, xw_ref, state_ref, tail_ref):
    t = pl.program_id(1)

    @pl.when(t == 0)
    def _():
        state_ref[...] = jnp.zeros_like(state_ref)
        xw_ref[0:CONV_HEAD, :] = jnp.zeros((CONV_HEAD, D_CONV), F32)

    @pl.when(t > 0)
    def _():
        xw_ref[0:CONV_HEAD, :] = tail_ref[...]

    _ssd_tile(CHUNKS_PER_TILE, z_ref, xbc_ref, dt_ref, dtp_ref, cw_ref, cb_ref, dtb_ref, dtbp_ref,
              alog_e_ref, alog_p_ref, dskip_e_ref, ng_ref, expand_ref, tril_ref, triu2_ref,
              y_ref, xw_ref, state_ref)
    tail_ref[...] = xw_ref[TM:TM + CONV_HEAD, :]

    @pl.when(t == TILES_PER_SEQ - 1)
    def _():
        _state_store(state_ref, ssm_ref)


def _ssd_sample_kernel(z_ref, xbc_ref, dt_ref, dtp_ref, cprev_ref, sprev_ref,
                       cw_ref, cb_ref, dtb_ref, dtbp_ref,
                       alog_e_ref, alog_p_ref, dskip_e_ref, ng_ref, expand_ref, tril_ref, triu2_ref,
                       y_ref, ssm_ref, xw_ref, state_ref):
    xw_ref[0:CONV_HEAD, :] = jnp.zeros((CONV_HEAD, D_CONV), F32)
    xw_ref[CONV_HEAD - (CONV_W - 1):CONV_HEAD, :] = cprev_ref[0]
    for j in range(HEAD_PAIRS):
        state_ref[:, j * 128:(j + 1) * 128] = sprev_ref[0, j * 128:(j + 1) * 128, :].T
    _ssd_tile(1, z_ref, xbc_ref, dt_ref, dtp_ref, cw_ref, cb_ref, dtb_ref, dtbp_ref,
              alog_e_ref, alog_p_ref, dskip_e_ref, ng_ref, expand_ref, tril_ref, triu2_ref,
              y_ref, xw_ref, state_ref)
    _state_store(state_ref, ssm_ref)


def _ssd_const_specs():
    return [
        _const_spec((CONV_W, D_CONV)), _const_spec((1, D_CONV)),
        _const_spec((1, SSM_HEADS)), _const_spec((HEAD_PAIRS, 128)),
        _const_spec((1, D_SSM)), _const_spec((HEAD_PAIRS, 128)),
        _const_spec((1, D_SSM)), _const_spec((1, D_SSM)),
        _const_spec((SSM_HEADS, D_SSM)), _const_spec((CHUNK, CHUNK)), _const_spec((128, 128)),
    ]


def _ssd_prompt(z, xbc, dt, dtp, consts):
    tile = lambda b, t: (b * TILES_PER_SEQ + t, 0)
    return pl.pallas_call(
        _ssd_prompt_kernel,
        grid=(BATCH, TILES_PER_SEQ),
        in_specs=[
            pl.BlockSpec((TM, D_SSM), tile),
            pl.BlockSpec((TM, D_CONV), tile),
            pl.BlockSpec((TM, SSM_HEADS), tile),
            pl.BlockSpec((CHUNKS_PER_TILE, HEAD_PAIRS, 128), lambda b, t: (b * TILES_PER_SEQ + t, 0, 0)),
        ] + _ssd_const_specs(),
        out_specs=[
            pl.BlockSpec((TM, D_SSM), tile),
            pl.BlockSpec((1, D_SSM, D_STATE), lambda b, t: (b, 0, 0)),
        ],
        out_shape=[
            jax.ShapeDtypeStruct((N_PROMPT, D_SSM), BF16),
            jax.ShapeDtypeStruct((BATCH, D_SSM, D_STATE), F32),
        ],
        scratch_shapes=[
            pltpu.VMEM((CONV_HEAD + TM, D_CONV), F32),
            pltpu.VMEM((D_STATE, D_SSM), F32),
            pltpu.VMEM((CONV_HEAD, D_CONV), F32),
        ],
        compiler_params=_params(2),
        name="ssd_prompt",
    )(z, xbc, dt, dtp, *consts)


def _ssd_sample(z, xbc, dt, dtp, conv_prev, ssm_prev, consts):
    first = N_PROMPT // CHUNK
    row = lambda b: (first + b, 0)
    return pl.pallas_call(
        _ssd_sample_kernel,
        grid=(DEC_BATCH,),
        in_specs=[
            pl.BlockSpec((CHUNK, D_SSM), row),
            pl.BlockSpec((CHUNK, D_CONV), row),
            pl.BlockSpec((CHUNK, SSM_HEADS), row),
            pl.BlockSpec((1, HEAD_PAIRS, 128), lambda b: (first + b, 0, 0)),
            pl.BlockSpec((1, CONV_W - 1, D_CONV), lambda b: (b, 0, 0)),
            pl.BlockSpec((1, D_SSM, D_STATE), lambda b: (b, 0, 0)),
        ] + _ssd_const_specs(),
        out_specs=[
            pl.BlockSpec((CHUNK, D_SSM), lambda b: (b, 0)),
            pl.BlockSpec((1, D_SSM, D_STATE), lambda b: (b, 0, 0)),
        ],
        out_shape=[
            jax.ShapeDtypeStruct((N_SAMPLE, D_SSM), BF16),
            jax.ShapeDtypeStruct((DEC_BATCH, D_SSM, D_STATE), F32),
        ],
        scratch_shapes=[
            pltpu.VMEM((CONV_HEAD + CHUNK, D_CONV), F32),
            pltpu.VMEM((D_STATE, D_SSM), F32),
        ],
        compiler_params=_params(),
        name="ssd_sample",
    )(z, xbc, dt, dtp, conv_prev, ssm_prev, *consts)


def _attn_chunks(n_chunks, first_chunk, q_ref, kpad_ref, vpad_ref, bias_ref, o_ref):
    lane = lax.broadcasted_iota(jnp.int32, (CHUNK, 128), 1)
    low = lane < ATT_HEAD_DIM
    kj = lax.broadcasted_iota(jnp.int32, (2 * CHUNK, BAND), 1)

    def chunk(c, carry):
        r0 = pl.multiple_of(c * CHUNK, CHUNK)
        first_valid = jnp.maximum(LEFT_CHUNKS - (first_chunk + c), 0) * CHUNK
        valid = kj >= first_valid
        outs = []
        for j in range(ATT_PAIRS):
            qp = q_ref[pl.ds(r0, CHUNK), j * 128:(j + 1) * 128]
            zero = jnp.zeros_like(qp)
            q2 = jnp.concatenate([jnp.where(low, qp, zero), jnp.where(low, zero, qp)], axis=0)
            kb = kpad_ref[pl.ds(r0, BAND), j * 128:(j + 1) * 128]
            vb = vpad_ref[pl.ds(r0, BAND), j * 128:(j + 1) * 128]
            s = _dot_nt(q2, kb) + bias_ref[j]
            s = jnp.where(valid, s, -jnp.inf)
            m = jnp.max(s, axis=-1, keepdims=True)
            e = jnp.exp(s - m)
            denom = jnp.sum(e, axis=-1, keepdims=True)
            r = _dot(e.astype(BF16), vb) / denom
            outs.append(jnp.where(low, r[0:CHUNK], r[CHUNK:2 * CHUNK]))
        o_ref[pl.ds(r0, CHUNK), :] = jnp.concatenate(outs, axis=1).astype(BF16)
        return carry

    lax.fori_loop(0, n_chunks, chunk, 0, unroll=2 if n_chunks % 2 == 0 else 1)


def _attn_prompt_kernel(q_ref, k_ref, v_ref, bias_ref, o_ref, kpad_ref, vpad_ref):
    kpad_ref[0:ATT_LEFT, :] = jnp.zeros((ATT_LEFT, D_ATT), BF16)
    vpad_ref[0:ATT_LEFT, :] = jnp.zeros((ATT_LEFT, D_ATT), BF16)
    kpad_ref[ATT_LEFT:ATT_LEFT + SEQ, :] = k_ref[...]
    vpad_ref[ATT_LEFT:ATT_LEFT + SEQ, :] = v_ref[...]
    _attn_chunks(N_CHUNKS_SEQ, 0, q_ref, kpad_ref, vpad_ref, bias_ref, o_ref)


def _attn_sample_kernel(q_ref, k_ref, v_ref, ck_ref, cv_ref, bias_ref, o_ref, kpad_ref, vpad_ref):
    kpad_ref[0:ATT_LEFT, :] = ck_ref[0].astype(BF16)
    vpad_ref[0:ATT_LEFT, :] = cv_ref[0].astype(BF16)
    kpad_ref[ATT_LEFT:BAND, :] = k_ref[...]
    vpad_ref[ATT_LEFT:BAND, :] = v_ref[...]
    _attn_chunks(1, LEFT_CHUNKS, q_ref, kpad_ref, vpad_ref, bias_ref, o_ref)


def _attn_prompt(q, k, v, bias2):
    seq = pl.BlockSpec((SEQ, D_ATT), lambda b: (b, 0))
    return pl.pallas_call(
        _attn_prompt_kernel,
        grid=(BATCH,),
        in_specs=[seq, seq, seq, _const_spec((ATT_PAIRS, 2 * CHUNK, BAND))],
        out_specs=seq,
        out_shape=jax.ShapeDtypeStruct((N_PROMPT, D_ATT), BF16),
        scratch_shapes=[pltpu.VMEM((ATT_LEFT + SEQ, D_ATT), BF16),
                        pltpu.VMEM((ATT_LEFT + SEQ, D_ATT), BF16)],
        compiler_params=_params(),
        name="attn_prompt",
    )(q, k, v, bias2)


def _attn_sample(q, k, v, cache_k, cache_v, bias2):
    first = N_PROMPT // CHUNK
    row = pl.BlockSpec((CHUNK, D_ATT), lambda b: (first + b, 0))
    cache = pl.BlockSpec((1, ATT_LEFT, D_ATT), lambda b: (b, 0, 0))
    return pl.pallas_call(
        _attn_sample_kernel,
        grid=(DEC_BATCH,),
        in_specs=[row, row, row, cache, cache, _const_spec((ATT_PAIRS, 2 * CHUNK, BAND))],
        out_specs=pl.BlockSpec((CHUNK, D_ATT), lambda b: (b, 0)),
        out_shape=jax.ShapeDtypeStruct((N_SAMPLE, D_ATT), BF16),
        scratch_shapes=[pltpu.VMEM((BAND, D_ATT), BF16), pltpu.VMEM((BAND, D_ATT), BF16)],
        compiler_params=_params(),
        name="attn_sample",
    )(q, k, v, cache_k, cache_v, bias2)


def _outproj_kernel(xp_ref, xs_ref, yp_ref, ys_ref, op_ref, os_ref, ag_ref, wos_ref, woa_ref, fg_ref,
                    wr_ref, br_ref, ltri_ref,
                    xmid_ref, h_ref, idx_ref, gate_ref, rank_ref, cnt_ref, carry_ref):
    i = pl.program_id(0)

    @pl.when(i == 0)
    def _():
        carry_ref[...] = jnp.zeros_like(carry_ref)

    is_sample = i == N_PROMPT_TILES
    x = jnp.where(is_sample, xs_ref[...], xp_ref[...])
    y = jnp.where(is_sample, ys_ref[...], yp_ref[...])
    o = jnp.where(is_sample, os_ref[...], op_ref[...])
    o = _rms(o.astype(F32), ag_ref[...]).astype(BF16)
    xm = x + _dot(y, wos_ref[...]) + _dot(o, woa_ref[...])
    xmid_ref[...] = xm
    h = _rms(xm, fg_ref[...])
    h_ref[...] = h
    h1 = h.astype(BF16)
    h2 = (h - h1.astype(F32)).astype(BF16)
    w1 = wr_ref[0]
    w2 = wr_ref[1]
    logits = _dot(h1, w1) + (_dot(h1, w2) + _dot(h2, w1)) + br_ref[...]
    eidx = lax.broadcasted_iota(jnp.int32, (TM, N_EXPERTS), 1)
    lane = lax.broadcasted_iota(jnp.int32, (TM, 128), 1)
    work = logits
    vals, sels = [], []
    idx_out = jnp.zeros((TM, 128), jnp.int32)
    for k in range(TOP_K):
        m = jnp.max(work, axis=-1, keepdims=True)
        idx = jnp.min(jnp.where(work == m, eidx, N_EXPERTS), axis=-1, keepdims=True)
        sel = eidx == idx
        vals.append(m)
        sels.append(sel)
        idx_out = jnp.where(lane == k, idx, idx_out)
        work = jnp.where(sel, -jnp.inf, work)
    es = [jnp.exp(v - vals[0]) for v in vals]
    tot = es[0] + es[1] + es[2] + es[3]
    gate_out = jnp.zeros((TM, 128), F32)
    for k in range(TOP_K):
        gate_out = jnp.where(lane == k, es[k] / tot, gate_out)
    idx_ref[...] = idx_out
    gate_ref[...] = gate_out
    multi = jnp.zeros((TM, N_EXPERTS), F32)
    for sel in sels:
        multi = jnp.where(sel, 1.0, multi)
    before = _dot(ltri_ref[...], multi.astype(BF16)) + carry_ref[...]
    rank_out = jnp.zeros((TM, 128), jnp.int32)
    for k in range(TOP_K):
        rk = jnp.sum(jnp.where(sels[k], before, 0.0), axis=-1, keepdims=True).astype(jnp.int32)
        rank_out = jnp.where(lane == k, rk, rank_out)
    rank_ref[...] = rank_out
    carry_ref[...] = carry_ref[...] + jnp.sum(multi, axis=0, keepdims=True)
    cnt_ref[...] = carry_ref[...]


def _outproj(xp, xs, yp, ys, op, os_, ag, wos, woa, fg, wr, br, ltri):
    tok = lambda n: pl.BlockSpec((TM, n), lambda i: (i, 0))
    prompt = lambda n: pl.BlockSpec((TM, n), lambda i: (jnp.minimum(i, N_PROMPT_TILES - 1), 0))
    sample = lambda n: pl.BlockSpec((TM, n), lambda i: (0, 0))
    return pl.pallas_call(
        _outproj_kernel,
        grid=(N_TILES,),
        in_specs=[
            prompt(D_MODEL), sample(D_MODEL), prompt(D_SSM), sample(D_SSM), prompt(D_ATT), sample(D_ATT),
            _const_spec((1, D_ATT)),
            _const_spec((D_SSM, D_MODEL)), _const_spec((D_ATT, D_MODEL)),
            _const_spec((1, D_MODEL)),
            _const_spec((2, D_MODEL, N_EXPERTS)), _const_spec((1, N_EXPERTS)),
            _const_spec((TM, TM)),
        ],
        out_specs=[tok(D_MODEL), tok(D_MODEL), tok(128), tok(128), tok(128),
                   _const_spec((1, N_EXPERTS))],
        out_shape=[
            jax.ShapeDtypeStruct((N_TOK, D_MODEL), F32),
            jax.ShapeDtypeStruct((N_TOK, D_MODEL), F32),
            jax.ShapeDtypeStruct((N_TOK, 128), jnp.int32),
            jax.ShapeDtypeStruct((N_TOK, 128), F32),
            jax.ShapeDtypeStruct((N_TOK, 128), jnp.int32),
            jax.ShapeDtypeStruct((1, N_EXPERTS), F32),
        ],
        scratch_shapes=[pltpu.VMEM((1, N_EXPERTS), F32)],
        compiler_params=_params(),
        name="outproj_router",
    )(xp, xs, yp, ys, op, os_, ag, wos, woa, fg, wr, br, ltri)


def _scatter_kernel(dest_ref, pend_ref, h_ref, rows_ref, zero_ref, sem, zsem):
    i = pl.program_id(0)

    @pl.when(i == 0)
    def _():
        zero_ref[...] = jnp.zeros_like(zero_ref)

        def last_block(e):
            end = pend_ref[e]
            start = pl.multiple_of(jnp.maximum(end - MOE_BM, 0), MOE_BM)
            nonempty = end > (pend_ref[e - 1] if e > 0 else 0)
            return nonempty, pltpu.make_async_copy(zero_ref, rows_ref.at[pl.ds(start, MOE_BM)], zsem)

        for e in range(N_EXPERTS):
            nonempty, cp = last_block(e)
            pl.when(nonempty)(cp.start)
        for e in range(N_EXPERTS):
            nonempty, cp = last_block(e)
            pl.when(nonempty)(cp.wait)

        def unused_block(b):
            start = pl.multiple_of(b * MOE_BM, MOE_BM)
            return pltpu.make_async_copy(zero_ref, rows_ref.at[pl.ds(start, MOE_BM)], zsem)

        first_unused = pend_ref[N_EXPERTS - 1] // MOE_BM
        lax.fori_loop(first_unused, MOE_BLOCKS, lambda b, c: (unused_block(b).start(), c)[1], 0)
        lax.fori_loop(first_unused, MOE_BLOCKS, lambda b, c: (unused_block(b).wait(), c)[1], 0)

    def issue(r, carry):
        for k in range(TOP_K):
            d = dest_ref[r * TOP_K + k]
            pltpu.make_async_copy(h_ref.at[pl.ds(r, 1)], rows_ref.at[pl.ds(d, 1)], sem).start(priority=k % 2)
        return carry

    lax.fori_loop(0, TM, issue, 0)
    for _ in range(TOP_K):
        pltpu.make_async_copy(h_ref, rows_ref.at[pl.ds(0, TM)], sem).wait()


def _scatter_rows(dest_flat, pad_end, h):
    return pl.pallas_call(
        _scatter_kernel,
        grid=(N_TILES,),
        in_specs=[
            pl.BlockSpec((TM * TOP_K,), lambda i: (i,), memory_space=pltpu.SMEM),
            pl.BlockSpec((N_EXPERTS,), lambda i: (0,), memory_space=pltpu.SMEM),
            pl.BlockSpec((TM, D_MODEL), lambda i: (i, 0)),
        ],
        out_specs=pl.BlockSpec(memory_space=pl.ANY),
        out_shape=jax.ShapeDtypeStruct((MOE_ROWS, D_MODEL), F32),
        scratch_shapes=[pltpu.VMEM((MOE_BM, D_MODEL), F32), pltpu.SemaphoreType.DMA(()),
                        pltpu.SemaphoreType.DMA(())],
        compiler_params=_params(),
        name="moe_scatter",
    )(dest_flat, pad_end, h)


def _expert_kernel(be_ref, nu_ref, x_ref, wgu_ref, bgu_ref, wd_ref, bd_ref, y_ref, wgu_s, wd_s):
    i = pl.program_id(0)
    prev = be_ref[jnp.maximum(i - 1, 0)]

    @pl.when((i == 0) | (be_ref[i] != prev))
    def _():
        wgu_s[...] = wgu_ref[0].astype(BF16)
        wd_s[...] = wd_ref[0].astype(BF16)

    @pl.when(i < nu_ref[0])
    def _():
        gu = _dot(x_ref[...].astype(BF16), wgu_s[...]) + bgu_ref[0]
        gate = jnp.minimum(gu[:, :D_FF], SWIGLU_LIMIT)
        up = jnp.clip(gu[:, D_FF:], -SWIGLU_LIMIT, SWIGLU_LIMIT)
        act = (up + 1.0) * gate * _sigmoid(gate * SWIGLU_ALPHA)
        y_ref[...] = _dot(act.astype(BF16), wd_s[...]) + bd_ref[0]

    @pl.when(i >= nu_ref[0])
    def _():
        y_ref[...] = jnp.zeros_like(y_ref)


def _experts(block_expert, n_used, rows, wgu, bgu, wd, bd):
    grid_spec = pltpu.PrefetchScalarGridSpec(
        num_scalar_prefetch=2,
        grid=(MOE_BLOCKS,),
        in_specs=[
            pl.BlockSpec((MOE_BM, D_MODEL), lambda i, be, nu: (jnp.minimum(i, nu[0] - 1), 0)),
            pl.BlockSpec((1, D_MODEL, 2 * D_FF), lambda i, be, nu: (be[i], 0, 0)),
            pl.BlockSpec((1, 1, 2 * D_FF), lambda i, be, nu: (be[i], 0, 0)),
            pl.BlockSpec((1, D_FF, D_MODEL), lambda i, be, nu: (be[i], 0, 0)),
            pl.BlockSpec((1, 1, D_MODEL), lambda i, be, nu: (be[i], 0, 0)),
        ],
        out_specs=pl.BlockSpec((MOE_BM, D_MODEL), lambda i, be, nu: (i, 0)),
        scratch_shapes=[pltpu.VMEM((D_MODEL, 2 * D_FF), BF16), pltpu.VMEM((D_FF, D_MODEL), BF16)],
    )
    return pl.pallas_call(
        _expert_kernel,
        grid_spec=grid_spec,
        out_shape=jax.ShapeDtypeStruct((MOE_ROWS, D_MODEL), F32),
        compiler_params=_params(),
        name="moe_experts",
    )(block_expert, n_used, rows, wgu, bgu, wd, bd)


def _combine_kernel(dest_ref, gate_ref, xmid_ref, g_ref, rows_ref, yp_ref, ys_ref, buf_ref, sem):
    i = pl.program_id(0)

    def issue(r, carry):
        for k in range(TOP_K):
            d = dest_ref[r * TOP_K + k]
            pltpu.make_async_copy(rows_ref.at[pl.ds(d, 1)], buf_ref.at[k, pl.ds(r, 1)], sem).start(priority=k % 2)
        return carry

    lax.fori_loop(0, TM, issue, 0)
    for k in range(TOP_K):
        pltpu.make_async_copy(rows_ref.at[pl.ds(0, TM)], buf_ref.at[k], sem).wait()
    acc = xmid_ref[...]
    for k in range(TOP_K):
        acc = acc + buf_ref[k] * gate_ref[:, k:k + 1]
    y = _rms(acc, g_ref[...])

    @pl.when(i < N_PROMPT_TILES)
    def _():
        yp_ref[...] = y

    @pl.when(i == N_PROMPT_TILES)
    def _():
        ys_ref[...] = y


def _combine(dest_flat, gates, xmid, g, y_rows):
    return pl.pallas_call(
        _combine_kernel,
        grid=(N_TILES,),
        in_specs=[
            pl.BlockSpec((TM * TOP_K,), lambda i: (i,), memory_space=pltpu.SMEM),
            pl.BlockSpec((TM, 128), lambda i: (i, 0)),
            pl.BlockSpec((TM, D_MODEL), lambda i: (i, 0)),
            _const_spec((1, D_MODEL)),
            pl.BlockSpec(memory_space=pl.ANY),
        ],
        out_specs=[
            pl.BlockSpec((TM, D_MODEL), lambda i: (jnp.minimum(i, N_PROMPT_TILES - 1), 0)),
            pl.BlockSpec((TM, D_MODEL), lambda i: (0, 0)),
        ],
        out_shape=[
            jax.ShapeDtypeStruct((N_PROMPT, D_MODEL), F32),
            jax.ShapeDtypeStruct((N_SAMPLE, D_MODEL), F32),
        ],
        scratch_shapes=[pltpu.VMEM((TOP_K, TM, D_MODEL), F32), pltpu.SemaphoreType.DMA(())],
        compiler_params=_params(),
        name="moe_combine",
    )(dest_flat, gates, xmid, g, y_rows)


def _band_bias(table):
    n_diag = BAND + CHUNK - 1
    idx = np.clip(ATT_LEFT + (CHUNK - 1) - np.arange(n_diag), -REL_CLIP, REL_CLIP) + REL_CLIP
    pick = (np.arange(2 * REL_CLIP + 1)[:, None] == idx[None, :]).astype(np.float32)
    diag = jnp.dot(table, jnp.asarray(pick), precision=lax.Precision.HIGHEST)
    return jnp.stack([diag[:, CHUNK - 1 - qi:CHUNK - 1 - qi + BAND] for qi in range(CHUNK)], axis=1)


def _pair_rows(v):
    return jnp.repeat(v.reshape(HEAD_PAIRS, 2), CHUNK, axis=1)


def _layer(l, xp, xs, cache_k, cache_v, state_conv, state_ssm,
           norm_mix_g, w_in, conv_w, conv_b, dt_bias, a_log, d_skip, ssm_norm_g,
           att_norm_g, rel_bias_table, w_out, norm_ffn_g, w_router, b_router,
           w_gate_up, b_gate_up, w_down, b_down, norm_final_g):
    wb = w_in[l].astype(BF16)
    c0 = D_SSM
    c1 = c0 + D_CONV
    c2 = c1 + SSM_HEADS
    c3 = c2 + D_ATT
    c4 = c3 + D_ATT
    z, xbc, dt, dtt, q, k, v, kf, vf, ctail = _inproj(
        xp, xs, norm_mix_g[l][None], wb[:, :c0], wb[:, c0:c1], wb[:, c1:c2], wb[:, c1:c2].T,
        wb[:, c2:c3], wb[:, c3:c4], wb[:, c4:])

    n_chunks = N_TOK // CHUNK
    dtp = dtt.reshape(HEAD_PAIRS, 2, n_chunks, CHUNK).transpose(2, 0, 1, 3).reshape(n_chunks, HEAD_PAIRS, 128)
    hp = jnp.arange(D_SSM) // SSM_HEAD_DIM
    expand = (hp[None, :] == jnp.arange(SSM_HEADS)[:, None]).astype(BF16)
    tril = jnp.tril(jnp.ones((CHUNK, CHUNK), BF16))
    r128 = jnp.arange(128)
    triu2 = ((r128[:, None] // CHUNK == r128[None, :] // CHUNK) & (r128[:, None] <= r128[None, :])).astype(BF16)
    consts = (conv_w[l], conv_b[l][None], dt_bias[l][None], _pair_rows(dt_bias[l]),
              jnp.repeat(a_log[l], SSM_HEAD_DIM)[None], _pair_rows(a_log[l]),
              jnp.repeat(d_skip[l], SSM_HEAD_DIM)[None], ssm_norm_g[l][None],
              expand, tril, triu2)
    y_ssm_p, ssm_p = _ssd_prompt(z, xbc, dt, dtp, consts)
    y_ssm_s, ssm_s = _ssd_sample(z, xbc, dt, dtp, state_conv[l],
                                 state_ssm[l].reshape(DEC_BATCH, D_SSM, D_STATE), consts)

    bias2 = _band_bias(rel_bias_table[l]).reshape(ATT_PAIRS, 2 * CHUNK, BAND)
    o_att_p = _attn_prompt(q, k, v, bias2)
    o_att_s = _attn_sample(q, k, v, cache_k[l].reshape(DEC_BATCH, ATT_LEFT, D_ATT),
                           cache_v[l].reshape(DEC_BATCH, ATT_LEFT, D_ATT), bias2)

    wo = w_out[l].astype(BF16)
    wr = w_router[l]
    wr1 = wr.astype(BF16)
    wr2 = (wr - wr1.astype(F32)).astype(BF16)
    ltri = jnp.tril(jnp.ones((TM, TM), BF16), -1)
    xmid, h, top_idx, gates, rank, counts = _outproj(
        xp, xs, y_ssm_p, y_ssm_s, o_att_p, o_att_s, att_norm_g[l][None], wo[:D_SSM], wo[D_SSM:], norm_ffn_g[l][None],
        jnp.stack([wr1, wr2]), b_router[l][None], ltri)

    counts = counts[0].astype(jnp.int32)
    padded = (counts + MOE_BM - 1) // MOE_BM * MOE_BM
    pad_end = jnp.cumsum(padded)
    pad_start = pad_end - padded
    experts = jnp.arange(N_EXPERTS, dtype=jnp.int32)
    start_of = jnp.sum(jnp.where(top_idx[:, :TOP_K, None] == experts, pad_start, 0), axis=-1)
    dest = (start_of + rank[:, :TOP_K]).reshape(-1).astype(jnp.int32)
    block_start = jnp.arange(MOE_BLOCKS, dtype=jnp.int32) * MOE_BM
    block_expert = jnp.minimum(jnp.sum((pad_end[None, :] <= block_start[:, None]).astype(jnp.int32), axis=1),
                               N_EXPERTS - 1).astype(jnp.int32)
    n_used = (pad_end[-1:] // MOE_BM).astype(jnp.int32)

    rows = _scatter_rows(dest, pad_end.astype(jnp.int32), h)
    y_rows = _experts(block_expert, n_used, rows, w_gate_up[l], b_gate_up[l][:, None, :],
                      w_down[l], b_down[l][:, None, :])
    y_p, y_s = _combine(dest, gates, xmid, norm_final_g[None], y_rows)

    keep = min(ATT_LEFT, SEQ)
    k_p = kf[:BATCH, TM - keep:].reshape(BATCH, keep, ATT_HEADS, ATT_HEAD_DIM)
    v_p = vf[:BATCH, TM - keep:].reshape(BATCH, keep, ATT_HEADS, ATT_HEAD_DIM)
    k_s = kf[BATCH].reshape(DEC_BATCH, DEC_SEQ, ATT_HEADS, ATT_HEAD_DIM)
    v_s = vf[BATCH].reshape(DEC_BATCH, DEC_SEQ, ATT_HEADS, ATT_HEAD_DIM)
    conv_p = ctail[:BATCH, -(CONV_W - 1):]
    conv_s = ctail[BATCH].reshape(DEC_BATCH, 8, D_CONV)[:, -(CONV_W - 1):]
    ssm_p = ssm_p.reshape(BATCH, SSM_HEADS, SSM_HEAD_DIM, D_STATE)
    ssm_s = ssm_s.reshape(DEC_BATCH, SSM_HEADS, SSM_HEAD_DIM, D_STATE)
    return (y_p.reshape(BATCH, SEQ, D_MODEL), y_s.reshape(DEC_BATCH, DEC_SEQ, D_MODEL),
            k_p, v_p, conv_p, ssm_p, k_s, v_s, conv_s, ssm_s)


def kernel(x_prompt, x_sample, cache_k, cache_v, state_conv, state_ssm, norm_mix_g, w_in, conv_w, conv_b,
           dt_bias, a_log, d_skip, ssm_norm_g, att_norm_g, rel_bias_table, w_out, norm_ffn_g, w_router,
           b_router, w_gate_up, b_gate_up, w_down, b_down, norm_final_g):
    assert w_in.shape[0] == 1, "single trunk layer"
    xp = x_prompt.reshape(N_PROMPT, D_MODEL)
    xs = x_sample.reshape(N_SAMPLE, D_MODEL)
    outs = _layer(0, xp, xs, cache_k, cache_v, state_conv, state_ssm,
                  norm_mix_g, w_in, conv_w, conv_b, dt_bias, a_log, d_skip, ssm_norm_g,
                  att_norm_g, rel_bias_table, w_out, norm_ffn_g, w_router, b_router,
                  w_gate_up, b_gate_up, w_down, b_down, norm_final_g)
    y_p, y_s, k_p, v_p, conv_p, ssm_p, k_s, v_s, conv_s, ssm_s = outs
    return (y_p, y_s, k_p[None], v_p[None], conv_p[None], ssm_p[None],
            k_s[None], v_s[None], conv_s[None], ssm_s[None])
```

```python
import jax
import jax.numpy as jnp
import numpy as np
from jax import lax
from jax.experimental import pallas as pl
from jax.experimental.pallas import tpu as pltpu

D_MODEL = 1024
BATCH = 8
SEQ = 2048
DEC_BATCH = 8
DEC_SEQ = 64
CHUNK = 64
SSM_HEADS = 16
SSM_HEAD_DIM = 64
D_SSM = SSM_HEADS * SSM_HEAD_DIM
SSM_GROUPS = 2
D_STATE = 128
CONV_W = 4
D_BC = SSM_GROUPS * D_STATE
D_CONV = D_SSM + 2 * D_BC
ATT_HEADS = 8
ATT_HEAD_DIM = 64
D_ATT = ATT_HEADS * ATT_HEAD_DIM
LEFT_CHUNKS = 8
ATT_LEFT = LEFT_CHUNKS * CHUNK
BAND = ATT_LEFT + CHUNK
REL_CLIP = 128
ATT_SCALE = ATT_HEAD_DIM ** -0.5
N_EXPERTS = 32
TOP_K = 4
D_FF = D_MODEL
SWIGLU_ALPHA = 1.702
SWIGLU_LIMIT = 7.0
EPS = 1e-5

F32 = jnp.float32
BF16 = jnp.bfloat16

N_PROMPT = BATCH * SEQ
N_SAMPLE = DEC_BATCH * DEC_SEQ
N_TOK = N_PROMPT + N_SAMPLE
TM = 512
N_PROMPT_TILES = N_PROMPT // TM
N_TILES = N_TOK // TM
TILES_PER_SEQ = SEQ // TM
CHUNKS_PER_TILE = TM // CHUNK
N_CHUNKS_SEQ = SEQ // CHUNK
HEAD_PAIRS = SSM_HEADS // 2
PAIRS_PER_GROUP = HEAD_PAIRS // SSM_GROUPS
ATT_PAIRS = ATT_HEADS // 2
CONV_HEAD = 8
MOE_BM = 256
N_ASSIGN = N_TOK * TOP_K
MOE_BLOCKS = N_ASSIGN // MOE_BM + N_EXPERTS
MOE_ROWS = MOE_BLOCKS * MOE_BM
ROW_TILE = (8, 128)
assert ROW_TILE[0] * ROW_TILE[1] == D_MODEL
VMEM_LIMIT = 56 * 1024 * 1024


def _dot(a, b):
    return jnp.dot(a, b, preferred_element_type=F32)


def _dot_nt(a, b):
    return lax.dot_general(a, b, (((1,), (1,)), ((), ())), preferred_element_type=F32)


def _dot_tn(a, b):
    return lax.dot_general(a, b, (((0,), (0,)), ((), ())), preferred_element_type=F32)


def _split3(x):
    x1 = x.astype(BF16)
    r1 = x - x1.astype(F32)
    x2 = r1.astype(BF16)
    r2 = r1 - x2.astype(F32)
    return x1, x2, r2.astype(BF16)


def _dot_exact_rhs(x, m):
    x1, x2, x3 = _split3(x)
    return _dot(x1, m) + _dot(x2, m) + _dot(x3, m)


def _dot_exact_lhs(m, x):
    x1, x2, x3 = _split3(x)
    return _dot(m, x1) + _dot(m, x2) + _dot(m, x3)


def _rms(x, g):
    return x * lax.rsqrt(jnp.mean(x * x, axis=-1, keepdims=True) + EPS) * g


def _sigmoid(x):
    return 1.0 / (1.0 + jnp.exp(-x))


def _softplus(x):
    return jnp.maximum(x, 0.0) + jnp.log(1.0 + jnp.exp(-jnp.abs(x)))


def _tiled(n):
    return (n * ROW_TILE[0], ROW_TILE[1])


def _tile_of(ref, row):
    return ref.at[pl.ds(pl.multiple_of(row * ROW_TILE[0], ROW_TILE[0]), ROW_TILE[0])]


def _rows_to_tiles(ref, x, first=0):
    sub, lanes = ROW_TILE
    for j in range(sub):
        ref[pl.ds(first * sub + j, x.shape[0], stride=sub), :] = x[:, j * lanes:(j + 1) * lanes]


def _tiles_to_rows(ref, n, first=0):
    sub = ROW_TILE[0]
    return jnp.concatenate([ref[pl.ds(first * sub + j, n, stride=sub), :] for j in range(sub)], axis=1)


def _const_spec(shape):
    nd = len(shape)
    return pl.BlockSpec(shape, lambda *_: (0,) * nd)


def _params(n_axes=1):
    return pltpu.CompilerParams(dimension_semantics=("arbitrary",) * n_axes,
                                vmem_limit_bytes=VMEM_LIMIT)


def _inproj_kernel(xp_ref, xs_ref, g_ref, wz_ref, wxbc_ref, wdt_ref, wdtt_ref, wq_ref, wk_ref, wv_ref,
                   z_ref, xbc_ref, dt_ref, dtt_ref, q_ref, k_ref, v_ref, kf_ref, vf_ref, ctail_ref):
    i = pl.program_id(0)
    x = jnp.where(i == N_PROMPT_TILES, xs_ref[...], xp_ref[...])
    h = _rms(x, g_ref[...]).astype(BF16)
    z_ref[...] = _dot(h, wz_ref[...]).astype(BF16)
    xbc = _dot(h, wxbc_ref[...])
    xbc_ref[...] = xbc.astype(BF16)
    for c in range(CHUNKS_PER_TILE):
        ctail_ref[0, c * 8:(c + 1) * 8, :] = xbc[c * CHUNK + CHUNK - 8:(c + 1) * CHUNK, :]
    dt_ref[...] = _dot(h, wdt_ref[...])
    dtt_ref[...] = _dot_nt(wdtt_ref[...], h)
    q_ref[...] = (_dot(h, wq_ref[...]) * ATT_SCALE).astype(BF16)
    k = _dot(h, wk_ref[...])
    v = _dot(h, wv_ref[...])
    k_ref[...] = k.astype(BF16)
    v_ref[...] = v.astype(BF16)
    kf_ref[0] = k
    vf_ref[0] = v


def _inproj(xp, xs, g, wz, wxbc, wdt, wdtt, wq, wk, wv):
    tok = lambda n: pl.BlockSpec((TM, n), lambda i: (i, 0))
    tail_idx = lambda i: (i // TILES_PER_SEQ, 0, 0)
    n_tail = BATCH + 1
    return pl.pallas_call(
        _inproj_kernel,
        grid=(N_TILES,),
        in_specs=[
            pl.BlockSpec((TM, D_MODEL), lambda i: (jnp.minimum(i, N_PROMPT_TILES - 1), 0)),
            pl.BlockSpec((TM, D_MODEL), lambda i: (0, 0)),
            _const_spec((1, D_MODEL)),
            _const_spec((D_MODEL, D_SSM)),
            _const_spec((D_MODEL, D_CONV)),
            _const_spec((D_MODEL, SSM_HEADS)),
            _const_spec((SSM_HEADS, D_MODEL)),
            _const_spec((D_MODEL, D_ATT)),
            _const_spec((D_MODEL, D_ATT)),
            _const_spec((D_MODEL, D_ATT)),
        ],
        out_specs=[
            tok(D_SSM), tok(D_CONV), tok(SSM_HEADS),
            pl.BlockSpec((SSM_HEADS, TM), lambda i: (0, i)),
            tok(D_ATT), tok(D_ATT), tok(D_ATT),
            pl.BlockSpec((1, TM, D_ATT), tail_idx),
            pl.BlockSpec((1, TM, D_ATT), tail_idx),
            pl.BlockSpec((1, CHUNKS_PER_TILE * 8, D_CONV), tail_idx),
        ],
        out_shape=[
            jax.ShapeDtypeStruct((N_TOK, D_SSM), BF16),
            jax.ShapeDtypeStruct((N_TOK, D_CONV), BF16),
            jax.ShapeDtypeStruct((N_TOK, SSM_HEADS), F32),
            jax.ShapeDtypeStruct((SSM_HEADS, N_TOK), F32),
            jax.ShapeDtypeStruct((N_TOK, D_ATT), BF16),
            jax.ShapeDtypeStruct((N_TOK, D_ATT), BF16),
            jax.ShapeDtypeStruct((N_TOK, D_ATT), BF16),
            jax.ShapeDtypeStruct((n_tail, TM, D_ATT), F32),
            jax.ShapeDtypeStruct((n_tail, TM, D_ATT), F32),
            jax.ShapeDtypeStruct((n_tail, CHUNKS_PER_TILE * 8, D_CONV), F32),
        ],
        compiler_params=_params(),
        name="inproj",
    )(xp, xs, g, wz, wxbc, wdt, wdtt, wq, wk, wv)


def _ssd_tile(n_chunks, z_ref, xbc_ref, dt_ref, dtp_ref, cw_ref, cb_ref, dtb_ref, dtbp_ref,
              alog_e_ref, alog_p_ref, dskip_e_ref, ng_ref, expand_ref, tril_ref, triu2_ref,
              y_ref, xw_ref, state_ref):
    rows = n_chunks * CHUNK
    xw_ref[CONV_HEAD:CONV_HEAD + rows, :] = xbc_ref[...].astype(F32)

    a_e = -jnp.exp(alog_e_ref[...])
    a_p = -jnp.exp(alog_p_ref[...])
    row_i = lax.broadcasted_iota(jnp.int32, (CHUNK, 128), 0)
    col_i = lax.broadcasted_iota(jnp.int32, (CHUNK, 128), 1)
    causal2 = row_i >= (col_i % CHUNK)
    bd_r = lax.broadcasted_iota(jnp.int32, (128, 128), 0) // CHUNK
    bd_c = lax.broadcasted_iota(jnp.int32, (128, 128), 1) // CHUNK
    blockdiag = bd_r == bd_c

    def chunk(c, carry):
        r0 = pl.multiple_of(c * CHUNK, CHUNK)
        win = xw_ref.at[pl.ds(r0, CONV_HEAD + CHUNK), :]
        acc = cb_ref[...]
        for tap in range(CONV_W):
            off = CONV_HEAD - (CONV_W - 1) + tap
            acc = acc + cw_ref[tap:tap + 1, :] * win[off:off + CHUNK, :]
        xa = acc * _sigmoid(acc)
        xs = xa[:, 0:D_SSM]
        bm = xa[:, D_SSM:D_SSM + D_BC].astype(BF16)
        cm = xa[:, D_SSM + D_BC:D_CONV].astype(BF16)
        dt = _softplus(dt_ref[pl.ds(r0, CHUNK), :] + dtb_ref[...])
        dt_e = _dot_exact_rhs(dt, expand_ref[...])
        acum = _dot_exact_lhs(tril_ref[...], dt_e * a_e)
        dtp = _softplus(dtp_ref[c] + dtbp_ref[...])
        acum_p = _dot_exact_rhs(dtp * a_p, triu2_ref[...])
        xdt = xs * dt_e
        a_last = acum[CHUNK - 1:CHUNK, :]
        xdt_end = (xdt * jnp.exp(a_last - acum)).astype(BF16)
        y_parts = []
        for g in range(SSM_GROUPS):
            bg = bm[:, g * D_STATE:(g + 1) * D_STATE]
            cg = cm[:, g * D_STATE:(g + 1) * D_STATE]
            cb2 = _dot_nt(cg, jnp.concatenate([bg, bg], axis=0))
            for jj in range(PAIRS_PER_GROUP):
                j = g * PAIRS_PER_GROUP + jj
                seg = acum[:, j * 128:(j + 1) * 128] - acum_p[j:j + 1, :]
                decay = jnp.exp(jnp.where(causal2, seg, -jnp.inf))
                s_pair = (cb2 * decay).astype(BF16)
                x2 = xdt[:, j * 128:(j + 1) * 128]
                rhs = jnp.where(blockdiag, jnp.concatenate([x2, x2], axis=0), 0.0).astype(BF16)
                y_parts.append(_dot(s_pair, rhs))
        y_diag = jnp.concatenate(y_parts, axis=1)
        half = D_SSM // SSM_GROUPS
        y_off = jnp.concatenate(
            [_dot(cm[:, g * D_STATE:(g + 1) * D_STATE], state_ref[:, g * half:(g + 1) * half].astype(BF16))
             for g in range(SSM_GROUPS)], axis=1)
        new_s = jnp.concatenate(
            [_dot_tn(bm[:, g * D_STATE:(g + 1) * D_STATE], xdt_end[:, g * half:(g + 1) * half])
             for g in range(SSM_GROUPS)], axis=1)
        state_ref[...] = state_ref[...] * jnp.exp(a_last) + new_s
        y = y_diag + y_off * jnp.exp(acum) + dskip_e_ref[...] * xs
        zc = z_ref[pl.ds(r0, CHUNK), :].astype(F32)
        y = y * (zc * _sigmoid(zc))
        yn = jnp.concatenate(
            [y[:, g * half:(g + 1) * half]
             * lax.rsqrt(jnp.mean(jnp.square(y[:, g * half:(g + 1) * half]), axis=-1, keepdims=True) + EPS)
             for g in range(SSM_GROUPS)], axis=1)
        y_ref[pl.ds(r0, CHUNK), :] = (yn * ng_ref[...]).astype(BF16)
        return carry

    lax.fori_loop(0, n_chunks, chunk, 0)


def _state_store(state_ref, out_ref):
    for j in range(HEAD_PAIRS):
        out_ref[0, j * 128:(j + 1) * 128, :] = state_ref[:, j * 128:(j + 1) * 128].T


def _ssd_prompt_kernel(z_ref, xbc_ref, dt_ref, dtp_ref, cw_ref, cb_ref, dtb_ref, dtbp_ref,
                       alog_e_ref, alog_p_ref, dskip_e_ref, ng_ref, expand_ref, tril_ref, triu2_ref,
                       y_ref, ssm_ref, xw_ref, state_ref, tail_ref):
    t = pl.program_id(1)

    @pl.when(t == 0)
    def _():
        state_ref[...] = jnp.zeros_like(state_ref)
        xw_ref[0:CONV_HEAD, :] = jnp.zeros((CONV_HEAD, D_CONV), F32)

    @pl.when(t > 0)
    def _():
        xw_ref[0:CONV_HEAD, :] = tail_ref[...]

    _ssd_tile(CHUNKS_PER_TILE, z_ref, xbc_ref, dt_ref, dtp_ref, cw_ref, cb_ref, dtb_ref, dtbp_ref,
              alog_e_ref, alog_p_ref, dskip_e_ref, ng_ref, expand_ref, tril_ref, triu2_ref,
              y_ref, xw_ref, state_ref)
    tail_ref[...] = xw_ref[TM:TM + CONV_HEAD, :]

    @pl.when(t == TILES_PER_SEQ - 1)
    def _():
        _state_store(state_ref, ssm_ref)


def _ssd_sample_kernel(z_ref, xbc_ref, dt_ref, dtp_ref, cprev_ref, sprev_ref,
                       cw_ref, cb_ref, dtb_ref, dtbp_ref,
                       alog_e_ref, alog_p_ref, dskip_e_ref, ng_ref, expand_ref, tril_ref, triu2_ref,
                       y_ref, ssm_ref, xw_ref, state_ref):
    xw_ref[0:CONV_HEAD, :] = jnp.zeros((CONV_HEAD, D_CONV), F32)
    xw_ref[CONV_HEAD - (CONV_W - 1):CONV_HEAD, :] = cprev_ref[0]
    for j in range(HEAD_PAIRS):
        state_ref[:, j * 128:(j + 1) * 128] = sprev_ref[0, j * 128:(j + 1) * 128, :].T
    _ssd_tile(1, z_ref, xbc_ref, dt_ref, dtp_ref, cw_ref, cb_ref, dtb_ref, dtbp_ref,
              alog_e_ref, alog_p_ref, dskip_e_ref, ng_ref, expand_ref, tril_ref, triu2_ref,
              y_ref, xw_ref, state_ref)
    _state_store(state_ref, ssm_ref)


def _ssd_const_specs():
    return [
        _const_spec((CONV_W, D_CONV)), _const_spec((1, D_CONV)),
        _const_spec((1, SSM_HEADS)), _const_spec((HEAD_PAIRS, 128)),
        _const_spec((1, D_SSM)), _const_spec((HEAD_PAIRS, 128)),
        _const_spec((1, D_SSM)), _const_spec((1, D_SSM)),
        _const_spec((SSM_HEADS, D_SSM)), _const_spec((CHUNK, CHUNK)), _const_spec((128, 128)),
    ]


def _ssd_prompt(z, xbc, dt, dtp, consts):
    tile = lambda b, t: (b * TILES_PER_SEQ + t, 0)
    return pl.pallas_call(
        _ssd_prompt_kernel,
        grid=(BATCH, TILES_PER_SEQ),
        in_specs=[
            pl.BlockSpec((TM, D_SSM), tile),
            pl.BlockSpec((TM, D_CONV), tile),
            pl.BlockSpec((TM, SSM_HEADS), tile),
            pl.BlockSpec((CHUNKS_PER_TILE, HEAD_PAIRS, 128), lambda b, t: (b * TILES_PER_SEQ + t, 0, 0)),
        ] + _ssd_const_specs(),
        out_specs=[
            pl.BlockSpec((TM, D_SSM), tile),
            pl.BlockSpec((1, D_SSM, D_STATE), lambda b, t: (b, 0, 0)),
        ],
        out_shape=[
            jax.ShapeDtypeStruct((N_PROMPT, D_SSM), BF16),
            jax.ShapeDtypeStruct((BATCH, D_SSM, D_STATE), F32),
        ],
        scratch_shapes=[
            pltpu.VMEM((CONV_HEAD + TM, D_CONV), F32),
            pltpu.VMEM((D_STATE, D_SSM), F32),
            pltpu.VMEM((CONV_HEAD, D_CONV), F32),
        ],
        compiler_params=_params(2),
        name="ssd_prompt",
    )(z, xbc, dt, dtp, *consts)


def _ssd_sample(z, xbc, dt, dtp, conv_prev, ssm_prev, consts):
    first = N_PROMPT // CHUNK
    row = lambda b: (first + b, 0)
    return pl.pallas_call(
        _ssd_sample_kernel,
        grid=(DEC_BATCH,),
        in_specs=[
            pl.BlockSpec((CHUNK, D_SSM), row),
            pl.BlockSpec((CHUNK, D_CONV), row),
            pl.BlockSpec((CHUNK, SSM_HEADS), row),
            pl.BlockSpec((1, HEAD_PAIRS, 128), lambda b: (first + b, 0, 0)),
            pl.BlockSpec((1, CONV_W - 1, D_CONV), lambda b: (b, 0, 0)),
            pl.BlockSpec((1, D_SSM, D_STATE), lambda b: (b, 0, 0)),
        ] + _ssd_const_specs(),
        out_specs=[
            pl.BlockSpec((CHUNK, D_SSM), lambda b: (b, 0)),
            pl.BlockSpec((1, D_SSM, D_STATE), lambda b: (b, 0, 0)),
        ],
        out_shape=[
            jax.ShapeDtypeStruct((N_SAMPLE, D_SSM), BF16),
            jax.ShapeDtypeStruct((DEC_BATCH, D_SSM, D_STATE), F32),
        ],
        scratch_shapes=[
            pltpu.VMEM((CONV_HEAD + CHUNK, D_CONV), F32),
            pltpu.VMEM((D_STATE, D_SSM), F32),
        ],
        compiler_params=_params(),
        name="ssd_sample",
    )(z, xbc, dt, dtp, conv_prev, ssm_prev, *consts)


def _attn_chunks(n_chunks, first_chunk, q_ref, kpad_ref, vpad_ref, bias_ref, o_ref):
    lane = lax.broadcasted_iota(jnp.int32, (CHUNK, 128), 1)
    low = lane < ATT_HEAD_DIM
    kj = lax.broadcasted_iota(jnp.int32, (2 * CHUNK, BAND), 1)

    def chunk(c, carry):
        r0 = pl.multiple_of(c * CHUNK, CHUNK)
        first_valid = jnp.maximum(LEFT_CHUNKS - (first_chunk + c), 0) * CHUNK
        valid = kj >= first_valid
        outs = []
        for j in range(ATT_PAIRS):
            qp = q_ref[pl.ds(r0, CHUNK), j * 128:(j + 1) * 128]
            zero = jnp.zeros_like(qp)
            q2 = jnp.concatenate([jnp.where(low, qp, zero), jnp.where(low, zero, qp)], axis=0)
            kb = kpad_ref[pl.ds(r0, BAND), j * 128:(j + 1) * 128]
            vb = vpad_ref[pl.ds(r0, BAND), j * 128:(j + 1) * 128]
            s = _dot_nt(q2, kb) + bias_ref[j]
            s = jnp.where(valid, s, -jnp.inf)
            m = jnp.max(s, axis=-1, keepdims=True)
            e = jnp.exp(s - m)
            denom = jnp.sum(e, axis=-1, keepdims=True)
            r = _dot(e.astype(BF16), vb) / denom
            outs.append(jnp.where(low, r[0:CHUNK], r[CHUNK:2 * CHUNK]))
        o_ref[pl.ds(r0, CHUNK), :] = jnp.concatenate(outs, axis=1).astype(BF16)
        return carry

    lax.fori_loop(0, n_chunks, chunk, 0, unroll=2 if n_chunks % 2 == 0 else 1)


def _attn_prompt_kernel(q_ref, k_ref, v_ref, bias_ref, o_ref, kpad_ref, vpad_ref):
    kpad_ref[0:ATT_LEFT, :] = jnp.zeros((ATT_LEFT, D_ATT), BF16)
    vpad_ref[0:ATT_LEFT, :] = jnp.zeros((ATT_LEFT, D_ATT), BF16)
    kpad_ref[ATT_LEFT:ATT_LEFT + SEQ, :] = k_ref[...]
    vpad_ref[ATT_LEFT:ATT_LEFT + SEQ, :] = v_ref[...]
    _attn_chunks(N_CHUNKS_SEQ, 0, q_ref, kpad_ref, vpad_ref, bias_ref, o_ref)


def _attn_sample_kernel(q_ref, k_ref, v_ref, ck_ref, cv_ref, bias_ref, o_ref, kpad_ref, vpad_ref):
    kpad_ref[0:ATT_LEFT, :] = ck_ref[0].astype(BF16)
    vpad_ref[0:ATT_LEFT, :] = cv_ref[0].astype(BF16)
    kpad_ref[ATT_LEFT:BAND, :] = k_ref[...]
    vpad_ref[ATT_LEFT:BAND, :] = v_ref[...]
    _attn_chunks(1, LEFT_CHUNKS, q_ref, kpad_ref, vpad_ref, bias_ref, o_ref)


def _attn_prompt(q, k, v, bias2):
    seq = pl.BlockSpec((SEQ, D_ATT), lambda b: (b, 0))
    return pl.pallas_call(
        _attn_prompt_kernel,
        grid=(BATCH,),
        in_specs=[seq, seq, seq, _const_spec((ATT_PAIRS, 2 * CHUNK, BAND))],
        out_specs=seq,
        out_shape=jax.ShapeDtypeStruct((N_PROMPT, D_ATT), BF16),
        scratch_shapes=[pltpu.VMEM((ATT_LEFT + SEQ, D_ATT), BF16),
                        pltpu.VMEM((ATT_LEFT + SEQ, D_ATT), BF16)],
        compiler_params=_params(),
        name="attn_prompt",
    )(q, k, v, bias2)


def _attn_sample(q, k, v, cache_k, cache_v, bias2):
    first = N_PROMPT // CHUNK
    row = pl.BlockSpec((CHUNK, D_ATT), lambda b: (first + b, 0))
    cache = pl.BlockSpec((1, ATT_LEFT, D_ATT), lambda b: (b, 0, 0))
    return pl.pallas_call(
        _attn_sample_kernel,
        grid=(DEC_BATCH,),
        in_specs=[row, row, row, cache, cache, _const_spec((ATT_PAIRS, 2 * CHUNK, BAND))],
        out_specs=pl.BlockSpec((CHUNK, D_ATT), lambda b: (b, 0)),
        out_shape=jax.ShapeDtypeStruct((N_SAMPLE, D_ATT), BF16),
        scratch_shapes=[pltpu.VMEM((BAND, D_ATT), BF16), pltpu.VMEM((BAND, D_ATT), BF16)],
        compiler_params=_params(),
        name="attn_sample",
    )(q, k, v, cache_k, cache_v, bias2)


def _outproj_kernel(xp_ref, xs_ref, yp_ref, ys_ref, op_ref, os_ref, ag_ref, wos_ref, woa_ref, fg_ref,
                    wr_ref, br_ref, ltri_ref,
                    xmid_ref, h_ref, idx_ref, gate_ref, rank_ref, cnt_ref, carry_ref):
    i = pl.program_id(0)

    @pl.when(i == 0)
    def _():
        carry_ref[...] = jnp.zeros_like(carry_ref)

    is_sample = i == N_PROMPT_TILES
    x = jnp.where(is_sample, xs_ref[...], xp_ref[...])
    y = jnp.where(is_sample, ys_ref[...], yp_ref[...])
    o = jnp.where(is_sample, os_ref[...], op_ref[...])
    o = _rms(o.astype(F32), ag_ref[...]).astype(BF16)
    xm = x + _dot(y, wos_ref[...]) + _dot(o, woa_ref[...])
    xmid_ref[...] = xm
    h = _rms(xm, fg_ref[...])
    _rows_to_tiles(h_ref, h)
    h1 = h.astype(BF16)
    h2 = (h - h1.astype(F32)).astype(BF16)
    w1 = wr_ref[0]
    w2 = wr_ref[1]
    logits = _dot(h1, w1) + (_dot(h1, w2) + _dot(h2, w1)) + br_ref[...]
    eidx = lax.broadcasted_iota(jnp.int32, (TM, N_EXPERTS), 1)
    lane = lax.broadcasted_iota(jnp.int32, (TM, 128), 1)
    work = logits
    vals, sels = [], []
    idx_out = jnp.zeros((TM, 128), jnp.int32)
    for k in range(TOP_K):
        m = jnp.max(work, axis=-1, keepdims=True)
        idx = jnp.min(jnp.where(work == m, eidx, N_EXPERTS), axis=-1, keepdims=True)
        sel = eidx == idx
        vals.append(m)
        sels.append(sel)
        idx_out = jnp.where(lane == k, idx, idx_out)
        work = jnp.where(sel, -jnp.inf, work)
    es = [jnp.exp(v - vals[0]) for v in vals]
    tot = es[0] + es[1] + es[2] + es[3]
    gate_out = jnp.zeros((TM, 128), F32)
    for k in range(TOP_K):
        gate_out = jnp.where(lane == k, es[k] / tot, gate_out)
    idx_ref[...] = idx_out
    gate_ref[...] = gate_out
    multi = jnp.zeros((TM, N_EXPERTS), F32)
    for sel in sels:
        multi = jnp.where(sel, 1.0, multi)
    before = _dot(ltri_ref[...], multi.astype(BF16)) + carry_ref[...]
    rank_out = jnp.zeros((TM, 128), jnp.int32)
    for k in range(TOP_K):
        rk = jnp.sum(jnp.where(sels[k], before, 0.0), axis=-1, keepdims=True).astype(jnp.int32)
        rank_out = jnp.where(lane == k, rk, rank_out)
    rank_ref[...] = rank_out
    carry_ref[...] = carry_ref[...] + jnp.sum(multi, axis=0, keepdims=True)
    cnt_ref[...] = carry_ref[...]


def _outproj(xp, xs, yp, ys, op, os_, ag, wos, woa, fg, wr, br, ltri):
    tok = lambda n: pl.BlockSpec((TM, n), lambda i: (i, 0))
    prompt = lambda n: pl.BlockSpec((TM, n), lambda i: (jnp.minimum(i, N_PROMPT_TILES - 1), 0))
    sample = lambda n: pl.BlockSpec((TM, n), lambda i: (0, 0))
    return pl.pallas_call(
        _outproj_kernel,
        grid=(N_TILES,),
        in_specs=[
            prompt(D_MODEL), sample(D_MODEL), prompt(D_SSM), sample(D_SSM), prompt(D_ATT), sample(D_ATT),
            _const_spec((1, D_ATT)),
            _const_spec((D_SSM, D_MODEL)), _const_spec((D_ATT, D_MODEL)),
            _const_spec((1, D_MODEL)),
            _const_spec((2, D_MODEL, N_EXPERTS)), _const_spec((1, N_EXPERTS)),
            _const_spec((TM, TM)),
        ],
        out_specs=[tok(D_MODEL), pl.BlockSpec(_tiled(TM), lambda i: (i, 0)), tok(128), tok(128), tok(128),
                   _const_spec((1, N_EXPERTS))],
        out_shape=[
            jax.ShapeDtypeStruct((N_TOK, D_MODEL), F32),
            jax.ShapeDtypeStruct(_tiled(N_TOK), F32),
            jax.ShapeDtypeStruct((N_TOK, 128), jnp.int32),
            jax.ShapeDtypeStruct((N_TOK, 128), F32),
            jax.ShapeDtypeStruct((N_TOK, 128), jnp.int32),
            jax.ShapeDtypeStruct((1, N_EXPERTS), F32),
        ],
        scratch_shapes=[pltpu.VMEM((1, N_EXPERTS), F32)],
        compiler_params=_params(),
        name="outproj_router",
    )(xp, xs, yp, ys, op, os_, ag, wos, woa, fg, wr, br, ltri)


def _scatter_kernel(dest_ref, pend_ref, h_ref, rows_ref, zero_ref, sem, zsem):
    i = pl.program_id(0)

    @pl.when(i == 0)
    def _():
        zero_ref[...] = jnp.zeros_like(zero_ref)

        block_tiles = _tiled(MOE_BM)[0]

        def zero_block(b):
            start = pl.multiple_of(b * block_tiles, block_tiles)
            return pltpu.make_async_copy(zero_ref, rows_ref.at[pl.ds(start, block_tiles)], zsem)

        def last_block(e):
            end = pend_ref[e]
            nonempty = end > (pend_ref[e - 1] if e > 0 else 0)
            return nonempty, zero_block(jnp.maximum(end // MOE_BM - 1, 0))

        for e in range(N_EXPERTS):
            nonempty, cp = last_block(e)
            pl.when(nonempty)(cp.start)
        for e in range(N_EXPERTS):
            nonempty, cp = last_block(e)
            pl.when(nonempty)(cp.wait)

        first_unused = pend_ref[N_EXPERTS - 1] // MOE_BM
        lax.fori_loop(first_unused, MOE_BLOCKS, lambda b, c: (zero_block(b).start(), c)[1], 0)
        lax.fori_loop(first_unused, MOE_BLOCKS, lambda b, c: (zero_block(b).wait(), c)[1], 0)

    def issue(r, carry):
        for k in range(TOP_K):
            d = dest_ref[r * TOP_K + k]
            pltpu.make_async_copy(_tile_of(h_ref, r), _tile_of(rows_ref, d), sem).start(priority=k % 2)
        return carry

    lax.fori_loop(0, TM, issue, 0)
    for _ in range(TOP_K):
        pltpu.make_async_copy(h_ref, rows_ref.at[pl.ds(0, _tiled(TM)[0])], sem).wait()


def _scatter_rows(dest_flat, pad_end, h):
    return pl.pallas_call(
        _scatter_kernel,
        grid=(N_TILES,),
        in_specs=[
            pl.BlockSpec((TM * TOP_K,), lambda i: (i,), memory_space=pltpu.SMEM),
            pl.BlockSpec((N_EXPERTS,), lambda i: (0,), memory_space=pltpu.SMEM),
            pl.BlockSpec(_tiled(TM), lambda i: (i, 0)),
        ],
        out_specs=pl.BlockSpec(memory_space=pl.ANY),
        out_shape=jax.ShapeDtypeStruct(_tiled(MOE_ROWS), F32),
        scratch_shapes=[pltpu.VMEM(_tiled(MOE_BM), F32), pltpu.SemaphoreType.DMA(()),
                        pltpu.SemaphoreType.DMA(())],
        compiler_params=_params(),
        name="moe_scatter",
    )(dest_flat, pad_end, h)


def _expert_kernel(be_ref, nu_ref, x_ref, wgu_ref, bgu_ref, wd_ref, bd_ref, y_ref, wgu_s, wd_s):
    i = pl.program_id(0)
    prev = be_ref[jnp.maximum(i - 1, 0)]

    @pl.when((i == 0) | (be_ref[i] != prev))
    def _():
        wgu_s[...] = wgu_ref[0].astype(BF16)
        wd_s[...] = wd_ref[0].astype(BF16)

    @pl.when(i < nu_ref[0])
    def _():
        gu = _dot(_tiles_to_rows(x_ref, MOE_BM).astype(BF16), wgu_s[...]) + bgu_ref[0]
        gate = jnp.minimum(gu[:, :D_FF], SWIGLU_LIMIT)
        up = jnp.clip(gu[:, D_FF:], -SWIGLU_LIMIT, SWIGLU_LIMIT)
        act = (up + 1.0) * gate * _sigmoid(gate * SWIGLU_ALPHA)
        _rows_to_tiles(y_ref, _dot(act.astype(BF16), wd_s[...]) + bd_ref[0])

    @pl.when(i >= nu_ref[0])
    def _():
        y_ref[...] = jnp.zeros_like(y_ref)


def _experts(block_expert, n_used, rows, wgu, bgu, wd, bd):
    grid_spec = pltpu.PrefetchScalarGridSpec(
        num_scalar_prefetch=2,
        grid=(MOE_BLOCKS,),
        in_specs=[
            pl.BlockSpec(_tiled(MOE_BM), lambda i, be, nu: (jnp.minimum(i, nu[0] - 1), 0)),
            pl.BlockSpec((1, D_MODEL, 2 * D_FF), lambda i, be, nu: (be[i], 0, 0)),
            pl.BlockSpec((1, 1, 2 * D_FF), lambda i, be, nu: (be[i], 0, 0)),
            pl.BlockSpec((1, D_FF, D_MODEL), lambda i, be, nu: (be[i], 0, 0)),
            pl.BlockSpec((1, 1, D_MODEL), lambda i, be, nu: (be[i], 0, 0)),
        ],
        out_specs=pl.BlockSpec(_tiled(MOE_BM), lambda i, be, nu: (i, 0)),
        scratch_shapes=[pltpu.VMEM((D_MODEL, 2 * D_FF), BF16), pltpu.VMEM((D_FF, D_MODEL), BF16)],
    )
    return pl.pallas_call(
        _expert_kernel,
        grid_spec=grid_spec,
        out_shape=jax.ShapeDtypeStruct(_tiled(MOE_ROWS), F32),
        compiler_params=_params(),
        name="moe_experts",
    )(block_expert, n_used, rows, wgu, bgu, wd, bd)


def _combine_kernel(dest_ref, gate_ref, xmid_ref, g_ref, rows_ref, yp_ref, ys_ref, buf_ref, sem):
    i = pl.program_id(0)

    def issue(r, carry):
        for k in range(TOP_K):
            d = dest_ref[r * TOP_K + k]
            pltpu.make_async_copy(_tile_of(rows_ref, d), _tile_of(buf_ref, k * TM + r), sem).start(priority=k % 2)
        return carry

    lax.fori_loop(0, TM, issue, 0)
    slot_tiles = _tiled(TM)[0]
    for k in range(TOP_K):
        pltpu.make_async_copy(rows_ref.at[pl.ds(0, slot_tiles)], buf_ref.at[pl.ds(k * slot_tiles, slot_tiles)],
                              sem).wait()
    acc = xmid_ref[...]
    for k in range(TOP_K):
        acc = acc + _tiles_to_rows(buf_ref, TM, first=k * TM) * gate_ref[:, k:k + 1]
    y = _rms(acc, g_ref[...])

    @pl.when(i < N_PROMPT_TILES)
    def _():
        yp_ref[...] = y

    @pl.when(i == N_PROMPT_TILES)
    def _():
        ys_ref[...] = y


def _combine(dest_flat, gates, xmid, g, y_rows):
    return pl.pallas_call(
        _combine_kernel,
        grid=(N_TILES,),
        in_specs=[
            pl.BlockSpec((TM * TOP_K,), lambda i: (i,), memory_space=pltpu.SMEM),
            pl.BlockSpec((TM, 128), lambda i: (i, 0)),
            pl.BlockSpec((TM, D_MODEL), lambda i: (i, 0)),
            _const_spec((1, D_MODEL)),
            pl.BlockSpec(memory_space=pl.ANY),
        ],
        out_specs=[
            pl.BlockSpec((TM, D_MODEL), lambda i: (jnp.minimum(i, N_PROMPT_TILES - 1), 0)),
            pl.BlockSpec((TM, D_MODEL), lambda i: (0, 0)),
        ],
        out_shape=[
            jax.ShapeDtypeStruct((N_PROMPT, D_MODEL), F32),
            jax.ShapeDtypeStruct((N_SAMPLE, D_MODEL), F32),
        ],
        scratch_shapes=[pltpu.VMEM(_tiled(TOP_K * TM), F32), pltpu.SemaphoreType.DMA(())],
        compiler_params=_params(),
        name="moe_combine",
    )(dest_flat, gates, xmid, g, y_rows)


def _band_bias(table):
    n_diag = BAND + CHUNK - 1
    idx = np.clip(ATT_LEFT + (CHUNK - 1) - np.arange(n_diag), -REL_CLIP, REL_CLIP) + REL_CLIP
    pick = (np.arange(2 * REL_CLIP + 1)[:, None] == idx[None, :]).astype(np.float32)
    diag = jnp.dot(table, jnp.asarray(pick), precision=lax.Precision.HIGHEST)
    return jnp.stack([diag[:, CHUNK - 1 - qi:CHUNK - 1 - qi + BAND] for qi in range(CHUNK)], axis=1)


def _pair_rows(v):
    return jnp.repeat(v.reshape(HEAD_PAIRS, 2), CHUNK, axis=1)


def _layer(l, xp, xs, cache_k, cache_v, state_conv, state_ssm,
           norm_mix_g, w_in, conv_w, conv_b, dt_bias, a_log, d_skip, ssm_norm_g,
           att_norm_g, rel_bias_table, w_out, norm_ffn_g, w_router, b_router,
           w_gate_up, b_gate_up, w_down, b_down, norm_final_g):
    wb = w_in[l].astype(BF16)
    c0 = D_SSM
    c1 = c0 + D_CONV
    c2 = c1 + SSM_HEADS
    c3 = c2 + D_ATT
    c4 = c3 + D_ATT
    z, xbc, dt, dtt, q, k, v, kf, vf, ctail = _inproj(
        xp, xs, norm_mix_g[l][None], wb[:, :c0], wb[:, c0:c1], wb[:, c1:c2], wb[:, c1:c2].T,
        wb[:, c2:c3], wb[:, c3:c4], wb[:, c4:])

    n_chunks = N_TOK // CHUNK
    dtp = dtt.reshape(HEAD_PAIRS, 2, n_chunks, CHUNK).transpose(2, 0, 1, 3).reshape(n_chunks, HEAD_PAIRS, 128)
    hp = jnp.arange(D_SSM) // SSM_HEAD_DIM
    expand = (hp[None, :] == jnp.arange(SSM_HEADS)[:, None]).astype(BF16)
    tril = jnp.tril(jnp.ones((CHUNK, CHUNK), BF16))
    r128 = jnp.arange(128)
    triu2 = ((r128[:, None] // CHUNK == r128[None, :] // CHUNK) & (r128[:, None] <= r128[None, :])).astype(BF16)
    consts = (conv_w[l], conv_b[l][None], dt_bias[l][None], _pair_rows(dt_bias[l]),
              jnp.repeat(a_log[l], SSM_HEAD_DIM)[None], _pair_rows(a_log[l]),
              jnp.repeat(d_skip[l], SSM_HEAD_DIM)[None], ssm_norm_g[l][None],
              expand, tril, triu2)
    y_ssm_p, ssm_p = _ssd_prompt(z, xbc, dt, dtp, consts)
    y_ssm_s, ssm_s = _ssd_sample(z, xbc, dt, dtp, state_conv[l],
                                 state_ssm[l].reshape(DEC_BATCH, D_SSM, D_STATE), consts)

    bias2 = _band_bias(rel_bias_table[l]).reshape(ATT_PAIRS, 2 * CHUNK, BAND)
    o_att_p = _attn_prompt(q, k, v, bias2)
    o_att_s = _attn_sample(q, k, v, cache_k[l].reshape(DEC_BATCH, ATT_LEFT, D_ATT),
                           cache_v[l].reshape(DEC_BATCH, ATT_LEFT, D_ATT), bias2)

    wo = w_out[l].astype(BF16)
    wr = w_router[l]
    wr1 = wr.astype(BF16)
    wr2 = (wr - wr1.astype(F32)).astype(BF16)
    ltri = jnp.tril(jnp.ones((TM, TM), BF16), -1)
    xmid, h, top_idx, gates, rank, counts = _outproj(
        xp, xs, y_ssm_p, y_ssm_s, o_att_p, o_att_s, att_norm_g[l][None], wo[:D_SSM], wo[D_SSM:], norm_ffn_g[l][None],
        jnp.stack([wr1, wr2]), b_router[l][None], ltri)

    counts = counts[0].astype(jnp.int32)
    padded = (counts + MOE_BM - 1) // MOE_BM * MOE_BM
    pad_end = jnp.cumsum(padded)
    pad_start = pad_end - padded
    experts = jnp.arange(N_EXPERTS, dtype=jnp.int32)
    start_of = jnp.sum(jnp.where(top_idx[:, :TOP_K, None] == experts, pad_start, 0), axis=-1)
    dest = (start_of + rank[:, :TOP_K]).reshape(-1).astype(jnp.int32)
    block_start = jnp.arange(MOE_BLOCKS, dtype=jnp.int32) * MOE_BM
    block_expert = jnp.minimum(jnp.sum((pad_end[None, :] <= block_start[:, None]).astype(jnp.int32), axis=1),
                               N_EXPERTS - 1).astype(jnp.int32)
    n_used = (pad_end[-1:] // MOE_BM).astype(jnp.int32)

    rows = _scatter_rows(dest, pad_end.astype(jnp.int32), h)
    y_rows = _experts(block_expert, n_used, rows, w_gate_up[l], b_gate_up[l][:, None, :],
                      w_down[l], b_down[l][:, None, :])
    y_p, y_s = _combine(dest, gates, xmid, norm_final_g[None], y_rows)

    keep = min(ATT_LEFT, SEQ)
    k_p = kf[:BATCH, TM - keep:].reshape(BATCH, keep, ATT_HEADS, ATT_HEAD_DIM)
    v_p = vf[:BATCH, TM - keep:].reshape(BATCH, keep, ATT_HEADS, ATT_HEAD_DIM)
    k_s = kf[BATCH].reshape(DEC_BATCH, DEC_SEQ, ATT_HEADS, ATT_HEAD_DIM)
    v_s = vf[BATCH].reshape(DEC_BATCH, DEC_SEQ, ATT_HEADS, ATT_HEAD_DIM)
    conv_p = ctail[:BATCH, -(CONV_W - 1):]
    conv_s = ctail[BATCH].reshape(DEC_BATCH, 8, D_CONV)[:, -(CONV_W - 1):]
    ssm_p = ssm_p.reshape(BATCH, SSM_HEADS, SSM_HEAD_DIM, D_STATE)
    ssm_s = ssm_s.reshape(DEC_BATCH, SSM_HEADS, SSM_HEAD_DIM, D_STATE)
    return (y_p.reshape(BATCH, SEQ, D_MODEL), y_s.reshape(DEC_BATCH, DEC_SEQ, D_MODEL),
            k_p, v_p, conv_p, ssm_p, k_s, v_s, conv_s, ssm_s)


def kernel(x_prompt, x_sample, cache_k, cache_v, state_conv, state_ssm, norm_mix_g, w_in, conv_w, conv_b,
           dt_bias, a_log, d_skip, ssm_norm_g, att_norm_g, rel_bias_table, w_out, norm_ffn_g, w_router,
           b_router, w_gate_up, b_gate_up, w_down, b_down, norm_final_g):
    assert w_in.shape[0] == 1, "single trunk layer"
    xp = x_prompt.reshape(N_PROMPT, D_MODEL)
    xs = x_sample.reshape(N_SAMPLE, D_MODEL)
    outs = _layer(0, xp, xs, cache_k, cache_v, state_conv, state_ssm,
                  norm_mix_g, w_in, conv_w, conv_b, dt_bias, a_log, d_skip, ssm_norm_g,
                  att_norm_g, rel_bias_table, w_out, norm_ffn_g, w_router, b_router,
                  w_gate_up, b_gate_up, w_down, b_down, norm_final_g)
    y_p, y_s, k_p, v_p, conv_p, ssm_p, k_s, v_s, conv_s, ssm_s = outs
    return (y_p, y_s, k_p[None], v_p[None], conv_p[None], ssm_p[None],
            k_s[None], v_s[None], conv_s[None], ssm_s[None])
```

```python
import jax
import jax.numpy as jnp
import numpy as np
from jax import lax
from jax.experimental import pallas as pl
from jax.experimental.pallas import tpu as pltpu

D_MODEL = 1024
BATCH = 8
SEQ = 2048
DEC_BATCH = 8
DEC_SEQ = 64
CHUNK = 64
SSM_HEADS = 16
SSM_HEAD_DIM = 64
D_SSM = SSM_HEADS * SSM_HEAD_DIM
SSM_GROUPS = 2
D_STATE = 128
CONV_W = 4
D_BC = SSM_GROUPS * D_STATE
D_CONV = D_SSM + 2 * D_BC
ATT_HEADS = 8
ATT_HEAD_DIM = 64
D_ATT = ATT_HEADS * ATT_HEAD_DIM
LEFT_CHUNKS = 8
ATT_LEFT = LEFT_CHUNKS * CHUNK
BAND = ATT_LEFT + CHUNK
REL_CLIP = 128
ATT_SCALE = ATT_HEAD_DIM ** -0.5
N_EXPERTS = 32
TOP_K = 4
D_FF = D_MODEL
SWIGLU_ALPHA = 1.702
SWIGLU_LIMIT = 7.0
EPS = 1e-5

F32 = jnp.float32
BF16 = jnp.bfloat16

N_PROMPT = BATCH * SEQ
N_SAMPLE = DEC_BATCH * DEC_SEQ
N_TOK = N_PROMPT + N_SAMPLE
TM = 512
N_PROMPT_TILES = N_PROMPT // TM
N_TILES = N_TOK // TM
TILES_PER_SEQ = SEQ // TM
CHUNKS_PER_TILE = TM // CHUNK
N_CHUNKS_SEQ = SEQ // CHUNK
HEAD_PAIRS = SSM_HEADS // 2
PAIRS_PER_GROUP = HEAD_PAIRS // SSM_GROUPS
ATT_PAIRS = ATT_HEADS // 2
CONV_HEAD = 8
MOE_BM = 256
N_ASSIGN = N_TOK * TOP_K
MOE_BLOCKS = N_ASSIGN // MOE_BM + N_EXPERTS
MOE_ROWS = MOE_BLOCKS * MOE_BM
ROW_TILE = (8, 128)
assert ROW_TILE[0] * ROW_TILE[1] == D_MODEL
VMEM_LIMIT = 56 * 1024 * 1024


def _dot(a, b):
    return jnp.dot(a, b, preferred_element_type=F32)


def _dot_nt(a, b):
    return lax.dot_general(a, b, (((1,), (1,)), ((), ())), preferred_element_type=F32)


def _dot_tn(a, b):
    return lax.dot_general(a, b, (((0,), (0,)), ((), ())), preferred_element_type=F32)


def _split3(x):
    x1 = x.astype(BF16)
    r1 = x - x1.astype(F32)
    x2 = r1.astype(BF16)
    r2 = r1 - x2.astype(F32)
    return x1, x2, r2.astype(BF16)


def _dot_exact_rhs(x, m):
    x1, x2, x3 = _split3(x)
    return _dot(x1, m) + _dot(x2, m) + _dot(x3, m)


def _dot_exact_lhs(m, x):
    x1, x2, x3 = _split3(x)
    return _dot(m, x1) + _dot(m, x2) + _dot(m, x3)


def _rms(x, g):
    return x * lax.rsqrt(jnp.mean(x * x, axis=-1, keepdims=True) + EPS) * g


def _sigmoid(x):
    return 1.0 / (1.0 + jnp.exp(-x))


def _softplus(x):
    return jnp.maximum(x, 0.0) + jnp.log(1.0 + jnp.exp(-jnp.abs(x)))


def _tiled(n):
    return (n * ROW_TILE[0], ROW_TILE[1])


def _tile_of(ref, row):
    return ref.at[pl.ds(pl.multiple_of(row * ROW_TILE[0], ROW_TILE[0]), ROW_TILE[0])]


def _rows_to_tiles(ref, x, first=0):
    sub, lanes = ROW_TILE
    for j in range(sub):
        ref[pl.ds(first * sub + j, x.shape[0], stride=sub), :] = x[:, j * lanes:(j + 1) * lanes]


def _tiles_to_rows(ref, n, first=0):
    sub = ROW_TILE[0]
    return jnp.concatenate([ref[pl.ds(first * sub + j, n, stride=sub), :] for j in range(sub)], axis=1)


def _const_spec(shape):
    nd = len(shape)
    return pl.BlockSpec(shape, lambda *_: (0,) * nd)


def _params(n_axes=1):
    return pltpu.CompilerParams(dimension_semantics=("arbitrary",) * n_axes,
                                vmem_limit_bytes=VMEM_LIMIT)


def _inproj_kernel(xp_ref, xs_ref, g_ref, wz_ref, wxbc_ref, wdt_ref, wdtt_ref, wq_ref, wk_ref, wv_ref,
                   z_ref, xbc_ref, dt_ref, dtt_ref, q_ref, k_ref, v_ref, kf_ref, vf_ref, ctail_ref):
    i = pl.program_id(0)
    x = jnp.where(i == N_PROMPT_TILES, xs_ref[...], xp_ref[...])
    h = _rms(x, g_ref[...]).astype(BF16)
    z_ref[...] = _dot(h, wz_ref[...]).astype(BF16)
    xbc = _dot(h, wxbc_ref[...])
    xbc_ref[...] = xbc.astype(BF16)
    for c in range(CHUNKS_PER_TILE):
        ctail_ref[0, c * 8:(c + 1) * 8, :] = xbc[c * CHUNK + CHUNK - 8:(c + 1) * CHUNK, :]
    dt_ref[...] = _dot(h, wdt_ref[...])
    dtt_ref[...] = _dot_nt(wdtt_ref[...], h)
    q_ref[...] = (_dot(h, wq_ref[...]) * ATT_SCALE).astype(BF16)
    k = _dot(h, wk_ref[...])
    v = _dot(h, wv_ref[...])
    k_ref[...] = k.astype(BF16)
    v_ref[...] = v.astype(BF16)
    kf_ref[0] = k
    vf_ref[0] = v


def _inproj(xp, xs, g, wz, wxbc, wdt, wdtt, wq, wk, wv):
    tok = lambda n: pl.BlockSpec((TM, n), lambda i: (i, 0))
    tail_idx = lambda i: (i // TILES_PER_SEQ, 0, 0)
    n_tail = BATCH + 1
    return pl.pallas_call(
        _inproj_kernel,
        grid=(N_TILES,),
        in_specs=[
            pl.BlockSpec((TM, D_MODEL), lambda i: (jnp.minimum(i, N_PROMPT_TILES - 1), 0)),
            pl.BlockSpec((TM, D_MODEL), lambda i: (0, 0)),
            _const_spec((1, D_MODEL)),
            _const_spec((D_MODEL, D_SSM)),
            _const_spec((D_MODEL, D_CONV)),
            _const_spec((D_MODEL, SSM_HEADS)),
            _const_spec((SSM_HEADS, D_MODEL)),
            _const_spec((D_MODEL, D_ATT)),
            _const_spec((D_MODEL, D_ATT)),
            _const_spec((D_MODEL, D_ATT)),
        ],
        out_specs=[
            tok(D_SSM), tok(D_CONV), tok(SSM_HEADS),
            pl.BlockSpec((SSM_HEADS, TM), lambda i: (0, i)),
            tok(D_ATT), tok(D_ATT), tok(D_ATT),
            pl.BlockSpec((1, TM, D_ATT), tail_idx),
            pl.BlockSpec((1, TM, D_ATT), tail_idx),
            pl.BlockSpec((1, CHUNKS_PER_TILE * 8, D_CONV), tail_idx),
        ],
        out_shape=[
            jax.ShapeDtypeStruct((N_TOK, D_SSM), BF16),
            jax.ShapeDtypeStruct((N_TOK, D_CONV), BF16),
            jax.ShapeDtypeStruct((N_TOK, SSM_HEADS), F32),
            jax.ShapeDtypeStruct((SSM_HEADS, N_TOK), F32),
            jax.ShapeDtypeStruct((N_TOK, D_ATT), BF16),
            jax.ShapeDtypeStruct((N_TOK, D_ATT), BF16),
            jax.ShapeDtypeStruct((N_TOK, D_ATT), BF16),
            jax.ShapeDtypeStruct((n_tail, TM, D_ATT), F32),
            jax.ShapeDtypeStruct((n_tail, TM, D_ATT), F32),
            jax.ShapeDtypeStruct((n_tail, CHUNKS_PER_TILE * 8, D_CONV), F32),
        ],
        compiler_params=_params(),
        name="inproj",
    )(xp, xs, g, wz, wxbc, wdt, wdtt, wq, wk, wv)


def _ssd_tile(n_chunks, z_ref, xbc_ref, dt_ref, dtp_ref, cw_ref, cb_ref, dtb_ref, dtbp_ref,
              alog_e_ref, alog_p_ref, dskip_e_ref, ng_ref, expand_ref, tril_ref, triu2_ref,
              y_ref, xw_ref, state_ref):
    rows = n_chunks * CHUNK
    xw_ref[CONV_HEAD:CONV_HEAD + rows, :] = xbc_ref[...].astype(F32)

    a_e = -jnp.exp(alog_e_ref[...])
    a_p = -jnp.exp(alog_p_ref[...])
    row_i = lax.broadcasted_iota(jnp.int32, (CHUNK, 128), 0)
    col_i = lax.broadcasted_iota(jnp.int32, (CHUNK, 128), 1)
    causal2 = row_i >= (col_i % CHUNK)
    bd_r = lax.broadcasted_iota(jnp.int32, (128, 128), 0) // CHUNK
    bd_c = lax.broadcasted_iota(jnp.int32, (128, 128), 1) // CHUNK
    blockdiag = bd_r == bd_c

    def chunk(c, carry):
        r0 = pl.multiple_of(c * CHUNK, CHUNK)
        win = xw_ref.at[pl.ds(r0, CONV_HEAD + CHUNK), :]
        acc = cb_ref[...]
        for tap in range(CONV_W):
            off = CONV_HEAD - (CONV_W - 1) + tap
            acc = acc + cw_ref[tap:tap + 1, :] * win[off:off + CHUNK, :]
        xa = acc * _sigmoid(acc)
        xs = xa[:, 0:D_SSM]
        bm = xa[:, D_SSM:D_SSM + D_BC].astype(BF16)
        cm = xa[:, D_SSM + D_BC:D_CONV].astype(BF16)
        dt = _softplus(dt_ref[pl.ds(r0, CHUNK), :] + dtb_ref[...])
        dt_e = _dot_exact_rhs(dt, expand_ref[...])
        acum = _dot_exact_lhs(tril_ref[...], dt_e * a_e)
        dtp = _softplus(dtp_ref[c] + dtbp_ref[...])
        acum_p = _dot_exact_rhs(dtp * a_p, triu2_ref[...])
        xdt = xs * dt_e
        a_last = acum[CHUNK - 1:CHUNK, :]
        xdt_end = (xdt * jnp.exp(a_last - acum)).astype(BF16)
        y_parts = []
        for g in range(SSM_GROUPS):
            bg = bm[:, g * D_STATE:(g + 1) * D_STATE]
            cg = cm[:, g * D_STATE:(g + 1) * D_STATE]
            cb2 = _dot_nt(cg, jnp.concatenate([bg, bg], axis=0))
            for jj in range(PAIRS_PER_GROUP):
                j = g * PAIRS_PER_GROUP + jj
                seg = acum[:, j * 128:(j + 1) * 128] - acum_p[j:j + 1, :]
                decay = jnp.exp(jnp.where(causal2, seg, -jnp.inf))
                s_pair = (cb2 * decay).astype(BF16)
                x2 = xdt[:, j * 128:(j + 1) * 128]
                rhs = jnp.where(blockdiag, jnp.concatenate([x2, x2], axis=0), 0.0).astype(BF16)
                y_parts.append(_dot(s_pair, rhs))
        y_diag = jnp.concatenate(y_parts, axis=1)
        half = D_SSM // SSM_GROUPS
        y_off = jnp.concatenate(
            [_dot(cm[:, g * D_STATE:(g + 1) * D_STATE], state_ref[:, g * half:(g + 1) * half].astype(BF16))
             for g in range(SSM_GROUPS)], axis=1)
        new_s = jnp.concatenate(
            [_dot_tn(bm[:, g * D_STATE:(g + 1) * D_STATE], xdt_end[:, g * half:(g + 1) * half])
             for g in range(SSM_GROUPS)], axis=1)
        state_ref[...] = state_ref[...] * jnp.exp(a_last) + new_s
        y = y_diag + y_off * jnp.exp(acum) + dskip_e_ref[...] * xs
        zc = z_ref[pl.ds(r0, CHUNK), :].astype(F32)
        y = y * (zc * _sigmoid(zc))
        yn = jnp.concatenate(
            [y[:, g * half:(g + 1) * half]
             * lax.rsqrt(jnp.mean(jnp.square(y[:, g * half:(g + 1) * half]), axis=-1, keepdims=True) + EPS)
             for g in range(SSM_GROUPS)], axis=1)
        y_ref[pl.ds(r0, CHUNK), :] = (yn * ng_ref[...]).astype(BF16)
        return carry

    lax.fori_loop(0, n_chunks, chunk, 0)


def _state_store(state_ref, out_ref):
    for j in range(HEAD_PAIRS):
        out_ref[0, j * 128:(j + 1) * 128, :] = state_ref[:, j * 128:(j + 1) * 128].T


def _ssd_prompt_kernel(z_ref, xbc_ref, dt_ref, dtp_ref, cw_ref, cb_ref, dtb_ref, dtbp_ref,
                       alog_e_ref, alog_p_ref, dskip_e_ref, ng_ref, expand_ref, tril_ref, triu2_ref,
                       y_ref, ssm_ref, xw_ref, state_ref, tail_ref):
    t = pl.program_id(1)

    @pl.when(t == 0)
    def _():
        state_ref[...] = jnp.zeros_like(state_ref)
        xw_ref[0:CONV_HEAD, :] = jnp.zeros((CONV_HEAD, D_CONV), F32)

    @pl.when(t > 0)
    def _():
        xw_ref[0:CONV_HEAD, :] = tail_ref[...]

    _ssd_tile(CHUNKS_PER_TILE, z_ref, xbc_ref, dt_ref, dtp_ref, cw_ref, cb_ref, dtb_ref, dtbp_ref,
              alog_e_ref, alog_p_ref, dskip_e_ref, ng_ref, expand_ref, tril_ref, triu2_ref,
              y_ref, xw_ref, state_ref)
    tail_ref[...] = xw_ref[TM:TM + CONV_HEAD, :]

    @pl.when(t == TILES_PER_SEQ - 1)
    def _():
        _state_store(state_ref, ssm_ref)


def _ssd_sample_kernel(z_ref, xbc_ref, dt_ref, dtp_ref, cprev_ref, sprev_ref,
                       cw_ref, cb_ref, dtb_ref, dtbp_ref,
                       alog_e_ref, alog_p_ref, dskip_e_ref, ng_ref, expand_ref, tril_ref, triu2_ref,
                       y_ref, ssm_ref, xw_ref, state_ref):
    xw_ref[0:CONV_HEAD, :] = jnp.zeros((CONV_HEAD, D_CONV), F32)
    xw_ref[CONV_HEAD - (CONV_W - 1):CONV_HEAD, :] = cprev_ref[0]
    for j in range(HEAD_PAIRS):
        state_ref[:, j * 128:(j + 1) * 128] = sprev_ref[0, j * 128:(j + 1) * 128, :].T
    _ssd_tile(1, z_ref, xbc_ref, dt_ref, dtp_ref, cw_ref, cb_ref, dtb_ref, dtbp_ref,
              alog_e_ref, alog_p_ref, dskip_e_ref, ng_ref, expand_ref, tril_ref, triu2_ref,
              y_ref, xw_ref, state_ref)
    _state_store(state_ref, ssm_ref)


def _ssd_const_specs():
    return [
        _const_spec((CONV_W, D_CONV)), _const_spec((1, D_CONV)),
        _const_spec((1, SSM_HEADS)), _const_spec((HEAD_PAIRS, 128)),
        _const_spec((1, D_SSM)), _const_spec((HEAD_PAIRS, 128)),
        _const_spec((1, D_SSM)), _const_spec((1, D_SSM)),
        _const_spec((SSM_HEADS, D_SSM)), _const_spec((CHUNK, CHUNK)), _const_spec((128, 128)),
    ]


def _ssd_prompt(z, xbc, dt, dtp, consts):
    tile = lambda b, t: (b * TILES_PER_SEQ + t, 0)
    return pl.pallas_call(
        _ssd_prompt_kernel,
        grid=(BATCH, TILES_PER_SEQ),
        in_specs=[
            pl.BlockSpec((TM, D_SSM), tile),
            pl.BlockSpec((TM, D_CONV), tile),
            pl.BlockSpec((TM, SSM_HEADS), tile),
            pl.BlockSpec((CHUNKS_PER_TILE, HEAD_PAIRS, 128), lambda b, t: (b * TILES_PER_SEQ + t, 0, 0)),
        ] + _ssd_const_specs(),
        out_specs=[
            pl.BlockSpec((TM, D_SSM), tile),
            pl.BlockSpec((1, D_SSM, D_STATE), lambda b, t: (b, 0, 0)),
        ],
        out_shape=[
            jax.ShapeDtypeStruct((N_PROMPT, D_SSM), BF16),
            jax.ShapeDtypeStruct((BATCH, D_SSM, D_STATE), F32),
        ],
        scratch_shapes=[
            pltpu.VMEM((CONV_HEAD + TM, D_CONV), F32),
            pltpu.VMEM((D_STATE, D_SSM), F32),
            pltpu.VMEM((CONV_HEAD, D_CONV), F32),
        ],
        compiler_params=_params(2),
        name="ssd_prompt",
    )(z, xbc, dt, dtp, *consts)


def _ssd_sample(z, xbc, dt, dtp, conv_prev, ssm_prev, consts):
    first = N_PROMPT // CHUNK
    row = lambda b: (first + b, 0)
    return pl.pallas_call(
        _ssd_sample_kernel,
        grid=(DEC_BATCH,),
        in_specs=[
            pl.BlockSpec((CHUNK, D_SSM), row),
            pl.BlockSpec((CHUNK, D_CONV), row),
            pl.BlockSpec((CHUNK, SSM_HEADS), row),
            pl.BlockSpec((1, HEAD_PAIRS, 128), lambda b: (first + b, 0, 0)),
            pl.BlockSpec((1, CONV_W - 1, D_CONV), lambda b: (b, 0, 0)),
            pl.BlockSpec((1, D_SSM, D_STATE), lambda b: (b, 0, 0)),
        ] + _ssd_const_specs(),
        out_specs=[
            pl.BlockSpec((CHUNK, D_SSM), lambda b: (b, 0)),
            pl.BlockSpec((1, D_SSM, D_STATE), lambda b: (b, 0, 0)),
        ],
        out_shape=[
            jax.ShapeDtypeStruct((N_SAMPLE, D_SSM), BF16),
            jax.ShapeDtypeStruct((DEC_BATCH, D_SSM, D_STATE), F32),
        ],
        scratch_shapes=[
            pltpu.VMEM((CONV_HEAD + CHUNK, D_CONV), F32),
            pltpu.VMEM((D_STATE, D_SSM), F32),
        ],
        compiler_params=_params(),
        name="ssd_sample",
    )(z, xbc, dt, dtp, conv_prev, ssm_prev, *consts)


def _attn_chunks(n_chunks, first_chunk, q_ref, kpad_ref, vpad_ref, bias_ref, o_ref):
    lane = lax.broadcasted_iota(jnp.int32, (CHUNK, 128), 1)
    low = lane < ATT_HEAD_DIM
    kj = lax.broadcasted_iota(jnp.int32, (2 * CHUNK, BAND), 1)

    def chunk(c, carry):
        r0 = pl.multiple_of(c * CHUNK, CHUNK)
        first_valid = jnp.maximum(LEFT_CHUNKS - (first_chunk + c), 0) * CHUNK
        valid = kj >= first_valid
        outs = []
        for j in range(ATT_PAIRS):
            qp = q_ref[pl.ds(r0, CHUNK), j * 128:(j + 1) * 128]
            zero = jnp.zeros_like(qp)
            q2 = jnp.concatenate([jnp.where(low, qp, zero), jnp.where(low, zero, qp)], axis=0)
            kb = kpad_ref[pl.ds(r0, BAND), j * 128:(j + 1) * 128]
            vb = vpad_ref[pl.ds(r0, BAND), j * 128:(j + 1) * 128]
            s = _dot_nt(q2, kb) + bias_ref[j]
            s = jnp.where(valid, s, -jnp.inf)
            m = jnp.max(s, axis=-1, keepdims=True)
            e = jnp.exp(s - m)
            denom = jnp.sum(e, axis=-1, keepdims=True)
            r = _dot(e.astype(BF16), vb) / denom
            outs.append(jnp.where(low, r[0:CHUNK], r[CHUNK:2 * CHUNK]))
        o_ref[pl.ds(r0, CHUNK), :] = jnp.concatenate(outs, axis=1).astype(BF16)
        return carry

    lax.fori_loop(0, n_chunks, chunk, 0, unroll=2 if n_chunks % 2 == 0 else 1)


def _attn_prompt_kernel(q_ref, k_ref, v_ref, bias_ref, o_ref, kpad_ref, vpad_ref):
    kpad_ref[0:ATT_LEFT, :] = jnp.zeros((ATT_LEFT, D_ATT), BF16)
    vpad_ref[0:ATT_LEFT, :] = jnp.zeros((ATT_LEFT, D_ATT), BF16)
    kpad_ref[ATT_LEFT:ATT_LEFT + SEQ, :] = k_ref[...]
    vpad_ref[ATT_LEFT:ATT_LEFT + SEQ, :] = v_ref[...]
    _attn_chunks(N_CHUNKS_SEQ, 0, q_ref, kpad_ref, vpad_ref, bias_ref, o_ref)


def _attn_sample_kernel(q_ref, k_ref, v_ref, ck_ref, cv_ref, bias_ref, o_ref, kpad_ref, vpad_ref):
    kpad_ref[0:ATT_LEFT, :] = ck_ref[0].astype(BF16)
    vpad_ref[0:ATT_LEFT, :] = cv_ref[0].astype(BF16)
    kpad_ref[ATT_LEFT:BAND, :] = k_ref[...]
    vpad_ref[ATT_LEFT:BAND, :] = v_ref[...]
    _attn_chunks(1, LEFT_CHUNKS, q_ref, kpad_ref, vpad_ref, bias_ref, o_ref)


def _attn_prompt(q, k, v, bias2):
    seq = pl.BlockSpec((SEQ, D_ATT), lambda b: (b, 0))
    return pl.pallas_call(
        _attn_prompt_kernel,
        grid=(BATCH,),
        in_specs=[seq, seq, seq, _const_spec((ATT_PAIRS, 2 * CHUNK, BAND))],
        out_specs=seq,
        out_shape=jax.ShapeDtypeStruct((N_PROMPT, D_ATT), BF16),
        scratch_shapes=[pltpu.VMEM((ATT_LEFT + SEQ, D_ATT), BF16),
                        pltpu.VMEM((ATT_LEFT + SEQ, D_ATT), BF16)],
        compiler_params=_params(),
        name="attn_prompt",
    )(q, k, v, bias2)


def _attn_sample(q, k, v, cache_k, cache_v, bias2):
    first = N_PROMPT // CHUNK
    row = pl.BlockSpec((CHUNK, D_ATT), lambda b: (first + b, 0))
    cache = pl.BlockSpec((1, ATT_LEFT, D_ATT), lambda b: (b, 0, 0))
    return pl.pallas_call(
        _attn_sample_kernel,
        grid=(DEC_BATCH,),
        in_specs=[row, row, row, cache, cache, _const_spec((ATT_PAIRS, 2 * CHUNK, BAND))],
        out_specs=pl.BlockSpec((CHUNK, D_ATT), lambda b: (b, 0)),
        out_shape=jax.ShapeDtypeStruct((N_SAMPLE, D_ATT), BF16),
        scratch_shapes=[pltpu.VMEM((BAND, D_ATT), BF16), pltpu.VMEM((BAND, D_ATT), BF16)],
        compiler_params=_params(),
        name="attn_sample",
    )(q, k, v, cache_k, cache_v, bias2)


def _outproj_kernel(xp_ref, xs_ref, yp_ref, ys_ref, op_ref, os_ref, ag_ref, wos_ref, woa_ref, fg_ref,
                    wr_ref, br_ref, ltri_ref,
                    xmid_ref, h_ref, idx_ref, gate_ref, rank_ref, cnt_ref, carry_ref):
    i = pl.program_id(0)

    @pl.when(i == 0)
    def _():
        carry_ref[...] = jnp.zeros_like(carry_ref)

    is_sample = i == N_PROMPT_TILES
    x = jnp.where(is_sample, xs_ref[...], xp_ref[...])
    y = jnp.where(is_sample, ys_ref[...], yp_ref[...])
    o = jnp.where(is_sample, os_ref[...], op_ref[...])
    o = _rms(o.astype(F32), ag_ref[...]).astype(BF16)
    xm = x + _dot(y, wos_ref[...]) + _dot(o, woa_ref[...])
    xmid_ref[...] = xm
    h = _rms(xm, fg_ref[...])
    _rows_to_tiles(h_ref, h)
    h1 = h.astype(BF16)
    h2 = (h - h1.astype(F32)).astype(BF16)
    w1 = wr_ref[0]
    w2 = wr_ref[1]
    logits = _dot(h1, w1) + (_dot(h1, w2) + _dot(h2, w1)) + br_ref[...]
    eidx = lax.broadcasted_iota(jnp.int32, (TM, N_EXPERTS), 1)
    lane = lax.broadcasted_iota(jnp.int32, (TM, 128), 1)
    work = logits
    vals, sels = [], []
    idx_out = jnp.zeros((TM, 128), jnp.int32)
    for k in range(TOP_K):
        m = jnp.max(work, axis=-1, keepdims=True)
        idx = jnp.min(jnp.where(work == m, eidx, N_EXPERTS), axis=-1, keepdims=True)
        sel = eidx == idx
        vals.append(m)
        sels.append(sel)
        idx_out = jnp.where(lane == k, idx, idx_out)
        work = jnp.where(sel, -jnp.inf, work)
    es = [jnp.exp(v - vals[0]) for v in vals]
    tot = es[0] + es[1] + es[2] + es[3]
    gate_out = jnp.zeros((TM, 128), F32)
    for k in range(TOP_K):
        gate_out = jnp.where(lane == k, es[k] / tot, gate_out)
    idx_ref[...] = idx_out
    gate_ref[...] = gate_out
    multi = jnp.zeros((TM, N_EXPERTS), F32)
    for sel in sels:
        multi = jnp.where(sel, 1.0, multi)
    before = _dot(ltri_ref[...], multi.astype(BF16)) + carry_ref[...]
    rank_out = jnp.zeros((TM, 128), jnp.int32)
    for k in range(TOP_K):
        rk = jnp.sum(jnp.where(sels[k], before, 0.0), axis=-1, keepdims=True).astype(jnp.int32)
        rank_out = jnp.where(lane == k, rk, rank_out)
    rank_ref[...] = rank_out
    carry_ref[...] = carry_ref[...] + jnp.sum(multi, axis=0, keepdims=True)
    cnt_ref[...] = carry_ref[...]


def _outproj(xp, xs, yp, ys, op, os_, ag, wos, woa, fg, wr, br, ltri):
    tok = lambda n: pl.BlockSpec((TM, n), lambda i: (i, 0))
    prompt = lambda n: pl.BlockSpec((TM, n), lambda i: (jnp.minimum(i, N_PROMPT_TILES - 1), 0))
    sample = lambda n: pl.BlockSpec((TM, n), lambda i: (0, 0))
    return pl.pallas_call(
        _outproj_kernel,
        grid=(N_TILES,),
        in_specs=[
            prompt(D_MODEL), sample(D_MODEL), prompt(D_SSM), sample(D_SSM), prompt(D_ATT), sample(D_ATT),
            _const_spec((1, D_ATT)),
            _const_spec((D_SSM, D_MODEL)), _const_spec((D_ATT, D_MODEL)),
            _const_spec((1, D_MODEL)),
            _const_spec((2, D_MODEL, N_EXPERTS)), _const_spec((1, N_EXPERTS)),
            _const_spec((TM, TM)),
        ],
        out_specs=[tok(D_MODEL), pl.BlockSpec(_tiled(TM), lambda i: (i, 0)), tok(128), tok(128), tok(128),
                   _const_spec((1, N_EXPERTS))],
        out_shape=[
            jax.ShapeDtypeStruct((N_TOK, D_MODEL), F32),
            jax.ShapeDtypeStruct(_tiled(N_TOK), F32),
            jax.ShapeDtypeStruct((N_TOK, 128), jnp.int32),
            jax.ShapeDtypeStruct((N_TOK, 128), F32),
            jax.ShapeDtypeStruct((N_TOK, 128), jnp.int32),
            jax.ShapeDtypeStruct((1, N_EXPERTS), F32),
        ],
        scratch_shapes=[pltpu.VMEM((1, N_EXPERTS), F32)],
        compiler_params=_params(),
        name="outproj_router",
    )(xp, xs, yp, ys, op, os_, ag, wos, woa, fg, wr, br, ltri)


def _scatter_kernel(dest_ref, pend_ref, h_ref, rows_ref, zero_ref, sem, zsem):
    i = pl.program_id(0)

    @pl.when(i == 0)
    def _():
        zero_ref[...] = jnp.zeros_like(zero_ref)

        block_tiles = _tiled(MOE_BM)[0]

        def zero_block(b):
            start = pl.multiple_of(b * block_tiles, block_tiles)
            return pltpu.make_async_copy(zero_ref, rows_ref.at[pl.ds(start, block_tiles)], zsem)

        def last_block(e):
            end = pend_ref[e]
            nonempty = end > (pend_ref[e - 1] if e > 0 else 0)
            return nonempty, zero_block(jnp.maximum(end // MOE_BM - 1, 0))

        for e in range(N_EXPERTS):
            nonempty, cp = last_block(e)
            pl.when(nonempty)(cp.start)
        for e in range(N_EXPERTS):
            nonempty, cp = last_block(e)
            pl.when(nonempty)(cp.wait)

        first_unused = pend_ref[N_EXPERTS - 1] // MOE_BM
        lax.fori_loop(first_unused, MOE_BLOCKS, lambda b, c: (zero_block(b).start(), c)[1], 0)
        lax.fori_loop(first_unused, MOE_BLOCKS, lambda b, c: (zero_block(b).wait(), c)[1], 0)

    def issue(r, carry):
        for k in range(TOP_K):
            d = dest_ref[r * TOP_K + k]
            pltpu.make_async_copy(_tile_of(h_ref, r), _tile_of(rows_ref, d), sem).start(priority=k % 2)
        return carry

    lax.fori_loop(0, TM, issue, 0)
    for _ in range(TOP_K):
        pltpu.make_async_copy(h_ref, rows_ref.at[pl.ds(0, _tiled(TM)[0])], sem).wait()


def _scatter_rows(dest_flat, pad_end, h):
    return pl.pallas_call(
        _scatter_kernel,
        grid=(N_TILES,),
        in_specs=[
            pl.BlockSpec((TM * TOP_K,), lambda i: (i,), memory_space=pltpu.SMEM),
            pl.BlockSpec((N_EXPERTS,), lambda i: (0,), memory_space=pltpu.SMEM),
            pl.BlockSpec(_tiled(TM), lambda i: (i, 0)),
        ],
        out_specs=pl.BlockSpec(memory_space=pl.ANY),
        out_shape=jax.ShapeDtypeStruct(_tiled(MOE_ROWS), F32),
        scratch_shapes=[pltpu.VMEM(_tiled(MOE_BM), F32), pltpu.SemaphoreType.DMA(()),
                        pltpu.SemaphoreType.DMA(())],
        compiler_params=_params(),
        name="moe_scatter",
    )(dest_flat, pad_end, h)


def _expert_kernel(be_ref, nu_ref, nxt_ref, x_ref, wgu_hbm, bgu_ref, wd_hbm, bd_ref, y_ref,
                   wgu_f, wd_f, wgu_s, wd_s, sem):
    i = pl.program_id(0)
    active = i < nu_ref[0]
    e = be_ref[i]

    def fetch(expert):
        return (pltpu.make_async_copy(wgu_hbm.at[expert], wgu_f, sem.at[0]),
                pltpu.make_async_copy(wd_hbm.at[expert], wd_f, sem.at[1]))

    @pl.when(active & (i == 0))
    def _():
        for cp in fetch(e):
            cp.start()

    @pl.when(active & ((i == 0) | (e != be_ref[jnp.maximum(i - 1, 0)])))
    def _():
        for cp in fetch(e):
            cp.wait()
        wgu_s[...] = wgu_f[...].astype(BF16)
        wd_s[...] = wd_f[...].astype(BF16)
        nxt = nxt_ref[e]

        @pl.when(nxt >= 0)
        def _():
            for cp in fetch(nxt):
                cp.start()

    @pl.when(active)
    def _():
        gu = _dot(_tiles_to_rows(x_ref, MOE_BM).astype(BF16), wgu_s[...]) + bgu_ref[0]
        gate = jnp.minimum(gu[:, :D_FF], SWIGLU_LIMIT)
        up = jnp.clip(gu[:, D_FF:], -SWIGLU_LIMIT, SWIGLU_LIMIT)
        act = (up + 1.0) * gate * _sigmoid(gate * SWIGLU_ALPHA)
        _rows_to_tiles(y_ref, _dot(act.astype(BF16), wd_s[...]) + bd_ref[0])

    @pl.when(jnp.logical_not(active))
    def _():
        y_ref[...] = jnp.zeros_like(y_ref)


def _experts(block_expert, n_used, next_expert, rows, wgu, bgu, wd, bd):
    grid_spec = pltpu.PrefetchScalarGridSpec(
        num_scalar_prefetch=3,
        grid=(MOE_BLOCKS,),
        in_specs=[
            pl.BlockSpec(_tiled(MOE_BM), lambda i, be, nu, nx: (jnp.minimum(i, nu[0] - 1), 0)),
            pl.BlockSpec(memory_space=pl.ANY),
            pl.BlockSpec((1, 1, 2 * D_FF), lambda i, be, nu, nx: (be[i], 0, 0)),
            pl.BlockSpec(memory_space=pl.ANY),
            pl.BlockSpec((1, 1, D_MODEL), lambda i, be, nu, nx: (be[i], 0, 0)),
        ],
        out_specs=pl.BlockSpec(_tiled(MOE_BM), lambda i, be, nu, nx: (i, 0)),
        scratch_shapes=[pltpu.VMEM((D_MODEL, 2 * D_FF), F32), pltpu.VMEM((D_FF, D_MODEL), F32),
                        pltpu.VMEM((D_MODEL, 2 * D_FF), BF16), pltpu.VMEM((D_FF, D_MODEL), BF16),
                        pltpu.SemaphoreType.DMA((2,))],
    )
    return pl.pallas_call(
        _expert_kernel,
        grid_spec=grid_spec,
        out_shape=jax.ShapeDtypeStruct(_tiled(MOE_ROWS), F32),
        compiler_params=_params(),
        name="moe_experts",
    )(block_expert, n_used, next_expert, rows, wgu, bgu, wd, bd)


def _combine_kernel(dest_ref, dest_next_ref, gate_ref, xmid_ref, g_ref, rows_ref, yp_ref, ys_ref, buf_ref, sem):
    i = pl.program_id(0)
    slot = i % 2

    def issue(idx_ref, s):
        def body(r, carry):
            for k in range(TOP_K):
                d = idx_ref[r * TOP_K + k]
                pltpu.make_async_copy(_tile_of(rows_ref, d), _tile_of(buf_ref.at[s], k * TM + r),
                                      sem.at[s]).start(priority=k % 2)
            return carry

        lax.fori_loop(0, TM, body, 0)

    @pl.when(i == 0)
    def _():
        issue(dest_ref, 0)

    @pl.when(i + 1 < N_TILES)
    def _():
        issue(dest_next_ref, 1 - slot)

    buf = buf_ref.at[slot]
    slot_tiles = _tiled(TM)[0]
    for k in range(TOP_K):
        pltpu.make_async_copy(rows_ref.at[pl.ds(0, slot_tiles)], buf.at[pl.ds(k * slot_tiles, slot_tiles)],
                              sem.at[slot]).wait()
    acc = xmid_ref[...]
    for k in range(TOP_K):
        acc = acc + _tiles_to_rows(buf, TM, first=k * TM) * gate_ref[:, k:k + 1]
    y = _rms(acc, g_ref[...])

    @pl.when(i < N_PROMPT_TILES)
    def _():
        yp_ref[...] = y

    @pl.when(i == N_PROMPT_TILES)
    def _():
        ys_ref[...] = y


def _combine(dest_flat, gates, xmid, g, y_rows):
    return pl.pallas_call(
        _combine_kernel,
        grid=(N_TILES,),
        in_specs=[
            pl.BlockSpec((TM * TOP_K,), lambda i: (i,), memory_space=pltpu.SMEM),
            pl.BlockSpec((TM * TOP_K,), lambda i: (jnp.minimum(i + 1, N_TILES - 1),), memory_space=pltpu.SMEM),
            pl.BlockSpec((TM, 128), lambda i: (i, 0)),
            pl.BlockSpec((TM, D_MODEL), lambda i: (i, 0)),
            _const_spec((1, D_MODEL)),
            pl.BlockSpec(memory_space=pl.ANY),
        ],
        out_specs=[
            pl.BlockSpec((TM, D_MODEL), lambda i: (jnp.minimum(i, N_PROMPT_TILES - 1), 0)),
            pl.BlockSpec((TM, D_MODEL), lambda i: (0, 0)),
        ],
        out_shape=[
            jax.ShapeDtypeStruct((N_PROMPT, D_MODEL), F32),
            jax.ShapeDtypeStruct((N_SAMPLE, D_MODEL), F32),
        ],
        scratch_shapes=[pltpu.VMEM((2,) + _tiled(TOP_K * TM), F32), pltpu.SemaphoreType.DMA((2,))],
        compiler_params=_params(),
        name="moe_combine",
    )(dest_flat, dest_flat, gates, xmid, g, y_rows)


def _band_bias(table):
    n_diag = BAND + CHUNK - 1
    idx = np.clip(ATT_LEFT + (CHUNK - 1) - np.arange(n_diag), -REL_CLIP, REL_CLIP) + REL_CLIP
    pick = (np.arange(2 * REL_CLIP + 1)[:, None] == idx[None, :]).astype(np.float32)
    diag = jnp.dot(table, jnp.asarray(pick), precision=lax.Precision.HIGHEST)
    return jnp.stack([diag[:, CHUNK - 1 - qi:CHUNK - 1 - qi + BAND] for qi in range(CHUNK)], axis=1)


def _pair_rows(v):
    return jnp.repeat(v.reshape(HEAD_PAIRS, 2), CHUNK, axis=1)


def _layer(l, xp, xs, cache_k, cache_v, state_conv, state_ssm,
           norm_mix_g, w_in, conv_w, conv_b, dt_bias, a_log, d_skip, ssm_norm_g,
           att_norm_g, rel_bias_table, w_out, norm_ffn_g, w_router, b_router,
           w_gate_up, b_gate_up, w_down, b_down, norm_final_g):
    wb = w_in[l].astype(BF16)
    c0 = D_SSM
    c1 = c0 + D_CONV
    c2 = c1 + SSM_HEADS
    c3 = c2 + D_ATT
    c4 = c3 + D_ATT
    z, xbc, dt, dtt, q, k, v, kf, vf, ctail = _inproj(
        xp, xs, norm_mix_g[l][None], wb[:, :c0], wb[:, c0:c1], wb[:, c1:c2], wb[:, c1:c2].T,
        wb[:, c2:c3], wb[:, c3:c4], wb[:, c4:])

    n_chunks = N_TOK // CHUNK
    dtp = dtt.reshape(HEAD_PAIRS, 2, n_chunks, CHUNK).transpose(2, 0, 1, 3).reshape(n_chunks, HEAD_PAIRS, 128)
    hp = jnp.arange(D_SSM) // SSM_HEAD_DIM
    expand = (hp[None, :] == jnp.arange(SSM_HEADS)[:, None]).astype(BF16)
    tril = jnp.tril(jnp.ones((CHUNK, CHUNK), BF16))
    r128 = jnp.arange(128)
    triu2 = ((r128[:, None] // CHUNK == r128[None, :] // CHUNK) & (r128[:, None] <= r128[None, :])).astype(BF16)
    consts = (conv_w[l], conv_b[l][None], dt_bias[l][None], _pair_rows(dt_bias[l]),
              jnp.repeat(a_log[l], SSM_HEAD_DIM)[None], _pair_rows(a_log[l]),
              jnp.repeat(d_skip[l], SSM_HEAD_DIM)[None], ssm_norm_g[l][None],
              expand, tril, triu2)
    y_ssm_p, ssm_p = _ssd_prompt(z, xbc, dt, dtp, consts)
    y_ssm_s, ssm_s = _ssd_sample(z, xbc, dt, dtp, state_conv[l],
                                 state_ssm[l].reshape(DEC_BATCH, D_SSM, D_STATE), consts)

    bias2 = _band_bias(rel_bias_table[l]).reshape(ATT_PAIRS, 2 * CHUNK, BAND)
    o_att_p = _attn_prompt(q, k, v, bias2)
    o_att_s = _attn_sample(q, k, v, cache_k[l].reshape(DEC_BATCH, ATT_LEFT, D_ATT),
                           cache_v[l].reshape(DEC_BATCH, ATT_LEFT, D_ATT), bias2)

    wo = w_out[l].astype(BF16)
    wr = w_router[l]
    wr1 = wr.astype(BF16)
    wr2 = (wr - wr1.astype(F32)).astype(BF16)
    ltri = jnp.tril(jnp.ones((TM, TM), BF16), -1)
    xmid, h, top_idx, gates, rank, counts = _outproj(
        xp, xs, y_ssm_p, y_ssm_s, o_att_p, o_att_s, att_norm_g[l][None], wo[:D_SSM], wo[D_SSM:], norm_ffn_g[l][None],
        jnp.stack([wr1, wr2]), b_router[l][None], ltri)

    counts = counts[0].astype(jnp.int32)
    padded = (counts + MOE_BM - 1) // MOE_BM * MOE_BM
    pad_end = jnp.cumsum(padded)
    pad_start = pad_end - padded
    experts = jnp.arange(N_EXPERTS, dtype=jnp.int32)
    start_of = jnp.sum(jnp.where(top_idx[:, :TOP_K, None] == experts, pad_start, 0), axis=-1)
    dest = (start_of + rank[:, :TOP_K]).reshape(-1).astype(jnp.int32)
    block_start = jnp.arange(MOE_BLOCKS, dtype=jnp.int32) * MOE_BM
    block_expert = jnp.minimum(jnp.sum((pad_end[None, :] <= block_start[:, None]).astype(jnp.int32), axis=1),
                               N_EXPERTS - 1).astype(jnp.int32)
    n_used = (pad_end[-1:] // MOE_BM).astype(jnp.int32)
    later_nonempty = (experts[None, :] > experts[:, None]) & (padded[None, :] > 0)
    next_expert = jnp.min(jnp.where(later_nonempty, experts[None, :], N_EXPERTS), axis=1)
    next_expert = jnp.where(next_expert < N_EXPERTS, next_expert, -1).astype(jnp.int32)

    rows = _scatter_rows(dest, pad_end.astype(jnp.int32), h)
    y_rows = _experts(block_expert, n_used, next_expert, rows, w_gate_up[l], b_gate_up[l][:, None, :],
                      w_down[l], b_down[l][:, None, :])
    y_p, y_s = _combine(dest, gates, xmid, norm_final_g[None], y_rows)

    keep = min(ATT_LEFT, SEQ)
    k_p = kf[:BATCH, TM - keep:].reshape(BATCH, keep, ATT_HEADS, ATT_HEAD_DIM)
    v_p = vf[:BATCH, TM - keep:].reshape(BATCH, keep, ATT_HEADS, ATT_HEAD_DIM)
    k_s = kf[BATCH].reshape(DEC_BATCH, DEC_SEQ, ATT_HEADS, ATT_HEAD_DIM)
    v_s = vf[BATCH].reshape(DEC_BATCH, DEC_SEQ, ATT_HEADS, ATT_HEAD_DIM)
    conv_p = ctail[:BATCH, -(CONV_W - 1):]
    conv_s = ctail[BATCH].reshape(DEC_BATCH, 8, D_CONV)[:, -(CONV_W - 1):]
    ssm_p = ssm_p.reshape(BATCH, SSM_HEADS, SSM_HEAD_DIM, D_STATE)
    ssm_s = ssm_s.reshape(DEC_BATCH, SSM_HEADS, SSM_HEAD_DIM, D_STATE)
    return (y_p.reshape(BATCH, SEQ, D_MODEL), y_s.reshape(DEC_BATCH, DEC_SEQ, D_MODEL),
            k_p, v_p, conv_p, ssm_p, k_s, v_s, conv_s, ssm_s)


def kernel(x_prompt, x_sample, cache_k, cache_v, state_conv, state_ssm, norm_mix_g, w_in, conv_w, conv_b,
           dt_bias, a_log, d_skip, ssm_norm_g, att_norm_g, rel_bias_table, w_out, norm_ffn_g, w_router,
           b_router, w_gate_up, b_gate_up, w_down, b_down, norm_final_g):
    assert w_in.shape[0] == 1, "single trunk layer"
    xp = x_prompt.reshape(N_PROMPT, D_MODEL)
    xs = x_sample.reshape(N_SAMPLE, D_MODEL)
    outs = _layer(0, xp, xs, cache_k, cache_v, state_conv, state_ssm,
                  norm_mix_g, w_in, conv_w, conv_b, dt_bias, a_log, d_skip, ssm_norm_g,
                  att_norm_g, rel_bias_table, w_out, norm_ffn_g, w_router, b_router,
                  w_gate_up, b_gate_up, w_down, b_down, norm_final_g)
    y_p, y_s, k_p, v_p, conv_p, ssm_p, k_s, v_s, conv_s, ssm_s = outs
    return (y_p, y_s, k_p[None], v_p[None], conv_p[None], ssm_p[None],
            k_s[None], v_s[None], conv_s[None], ssm_s[None])
```

```python
import jax
import jax.numpy as jnp
import numpy as np
from jax import lax
from jax.experimental import pallas as pl
from jax.experimental.pallas import tpu as pltpu

D_MODEL = 1024
BATCH = 8
SEQ = 2048
DEC_BATCH = 8
DEC_SEQ = 64
CHUNK = 64
SSM_HEADS = 16
SSM_HEAD_DIM = 64
D_SSM = SSM_HEADS * SSM_HEAD_DIM
SSM_GROUPS = 2
D_STATE = 128
CONV_W = 4
D_BC = SSM_GROUPS * D_STATE
D_CONV = D_SSM + 2 * D_BC
ATT_HEADS = 8
ATT_HEAD_DIM = 64
D_ATT = ATT_HEADS * ATT_HEAD_DIM
LEFT_CHUNKS = 8
ATT_LEFT = LEFT_CHUNKS * CHUNK
BAND = ATT_LEFT + CHUNK
REL_CLIP = 128
ATT_SCALE = ATT_HEAD_DIM ** -0.5
N_EXPERTS = 32
TOP_K = 4
D_FF = D_MODEL
SWIGLU_ALPHA = 1.702
SWIGLU_LIMIT = 7.0
EPS = 1e-5

F32 = jnp.float32
BF16 = jnp.bfloat16
U32 = jnp.uint32

N_PROMPT = BATCH * SEQ
N_SAMPLE = DEC_BATCH * DEC_SEQ
N_TOK = N_PROMPT + N_SAMPLE
TM = 512
N_PROMPT_TILES = N_PROMPT // TM
N_TILES = N_TOK // TM
TILES_PER_SEQ = SEQ // TM
CHUNKS_PER_TILE = TM // CHUNK
N_CHUNKS_SEQ = SEQ // CHUNK
HEAD_PAIRS = SSM_HEADS // 2
PAIRS_PER_GROUP = HEAD_PAIRS // SSM_GROUPS
ATT_PAIRS = ATT_HEADS // 2
CONV_HEAD = 8
MOE_BM = 256
N_ASSIGN = N_TOK * TOP_K
MOE_BLOCKS = N_ASSIGN // MOE_BM + N_EXPERTS
MOE_ROWS = MOE_BLOCKS * MOE_BM
ISSUE_UNROLL = 8
ROW_TILE = (4, 128)
assert 2 * ROW_TILE[0] * ROW_TILE[1] == D_MODEL
VMEM_LIMIT = 56 * 1024 * 1024


def _dot(a, b):
    return jnp.dot(a, b, preferred_element_type=F32)


def _dot_nt(a, b):
    return lax.dot_general(a, b, (((1,), (1,)), ((), ())), preferred_element_type=F32)


def _dot_tn(a, b):
    return lax.dot_general(a, b, (((0,), (0,)), ((), ())), preferred_element_type=F32)


def _split3(x):
    x1 = x.astype(BF16)
    r1 = x - x1.astype(F32)
    x2 = r1.astype(BF16)
    r2 = r1 - x2.astype(F32)
    return x1, x2, r2.astype(BF16)


def _dot_exact_rhs(x, m):
    x1, x2, x3 = _split3(x)
    return _dot(x1, m) + _dot(x2, m) + _dot(x3, m)


def _dot_exact_lhs(m, x):
    x1, x2, x3 = _split3(x)
    return _dot(m, x1) + _dot(m, x2) + _dot(m, x3)


def _rms(x, g):
    return x * lax.rsqrt(jnp.mean(x * x, axis=-1, keepdims=True) + EPS) * g


def _sigmoid(x):
    return 1.0 / (1.0 + jnp.exp(-x))


def _softplus(x):
    return jnp.maximum(x, 0.0) + jnp.log(1.0 + jnp.exp(-jnp.abs(x)))


def _tiled(n):
    return (n * ROW_TILE[0], ROW_TILE[1])


def _tile_of(ref, row):
    return ref.at[pl.ds(pl.multiple_of(row * ROW_TILE[0], ROW_TILE[0]), ROW_TILE[0])]


def _rows_to_tiles(ref, x, first=0):
    sub, lanes = ROW_TILE
    half = D_MODEL // 2
    hi = lax.bitcast_convert_type(x[:, :half].astype(BF16).astype(F32), U32)
    lo = lax.bitcast_convert_type(x[:, half:].astype(BF16).astype(F32), U32)
    words = hi | (lo >> 16)
    for j in range(sub):
        ref[pl.ds(first * sub + j, x.shape[0], stride=sub), :] = words[:, j * lanes:(j + 1) * lanes]


def _tiles_to_rows(ref, n, first=0):
    sub = ROW_TILE[0]
    words = jnp.concatenate([ref[pl.ds(first * sub + j, n, stride=sub), :] for j in range(sub)], axis=1)
    hi = lax.bitcast_convert_type(words & jnp.uint32(0xFFFF0000), F32)
    lo = lax.bitcast_convert_type(words << 16, F32)
    return jnp.concatenate([hi, lo], axis=1)


def _const_spec(shape):
    nd = len(shape)
    return pl.BlockSpec(shape, lambda *_: (0,) * nd)


def _params(n_axes=1):
    return pltpu.CompilerParams(dimension_semantics=("arbitrary",) * n_axes,
                                vmem_limit_bytes=VMEM_LIMIT)


def _inproj_kernel(xp_ref, xs_ref, g_ref, wz_ref, wxbc_ref, wdt_ref, wdtt_ref, wq_ref, wk_ref, wv_ref,
                   z_ref, xbc_ref, dt_ref, dtt_ref, q_ref, k_ref, v_ref, kf_ref, vf_ref, ctail_ref):
    i = pl.program_id(0)
    x = jnp.where(i == N_PROMPT_TILES, xs_ref[...], xp_ref[...])
    h = _rms(x, g_ref[...]).astype(BF16)
    z_ref[...] = _dot(h, wz_ref[...]).astype(BF16)
    xbc = _dot(h, wxbc_ref[...])
    xbc_ref[...] = xbc.astype(BF16)
    for c in range(CHUNKS_PER_TILE):
        ctail_ref[0, c * 8:(c + 1) * 8, :] = xbc[c * CHUNK + CHUNK - 8:(c + 1) * CHUNK, :]
    dt_ref[...] = _dot(h, wdt_ref[...])
    dtt_ref[...] = _dot_nt(wdtt_ref[...], h)
    q_ref[...] = (_dot(h, wq_ref[...]) * ATT_SCALE).astype(BF16)
    k = _dot(h, wk_ref[...])
    v = _dot(h, wv_ref[...])
    k_ref[...] = k.astype(BF16)
    v_ref[...] = v.astype(BF16)
    kf_ref[0] = k
    vf_ref[0] = v


def _inproj(xp, xs, g, wz, wxbc, wdt, wdtt, wq, wk, wv):
    tok = lambda n: pl.BlockSpec((TM, n), lambda i: (i, 0))
    tail_idx = lambda i: (i // TILES_PER_SEQ, 0, 0)
    n_tail = BATCH + 1
    return pl.pallas_call(
        _inproj_kernel,
        grid=(N_TILES,),
        in_specs=[
            pl.BlockSpec((TM, D_MODEL), lambda i: (jnp.minimum(i, N_PROMPT_TILES - 1), 0)),
            pl.BlockSpec((TM, D_MODEL), lambda i: (0, 0)),
            _const_spec((1, D_MODEL)),
            _const_spec((D_MODEL, D_SSM)),
            _const_spec((D_MODEL, D_CONV)),
            _const_spec((D_MODEL, SSM_HEADS)),
            _const_spec((SSM_HEADS, D_MODEL)),
            _const_spec((D_MODEL, D_ATT)),
            _const_spec((D_MODEL, D_ATT)),
            _const_spec((D_MODEL, D_ATT)),
        ],
        out_specs=[
            tok(D_SSM), tok(D_CONV), tok(SSM_HEADS),
            pl.BlockSpec((SSM_HEADS, TM), lambda i: (0, i)),
            tok(D_ATT), tok(D_ATT), tok(D_ATT),
            pl.BlockSpec((1, TM, D_ATT), tail_idx),
            pl.BlockSpec((1, TM, D_ATT), tail_idx),
            pl.BlockSpec((1, CHUNKS_PER_TILE * 8, D_CONV), tail_idx),
        ],
        out_shape=[
            jax.ShapeDtypeStruct((N_TOK, D_SSM), BF16),
            jax.ShapeDtypeStruct((N_TOK, D_CONV), BF16),
            jax.ShapeDtypeStruct((N_TOK, SSM_HEADS), F32),
            jax.ShapeDtypeStruct((SSM_HEADS, N_TOK), F32),
            jax.ShapeDtypeStruct((N_TOK, D_ATT), BF16),
            jax.ShapeDtypeStruct((N_TOK, D_ATT), BF16),
            jax.ShapeDtypeStruct((N_TOK, D_ATT), BF16),
            jax.ShapeDtypeStruct((n_tail, TM, D_ATT), F32),
            jax.ShapeDtypeStruct((n_tail, TM, D_ATT), F32),
            jax.ShapeDtypeStruct((n_tail, CHUNKS_PER_TILE * 8, D_CONV), F32),
        ],
        compiler_params=_params(),
        name="inproj",
    )(xp, xs, g, wz, wxbc, wdt, wdtt, wq, wk, wv)


def _ssd_tile(n_chunks, z_ref, xbc_ref, dt_ref, dtp_ref, cw_ref, cb_ref, dtb_ref, dtbp_ref,
              alog_e_ref, alog_p_ref, dskip_e_ref, ng_ref, expand_ref, tril_ref, triu2_ref,
              y_ref, xw_ref, state_ref):
    rows = n_chunks * CHUNK
    xw_ref[CONV_HEAD:CONV_HEAD + rows, :] = xbc_ref[...].astype(F32)

    a_e = -jnp.exp(alog_e_ref[...])
    a_p = -jnp.exp(alog_p_ref[...])
    row_i = lax.broadcasted_iota(jnp.int32, (CHUNK, 128), 0)
    col_i = lax.broadcasted_iota(jnp.int32, (CHUNK, 128), 1)
    causal2 = row_i >= (col_i % CHUNK)
    bd_r = lax.broadcasted_iota(jnp.int32, (128, 128), 0) // CHUNK
    bd_c = lax.broadcasted_iota(jnp.int32, (128, 128), 1) // CHUNK
    blockdiag = bd_r == bd_c

    def chunk(c, carry):
        r0 = pl.multiple_of(c * CHUNK, CHUNK)
        win = xw_ref.at[pl.ds(r0, CONV_HEAD + CHUNK), :]
        acc = cb_ref[...]
        for tap in range(CONV_W):
            off = CONV_HEAD - (CONV_W - 1) + tap
            acc = acc + cw_ref[tap:tap + 1, :] * win[off:off + CHUNK, :]
        xa = acc * _sigmoid(acc)
        xs = xa[:, 0:D_SSM]
        bm = xa[:, D_SSM:D_SSM + D_BC].astype(BF16)
        cm = xa[:, D_SSM + D_BC:D_CONV].astype(BF16)
        dt = _softplus(dt_ref[pl.ds(r0, CHUNK), :] + dtb_ref[...])
        dt_e = _dot_exact_rhs(dt, expand_ref[...])
        acum = _dot_exact_lhs(tril_ref[...], dt_e * a_e)
        dtp = _softplus(dtp_ref[c] + dtbp_ref[...])
        acum_p = _dot_exact_rhs(dtp * a_p, triu2_ref[...])
        xdt = xs * dt_e
        a_last = acum[CHUNK - 1:CHUNK, :]
        xdt_end = (xdt * jnp.exp(a_last - acum)).astype(BF16)
        y_parts = []
        for g in range(SSM_GROUPS):
            bg = bm[:, g * D_STATE:(g + 1) * D_STATE]
            cg = cm[:, g * D_STATE:(g + 1) * D_STATE]
            cb2 = _dot_nt(cg, jnp.concatenate([bg, bg], axis=0))
            for jj in range(PAIRS_PER_GROUP):
                j = g * PAIRS_PER_GROUP + jj
                seg = acum[:, j * 128:(j + 1) * 128] - acum_p[j:j + 1, :]
                decay = jnp.exp(jnp.where(causal2, seg, -jnp.inf))
                s_pair = (cb2 * decay).astype(BF16)
                x2 = xdt[:, j * 128:(j + 1) * 128]
                rhs = jnp.where(blockdiag, jnp.concatenate([x2, x2], axis=0), 0.0).astype(BF16)
                y_parts.append(_dot(s_pair, rhs))
        y_diag = jnp.concatenate(y_parts, axis=1)
        half = D_SSM // SSM_GROUPS
        y_off = jnp.concatenate(
            [_dot(cm[:, g * D_STATE:(g + 1) * D_STATE], state_ref[:, g * half:(g + 1) * half].astype(BF16))
             for g in range(SSM_GROUPS)], axis=1)
        new_s = jnp.concatenate(
            [_dot_tn(bm[:, g * D_STATE:(g + 1) * D_STATE], xdt_end[:, g * half:(g + 1) * half])
             for g in range(SSM_GROUPS)], axis=1)
        state_ref[...] = state_ref[...] * jnp.exp(a_last) + new_s
        y = y_diag + y_off * jnp.exp(acum) + dskip_e_ref[...] * xs
        zc = z_ref[pl.ds(r0, CHUNK), :].astype(F32)
        y = y * (zc * _sigmoid(zc))
        yn = jnp.concatenate(
            [y[:, g * half:(g + 1) * half]
             * lax.rsqrt(jnp.mean(jnp.square(y[:, g * half:(g + 1) * half]), axis=-1, keepdims=True) + EPS)
             for g in range(SSM_GROUPS)], axis=1)
        y_ref[pl.ds(r0, CHUNK), :] = (yn * ng_ref[...]).astype(BF16)
        return carry

    lax.fori_loop(0, n_chunks, chunk, 0)


def _state_store(state_ref, out_ref):
    for j in range(HEAD_PAIRS):
        out_ref[0, j * 128:(j + 1) * 128, :] = state_ref[:, j * 128:(j + 1) * 128].T


def _ssd_prompt_kernel(z_ref, xbc_ref, dt_ref, dtp_ref, cw_ref, cb_ref, dtb_ref, dtbp_ref,
                       alog_e_ref, alog_p_ref, dskip_e_ref, ng_ref, expand_ref, tril_ref, triu2_ref,
                       y_ref, ssm_ref, xw_ref, state_ref, tail_ref):
    t = pl.program_id(1)

    @pl.when(t == 0)
    def _():
        state_ref[...] = jnp.zeros_like(state_ref)
        xw_ref[0:CONV_HEAD, :] = jnp.zeros((CONV_HEAD, D_CONV), F32)

    @pl.when(t > 0)
    def _():
        xw_ref[0:CONV_HEAD, :] = tail_ref[...]

    _ssd_tile(CHUNKS_PER_TILE, z_ref, xbc_ref, dt_ref, dtp_ref, cw_ref, cb_ref, dtb_ref, dtbp_ref,
              alog_e_ref, alog_p_ref, dskip_e_ref, ng_ref, expand_ref, tril_ref, triu2_ref,
              y_ref, xw_ref, state_ref)
    tail_ref[...] = xw_ref[TM:TM + CONV_HEAD, :]

    @pl.when(t == TILES_PER_SEQ - 1)
    def _():
        _state_store(state_ref, ssm_ref)


def _ssd_sample_kernel(z_ref, xbc_ref, dt_ref, dtp_ref, cprev_ref, sprev_ref,
                       cw_ref, cb_ref, dtb_ref, dtbp_ref,
                       alog_e_ref, alog_p_ref, dskip_e_ref, ng_ref, expand_ref, tril_ref, triu2_ref,
                       y_ref, ssm_ref, xw_ref, state_ref):
    xw_ref[0:CONV_HEAD, :] = jnp.zeros((CONV_HEAD, D_CONV), F32)
    xw_ref[CONV_HEAD - (CONV_W - 1):CONV_HEAD, :] = cprev_ref[0]
    for j in range(HEAD_PAIRS):
        state_ref[:, j * 128:(j + 1) * 128] = sprev_ref[0, j * 128:(j + 1) * 128, :].T
    _ssd_tile(1, z_ref, xbc_ref, dt_ref, dtp_ref, cw_ref, cb_ref, dtb_ref, dtbp_ref,
              alog_e_ref, alog_p_ref, dskip_e_ref, ng_ref, expand_ref, tril_ref, triu2_ref,
              y_ref, xw_ref, state_ref)
    _state_store(state_ref, ssm_ref)


def _ssd_const_specs():
    return [
        _const_spec((CONV_W, D_CONV)), _const_spec((1, D_CONV)),
        _const_spec((1, SSM_HEADS)), _const_spec((HEAD_PAIRS, 128)),
        _const_spec((1, D_SSM)), _const_spec((HEAD_PAIRS, 128)),
        _const_spec((1, D_SSM)), _const_spec((1, D_SSM)),
        _const_spec((SSM_HEADS, D_SSM)), _const_spec((CHUNK, CHUNK)), _const_spec((128, 128)),
    ]


def _ssd_prompt(z, xbc, dt, dtp, consts):
    tile = lambda b, t: (b * TILES_PER_SEQ + t, 0)
    return pl.pallas_call(
        _ssd_prompt_kernel,
        grid=(BATCH, TILES_PER_SEQ),
        in_specs=[
            pl.BlockSpec((TM, D_SSM), tile),
            pl.BlockSpec((TM, D_CONV), tile),
            pl.BlockSpec((TM, SSM_HEADS), tile),
            pl.BlockSpec((CHUNKS_PER_TILE, HEAD_PAIRS, 128), lambda b, t: (b * TILES_PER_SEQ + t, 0, 0)),
        ] + _ssd_const_specs(),
        out_specs=[
            pl.BlockSpec((TM, D_SSM), tile),
            pl.BlockSpec((1, D_SSM, D_STATE), lambda b, t: (b, 0, 0)),
        ],
        out_shape=[
            jax.ShapeDtypeStruct((N_PROMPT, D_SSM), BF16),
            jax.ShapeDtypeStruct((BATCH, D_SSM, D_STATE), F32),
        ],
        scratch_shapes=[
            pltpu.VMEM((CONV_HEAD + TM, D_CONV), F32),
            pltpu.VMEM((D_STATE, D_SSM), F32),
            pltpu.VMEM((CONV_HEAD, D_CONV), F32),
        ],
        compiler_params=_params(2),
        name="ssd_prompt",
    )(z, xbc, dt, dtp, *consts)


def _ssd_sample(z, xbc, dt, dtp, conv_prev, ssm_prev, consts):
    first = N_PROMPT // CHUNK
    row = lambda b: (first + b, 0)
    return pl.pallas_call(
        _ssd_sample_kernel,
        grid=(DEC_BATCH,),
        in_specs=[
            pl.BlockSpec((CHUNK, D_SSM), row),
            pl.BlockSpec((CHUNK, D_CONV), row),
            pl.BlockSpec((CHUNK, SSM_HEADS), row),
            pl.BlockSpec((1, HEAD_PAIRS, 128), lambda b: (first + b, 0, 0)),
            pl.BlockSpec((1, CONV_W - 1, D_CONV), lambda b: (b, 0, 0)),
            pl.BlockSpec((1, D_SSM, D_STATE), lambda b: (b, 0, 0)),
        ] + _ssd_const_specs(),
        out_specs=[
            pl.BlockSpec((CHUNK, D_SSM), lambda b: (b, 0)),
            pl.BlockSpec((1, D_SSM, D_STATE), lambda b: (b, 0, 0)),
        ],
        out_shape=[
            jax.ShapeDtypeStruct((N_SAMPLE, D_SSM), BF16),
            jax.ShapeDtypeStruct((DEC_BATCH, D_SSM, D_STATE), F32),
        ],
        scratch_shapes=[
            pltpu.VMEM((CONV_HEAD + CHUNK, D_CONV), F32),
            pltpu.VMEM((D_STATE, D_SSM), F32),
        ],
        compiler_params=_params(),
        name="ssd_sample",
    )(z, xbc, dt, dtp, conv_prev, ssm_prev, *consts)


def _attn_chunks(n_chunks, first_chunk, q_ref, kpad_ref, vpad_ref, bias_ref, o_ref):
    lane = lax.broadcasted_iota(jnp.int32, (CHUNK, 128), 1)
    low = lane < ATT_HEAD_DIM
    kj = lax.broadcasted_iota(jnp.int32, (2 * CHUNK, BAND), 1)

    def chunk(c, carry):
        r0 = pl.multiple_of(c * CHUNK, CHUNK)
        first_valid = jnp.maximum(LEFT_CHUNKS - (first_chunk + c), 0) * CHUNK
        valid = kj >= first_valid
        outs = []
        for j in range(ATT_PAIRS):
            qp = q_ref[pl.ds(r0, CHUNK), j * 128:(j + 1) * 128]
            zero = jnp.zeros_like(qp)
            q2 = jnp.concatenate([jnp.where(low, qp, zero), jnp.where(low, zero, qp)], axis=0)
            kb = kpad_ref[pl.ds(r0, BAND), j * 128:(j + 1) * 128]
            vb = vpad_ref[pl.ds(r0, BAND), j * 128:(j + 1) * 128]
            s = _dot_nt(q2, kb) + bias_ref[j]
            s = jnp.where(valid, s, -jnp.inf)
            m = jnp.max(s, axis=-1, keepdims=True)
            e = jnp.exp(s - m)
            denom = jnp.sum(e, axis=-1, keepdims=True)
            r = _dot(e.astype(BF16), vb) / denom
            outs.append(jnp.where(low, r[0:CHUNK], r[CHUNK:2 * CHUNK]))
        o_ref[pl.ds(r0, CHUNK), :] = jnp.concatenate(outs, axis=1).astype(BF16)
        return carry

    lax.fori_loop(0, n_chunks, chunk, 0, unroll=2 if n_chunks % 2 == 0 else 1)


def _attn_prompt_kernel(q_ref, k_ref, v_ref, bias_ref, o_ref, kpad_ref, vpad_ref):
    kpad_ref[0:ATT_LEFT, :] = jnp.zeros((ATT_LEFT, D_ATT), BF16)
    vpad_ref[0:ATT_LEFT, :] = jnp.zeros((ATT_LEFT, D_ATT), BF16)
    kpad_ref[ATT_LEFT:ATT_LEFT + SEQ, :] = k_ref[...]
    vpad_ref[ATT_LEFT:ATT_LEFT + SEQ, :] = v_ref[...]
    _attn_chunks(N_CHUNKS_SEQ, 0, q_ref, kpad_ref, vpad_ref, bias_ref, o_ref)


def _attn_sample_kernel(q_ref, k_ref, v_ref, ck_ref, cv_ref, bias_ref, o_ref, kpad_ref, vpad_ref):
    kpad_ref[0:ATT_LEFT, :] = ck_ref[0].astype(BF16)
    vpad_ref[0:ATT_LEFT, :] = cv_ref[0].astype(BF16)
    kpad_ref[ATT_LEFT:BAND, :] = k_ref[...]
    vpad_ref[ATT_LEFT:BAND, :] = v_ref[...]
    _attn_chunks(1, LEFT_CHUNKS, q_ref, kpad_ref, vpad_ref, bias_ref, o_ref)


def _attn_prompt(q, k, v, bias2):
    seq = pl.BlockSpec((SEQ, D_ATT), lambda b: (b, 0))
    return pl.pallas_call(
        _attn_prompt_kernel,
        grid=(BATCH,),
        in_specs=[seq, seq, seq, _const_spec((ATT_PAIRS, 2 * CHUNK, BAND))],
        out_specs=seq,
        out_shape=jax.ShapeDtypeStruct((N_PROMPT, D_ATT), BF16),
        scratch_shapes=[pltpu.VMEM((ATT_LEFT + SEQ, D_ATT), BF16),
                        pltpu.VMEM((ATT_LEFT + SEQ, D_ATT), BF16)],
        compiler_params=_params(),
        name="attn_prompt",
    )(q, k, v, bias2)


def _attn_sample(q, k, v, cache_k, cache_v, bias2):
    first = N_PROMPT // CHUNK
    row = pl.BlockSpec((CHUNK, D_ATT), lambda b: (first + b, 0))
    cache = pl.BlockSpec((1, ATT_LEFT, D_ATT), lambda b: (b, 0, 0))
    return pl.pallas_call(
        _attn_sample_kernel,
        grid=(DEC_BATCH,),
        in_specs=[row, row, row, cache, cache, _const_spec((ATT_PAIRS, 2 * CHUNK, BAND))],
        out_specs=pl.BlockSpec((CHUNK, D_ATT), lambda b: (b, 0)),
        out_shape=jax.ShapeDtypeStruct((N_SAMPLE, D_ATT), BF16),
        scratch_shapes=[pltpu.VMEM((BAND, D_ATT), BF16), pltpu.VMEM((BAND, D_ATT), BF16)],
        compiler_params=_params(),
        name="attn_sample",
    )(q, k, v, cache_k, cache_v, bias2)


def _outproj_kernel(xp_ref, xs_ref, yp_ref, ys_ref, op_ref, os_ref, ag_ref, wos_ref, woa_ref, fg_ref,
                    wr_ref, br_ref, ltri_ref,
                    xmid_ref, h_ref, idx_ref, gate_ref, rank_ref, cnt_ref, carry_ref):
    i = pl.program_id(0)

    @pl.when(i == 0)
    def _():
        carry_ref[...] = jnp.zeros_like(carry_ref)

    is_sample = i == N_PROMPT_TILES
    x = jnp.where(is_sample, xs_ref[...], xp_ref[...])
    y = jnp.where(is_sample, ys_ref[...], yp_ref[...])
    o = jnp.where(is_sample, os_ref[...], op_ref[...])
    o = _rms(o.astype(F32), ag_ref[...]).astype(BF16)
    xm = x + _dot(y, wos_ref[...]) + _dot(o, woa_ref[...])
    xmid_ref[...] = xm
    h = _rms(xm, fg_ref[...])
    _rows_to_tiles(h_ref, h)
    h1 = h.astype(BF16)
    h2 = (h - h1.astype(F32)).astype(BF16)
    w1 = wr_ref[0]
    w2 = wr_ref[1]
    logits = _dot(h1, w1) + (_dot(h1, w2) + _dot(h2, w1)) + br_ref[...]
    eidx = lax.broadcasted_iota(jnp.int32, (TM, N_EXPERTS), 1)
    lane = lax.broadcasted_iota(jnp.int32, (TM, 128), 1)
    work = logits
    vals, sels = [], []
    idx_out = jnp.zeros((TM, 128), jnp.int32)
    for k in range(TOP_K):
        m = jnp.max(work, axis=-1, keepdims=True)
        idx = jnp.min(jnp.where(work == m, eidx, N_EXPERTS), axis=-1, keepdims=True)
        sel = eidx == idx
        vals.append(m)
        sels.append(sel)
        idx_out = jnp.where(lane == k, idx, idx_out)
        work = jnp.where(sel, -jnp.inf, work)
    es = [jnp.exp(v - vals[0]) for v in vals]
    tot = es[0] + es[1] + es[2] + es[3]
    gate_out = jnp.zeros((TM, 128), F32)
    for k in range(TOP_K):
        gate_out = jnp.where(lane == k, es[k] / tot, gate_out)
    idx_ref[...] = idx_out
    gate_ref[...] = gate_out
    multi = jnp.zeros((TM, N_EXPERTS), F32)
    for sel in sels:
        multi = jnp.where(sel, 1.0, multi)
    before = _dot(ltri_ref[...], multi.astype(BF16)) + carry_ref[...]
    rank_out = jnp.zeros((TM, 128), jnp.int32)
    for k in range(TOP_K):
        rk = jnp.sum(jnp.where(sels[k], before, 0.0), axis=-1, keepdims=True).astype(jnp.int32)
        rank_out = jnp.where(lane == k, rk, rank_out)
    rank_ref[...] = rank_out
    carry_ref[...] = carry_ref[...] + jnp.sum(multi, axis=0, keepdims=True)
    cnt_ref[...] = carry_ref[...]


def _outproj(xp, xs, yp, ys, op, os_, ag, wos, woa, fg, wr, br, ltri):
    tok = lambda n: pl.BlockSpec((TM, n), lambda i: (i, 0))
    prompt = lambda n: pl.BlockSpec((TM, n), lambda i: (jnp.minimum(i, N_PROMPT_TILES - 1), 0))
    sample = lambda n: pl.BlockSpec((TM, n), lambda i: (0, 0))
    return pl.pallas_call(
        _outproj_kernel,
        grid=(N_TILES,),
        in_specs=[
            prompt(D_MODEL), sample(D_MODEL), prompt(D_SSM), sample(D_SSM), prompt(D_ATT), sample(D_ATT),
            _const_spec((1, D_ATT)),
            _const_spec((D_SSM, D_MODEL)), _const_spec((D_ATT, D_MODEL)),
            _const_spec((1, D_MODEL)),
            _const_spec((2, D_MODEL, N_EXPERTS)), _const_spec((1, N_EXPERTS)),
            _const_spec((TM, TM)),
        ],
        out_specs=[tok(D_MODEL), pl.BlockSpec(_tiled(TM), lambda i: (i, 0)), tok(128), tok(128), tok(128),
                   _const_spec((1, N_EXPERTS))],
        out_shape=[
            jax.ShapeDtypeStruct((N_TOK, D_MODEL), F32),
            jax.ShapeDtypeStruct(_tiled(N_TOK), U32),
            jax.ShapeDtypeStruct((N_TOK, 128), jnp.int32),
            jax.ShapeDtypeStruct((N_TOK, 128), F32),
            jax.ShapeDtypeStruct((N_TOK, 128), jnp.int32),
            jax.ShapeDtypeStruct((1, N_EXPERTS), F32),
        ],
        scratch_shapes=[pltpu.VMEM((1, N_EXPERTS), F32)],
        compiler_params=_params(),
        name="outproj_router",
    )(xp, xs, yp, ys, op, os_, ag, wos, woa, fg, wr, br, ltri)


def _scatter_kernel(dest_ref, pend_ref, h_ref, rows_ref, zero_ref, sem, zsem):
    i = pl.program_id(0)

    @pl.when(i == 0)
    def _():
        zero_ref[...] = jnp.zeros_like(zero_ref)

        block_tiles = _tiled(MOE_BM)[0]

        def zero_block(b):
            start = pl.multiple_of(b * block_tiles, block_tiles)
            return pltpu.make_async_copy(zero_ref, rows_ref.at[pl.ds(start, block_tiles)], zsem)

        def last_block(e):
            end = pend_ref[e]
            nonempty = end > (pend_ref[e - 1] if e > 0 else 0)
            return nonempty, zero_block(jnp.maximum(end // MOE_BM - 1, 0))

        for e in range(N_EXPERTS):
            nonempty, cp = last_block(e)
            pl.when(nonempty)(cp.start)
        for e in range(N_EXPERTS):
            nonempty, cp = last_block(e)
            pl.when(nonempty)(cp.wait)

        first_unused = pend_ref[N_EXPERTS - 1] // MOE_BM
        lax.fori_loop(first_unused, MOE_BLOCKS, lambda b, c: (zero_block(b).start(), c)[1], 0)
        lax.fori_loop(first_unused, MOE_BLOCKS, lambda b, c: (zero_block(b).wait(), c)[1], 0)

    def issue(r, carry):
        for k in range(TOP_K):
            d = dest_ref[r * TOP_K + k]
            pltpu.make_async_copy(_tile_of(h_ref, r), _tile_of(rows_ref, d), sem).start(priority=k % 2)
        return carry

    lax.fori_loop(0, TM, issue, 0, unroll=ISSUE_UNROLL)
    for _ in range(TOP_K):
        pltpu.make_async_copy(h_ref, rows_ref.at[pl.ds(0, _tiled(TM)[0])], sem).wait()


def _scatter_rows(dest_flat, pad_end, h):
    return pl.pallas_call(
        _scatter_kernel,
        grid=(N_TILES,),
        in_specs=[
            pl.BlockSpec((TM * TOP_K,), lambda i: (i,), memory_space=pltpu.SMEM),
            pl.BlockSpec((N_EXPERTS,), lambda i: (0,), memory_space=pltpu.SMEM),
            pl.BlockSpec(_tiled(TM), lambda i: (i, 0)),
        ],
        out_specs=pl.BlockSpec(memory_space=pl.ANY),
        out_shape=jax.ShapeDtypeStruct(_tiled(MOE_ROWS), U32),
        scratch_shapes=[pltpu.VMEM(_tiled(MOE_BM), U32), pltpu.SemaphoreType.DMA(()),
                        pltpu.SemaphoreType.DMA(())],
        compiler_params=_params(),
        name="moe_scatter",
    )(dest_flat, pad_end, h)


def _expert_kernel(be_ref, nu_ref, nxt_ref, x_ref, wgu_hbm, bgu_ref, wd_hbm, bd_ref, y_ref,
                   wgu_f, wd_f, wgu_s, wd_s, sem):
    i = pl.program_id(0)
    active = i < nu_ref[0]
    e = be_ref[i]

    def fetch(expert):
        return (pltpu.make_async_copy(wgu_hbm.at[expert], wgu_f, sem.at[0]),
                pltpu.make_async_copy(wd_hbm.at[expert], wd_f, sem.at[1]))

    @pl.when(active & (i == 0))
    def _():
        for cp in fetch(e):
            cp.start()

    @pl.when(active & ((i == 0) | (e != be_ref[jnp.maximum(i - 1, 0)])))
    def _():
        for cp in fetch(e):
            cp.wait()
        wgu_s[...] = wgu_f[...].astype(BF16)
        wd_s[...] = wd_f[...].astype(BF16)
        nxt = nxt_ref[e]

        @pl.when(nxt >= 0)
        def _():
            for cp in fetch(nxt):
                cp.start()

    @pl.when(active)
    def _():
        gu = _dot(_tiles_to_rows(x_ref, MOE_BM).astype(BF16), wgu_s[...]) + bgu_ref[0]
        gate = jnp.minimum(gu[:, :D_FF], SWIGLU_LIMIT)
        up = jnp.clip(gu[:, D_FF:], -SWIGLU_LIMIT, SWIGLU_LIMIT)
        act = (up + 1.0) * gate * _sigmoid(gate * SWIGLU_ALPHA)
        _rows_to_tiles(y_ref, _dot(act.astype(BF16), wd_s[...]) + bd_ref[0])

    @pl.when(jnp.logical_not(active))
    def _():
        y_ref[...] = jnp.zeros_like(y_ref)


def _experts(block_expert, n_used, next_expert, rows, wgu, bgu, wd, bd):
    grid_spec = pltpu.PrefetchScalarGridSpec(
        num_scalar_prefetch=3,
        grid=(MOE_BLOCKS,),
        in_specs=[
            pl.BlockSpec(_tiled(MOE_BM), lambda i, be, nu, nx: (jnp.minimum(i, nu[0] - 1), 0)),
            pl.BlockSpec(memory_space=pl.ANY),
            pl.BlockSpec((1, 1, 2 * D_FF), lambda i, be, nu, nx: (be[i], 0, 0)),
            pl.BlockSpec(memory_space=pl.ANY),
            pl.BlockSpec((1, 1, D_MODEL), lambda i, be, nu, nx: (be[i], 0, 0)),
        ],
        out_specs=pl.BlockSpec(_tiled(MOE_BM), lambda i, be, nu, nx: (i, 0)),
        scratch_shapes=[pltpu.VMEM((D_MODEL, 2 * D_FF), F32), pltpu.VMEM((D_FF, D_MODEL), F32),
                        pltpu.VMEM((D_MODEL, 2 * D_FF), BF16), pltpu.VMEM((D_FF, D_MODEL), BF16),
                        pltpu.SemaphoreType.DMA((2,))],
    )
    return pl.pallas_call(
        _expert_kernel,
        grid_spec=grid_spec,
        out_shape=jax.ShapeDtypeStruct(_tiled(MOE_ROWS), U32),
        compiler_params=_params(),
        name="moe_experts",
    )(block_expert, n_used, next_expert, rows, wgu, bgu, wd, bd)


def _combine_kernel(dest_ref, dest_next_ref, gate_ref, xmid_ref, g_ref, rows_ref, yp_ref, ys_ref, buf_ref, sem):
    i = pl.program_id(0)
    slot = i % 2

    def issue(idx_ref, s):
        def body(r, carry):
            for k in range(TOP_K):
                d = idx_ref[r * TOP_K + k]
                pltpu.make_async_copy(_tile_of(rows_ref, d), _tile_of(buf_ref.at[s], k * TM + r),
                                      sem.at[s]).start(priority=k % 2)
            return carry

        lax.fori_loop(0, TM, body, 0, unroll=ISSUE_UNROLL)

    @pl.when(i == 0)
    def _():
        issue(dest_ref, 0)

    @pl.when(i + 1 < N_TILES)
    def _():
        issue(dest_next_ref, 1 - slot)

    buf = buf_ref.at[slot]
    slot_tiles = _tiled(TM)[0]
    for k in range(TOP_K):
        pltpu.make_async_copy(rows_ref.at[pl.ds(0, slot_tiles)], buf.at[pl.ds(k * slot_tiles, slot_tiles)],
                              sem.at[slot]).wait()
    acc = xmid_ref[...]
    for k in range(TOP_K):
        acc = acc + _tiles_to_rows(buf, TM, first=k * TM) * gate_ref[:, k:k + 1]
    y = _rms(acc, g_ref[...])

    @pl.when(i < N_PROMPT_TILES)
    def _():
        yp_ref[...] = y

    @pl.when(i == N_PROMPT_TILES)
    def _():
        ys_ref[...] = y


def _combine(dest_flat, gates, xmid, g, y_rows):
    return pl.pallas_call(
        _combine_kernel,
        grid=(N_TILES,),
        in_specs=[
            pl.BlockSpec((TM * TOP_K,), lambda i: (i,), memory_space=pltpu.SMEM),
            pl.BlockSpec((TM * TOP_K,), lambda i: (jnp.minimum(i + 1, N_TILES - 1),), memory_space=pltpu.SMEM),
            pl.BlockSpec((TM, 128), lambda i: (i, 0)),
            pl.BlockSpec((TM, D_MODEL), lambda i: (i, 0)),
            _const_spec((1, D_MODEL)),
            pl.BlockSpec(memory_space=pl.ANY),
        ],
        out_specs=[
            pl.BlockSpec((TM, D_MODEL), lambda i: (jnp.minimum(i, N_PROMPT_TILES - 1), 0)),
            pl.BlockSpec((TM, D_MODEL), lambda i: (0, 0)),
        ],
        out_shape=[
            jax.ShapeDtypeStruct((N_PROMPT, D_MODEL), F32),
            jax.ShapeDtypeStruct((N_SAMPLE, D_MODEL), F32),
        ],
        scratch_shapes=[pltpu.VMEM((2,) + _tiled(TOP_K * TM), U32), pltpu.SemaphoreType.DMA((2,))],
        compiler_params=_params(),
        name="moe_combine",
    )(dest_flat, dest_flat, gates, xmid, g, y_rows)


def _band_bias(table):
    n_diag = BAND + CHUNK - 1
    idx = np.clip(ATT_LEFT + (CHUNK - 1) - np.arange(n_diag), -REL_CLIP, REL_CLIP) + REL_CLIP
    pick = (np.arange(2 * REL_CLIP + 1)[:, None] == idx[None, :]).astype(np.float32)
    diag = jnp.dot(table, jnp.asarray(pick), precision=lax.Precision.HIGHEST)
    return jnp.stack([diag[:, CHUNK - 1 - qi:CHUNK - 1 - qi + BAND] for qi in range(CHUNK)], axis=1)


def _pair_rows(v):
    return jnp.repeat(v.reshape(HEAD_PAIRS, 2), CHUNK, axis=1)


def _layer(l, xp, xs, cache_k, cache_v, state_conv, state_ssm,
           norm_mix_g, w_in, conv_w, conv_b, dt_bias, a_log, d_skip, ssm_norm_g,
           att_norm_g, rel_bias_table, w_out, norm_ffn_g, w_router, b_router,
           w_gate_up, b_gate_up, w_down, b_down, norm_final_g):
    wb = w_in[l].astype(BF16)
    c0 = D_SSM
    c1 = c0 + D_CONV
    c2 = c1 + SSM_HEADS
    c3 = c2 + D_ATT
    c4 = c3 + D_ATT
    z, xbc, dt, dtt, q, k, v, kf, vf, ctail = _inproj(
        xp, xs, norm_mix_g[l][None], wb[:, :c0], wb[:, c0:c1], wb[:, c1:c2], wb[:, c1:c2].T,
        wb[:, c2:c3], wb[:, c3:c4], wb[:, c4:])

    n_chunks = N_TOK // CHUNK
    dtp = dtt.reshape(HEAD_PAIRS, 2, n_chunks, CHUNK).transpose(2, 0, 1, 3).reshape(n_chunks, HEAD_PAIRS, 128)
    hp = jnp.arange(D_SSM) // SSM_HEAD_DIM
    expand = (hp[None, :] == jnp.arange(SSM_HEADS)[:, None]).astype(BF16)
    tril = jnp.tril(jnp.ones((CHUNK, CHUNK), BF16))
    r128 = jnp.arange(128)
    triu2 = ((r128[:, None] // CHUNK == r128[None, :] // CHUNK) & (r128[:, None] <= r128[None, :])).astype(BF16)
    consts = (conv_w[l], conv_b[l][None], dt_bias[l][None], _pair_rows(dt_bias[l]),
              jnp.repeat(a_log[l], SSM_HEAD_DIM)[None], _pair_rows(a_log[l]),
              jnp.repeat(d_skip[l], SSM_HEAD_DIM)[None], ssm_norm_g[l][None],
              expand, tril, triu2)
    y_ssm_p, ssm_p = _ssd_prompt(z, xbc, dt, dtp, consts)
    y_ssm_s, ssm_s = _ssd_sample(z, xbc, dt, dtp, state_conv[l],
                                 state_ssm[l].reshape(DEC_BATCH, D_SSM, D_STATE), consts)

    bias2 = _band_bias(rel_bias_table[l]).reshape(ATT_PAIRS, 2 * CHUNK, BAND)
    o_att_p = _attn_prompt(q, k, v, bias2)
    o_att_s = _attn_sample(q, k, v, cache_k[l].reshape(DEC_BATCH, ATT_LEFT, D_ATT),
                           cache_v[l].reshape(DEC_BATCH, ATT_LEFT, D_ATT), bias2)

    wo = w_out[l].astype(BF16)
    wr = w_router[l]
    wr1 = wr.astype(BF16)
    wr2 = (wr - wr1.astype(F32)).astype(BF16)
    ltri = jnp.tril(jnp.ones((TM, TM), BF16), -1)
    xmid, h, top_idx, gates, rank, counts = _outproj(
        xp, xs, y_ssm_p, y_ssm_s, o_att_p, o_att_s, att_norm_g[l][None], wo[:D_SSM], wo[D_SSM:], norm_ffn_g[l][None],
        jnp.stack([wr1, wr2]), b_router[l][None], ltri)

    counts = counts[0].astype(jnp.int32)
    padded = (counts + MOE_BM - 1) // MOE_BM * MOE_BM
    pad_end = jnp.cumsum(padded)
    pad_start = pad_end - padded
    experts = jnp.arange(N_EXPERTS, dtype=jnp.int32)
    start_of = jnp.sum(jnp.where(top_idx[:, :TOP_K, None] == experts, pad_start, 0), axis=-1)
    dest = (start_of + rank[:, :TOP_K]).reshape(-1).astype(jnp.int32)
    block_start = jnp.arange(MOE_BLOCKS, dtype=jnp.int32) * MOE_BM
    block_expert = jnp.minimum(jnp.sum((pad_end[None, :] <= block_start[:, None]).astype(jnp.int32), axis=1),
                               N_EXPERTS - 1).astype(jnp.int32)
    n_used = (pad_end[-1:] // MOE_BM).astype(jnp.int32)
    later_nonempty = (experts[None, :] > experts[:, None]) & (padded[None, :] > 0)
    next_expert = jnp.min(jnp.where(later_nonempty, experts[None, :], N_EXPERTS), axis=1)
    next_expert = jnp.where(next_expert < N_EXPERTS, next_expert, -1).astype(jnp.int32)

    rows = _scatter_rows(dest, pad_end.astype(jnp.int32), h)
    y_rows = _experts(block_expert, n_used, next_expert, rows, w_gate_up[l], b_gate_up[l][:, None, :],
                      w_down[l], b_down[l][:, None, :])
    y_p, y_s = _combine(dest, gates, xmid, norm_final_g[None], y_rows)

    keep = min(ATT_LEFT, SEQ)
    k_p = kf[:BATCH, TM - keep:].reshape(BATCH, keep, ATT_HEADS, ATT_HEAD_DIM)
    v_p = vf[:BATCH, TM - keep:].reshape(BATCH, keep, ATT_HEADS, ATT_HEAD_DIM)
    k_s = kf[BATCH].reshape(DEC_BATCH, DEC_SEQ, ATT_HEADS, ATT_HEAD_DIM)
    v_s = vf[BATCH].reshape(DEC_BATCH, DEC_SEQ, ATT_HEADS, ATT_HEAD_DIM)
    conv_p = ctail[:BATCH, -(CONV_W - 1):]
    conv_s = ctail[BATCH].reshape(DEC_BATCH, 8, D_CONV)[:, -(CONV_W - 1):]
    ssm_p = ssm_p.reshape(BATCH, SSM_HEADS, SSM_HEAD_DIM, D_STATE)
    ssm_s = ssm_s.reshape(DEC_BATCH, SSM_HEADS, SSM_HEAD_DIM, D_STATE)
    return (y_p.reshape(BATCH, SEQ, D_MODEL), y_s.reshape(DEC_BATCH, DEC_SEQ, D_MODEL),
            k_p, v_p, conv_p, ssm_p, k_s, v_s, conv_s, ssm_s)


def kernel(x_prompt, x_sample, cache_k, cache_v, state_conv, state_ssm, norm_mix_g, w_in, conv_w, conv_b,
           dt_bias, a_log, d_skip, ssm_norm_g, att_norm_g, rel_bias_table, w_out, norm_ffn_g, w_router,
           b_router, w_gate_up, b_gate_up, w_down, b_down, norm_final_g):
    assert w_in.shape[0] == 1, "single trunk layer"
    xp = x_prompt.reshape(N_PROMPT, D_MODEL)
    xs = x_sample.reshape(N_SAMPLE, D_MODEL)
    outs = _layer(0, xp, xs, cache_k, cache_v, state_conv, state_ssm,
                  norm_mix_g, w_in, conv_w, conv_b, dt_bias, a_log, d_skip, ssm_norm_g,
                  att_norm_g, rel_bias_table, w_out, norm_ffn_g, w_router, b_router,
                  w_gate_up, b_gate_up, w_down, b_down, norm_final_g)
    y_p, y_s, k_p, v_p, conv_p, ssm_p, k_s, v_s, conv_s, ssm_s = outs
    return (y_p, y_s, k_p[None], v_p[None], conv_p[None], ssm_p[None],
            k_s[None], v_s[None], conv_s[None], ssm_s[None])
```

```python
import jax
import jax.numpy as jnp
import numpy as np
from jax import lax
from jax.experimental import pallas as pl
from jax.experimental.pallas import tpu as pltpu

D_MODEL = 1024
BATCH = 8
SEQ = 2048
DEC_BATCH = 8
DEC_SEQ = 64
CHUNK = 64
SSM_HEADS = 16
SSM_HEAD_DIM = 64
D_SSM = SSM_HEADS * SSM_HEAD_DIM
SSM_GROUPS = 2
D_STATE = 128
CONV_W = 4
D_BC = SSM_GROUPS * D_STATE
D_CONV = D_SSM + 2 * D_BC
ATT_HEADS = 8
ATT_HEAD_DIM = 64
D_ATT = ATT_HEADS * ATT_HEAD_DIM
LEFT_CHUNKS = 8
ATT_LEFT = LEFT_CHUNKS * CHUNK
BAND = ATT_LEFT + CHUNK
REL_CLIP = 128
ATT_SCALE = ATT_HEAD_DIM ** -0.5
N_EXPERTS = 32
TOP_K = 4
D_FF = D_MODEL
SWIGLU_ALPHA = 1.702
SWIGLU_LIMIT = 7.0
EPS = 1e-5

F32 = jnp.float32
BF16 = jnp.bfloat16
U32 = jnp.uint32

N_PROMPT = BATCH * SEQ
N_SAMPLE = DEC_BATCH * DEC_SEQ
N_TOK = N_PROMPT + N_SAMPLE
TM = 512
N_PROMPT_TILES = N_PROMPT // TM
N_TILES = N_TOK // TM
TILES_PER_SEQ = SEQ // TM
CHUNKS_PER_TILE = TM // CHUNK
N_CHUNKS_SEQ = SEQ // CHUNK
HEAD_PAIRS = SSM_HEADS // 2
PAIRS_PER_GROUP = HEAD_PAIRS // SSM_GROUPS
ATT_PAIRS = ATT_HEADS // 2
SSD_BLOCK = 2
CONV_HEAD = 8
MOE_BM = 256
N_ASSIGN = N_TOK * TOP_K
MOE_BLOCKS = N_ASSIGN // MOE_BM + N_EXPERTS
MOE_ROWS = MOE_BLOCKS * MOE_BM
ISSUE_UNROLL = 8
ROW_TILE = (4, 128)
assert 2 * ROW_TILE[0] * ROW_TILE[1] == D_MODEL
VMEM_LIMIT = 56 * 1024 * 1024


def _dot(a, b):
    return jnp.dot(a, b, preferred_element_type=F32)


def _dot_nt(a, b):
    return lax.dot_general(a, b, (((1,), (1,)), ((), ())), preferred_element_type=F32)


def _dot_tn(a, b):
    return lax.dot_general(a, b, (((0,), (0,)), ((), ())), preferred_element_type=F32)


def _split3(x):
    x1 = x.astype(BF16)
    r1 = x - x1.astype(F32)
    x2 = r1.astype(BF16)
    r2 = r1 - x2.astype(F32)
    return x1, x2, r2.astype(BF16)


def _dot_exact_rhs(x, m):
    x1, x2, x3 = _split3(x)
    return _dot(x1, m) + _dot(x2, m) + _dot(x3, m)


def _dot_exact_lhs(m, x):
    x1, x2, x3 = _split3(x)
    return _dot(m, x1) + _dot(m, x2) + _dot(m, x3)


def _rms(x, g):
    return x * lax.rsqrt(jnp.mean(x * x, axis=-1, keepdims=True) + EPS) * g


def _sigmoid(x):
    return 1.0 / (1.0 + jnp.exp(-x))


def _softplus(x):
    return jnp.maximum(x, 0.0) + jnp.log(1.0 + jnp.exp(-jnp.abs(x)))


def _tiled(n):
    return (n * ROW_TILE[0], ROW_TILE[1])


def _tile_of(ref, row):
    return ref.at[pl.ds(pl.multiple_of(row * ROW_TILE[0], ROW_TILE[0]), ROW_TILE[0])]


def _rows_to_tiles(ref, x, first=0):
    sub, lanes = ROW_TILE
    half = D_MODEL // 2
    hi = lax.bitcast_convert_type(x[:, :half].astype(BF16).astype(F32), U32)
    lo = lax.bitcast_convert_type(x[:, half:].astype(BF16).astype(F32), U32)
    words = hi | (lo >> 16)
    for j in range(sub):
        ref[pl.ds(first * sub + j, x.shape[0], stride=sub), :] = words[:, j * lanes:(j + 1) * lanes]


def _tiles_to_rows(ref, n, first=0):
    sub = ROW_TILE[0]
    words = jnp.concatenate([ref[pl.ds(first * sub + j, n, stride=sub), :] for j in range(sub)], axis=1)
    hi = lax.bitcast_convert_type(words & jnp.uint32(0xFFFF0000), F32)
    lo = lax.bitcast_convert_type(words << 16, F32)
    return jnp.concatenate([hi, lo], axis=1)


def _const_spec(shape):
    nd = len(shape)
    return pl.BlockSpec(shape, lambda *_: (0,) * nd)


def _params(n_axes=1):
    return pltpu.CompilerParams(dimension_semantics=("arbitrary",) * n_axes,
                                vmem_limit_bytes=VMEM_LIMIT)


def _inproj_kernel(xp_ref, xs_ref, g_ref, wz_ref, wxbc_ref, wdt_ref, wdtt_ref, wq_ref, wk_ref, wv_ref,
                   z_ref, xbc_ref, dt_ref, dtt_ref, q_ref, k_ref, v_ref, kf_ref, vf_ref, ctail_ref):
    i = pl.program_id(0)
    x = jnp.where(i == N_PROMPT_TILES, xs_ref[...], xp_ref[...])
    h = _rms(x, g_ref[...]).astype(BF16)
    z_ref[...] = _dot(h, wz_ref[...]).astype(BF16)
    xbc = _dot(h, wxbc_ref[...])
    xbc_ref[...] = xbc.astype(BF16)
    for c in range(CHUNKS_PER_TILE):
        ctail_ref[0, c * 8:(c + 1) * 8, :] = xbc[c * CHUNK + CHUNK - 8:(c + 1) * CHUNK, :]
    dt_ref[...] = _dot(h, wdt_ref[...])
    dtt_ref[...] = _dot_nt(wdtt_ref[...], h)
    q_ref[...] = (_dot(h, wq_ref[...]) * ATT_SCALE).astype(BF16)
    k = _dot(h, wk_ref[...])
    v = _dot(h, wv_ref[...])
    k_ref[...] = k.astype(BF16)
    v_ref[...] = v.astype(BF16)
    kf_ref[0] = k
    vf_ref[0] = v


def _inproj(xp, xs, g, wz, wxbc, wdt, wdtt, wq, wk, wv):
    tok = lambda n: pl.BlockSpec((TM, n), lambda i: (i, 0))
    tail_idx = lambda i: (i // TILES_PER_SEQ, 0, 0)
    n_tail = BATCH + 1
    return pl.pallas_call(
        _inproj_kernel,
        grid=(N_TILES,),
        in_specs=[
            pl.BlockSpec((TM, D_MODEL), lambda i: (jnp.minimum(i, N_PROMPT_TILES - 1), 0)),
            pl.BlockSpec((TM, D_MODEL), lambda i: (0, 0)),
            _const_spec((1, D_MODEL)),
            _const_spec((D_MODEL, D_SSM)),
            _const_spec((D_MODEL, D_CONV)),
            _const_spec((D_MODEL, SSM_HEADS)),
            _const_spec((SSM_HEADS, D_MODEL)),
            _const_spec((D_MODEL, D_ATT)),
            _const_spec((D_MODEL, D_ATT)),
            _const_spec((D_MODEL, D_ATT)),
        ],
        out_specs=[
            tok(D_SSM), tok(D_CONV), tok(SSM_HEADS),
            pl.BlockSpec((SSM_HEADS, TM), lambda i: (0, i)),
            tok(D_ATT), tok(D_ATT), tok(D_ATT),
            pl.BlockSpec((1, TM, D_ATT), tail_idx),
            pl.BlockSpec((1, TM, D_ATT), tail_idx),
            pl.BlockSpec((1, CHUNKS_PER_TILE * 8, D_CONV), tail_idx),
        ],
        out_shape=[
            jax.ShapeDtypeStruct((N_TOK, D_SSM), BF16),
            jax.ShapeDtypeStruct((N_TOK, D_CONV), BF16),
            jax.ShapeDtypeStruct((N_TOK, SSM_HEADS), F32),
            jax.ShapeDtypeStruct((SSM_HEADS, N_TOK), F32),
            jax.ShapeDtypeStruct((N_TOK, D_ATT), BF16),
            jax.ShapeDtypeStruct((N_TOK, D_ATT), BF16),
            jax.ShapeDtypeStruct((N_TOK, D_ATT), BF16),
            jax.ShapeDtypeStruct((n_tail, TM, D_ATT), F32),
            jax.ShapeDtypeStruct((n_tail, TM, D_ATT), F32),
            jax.ShapeDtypeStruct((n_tail, CHUNKS_PER_TILE * 8, D_CONV), F32),
        ],
        compiler_params=_params(),
        name="inproj",
    )(xp, xs, g, wz, wxbc, wdt, wdtt, wq, wk, wv)


def _ssd_tile(n_chunks, z_ref, xbc_ref, dt_ref, dtp_ref, cw_ref, cb_ref, dtb_ref, dtbp_ref,
              alog_h_ref, alog_p_ref, dskip_e_ref, ng_ref, expand_ref, triu2_ref,
              y_ref, xw_ref, state_ref):
    nb = SSD_BLOCK if n_chunks % SSD_BLOCK == 0 else 1
    rb = nb * CHUNK
    half = D_SSM // SSM_GROUPS
    xw_ref[CONV_HEAD:CONV_HEAD + n_chunks * CHUNK, :] = xbc_ref[...].astype(F32)

    a_h = -jnp.exp(alog_h_ref[...])
    a_p = jnp.concatenate([-jnp.exp(alog_p_ref[...])] * nb, axis=0)
    dtb_p = jnp.concatenate([dtbp_ref[...]] * nb, axis=0)
    row_i = lax.broadcasted_iota(jnp.int32, (CHUNK, 128), 0)
    col_i = lax.broadcasted_iota(jnp.int32, (CHUNK, 128), 1)
    causal2 = row_i >= (col_i % CHUNK)
    sel_r = lax.broadcasted_iota(jnp.int32, ((CONV_W - 1) * rb, CONV_HEAD + rb), 0)
    sel_c = lax.broadcasted_iota(jnp.int32, ((CONV_W - 1) * rb, CONV_HEAD + rb), 1)
    shift_sel = (sel_c == sel_r % rb + sel_r // rb + CONV_HEAD - (CONV_W - 1)).astype(F32)
    tr_r = lax.broadcasted_iota(jnp.int32, (rb, rb), 0)
    tr_c = lax.broadcasted_iota(jnp.int32, (rb, rb), 1)
    tril_b = jnp.where((tr_r // CHUNK == tr_c // CHUNK) & (tr_c <= tr_r), 1.0, 0.0).astype(BF16)
    bd_r = lax.broadcasted_iota(jnp.int32, (128, 128), 0) // CHUNK
    bd_c = lax.broadcasted_iota(jnp.int32, (128, 128), 1) // CHUNK
    blockdiag = bd_r == bd_c

    def block(i, carry):
        r0 = pl.multiple_of(i * rb, rb)
        win = xw_ref[pl.ds(r0, CONV_HEAD + rb), :]
        shifted = _dot(shift_sel, win)
        acc = cb_ref[...] + cw_ref[CONV_W - 1:CONV_W, :] * win[CONV_HEAD:, :]
        for tap in range(CONV_W - 1):
            acc = acc + cw_ref[tap:tap + 1, :] * shifted[tap * rb:(tap + 1) * rb, :]
        xa = acc * _sigmoid(acc)
        xs = xa[:, 0:D_SSM]
        bm = xa[:, D_SSM:D_SSM + D_BC].astype(BF16)
        cm = xa[:, D_SSM + D_BC:D_CONV].astype(BF16)
        dt = _softplus(dt_ref[pl.ds(r0, rb), :] + dtb_ref[...])
        dt_e = _dot_exact_rhs(dt, expand_ref[...])
        acum = _dot_exact_rhs(_dot_exact_lhs(tril_b, dt * a_h), expand_ref[...])
        dtp = _softplus(dtp_ref[pl.ds(i * nb, nb)].reshape(nb * HEAD_PAIRS, 128) + dtb_p)
        acum_p = _dot_exact_rhs(dtp * a_p, triu2_ref[...])
        xdt = xs * dt_e
        a_last = [acum[(c + 1) * CHUNK - 1:(c + 1) * CHUNK, :] for c in range(nb)]
        a_end = jnp.concatenate([jnp.broadcast_to(a, (CHUNK, D_SSM)) for a in a_last], axis=0)
        xdt_end = (xdt * jnp.exp(a_end - acum)).astype(BF16)

        y_diag, new_s = [], []
        for c in range(nb):
            rs = slice(c * CHUNK, (c + 1) * CHUNK)
            y_parts = []
            for g in range(SSM_GROUPS):
                bg = bm[rs, g * D_STATE:(g + 1) * D_STATE]
                cg = cm[rs, g * D_STATE:(g + 1) * D_STATE]
                cb2 = _dot_nt(cg, jnp.concatenate([bg, bg], axis=0))
                for jj in range(PAIRS_PER_GROUP):
                    j = g * PAIRS_PER_GROUP + jj
                    seg = acum[rs, j * 128:(j + 1) * 128] - acum_p[c * HEAD_PAIRS + j:c * HEAD_PAIRS + j + 1, :]
                    decay = jnp.exp(jnp.where(causal2, seg, -jnp.inf))
                    x2 = xdt[rs, j * 128:(j + 1) * 128]
                    rhs = jnp.where(blockdiag, jnp.concatenate([x2, x2], axis=0), 0.0).astype(BF16)
                    y_parts.append(_dot((cb2 * decay).astype(BF16), rhs))
            y_diag.append(jnp.concatenate(y_parts, axis=1))
            new_s.append(jnp.concatenate(
                [_dot_tn(bm[rs, g * D_STATE:(g + 1) * D_STATE], xdt_end[rs, g * half:(g + 1) * half])
                 for g in range(SSM_GROUPS)], axis=1))

        y_off = []
        for c in range(nb):
            rs = slice(c * CHUNK, (c + 1) * CHUNK)
            state = state_ref[...]
            y_off.append(jnp.concatenate(
                [_dot(cm[rs, g * D_STATE:(g + 1) * D_STATE], state[:, g * half:(g + 1) * half].astype(BF16))
                 for g in range(SSM_GROUPS)], axis=1))
            state_ref[...] = state * jnp.exp(a_last[c]) + new_s[c]

        y = (jnp.concatenate(y_diag, axis=0) + jnp.concatenate(y_off, axis=0) * jnp.exp(acum)
             + dskip_e_ref[...] * xs)
        zc = z_ref[pl.ds(r0, rb), :].astype(F32)
        y = y * (zc * _sigmoid(zc))
        yn = jnp.concatenate(
            [y[:, g * half:(g + 1) * half]
             * lax.rsqrt(jnp.mean(jnp.square(y[:, g * half:(g + 1) * half]), axis=-1, keepdims=True) + EPS)
             for g in range(SSM_GROUPS)], axis=1)
        y_ref[pl.ds(r0, rb), :] = (yn * ng_ref[...]).astype(BF16)
        return carry

    lax.fori_loop(0, n_chunks // nb, block, 0)


def _state_store(state_ref, out_ref):
    for j in range(HEAD_PAIRS):
        out_ref[0, j * 128:(j + 1) * 128, :] = state_ref[:, j * 128:(j + 1) * 128].T


def _ssd_prompt_kernel(z_ref, xbc_ref, dt_ref, dtp_ref, cw_ref, cb_ref, dtb_ref, dtbp_ref,
                       alog_h_ref, alog_p_ref, dskip_e_ref, ng_ref, expand_ref, triu2_ref,
                       y_ref, ssm_ref, xw_ref, state_ref, tail_ref):
    t = pl.program_id(1)

    @pl.when(t == 0)
    def _():
        state_ref[...] = jnp.zeros_like(state_ref)
        xw_ref[0:CONV_HEAD, :] = jnp.zeros((CONV_HEAD, D_CONV), F32)

    @pl.when(t > 0)
    def _():
        xw_ref[0:CONV_HEAD, :] = tail_ref[...]

    _ssd_tile(CHUNKS_PER_TILE, z_ref, xbc_ref, dt_ref, dtp_ref, cw_ref, cb_ref, dtb_ref, dtbp_ref,
              alog_h_ref, alog_p_ref, dskip_e_ref, ng_ref, expand_ref, triu2_ref,
              y_ref, xw_ref, state_ref)
    tail_ref[...] = xw_ref[TM:TM + CONV_HEAD, :]

    @pl.when(t == TILES_PER_SEQ - 1)
    def _():
        _state_store(state_ref, ssm_ref)


def _ssd_sample_kernel(z_ref, xbc_ref, dt_ref, dtp_ref, cprev_ref, sprev_ref,
                       cw_ref, cb_ref, dtb_ref, dtbp_ref,
                       alog_h_ref, alog_p_ref, dskip_e_ref, ng_ref, expand_ref, triu2_ref,
                       y_ref, ssm_ref, xw_ref, state_ref):
    xw_ref[0:CONV_HEAD, :] = jnp.zeros((CONV_HEAD, D_CONV), F32)
    xw_ref[CONV_HEAD - (CONV_W - 1):CONV_HEAD, :] = cprev_ref[0]
    for j in range(HEAD_PAIRS):
        state_ref[:, j * 128:(j + 1) * 128] = sprev_ref[0, j * 128:(j + 1) * 128, :].T
    _ssd_tile(1, z_ref, xbc_ref, dt_ref, dtp_ref, cw_ref, cb_ref, dtb_ref, dtbp_ref,
              alog_h_ref, alog_p_ref, dskip_e_ref, ng_ref, expand_ref, triu2_ref,
              y_ref, xw_ref, state_ref)
    _state_store(state_ref, ssm_ref)


def _ssd_const_specs():
    return [
        _const_spec((CONV_W, D_CONV)), _const_spec((1, D_CONV)),
        _const_spec((1, SSM_HEADS)), _const_spec((HEAD_PAIRS, 128)),
        _const_spec((1, SSM_HEADS)), _const_spec((HEAD_PAIRS, 128)),
        _const_spec((1, D_SSM)), _const_spec((1, D_SSM)),
        _const_spec((SSM_HEADS, D_SSM)), _const_spec((128, 128)),
    ]


def _ssd_prompt(z, xbc, dt, dtp, consts):
    tile = lambda b, t: (b * TILES_PER_SEQ + t, 0)
    return pl.pallas_call(
        _ssd_prompt_kernel,
        grid=(BATCH, TILES_PER_SEQ),
        in_specs=[
            pl.BlockSpec((TM, D_SSM), tile),
            pl.BlockSpec((TM, D_CONV), tile),
            pl.BlockSpec((TM, SSM_HEADS), tile),
            pl.BlockSpec((CHUNKS_PER_TILE, HEAD_PAIRS, 128), lambda b, t: (b * TILES_PER_SEQ + t, 0, 0)),
        ] + _ssd_const_specs(),
        out_specs=[
            pl.BlockSpec((TM, D_SSM), tile),
            pl.BlockSpec((1, D_SSM, D_STATE), lambda b, t: (b, 0, 0)),
        ],
        out_shape=[
            jax.ShapeDtypeStruct((N_PROMPT, D_SSM), BF16),
            jax.ShapeDtypeStruct((BATCH, D_SSM, D_STATE), F32),
        ],
        scratch_shapes=[
            pltpu.VMEM((CONV_HEAD + TM, D_CONV), F32),
            pltpu.VMEM((D_STATE, D_SSM), F32),
            pltpu.VMEM((CONV_HEAD, D_CONV), F32),
        ],
        compiler_params=_params(2),
        name="ssd_prompt",
    )(z, xbc, dt, dtp, *consts)


def _ssd_sample(z, xbc, dt, dtp, conv_prev, ssm_prev, consts):
    first = N_PROMPT // CHUNK
    row = lambda b: (first + b, 0)
    return pl.pallas_call(
        _ssd_sample_kernel,
        grid=(DEC_BATCH,),
        in_specs=[
            pl.BlockSpec((CHUNK, D_SSM), row),
            pl.BlockSpec((CHUNK, D_CONV), row),
            pl.BlockSpec((CHUNK, SSM_HEADS), row),
            pl.BlockSpec((1, HEAD_PAIRS, 128), lambda b: (first + b, 0, 0)),
            pl.BlockSpec((1, CONV_W - 1, D_CONV), lambda b: (b, 0, 0)),
            pl.BlockSpec((1, D_SSM, D_STATE), lambda b: (b, 0, 0)),
        ] + _ssd_const_specs(),
        out_specs=[
            pl.BlockSpec((CHUNK, D_SSM), lambda b: (b, 0)),
            pl.BlockSpec((1, D_SSM, D_STATE), lambda b: (b, 0, 0)),
        ],
        out_shape=[
            jax.ShapeDtypeStruct((N_SAMPLE, D_SSM), BF16),
            jax.ShapeDtypeStruct((DEC_BATCH, D_SSM, D_STATE), F32),
        ],
        scratch_shapes=[
            pltpu.VMEM((CONV_HEAD + CHUNK, D_CONV), F32),
            pltpu.VMEM((D_STATE, D_SSM), F32),
        ],
        compiler_params=_params(),
        name="ssd_sample",
    )(z, xbc, dt, dtp, conv_prev, ssm_prev, *consts)


def _attn_chunks(n_chunks, first_chunk, q_ref, kpad_ref, vpad_ref, bias_ref, o_ref):
    lane = lax.broadcasted_iota(jnp.int32, (CHUNK, 128), 1)
    low = lane < ATT_HEAD_DIM
    kj = lax.broadcasted_iota(jnp.int32, (2 * CHUNK, BAND), 1)

    def chunk(c, carry, masked):
        r0 = pl.multiple_of(c * CHUNK, CHUNK)
        if masked:
            valid = kj >= (LEFT_CHUNKS - (first_chunk + c)) * CHUNK
        scores = []
        for j in range(ATT_PAIRS):
            qp = q_ref[pl.ds(r0, CHUNK), j * 128:(j + 1) * 128]
            zero = jnp.zeros_like(qp)
            q2 = jnp.concatenate([jnp.where(low, qp, zero), jnp.where(low, zero, qp)], axis=0)
            kb = kpad_ref[pl.ds(r0, BAND), j * 128:(j + 1) * 128]
            s = _dot_nt(q2, kb) + bias_ref[j]
            scores.append(jnp.where(valid, s, -jnp.inf) if masked else s)
        probs = []
        for s in scores:
            e = jnp.exp(s - jnp.max(s, axis=-1, keepdims=True))
            probs.append((e.astype(BF16), jnp.sum(e, axis=-1, keepdims=True)))
        outs = []
        for j, (e, denom) in enumerate(probs):
            vb = vpad_ref[pl.ds(r0, BAND), j * 128:(j + 1) * 128]
            r = _dot(e, vb) / denom
            outs.append(jnp.where(low, r[0:CHUNK], r[CHUNK:2 * CHUNK]))
        o_ref[pl.ds(r0, CHUNK), :] = jnp.concatenate(outs, axis=1).astype(BF16)
        return carry

    n_masked = min(max(LEFT_CHUNKS - first_chunk, 0), n_chunks)
    for lo, hi, masked in ((0, n_masked, True), (n_masked, n_chunks, False)):
        if hi > lo:
            lax.fori_loop(lo, hi, lambda c, carry, masked=masked: chunk(c, carry, masked), 0,
                          unroll=2 if (hi - lo) % 2 == 0 else 1)


def _attn_prompt_kernel(q_ref, k_ref, v_ref, bias_ref, o_ref, kpad_ref, vpad_ref):
    kpad_ref[0:ATT_LEFT, :] = jnp.zeros((ATT_LEFT, D_ATT), BF16)
    vpad_ref[0:ATT_LEFT, :] = jnp.zeros((ATT_LEFT, D_ATT), BF16)
    kpad_ref[ATT_LEFT:ATT_LEFT + SEQ, :] = k_ref[...]
    vpad_ref[ATT_LEFT:ATT_LEFT + SEQ, :] = v_ref[...]
    _attn_chunks(N_CHUNKS_SEQ, 0, q_ref, kpad_ref, vpad_ref, bias_ref, o_ref)


def _attn_sample_kernel(q_ref, k_ref, v_ref, ck_ref, cv_ref, bias_ref, o_ref, kpad_ref, vpad_ref):
    kpad_ref[0:ATT_LEFT, :] = ck_ref[0].astype(BF16)
    vpad_ref[0:ATT_LEFT, :] = cv_ref[0].astype(BF16)
    kpad_ref[ATT_LEFT:BAND, :] = k_ref[...]
    vpad_ref[ATT_LEFT:BAND, :] = v_ref[...]
    _attn_chunks(1, LEFT_CHUNKS, q_ref, kpad_ref, vpad_ref, bias_ref, o_ref)


def _attn_prompt(q, k, v, bias2):
    seq = pl.BlockSpec((SEQ, D_ATT), lambda b: (b, 0))
    return pl.pallas_call(
        _attn_prompt_kernel,
        grid=(BATCH,),
        in_specs=[seq, seq, seq, _const_spec((ATT_PAIRS, 2 * CHUNK, BAND))],
        out_specs=seq,
        out_shape=jax.ShapeDtypeStruct((N_PROMPT, D_ATT), BF16),
        scratch_shapes=[pltpu.VMEM((ATT_LEFT + SEQ, D_ATT), BF16),
                        pltpu.VMEM((ATT_LEFT + SEQ, D_ATT), BF16)],
        compiler_params=_params(),
        name="attn_prompt",
    )(q, k, v, bias2)


def _attn_sample(q, k, v, cache_k, cache_v, bias2):
    first = N_PROMPT // CHUNK
    row = pl.BlockSpec((CHUNK, D_ATT), lambda b: (first + b, 0))
    cache = pl.BlockSpec((1, ATT_LEFT, D_ATT), lambda b: (b, 0, 0))
    return pl.pallas_call(
        _attn_sample_kernel,
        grid=(DEC_BATCH,),
        in_specs=[row, row, row, cache, cache, _const_spec((ATT_PAIRS, 2 * CHUNK, BAND))],
        out_specs=pl.BlockSpec((CHUNK, D_ATT), lambda b: (b, 0)),
        out_shape=jax.ShapeDtypeStruct((N_SAMPLE, D_ATT), BF16),
        scratch_shapes=[pltpu.VMEM((BAND, D_ATT), BF16), pltpu.VMEM((BAND, D_ATT), BF16)],
        compiler_params=_params(),
        name="attn_sample",
    )(q, k, v, cache_k, cache_v, bias2)


def _outproj_kernel(xp_ref, xs_ref, yp_ref, ys_ref, op_ref, os_ref, ag_ref, wos_ref, woa_ref, fg_ref,
                    wr_ref, br_ref, ltri_ref,
                    xmid_ref, h_ref, idx_ref, gate_ref, rank_ref, cnt_ref, carry_ref):
    i = pl.program_id(0)

    @pl.when(i == 0)
    def _():
        carry_ref[...] = jnp.zeros_like(carry_ref)

    is_sample = i == N_PROMPT_TILES
    x = jnp.where(is_sample, xs_ref[...], xp_ref[...])
    y = jnp.where(is_sample, ys_ref[...], yp_ref[...])
    o = jnp.where(is_sample, os_ref[...], op_ref[...])
    o = _rms(o.astype(F32), ag_ref[...]).astype(BF16)
    xm = x + _dot(y, wos_ref[...]) + _dot(o, woa_ref[...])
    xmid_ref[...] = xm
    h = _rms(xm, fg_ref[...])
    _rows_to_tiles(h_ref, h)
    h1 = h.astype(BF16)
    h2 = (h - h1.astype(F32)).astype(BF16)
    w1 = wr_ref[0]
    w2 = wr_ref[1]
    logits = _dot(h1, w1) + (_dot(h1, w2) + _dot(h2, w1)) + br_ref[...]
    eidx = lax.broadcasted_iota(jnp.int32, (TM, N_EXPERTS), 1)
    lane = lax.broadcasted_iota(jnp.int32, (TM, 128), 1)
    work = logits
    vals, sels = [], []
    idx_out = jnp.zeros((TM, 128), jnp.int32)
    for k in range(TOP_K):
        m = jnp.max(work, axis=-1, keepdims=True)
        idx = jnp.min(jnp.where(work == m, eidx, N_EXPERTS), axis=-1, keepdims=True)
        sel = eidx == idx
        vals.append(m)
        sels.append(sel)
        idx_out = jnp.where(lane == k, idx, idx_out)
        work = jnp.where(sel, -jnp.inf, work)
    es = [jnp.exp(v - vals[0]) for v in vals]
    tot = es[0] + es[1] + es[2] + es[3]
    gate_out = jnp.zeros((TM, 128), F32)
    for k in range(TOP_K):
        gate_out = jnp.where(lane == k, es[k] / tot, gate_out)
    idx_ref[...] = idx_out
    gate_ref[...] = gate_out
    multi = jnp.zeros((TM, N_EXPERTS), F32)
    for sel in sels:
        multi = jnp.where(sel, 1.0, multi)
    before = _dot(ltri_ref[...], multi.astype(BF16)) + carry_ref[...]
    rank_out = jnp.zeros((TM, 128), jnp.int32)
    for k in range(TOP_K):
        rk = jnp.sum(jnp.where(sels[k], before, 0.0), axis=-1, keepdims=True).astype(jnp.int32)
        rank_out = jnp.where(lane == k, rk, rank_out)
    rank_ref[...] = rank_out
    carry_ref[...] = carry_ref[...] + jnp.sum(multi, axis=0, keepdims=True)
    cnt_ref[...] = carry_ref[...]


def _outproj(xp, xs, yp, ys, op, os_, ag, wos, woa, fg, wr, br, ltri):
    tok = lambda n: pl.BlockSpec((TM, n), lambda i: (i, 0))
    prompt = lambda n: pl.BlockSpec((TM, n), lambda i: (jnp.minimum(i, N_PROMPT_TILES - 1), 0))
    sample = lambda n: pl.BlockSpec((TM, n), lambda i: (0, 0))
    return pl.pallas_call(
        _outproj_kernel,
        grid=(N_TILES,),
        in_specs=[
            prompt(D_MODEL), sample(D_MODEL), prompt(D_SSM), sample(D_SSM), prompt(D_ATT), sample(D_ATT),
            _const_spec((1, D_ATT)),
            _const_spec((D_SSM, D_MODEL)), _const_spec((D_ATT, D_MODEL)),
            _const_spec((1, D_MODEL)),
            _const_spec((2, D_MODEL, N_EXPERTS)), _const_spec((1, N_EXPERTS)),
            _const_spec((TM, TM)),
        ],
        out_specs=[tok(D_MODEL), pl.BlockSpec(_tiled(TM), lambda i: (i, 0)), tok(128), tok(128), tok(128),
                   _const_spec((1, N_EXPERTS))],
        out_shape=[
            jax.ShapeDtypeStruct((N_TOK, D_MODEL), F32),
            jax.ShapeDtypeStruct(_tiled(N_TOK), U32),
            jax.ShapeDtypeStruct((N_TOK, 128), jnp.int32),
            jax.ShapeDtypeStruct((N_TOK, 128), F32),
            jax.ShapeDtypeStruct((N_TOK, 128), jnp.int32),
            jax.ShapeDtypeStruct((1, N_EXPERTS), F32),
        ],
        scratch_shapes=[pltpu.VMEM((1, N_EXPERTS), F32)],
        compiler_params=_params(),
        name="outproj_router",
    )(xp, xs, yp, ys, op, os_, ag, wos, woa, fg, wr, br, ltri)


def _scatter_kernel(dest_ref, pend_ref, h_ref, rows_ref, zero_ref, sem, zsem):
    i = pl.program_id(0)

    @pl.when(i == 0)
    def _():
        zero_ref[...] = jnp.zeros_like(zero_ref)

        block_tiles = _tiled(MOE_BM)[0]

        def zero_block(b):
            start = pl.multiple_of(b * block_tiles, block_tiles)
            return pltpu.make_async_copy(zero_ref, rows_ref.at[pl.ds(start, block_tiles)], zsem)

        def last_block(e):
            end = pend_ref[e]
            nonempty = end > (pend_ref[e - 1] if e > 0 else 0)
            return nonempty, zero_block(jnp.maximum(end // MOE_BM - 1, 0))

        for e in range(N_EXPERTS):
            nonempty, cp = last_block(e)
            pl.when(nonempty)(cp.start)
        for e in range(N_EXPERTS):
            nonempty, cp = last_block(e)
            pl.when(nonempty)(cp.wait)

        first_unused = pend_ref[N_EXPERTS - 1] // MOE_BM
        lax.fori_loop(first_unused, MOE_BLOCKS, lambda b, c: (zero_block(b).start(), c)[1], 0)
        lax.fori_loop(first_unused, MOE_BLOCKS, lambda b, c: (zero_block(b).wait(), c)[1], 0)

    def issue(r, carry):
        for k in range(TOP_K):
            d = dest_ref[r * TOP_K + k]
            pltpu.make_async_copy(_tile_of(h_ref, r), _tile_of(rows_ref, d), sem).start(priority=k % 2)
        return carry

    lax.fori_loop(0, TM, issue, 0, unroll=ISSUE_UNROLL)
    for _ in range(TOP_K):
        pltpu.make_async_copy(h_ref, rows_ref.at[pl.ds(0, _tiled(TM)[0])], sem).wait()


def _scatter_rows(dest_flat, pad_end, h):
    return pl.pallas_call(
        _scatter_kernel,
        grid=(N_TILES,),
        in_specs=[
            pl.BlockSpec((TM * TOP_K,), lambda i: (i,), memory_space=pltpu.SMEM),
            pl.BlockSpec((N_EXPERTS,), lambda i: (0,), memory_space=pltpu.SMEM),
            pl.BlockSpec(_tiled(TM), lambda i: (i, 0)),
        ],
        out_specs=pl.BlockSpec(memory_space=pl.ANY),
        out_shape=jax.ShapeDtypeStruct(_tiled(MOE_ROWS), U32),
        scratch_shapes=[pltpu.VMEM(_tiled(MOE_BM), U32), pltpu.SemaphoreType.DMA(()),
                        pltpu.SemaphoreType.DMA(())],
        compiler_params=_params(),
        name="moe_scatter",
    )(dest_flat, pad_end, h)


def _expert_kernel(be_ref, nu_ref, nxt_ref, x_ref, wgu_hbm, bgu_ref, wd_hbm, bd_ref, y_ref,
                   wgu_f, wd_f, wgu_s, wd_s, sem):
    i = pl.program_id(0)
    active = i < nu_ref[0]
    e = be_ref[i]

    def fetch(expert):
        return (pltpu.make_async_copy(wgu_hbm.at[expert], wgu_f, sem.at[0]),
                pltpu.make_async_copy(wd_hbm.at[expert], wd_f, sem.at[1]))

    @pl.when(active & (i == 0))
    def _():
        for cp in fetch(e):
            cp.start()

    @pl.when(active & ((i == 0) | (e != be_ref[jnp.maximum(i - 1, 0)])))
    def _():
        for cp in fetch(e):
            cp.wait()
        wgu_s[...] = wgu_f[...].astype(BF16)
        wd_s[...] = wd_f[...].astype(BF16)
        nxt = nxt_ref[e]

        @pl.when(nxt >= 0)
        def _():
            for cp in fetch(nxt):
                cp.start()

    @pl.when(active)
    def _():
        gu = _dot(_tiles_to_rows(x_ref, MOE_BM).astype(BF16), wgu_s[...]) + bgu_ref[0]
        gate = jnp.minimum(gu[:, :D_FF], SWIGLU_LIMIT)
        up = jnp.clip(gu[:, D_FF:], -SWIGLU_LIMIT, SWIGLU_LIMIT)
        act = (up + 1.0) * gate * _sigmoid(gate * SWIGLU_ALPHA)
        _rows_to_tiles(y_ref, _dot(act.astype(BF16), wd_s[...]) + bd_ref[0])

    @pl.when(jnp.logical_not(active))
    def _():
        y_ref[...] = jnp.zeros_like(y_ref)


def _experts(block_expert, n_used, next_expert, rows, wgu, bgu, wd, bd):
    grid_spec = pltpu.PrefetchScalarGridSpec(
        num_scalar_prefetch=3,
        grid=(MOE_BLOCKS,),
        in_specs=[
            pl.BlockSpec(_tiled(MOE_BM), lambda i, be, nu, nx: (jnp.minimum(i, nu[0] - 1), 0)),
            pl.BlockSpec(memory_space=pl.ANY),
            pl.BlockSpec((1, 1, 2 * D_FF), lambda i, be, nu, nx: (be[i], 0, 0)),
            pl.BlockSpec(memory_space=pl.ANY),
            pl.BlockSpec((1, 1, D_MODEL), lambda i, be, nu, nx: (be[i], 0, 0)),
        ],
        out_specs=pl.BlockSpec(_tiled(MOE_BM), lambda i, be, nu, nx: (i, 0)),
        scratch_shapes=[pltpu.VMEM((D_MODEL, 2 * D_FF), F32), pltpu.VMEM((D_FF, D_MODEL), F32),
                        pltpu.VMEM((D_MODEL, 2 * D_FF), BF16), pltpu.VMEM((D_FF, D_MODEL), BF16),
                        pltpu.SemaphoreType.DMA((2,))],
    )
    return pl.pallas_call(
        _expert_kernel,
        grid_spec=grid_spec,
        out_shape=jax.ShapeDtypeStruct(_tiled(MOE_ROWS), U32),
        compiler_params=_params(),
        name="moe_experts",
    )(block_expert, n_used, next_expert, rows, wgu, bgu, wd, bd)


def _combine_kernel(dest_ref, dest_next_ref, gate_ref, xmid_ref, g_ref, rows_ref, yp_ref, ys_ref, buf_ref, sem):
    i = pl.program_id(0)
    slot = i % 2

    def issue(idx_ref, s):
        def body(r, carry):
            for k in range(TOP_K):
                d = idx_ref[r * TOP_K + k]
                pltpu.make_async_copy(_tile_of(rows_ref, d), _tile_of(buf_ref.at[s], k * TM + r),
                                      sem.at[s]).start(priority=k % 2)
            return carry

        lax.fori_loop(0, TM, body, 0, unroll=ISSUE_UNROLL)

    @pl.when(i == 0)
    def _():
        issue(dest_ref, 0)

    @pl.when(i + 1 < N_TILES)
    def _():
        issue(dest_next_ref, 1 - slot)

    buf = buf_ref.at[slot]
    slot_tiles = _tiled(TM)[0]
    for k in range(TOP_K):
        pltpu.make_async_copy(rows_ref.at[pl.ds(0, slot_tiles)], buf.at[pl.ds(k * slot_tiles, slot_tiles)],
                              sem.at[slot]).wait()
    acc = xmid_ref[...]
    for k in range(TOP_K):
        acc = acc + _tiles_to_rows(buf, TM, first=k * TM) * gate_ref[:, k:k + 1]
    y = _rms(acc, g_ref[...])

    @pl.when(i < N_PROMPT_TILES)
    def _():
        yp_ref[...] = y

    @pl.when(i == N_PROMPT_TILES)
    def _():
        ys_ref[...] = y


def _combine(dest_flat, gates, xmid, g, y_rows):
    return pl.pallas_call(
        _combine_kernel,
        grid=(N_TILES,),
        in_specs=[
            pl.BlockSpec((TM * TOP_K,), lambda i: (i,), memory_space=pltpu.SMEM),
            pl.BlockSpec((TM * TOP_K,), lambda i: (jnp.minimum(i + 1, N_TILES - 1),), memory_space=pltpu.SMEM),
            pl.BlockSpec((TM, 128), lambda i: (i, 0)),
            pl.BlockSpec((TM, D_MODEL), lambda i: (i, 0)),
            _const_spec((1, D_MODEL)),
            pl.BlockSpec(memory_space=pl.ANY),
        ],
        out_specs=[
            pl.BlockSpec((TM, D_MODEL), lambda i: (jnp.minimum(i, N_PROMPT_TILES - 1), 0)),
            pl.BlockSpec((TM, D_MODEL), lambda i: (0, 0)),
        ],
        out_shape=[
            jax.ShapeDtypeStruct((N_PROMPT, D_MODEL), F32),
            jax.ShapeDtypeStruct((N_SAMPLE, D_MODEL), F32),
        ],
        scratch_shapes=[pltpu.VMEM((2,) + _tiled(TOP_K * TM), U32), pltpu.SemaphoreType.DMA((2,))],
        compiler_params=_params(),
        name="moe_combine",
    )(dest_flat, dest_flat, gates, xmid, g, y_rows)


def _band_bias(table):
    n_diag = BAND + CHUNK - 1
    idx = np.clip(ATT_LEFT + (CHUNK - 1) - np.arange(n_diag), -REL_CLIP, REL_CLIP) + REL_CLIP
    pick = (np.arange(2 * REL_CLIP + 1)[:, None] == idx[None, :]).astype(np.float32)
    diag = jnp.dot(table, jnp.asarray(pick), precision=lax.Precision.HIGHEST)
    return jnp.stack([diag[:, CHUNK - 1 - qi:CHUNK - 1 - qi + BAND] for qi in range(CHUNK)], axis=1)


def _pair_rows(v):
    return jnp.repeat(v.reshape(HEAD_PAIRS, 2), CHUNK, axis=1)


def _layer(l, xp, xs, cache_k, cache_v, state_conv, state_ssm,
           norm_mix_g, w_in, conv_w, conv_b, dt_bias, a_log, d_skip, ssm_norm_g,
           att_norm_g, rel_bias_table, w_out, norm_ffn_g, w_router, b_router,
           w_gate_up, b_gate_up, w_down, b_down, norm_final_g):
    wb = w_in[l].astype(BF16)
    c0 = D_SSM
    c1 = c0 + D_CONV
    c2 = c1 + SSM_HEADS
    c3 = c2 + D_ATT
    c4 = c3 + D_ATT
    z, xbc, dt, dtt, q, k, v, kf, vf, ctail = _inproj(
        xp, xs, norm_mix_g[l][None], wb[:, :c0], wb[:, c0:c1], wb[:, c1:c2], wb[:, c1:c2].T,
        wb[:, c2:c3], wb[:, c3:c4], wb[:, c4:])

    n_chunks = N_TOK // CHUNK
    dtp = dtt.reshape(HEAD_PAIRS, 2, n_chunks, CHUNK).transpose(2, 0, 1, 3).reshape(n_chunks, HEAD_PAIRS, 128)
    hp = jnp.arange(D_SSM) // SSM_HEAD_DIM
    expand = (hp[None, :] == jnp.arange(SSM_HEADS)[:, None]).astype(BF16)
    r128 = jnp.arange(128)
    triu2 = ((r128[:, None] // CHUNK == r128[None, :] // CHUNK) & (r128[:, None] <= r128[None, :])).astype(BF16)
    consts = (conv_w[l], conv_b[l][None], dt_bias[l][None], _pair_rows(dt_bias[l]),
              a_log[l][None], _pair_rows(a_log[l]),
              jnp.repeat(d_skip[l], SSM_HEAD_DIM)[None], ssm_norm_g[l][None],
              expand, triu2)
    y_ssm_p, ssm_p = _ssd_prompt(z, xbc, dt, dtp, consts)
    y_ssm_s, ssm_s = _ssd_sample(z, xbc, dt, dtp, state_conv[l],
                                 state_ssm[l].reshape(DEC_BATCH, D_SSM, D_STATE), consts)

    bias2 = _band_bias(rel_bias_table[l]).reshape(ATT_PAIRS, 2 * CHUNK, BAND)
    o_att_p = _attn_prompt(q, k, v, bias2)
    o_att_s = _attn_sample(q, k, v, cache_k[l].reshape(DEC_BATCH, ATT_LEFT, D_ATT),
                           cache_v[l].reshape(DEC_BATCH, ATT_LEFT, D_ATT), bias2)

    wo = w_out[l].astype(BF16)
    wr = w_router[l]
    wr1 = wr.astype(BF16)
    wr2 = (wr - wr1.astype(F32)).astype(BF16)
    ltri = jnp.tril(jnp.ones((TM, TM), BF16), -1)
    xmid, h, top_idx, gates, rank, counts = _outproj(
        xp, xs, y_ssm_p, y_ssm_s, o_att_p, o_att_s, att_norm_g[l][None], wo[:D_SSM], wo[D_SSM:], norm_ffn_g[l][None],
        jnp.stack([wr1, wr2]), b_router[l][None], ltri)

    counts = counts[0].astype(jnp.int32)
    padded = (counts + MOE_BM - 1) // MOE_BM * MOE_BM
    pad_end = jnp.cumsum(padded)
    pad_start = pad_end - padded
    experts = jnp.arange(N_EXPERTS, dtype=jnp.int32)
    start_of = jnp.sum(jnp.where(top_idx[:, :TOP_K, None] == experts, pad_start, 0), axis=-1)
    dest = (start_of + rank[:, :TOP_K]).reshape(-1).astype(jnp.int32)
    block_start = jnp.arange(MOE_BLOCKS, dtype=jnp.int32) * MOE_BM
    block_expert = jnp.minimum(jnp.sum((pad_end[None, :] <= block_start[:, None]).astype(jnp.int32), axis=1),
                               N_EXPERTS - 1).astype(jnp.int32)
    n_used = (pad_end[-1:] // MOE_BM).astype(jnp.int32)
    later_nonempty = (experts[None, :] > experts[:, None]) & (padded[None, :] > 0)
    next_expert = jnp.min(jnp.where(later_nonempty, experts[None, :], N_EXPERTS), axis=1)
    next_expert = jnp.where(next_expert < N_EXPERTS, next_expert, -1).astype(jnp.int32)

    rows = _scatter_rows(dest, pad_end.astype(jnp.int32), h)
    y_rows = _experts(block_expert, n_used, next_expert, rows, w_gate_up[l], b_gate_up[l][:, None, :],
                      w_down[l], b_down[l][:, None, :])
    y_p, y_s = _combine(dest, gates, xmid, norm_final_g[None], y_rows)

    keep = min(ATT_LEFT, SEQ)
    k_p = kf[:BATCH, TM - keep:].reshape(BATCH, keep, ATT_HEADS, ATT_HEAD_DIM)
    v_p = vf[:BATCH, TM - keep:].reshape(BATCH, keep, ATT_HEADS, ATT_HEAD_DIM)
    k_s = kf[BATCH].reshape(DEC_BATCH, DEC_SEQ, ATT_HEADS, ATT_HEAD_DIM)
    v_s = vf[BATCH].reshape(DEC_BATCH, DEC_SEQ, ATT_HEADS, ATT_HEAD_DIM)
    conv_p = ctail[:BATCH, -(CONV_W - 1):]
    conv_s = ctail[BATCH].reshape(DEC_BATCH, 8, D_CONV)[:, -(CONV_W - 1):]
    ssm_p = ssm_p.reshape(BATCH, SSM_HEADS, SSM_HEAD_DIM, D_STATE)
    ssm_s = ssm_s.reshape(DEC_BATCH, SSM_HEADS, SSM_HEAD_DIM, D_STATE)
    return (y_p.reshape(BATCH, SEQ, D_MODEL), y_s.reshape(DEC_BATCH, DEC_SEQ, D_MODEL),
            k_p, v_p, conv_p, ssm_p, k_s, v_s, conv_s, ssm_s)


def kernel(x_prompt, x_sample, cache_k, cache_v, state_conv, state_ssm, norm_mix_g, w_in, conv_w, conv_b,
           dt_bias, a_log, d_skip, ssm_norm_g, att_norm_g, rel_bias_table, w_out, norm_ffn_g, w_router,
           b_router, w_gate_up, b_gate_up, w_down, b_down, norm_final_g):
    assert w_in.shape[0] == 1, "single trunk layer"
    xp = x_prompt.reshape(N_PROMPT, D_MODEL)
    xs = x_sample.reshape(N_SAMPLE, D_MODEL)
    outs = _layer(0, xp, xs, cache_k, cache_v, state_conv, state_ssm,
                  norm_mix_g, w_in, conv_w, conv_b, dt_bias, a_log, d_skip, ssm_norm_g,
                  att_norm_g, rel_bias_table, w_out, norm_ffn_g, w_router, b_router,
                  w_gate_up, b_gate_up, w_down, b_down, norm_final_g)
    y_p, y_s, k_p, v_p, conv_p, ssm_p, k_s, v_s, conv_s, ssm_s = outs
    return (y_p, y_s, k_p[None], v_p[None], conv_p[None], ssm_p[None],
            k_s[None], v_s[None], conv_s[None], ssm_s[None])
```

```python
import jax
import jax.numpy as jnp
import numpy as np
from jax import lax
from jax.experimental import pallas as pl
from jax.experimental.pallas import tpu as pltpu

D_MODEL = 1024
BATCH = 8
SEQ = 2048
DEC_BATCH = 8
DEC_SEQ = 64
CHUNK = 64
SSM_HEADS = 16
SSM_HEAD_DIM = 64
D_SSM = SSM_HEADS * SSM_HEAD_DIM
SSM_GROUPS = 2
D_STATE = 128
CONV_W = 4
D_BC = SSM_GROUPS * D_STATE
D_CONV = D_SSM + 2 * D_BC
ATT_HEADS = 8
ATT_HEAD_DIM = 64
D_ATT = ATT_HEADS * ATT_HEAD_DIM
LEFT_CHUNKS = 8
ATT_LEFT = LEFT_CHUNKS * CHUNK
BAND = ATT_LEFT + CHUNK
REL_CLIP = 128
ATT_SCALE = ATT_HEAD_DIM ** -0.5
N_EXPERTS = 32
TOP_K = 4
D_FF = D_MODEL
SWIGLU_ALPHA = 1.702
SWIGLU_LIMIT = 7.0
EPS = 1e-5

F32 = jnp.float32
BF16 = jnp.bfloat16
U32 = jnp.uint32

N_PROMPT = BATCH * SEQ
N_SAMPLE = DEC_BATCH * DEC_SEQ
N_TOK = N_PROMPT + N_SAMPLE
TM = 512
N_PROMPT_TILES = N_PROMPT // TM
N_TILES = N_TOK // TM
TILES_PER_SEQ = SEQ // TM
CHUNKS_PER_TILE = TM // CHUNK
N_CHUNKS_SEQ = SEQ // CHUNK
HEAD_PAIRS = SSM_HEADS // 2
PAIRS_PER_GROUP = HEAD_PAIRS // SSM_GROUPS
ATT_PAIRS = ATT_HEADS // 2
SSD_BLOCK = 2
CONV_HEAD = 8
MOE_BM = 512
N_ASSIGN = N_TOK * TOP_K
MOE_BLOCKS = N_ASSIGN // MOE_BM + N_EXPERTS
MOE_ROWS = MOE_BLOCKS * MOE_BM
ISSUE_UNROLL = 8
ROW_TILE = (4, 128)
assert 2 * ROW_TILE[0] * ROW_TILE[1] == D_MODEL
VMEM_LIMIT = 56 * 1024 * 1024


def _dot(a, b):
    return jnp.dot(a, b, preferred_element_type=F32)


def _dot_nt(a, b):
    return lax.dot_general(a, b, (((1,), (1,)), ((), ())), preferred_element_type=F32)


def _dot_tn(a, b):
    return lax.dot_general(a, b, (((0,), (0,)), ((), ())), preferred_element_type=F32)


def _split3(x):
    x1 = x.astype(BF16)
    r1 = x - x1.astype(F32)
    x2 = r1.astype(BF16)
    r2 = r1 - x2.astype(F32)
    return x1, x2, r2.astype(BF16)


def _dot_exact_rhs(x, m):
    x1, x2, x3 = _split3(x)
    return _dot(x1, m) + _dot(x2, m) + _dot(x3, m)


def _dot_exact_lhs(m, x):
    x1, x2, x3 = _split3(x)
    return _dot(m, x1) + _dot(m, x2) + _dot(m, x3)


def _rms(x, g):
    return x * lax.rsqrt(jnp.mean(x * x, axis=-1, keepdims=True) + EPS) * g


def _sigmoid(x):
    return 1.0 / (1.0 + jnp.exp(-x))


def _softplus(x):
    return jnp.maximum(x, 0.0) + jnp.log(1.0 + jnp.exp(-jnp.abs(x)))


def _tiled(n):
    return (n * ROW_TILE[0], ROW_TILE[1])


def _tile_of(ref, row):
    return ref.at[pl.ds(pl.multiple_of(row * ROW_TILE[0], ROW_TILE[0]), ROW_TILE[0])]


def _rows_to_tiles(ref, x, first=0):
    sub, lanes = ROW_TILE
    half = D_MODEL // 2
    hi = lax.bitcast_convert_type(x[:, :half].astype(BF16).astype(F32), U32)
    lo = lax.bitcast_convert_type(x[:, half:].astype(BF16).astype(F32), U32)
    words = hi | (lo >> 16)
    for j in range(sub):
        ref[pl.ds(first * sub + j, x.shape[0], stride=sub), :] = words[:, j * lanes:(j + 1) * lanes]


def _tiles_to_rows(ref, n, first=0):
    sub = ROW_TILE[0]
    words = jnp.concatenate([ref[pl.ds(first * sub + j, n, stride=sub), :] for j in range(sub)], axis=1)
    hi = lax.bitcast_convert_type(words & jnp.uint32(0xFFFF0000), F32)
    lo = lax.bitcast_convert_type(words << 16, F32)
    return jnp.concatenate([hi, lo], axis=1)


def _const_spec(shape):
    nd = len(shape)
    return pl.BlockSpec(shape, lambda *_: (0,) * nd)


def _params(n_axes=1):
    return pltpu.CompilerParams(dimension_semantics=("arbitrary",) * n_axes,
                                vmem_limit_bytes=VMEM_LIMIT)


def _inproj_kernel(xp_ref, xs_ref, g_ref, wz_ref, wxbc_ref, wdt_ref, wdtt_ref, wq_ref, wk_ref, wv_ref,
                   z_ref, xbc_ref, dt_ref, dtt_ref, q_ref, k_ref, v_ref, kp_ref, vp_ref, ks_ref, vs_ref, ctail_ref):
    i = pl.program_id(0)
    x = jnp.where(i == N_PROMPT_TILES, xs_ref[...], xp_ref[...])
    h = _rms(x, g_ref[...]).astype(BF16)
    z_ref[...] = _dot(h, wz_ref[...]).astype(BF16)
    xbc = _dot(h, wxbc_ref[...])
    xbc_ref[...] = xbc.astype(BF16)
    for c in range(CHUNKS_PER_TILE):
        ctail_ref[0, c * 8:(c + 1) * 8, :] = xbc[c * CHUNK + CHUNK - 8:(c + 1) * CHUNK, :]
    dt_ref[...] = _dot(h, wdt_ref[...])
    dtt_ref[...] = _dot_nt(wdtt_ref[...], h)
    q_ref[...] = (_dot(h, wq_ref[...]) * ATT_SCALE).astype(BF16)
    k = _dot(h, wk_ref[...])
    v = _dot(h, wv_ref[...])
    k_ref[...] = k.astype(BF16)
    v_ref[...] = v.astype(BF16)

    @pl.when(i < N_PROMPT_TILES)
    def _():
        kp_ref[0] = k
        vp_ref[0] = v

    @pl.when(i == N_PROMPT_TILES)
    def _():
        ks_ref[...] = k
        vs_ref[...] = v


def _inproj(xp, xs, g, wz, wxbc, wdt, wdtt, wq, wk, wv):
    tok = lambda n: pl.BlockSpec((TM, n), lambda i: (i, 0))
    tail_idx = lambda i: (i // TILES_PER_SEQ, 0, 0)
    seq_idx = lambda i: (jnp.minimum(i // TILES_PER_SEQ, BATCH - 1), 0, 0)
    n_tail = BATCH + 1
    return pl.pallas_call(
        _inproj_kernel,
        grid=(N_TILES,),
        in_specs=[
            pl.BlockSpec((TM, D_MODEL), lambda i: (jnp.minimum(i, N_PROMPT_TILES - 1), 0)),
            pl.BlockSpec((TM, D_MODEL), lambda i: (0, 0)),
            _const_spec((1, D_MODEL)),
            _const_spec((D_MODEL, D_SSM)),
            _const_spec((D_MODEL, D_CONV)),
            _const_spec((D_MODEL, SSM_HEADS)),
            _const_spec((SSM_HEADS, D_MODEL)),
            _const_spec((D_MODEL, D_ATT)),
            _const_spec((D_MODEL, D_ATT)),
            _const_spec((D_MODEL, D_ATT)),
        ],
        out_specs=[
            tok(D_SSM), tok(D_CONV), tok(SSM_HEADS),
            pl.BlockSpec((SSM_HEADS, TM), lambda i: (0, i)),
            tok(D_ATT), tok(D_ATT), tok(D_ATT),
            pl.BlockSpec((1, TM, D_ATT), seq_idx),
            pl.BlockSpec((1, TM, D_ATT), seq_idx),
            pl.BlockSpec((TM, D_ATT), lambda i: (0, 0)),
            pl.BlockSpec((TM, D_ATT), lambda i: (0, 0)),
            pl.BlockSpec((1, CHUNKS_PER_TILE * 8, D_CONV), tail_idx),
        ],
        out_shape=[
            jax.ShapeDtypeStruct((N_TOK, D_SSM), BF16),
            jax.ShapeDtypeStruct((N_TOK, D_CONV), BF16),
            jax.ShapeDtypeStruct((N_TOK, SSM_HEADS), F32),
            jax.ShapeDtypeStruct((SSM_HEADS, N_TOK), F32),
            jax.ShapeDtypeStruct((N_TOK, D_ATT), BF16),
            jax.ShapeDtypeStruct((N_TOK, D_ATT), BF16),
            jax.ShapeDtypeStruct((N_TOK, D_ATT), BF16),
            jax.ShapeDtypeStruct((BATCH, TM, D_ATT), F32),
            jax.ShapeDtypeStruct((BATCH, TM, D_ATT), F32),
            jax.ShapeDtypeStruct((N_SAMPLE, D_ATT), F32),
            jax.ShapeDtypeStruct((N_SAMPLE, D_ATT), F32),
            jax.ShapeDtypeStruct((n_tail, CHUNKS_PER_TILE * 8, D_CONV), F32),
        ],
        compiler_params=_params(),
        name="inproj",
    )(xp, xs, g, wz, wxbc, wdt, wdtt, wq, wk, wv)


def _ssd_tile(n_chunks, z_ref, xbc_ref, dt_ref, dtp_ref, cw_ref, cb_ref, dtb_ref, dtbp_ref,
              alog_h_ref, alog_p_ref, dskip_e_ref, ng_ref, expand_ref, triu2_ref,
              y_ref, xw_ref, state_ref):
    nb = SSD_BLOCK if n_chunks % SSD_BLOCK == 0 else 1
    rb = nb * CHUNK
    half = D_SSM // SSM_GROUPS
    xw_ref[CONV_HEAD:CONV_HEAD + n_chunks * CHUNK, :] = xbc_ref[...].astype(F32)

    a_h = -jnp.exp(alog_h_ref[...])
    a_p = jnp.concatenate([-jnp.exp(alog_p_ref[...])] * nb, axis=0)
    dtb_p = jnp.concatenate([dtbp_ref[...]] * nb, axis=0)
    tr_r = lax.broadcasted_iota(jnp.int32, (rb, rb), 0)
    tr_c = lax.broadcasted_iota(jnp.int32, (rb, rb), 1)
    tril_b = jnp.where((tr_r // CHUNK == tr_c // CHUNK) & (tr_c <= tr_r), 1.0, 0.0).astype(BF16)
    row_i = lax.broadcasted_iota(jnp.int32, (CHUNK, 128), 0)
    col_i = lax.broadcasted_iota(jnp.int32, (CHUNK, 128), 1)
    causal2 = row_i >= (col_i % CHUNK)
    sel_r = lax.broadcasted_iota(jnp.int32, ((CONV_W - 1) * rb, CONV_HEAD + rb), 0)
    sel_c = lax.broadcasted_iota(jnp.int32, ((CONV_W - 1) * rb, CONV_HEAD + rb), 1)
    shift_sel = (sel_c == sel_r % rb + sel_r // rb + CONV_HEAD - (CONV_W - 1)).astype(F32)
    bd_r = lax.broadcasted_iota(jnp.int32, (128, 128), 0) // CHUNK
    bd_c = lax.broadcasted_iota(jnp.int32, (128, 128), 1) // CHUNK
    blockdiag = bd_r == bd_c

    def block(i, carry):
        r0 = pl.multiple_of(i * rb, rb)
        win = xw_ref[pl.ds(r0, CONV_HEAD + rb), :]
        shifted = _dot(shift_sel, win)
        acc = cb_ref[...] + cw_ref[CONV_W - 1:CONV_W, :] * win[CONV_HEAD:, :]
        for tap in range(CONV_W - 1):
            acc = acc + cw_ref[tap:tap + 1, :] * shifted[tap * rb:(tap + 1) * rb, :]
        xa = acc * _sigmoid(acc)
        xs = xa[:, 0:D_SSM]
        bm = xa[:, D_SSM:D_SSM + D_BC].astype(BF16)
        cm = xa[:, D_SSM + D_BC:D_CONV].astype(BF16)
        dt = _softplus(dt_ref[pl.ds(r0, rb), :] + dtb_ref[...])
        dt_e = _dot_exact_rhs(dt, expand_ref[...])
        acum = _dot_exact_rhs(_dot_exact_lhs(tril_b, dt * a_h), expand_ref[...])
        dtp = _softplus(dtp_ref[pl.ds(i * nb, nb)].reshape(nb * HEAD_PAIRS, 128) + dtb_p)
        acum_p = _dot_exact_rhs(dtp * a_p, triu2_ref[...])
        xdt = xs * dt_e
        a_last = [acum[(c + 1) * CHUNK - 1:(c + 1) * CHUNK, :] for c in range(nb)]
        a_end = jnp.concatenate([jnp.broadcast_to(a, (CHUNK, D_SSM)) for a in a_last], axis=0)
        xdt_end = (xdt * jnp.exp(a_end - acum)).astype(BF16)

        y_diag, new_s = [], []
        for c in range(nb):
            rs = slice(c * CHUNK, (c + 1) * CHUNK)
            y_parts = []
            for g in range(SSM_GROUPS):
                bg = bm[rs, g * D_STATE:(g + 1) * D_STATE]
                cg = cm[rs, g * D_STATE:(g + 1) * D_STATE]
                cb2 = _dot_nt(cg, jnp.concatenate([bg, bg], axis=0))
                for jj in range(PAIRS_PER_GROUP):
                    j = g * PAIRS_PER_GROUP + jj
                    seg = acum[rs, j * 128:(j + 1) * 128] - acum_p[c * HEAD_PAIRS + j:c * HEAD_PAIRS + j + 1, :]
                    decay = jnp.exp(jnp.where(causal2, seg, -jnp.inf))
                    x2 = xdt[rs, j * 128:(j + 1) * 128]
                    rhs = jnp.where(blockdiag, jnp.concatenate([x2, x2], axis=0), 0.0).astype(BF16)
                    y_parts.append(_dot((cb2 * decay).astype(BF16), rhs))
            y_diag.append(jnp.concatenate(y_parts, axis=1))
            new_s.append(jnp.concatenate(
                [_dot_tn(bm[rs, g * D_STATE:(g + 1) * D_STATE], xdt_end[rs, g * half:(g + 1) * half])
                 for g in range(SSM_GROUPS)], axis=1))

        y_off = []
        for c in range(nb):
            rs = slice(c * CHUNK, (c + 1) * CHUNK)
            state = state_ref[...]
            y_off.append(jnp.concatenate(
                [_dot(cm[rs, g * D_STATE:(g + 1) * D_STATE], state[:, g * half:(g + 1) * half].astype(BF16))
                 for g in range(SSM_GROUPS)], axis=1))
            state_ref[...] = state * jnp.exp(a_last[c]) + new_s[c]

        y = (jnp.concatenate(y_diag, axis=0) + jnp.concatenate(y_off, axis=0) * jnp.exp(acum)
             + dskip_e_ref[...] * xs)
        zc = z_ref[pl.ds(r0, rb), :].astype(F32)
        y = y * (zc * _sigmoid(zc))
        yn = jnp.concatenate(
            [y[:, g * half:(g + 1) * half]
             * lax.rsqrt(jnp.mean(jnp.square(y[:, g * half:(g + 1) * half]), axis=-1, keepdims=True) + EPS)
             for g in range(SSM_GROUPS)], axis=1)
        y_ref[pl.ds(r0, rb), :] = (yn * ng_ref[...]).astype(BF16)
        return carry

    lax.fori_loop(0, n_chunks // nb, block, 0)


def _state_store(state_ref, out_ref):
    for j in range(HEAD_PAIRS):
        out_ref[0, j * 128:(j + 1) * 128, :] = state_ref[:, j * 128:(j + 1) * 128].T


def _ssd_prompt_kernel(z_ref, xbc_ref, dt_ref, dtp_ref, cw_ref, cb_ref, dtb_ref, dtbp_ref,
                       alog_h_ref, alog_p_ref, dskip_e_ref, ng_ref, expand_ref, triu2_ref,
                       y_ref, ssm_ref, xw_ref, state_ref, tail_ref):
    t = pl.program_id(1)

    @pl.when(t == 0)
    def _():
        state_ref[...] = jnp.zeros_like(state_ref)
        xw_ref[0:CONV_HEAD, :] = jnp.zeros((CONV_HEAD, D_CONV), F32)

    @pl.when(t > 0)
    def _():
        xw_ref[0:CONV_HEAD, :] = tail_ref[...]

    _ssd_tile(CHUNKS_PER_TILE, z_ref, xbc_ref, dt_ref, dtp_ref, cw_ref, cb_ref, dtb_ref, dtbp_ref,
              alog_h_ref, alog_p_ref, dskip_e_ref, ng_ref, expand_ref, triu2_ref,
              y_ref, xw_ref, state_ref)
    tail_ref[...] = xw_ref[TM:TM + CONV_HEAD, :]

    @pl.when(t == TILES_PER_SEQ - 1)
    def _():
        _state_store(state_ref, ssm_ref)


def _ssd_sample_kernel(z_ref, xbc_ref, dt_ref, dtp_ref, cprev_ref, sprev_ref,
                       cw_ref, cb_ref, dtb_ref, dtbp_ref,
                       alog_h_ref, alog_p_ref, dskip_e_ref, ng_ref, expand_ref, triu2_ref,
                       y_ref, ssm_ref, xw_ref, state_ref):
    xw_ref[0:CONV_HEAD, :] = jnp.zeros((CONV_HEAD, D_CONV), F32)
    xw_ref[CONV_HEAD - (CONV_W - 1):CONV_HEAD, :] = cprev_ref[0]
    for j in range(HEAD_PAIRS):
        state_ref[:, j * 128:(j + 1) * 128] = sprev_ref[0, j * 128:(j + 1) * 128, :].T
    _ssd_tile(1, z_ref, xbc_ref, dt_ref, dtp_ref, cw_ref, cb_ref, dtb_ref, dtbp_ref,
              alog_h_ref, alog_p_ref, dskip_e_ref, ng_ref, expand_ref, triu2_ref,
              y_ref, xw_ref, state_ref)
    _state_store(state_ref, ssm_ref)


def _ssd_const_specs():
    return [
        _const_spec((CONV_W, D_CONV)), _const_spec((1, D_CONV)),
        _const_spec((1, SSM_HEADS)), _const_spec((HEAD_PAIRS, 128)),
        _const_spec((1, SSM_HEADS)), _const_spec((HEAD_PAIRS, 128)),
        _const_spec((1, D_SSM)), _const_spec((1, D_SSM)),
        _const_spec((SSM_HEADS, D_SSM)), _const_spec((128, 128)),
    ]


def _ssd_prompt(z, xbc, dt, dtp, consts):
    tile = lambda b, t: (b * TILES_PER_SEQ + t, 0)
    return pl.pallas_call(
        _ssd_prompt_kernel,
        grid=(BATCH, TILES_PER_SEQ),
        in_specs=[
            pl.BlockSpec((TM, D_SSM), tile),
            pl.BlockSpec((TM, D_CONV), tile),
            pl.BlockSpec((TM, SSM_HEADS), tile),
            pl.BlockSpec((CHUNKS_PER_TILE, HEAD_PAIRS, 128), lambda b, t: (b * TILES_PER_SEQ + t, 0, 0)),
        ] + _ssd_const_specs(),
        out_specs=[
            pl.BlockSpec((TM, D_SSM), tile),
            pl.BlockSpec((1, D_SSM, D_STATE), lambda b, t: (b, 0, 0)),
        ],
        out_shape=[
            jax.ShapeDtypeStruct((N_PROMPT, D_SSM), BF16),
            jax.ShapeDtypeStruct((BATCH, D_SSM, D_STATE), F32),
        ],
        scratch_shapes=[
            pltpu.VMEM((CONV_HEAD + TM, D_CONV), F32),
            pltpu.VMEM((D_STATE, D_SSM), F32),
            pltpu.VMEM((CONV_HEAD, D_CONV), F32),
        ],
        compiler_params=_params(2),
        name="ssd_prompt",
    )(z, xbc, dt, dtp, *consts)


def _ssd_sample(z, xbc, dt, dtp, conv_prev, ssm_prev, consts):
    first = N_PROMPT // CHUNK
    row = lambda b: (first + b, 0)
    return pl.pallas_call(
        _ssd_sample_kernel,
        grid=(DEC_BATCH,),
        in_specs=[
            pl.BlockSpec((CHUNK, D_SSM), row),
            pl.BlockSpec((CHUNK, D_CONV), row),
            pl.BlockSpec((CHUNK, SSM_HEADS), row),
            pl.BlockSpec((1, HEAD_PAIRS, 128), lambda b: (first + b, 0, 0)),
            pl.BlockSpec((1, CONV_W - 1, D_CONV), lambda b: (b, 0, 0)),
            pl.BlockSpec((1, D_SSM, D_STATE), lambda b: (b, 0, 0)),
        ] + _ssd_const_specs(),
        out_specs=[
            pl.BlockSpec((CHUNK, D_SSM), lambda b: (b, 0)),
            pl.BlockSpec((1, D_SSM, D_STATE), lambda b: (b, 0, 0)),
        ],
        out_shape=[
            jax.ShapeDtypeStruct((N_SAMPLE, D_SSM), BF16),
            jax.ShapeDtypeStruct((DEC_BATCH, D_SSM, D_STATE), F32),
        ],
        scratch_shapes=[
            pltpu.VMEM((CONV_HEAD + CHUNK, D_CONV), F32),
            pltpu.VMEM((D_STATE, D_SSM), F32),
        ],
        compiler_params=_params(),
        name="ssd_sample",
    )(z, xbc, dt, dtp, conv_prev, ssm_prev, *consts)


def _attn_chunks(n_chunks, first_chunk, q_ref, kpad_ref, vpad_ref, bias_ref, o_ref):
    lane = lax.broadcasted_iota(jnp.int32, (CHUNK, 128), 1)
    low = lane < ATT_HEAD_DIM
    kj = lax.broadcasted_iota(jnp.int32, (2 * CHUNK, BAND), 1)

    def chunk(c, carry, masked):
        r0 = pl.multiple_of(c * CHUNK, CHUNK)
        if masked:
            valid = kj >= (LEFT_CHUNKS - (first_chunk + c)) * CHUNK
        scores = []
        for j in range(ATT_PAIRS):
            qp = q_ref[pl.ds(r0, CHUNK), j * 128:(j + 1) * 128]
            zero = jnp.zeros_like(qp)
            q2 = jnp.concatenate([jnp.where(low, qp, zero), jnp.where(low, zero, qp)], axis=0)
            kb = kpad_ref[pl.ds(r0, BAND), j * 128:(j + 1) * 128]
            s = _dot_nt(q2, kb) + bias_ref[j]
            scores.append(jnp.where(valid, s, -jnp.inf) if masked else s)
        probs = []
        for s in scores:
            e = jnp.exp(s - jnp.max(s, axis=-1, keepdims=True))
            probs.append((e.astype(BF16), jnp.sum(e, axis=-1, keepdims=True)))
        outs = []
        for j, (e, denom) in enumerate(probs):
            vb = vpad_ref[pl.ds(r0, BAND), j * 128:(j + 1) * 128]
            r = _dot(e, vb) / denom
            outs.append(jnp.where(low, r[0:CHUNK], r[CHUNK:2 * CHUNK]))
        o_ref[pl.ds(r0, CHUNK), :] = jnp.concatenate(outs, axis=1).astype(BF16)
        return carry

    n_masked = min(max(LEFT_CHUNKS - first_chunk, 0), n_chunks)
    for lo, hi, masked in ((0, n_masked, True), (n_masked, n_chunks, False)):
        if hi > lo:
            lax.fori_loop(lo, hi, lambda c, carry, masked=masked: chunk(c, carry, masked), 0,
                          unroll=2 if (hi - lo) % 2 == 0 else 1)


def _attn_prompt_kernel(q_ref, k_ref, v_ref, bias_ref, o_ref, kpad_ref, vpad_ref):
    kpad_ref[0:ATT_LEFT, :] = jnp.zeros((ATT_LEFT, D_ATT), BF16)
    vpad_ref[0:ATT_LEFT, :] = jnp.zeros((ATT_LEFT, D_ATT), BF16)
    kpad_ref[ATT_LEFT:ATT_LEFT + SEQ, :] = k_ref[...]
    vpad_ref[ATT_LEFT:ATT_LEFT + SEQ, :] = v_ref[...]
    _attn_chunks(N_CHUNKS_SEQ, 0, q_ref, kpad_ref, vpad_ref, bias_ref, o_ref)


def _attn_sample_kernel(q_ref, k_ref, v_ref, ck_ref, cv_ref, bias_ref, o_ref, kpad_ref, vpad_ref):
    kpad_ref[0:ATT_LEFT, :] = ck_ref[0].astype(BF16)
    vpad_ref[0:ATT_LEFT, :] = cv_ref[0].astype(BF16)
    kpad_ref[ATT_LEFT:BAND, :] = k_ref[...]
    vpad_ref[ATT_LEFT:BAND, :] = v_ref[...]
    _attn_chunks(1, LEFT_CHUNKS, q_ref, kpad_ref, vpad_ref, bias_ref, o_ref)


def _attn_prompt(q, k, v, bias2):
    seq = pl.BlockSpec((SEQ, D_ATT), lambda b: (b, 0))
    return pl.pallas_call(
        _attn_prompt_kernel,
        grid=(BATCH,),
        in_specs=[seq, seq, seq, _const_spec((ATT_PAIRS, 2 * CHUNK, BAND))],
        out_specs=seq,
        out_shape=jax.ShapeDtypeStruct((N_PROMPT, D_ATT), BF16),
        scratch_shapes=[pltpu.VMEM((ATT_LEFT + SEQ, D_ATT), BF16),
                        pltpu.VMEM((ATT_LEFT + SEQ, D_ATT), BF16)],
        compiler_params=_params(),
        name="attn_prompt",
    )(q, k, v, bias2)


def _attn_sample(q, k, v, cache_k, cache_v, bias2):
    first = N_PROMPT // CHUNK
    row = pl.BlockSpec((CHUNK, D_ATT), lambda b: (first + b, 0))
    cache = pl.BlockSpec((1, ATT_LEFT, D_ATT), lambda b: (b, 0, 0))
    return pl.pallas_call(
        _attn_sample_kernel,
        grid=(DEC_BATCH,),
        in_specs=[row, row, row, cache, cache, _const_spec((ATT_PAIRS, 2 * CHUNK, BAND))],
        out_specs=pl.BlockSpec((CHUNK, D_ATT), lambda b: (b, 0)),
        out_shape=jax.ShapeDtypeStruct((N_SAMPLE, D_ATT), BF16),
        scratch_shapes=[pltpu.VMEM((BAND, D_ATT), BF16), pltpu.VMEM((BAND, D_ATT), BF16)],
        compiler_params=_params(),
        name="attn_sample",
    )(q, k, v, cache_k, cache_v, bias2)


def _outproj_kernel(xp_ref, xs_ref, yp_ref, ys_ref, op_ref, os_ref, ag_ref, wos_ref, woa_ref, fg_ref,
                    wr_ref, br_ref, ltri_ref,
                    xmid_ref, h_ref, idx_ref, gate_ref, rank_ref, cnt_ref, carry_ref):
    i = pl.program_id(0)

    @pl.when(i == 0)
    def _():
        carry_ref[...] = jnp.zeros_like(carry_ref)

    is_sample = i == N_PROMPT_TILES
    x = jnp.where(is_sample, xs_ref[...], xp_ref[...])
    y = jnp.where(is_sample, ys_ref[...], yp_ref[...])
    o = jnp.where(is_sample, os_ref[...], op_ref[...])
    o = _rms(o.astype(F32), ag_ref[...]).astype(BF16)
    xm = x + _dot(y, wos_ref[...]) + _dot(o, woa_ref[...])
    xmid_ref[...] = xm
    h = _rms(xm, fg_ref[...])
    _rows_to_tiles(h_ref, h)
    h1 = h.astype(BF16)
    h2 = (h - h1.astype(F32)).astype(BF16)
    w1 = wr_ref[0]
    w2 = wr_ref[1]
    logits = _dot(h1, w1) + (_dot(h1, w2) + _dot(h2, w1)) + br_ref[...]
    eidx = lax.broadcasted_iota(jnp.int32, (TM, N_EXPERTS), 1)
    lane = lax.broadcasted_iota(jnp.int32, (TM, 128), 1)
    work = logits
    vals, sels = [], []
    idx_out = jnp.zeros((TM, 128), jnp.int32)
    for k in range(TOP_K):
        m = jnp.max(work, axis=-1, keepdims=True)
        idx = jnp.min(jnp.where(work == m, eidx, N_EXPERTS), axis=-1, keepdims=True)
        sel = eidx == idx
        vals.append(m)
        sels.append(sel)
        idx_out = jnp.where(lane == k, idx, idx_out)
        work = jnp.where(sel, -jnp.inf, work)
    es = [jnp.exp(v - vals[0]) for v in vals]
    tot = es[0] + es[1] + es[2] + es[3]
    gate_out = jnp.zeros((TM, 128), F32)
    for k in range(TOP_K):
        gate_out = jnp.where(lane == k, es[k] / tot, gate_out)
    idx_ref[...] = idx_out
    gate_ref[...] = gate_out
    multi = jnp.zeros((TM, N_EXPERTS), F32)
    for sel in sels:
        multi = jnp.where(sel, 1.0, multi)
    before = _dot(ltri_ref[...], multi.astype(BF16)) + carry_ref[...]
    rank_out = jnp.zeros((TM, 128), jnp.int32)
    for k in range(TOP_K):
        rk = jnp.sum(jnp.where(sels[k], before, 0.0), axis=-1, keepdims=True).astype(jnp.int32)
        rank_out = jnp.where(lane == k, rk, rank_out)
    rank_ref[...] = rank_out
    carry_ref[...] = carry_ref[...] + jnp.sum(multi, axis=0, keepdims=True)
    cnt_ref[...] = carry_ref[...]


def _outproj(xp, xs, yp, ys, op, os_, ag, wos, woa, fg, wr, br, ltri):
    tok = lambda n: pl.BlockSpec((TM, n), lambda i: (i, 0))
    prompt = lambda n: pl.BlockSpec((TM, n), lambda i: (jnp.minimum(i, N_PROMPT_TILES - 1), 0))
    sample = lambda n: pl.BlockSpec((TM, n), lambda i: (0, 0))
    return pl.pallas_call(
        _outproj_kernel,
        grid=(N_TILES,),
        in_specs=[
            prompt(D_MODEL), sample(D_MODEL), prompt(D_SSM), sample(D_SSM), prompt(D_ATT), sample(D_ATT),
            _const_spec((1, D_ATT)),
            _const_spec((D_SSM, D_MODEL)), _const_spec((D_ATT, D_MODEL)),
            _const_spec((1, D_MODEL)),
            _const_spec((2, D_MODEL, N_EXPERTS)), _const_spec((1, N_EXPERTS)),
            _const_spec((TM, TM)),
        ],
        out_specs=[tok(D_MODEL), pl.BlockSpec(_tiled(TM), lambda i: (i, 0)), tok(128), tok(128), tok(128),
                   _const_spec((1, N_EXPERTS))],
        out_shape=[
            jax.ShapeDtypeStruct((N_TOK, D_MODEL), F32),
            jax.ShapeDtypeStruct(_tiled(N_TOK), U32),
            jax.ShapeDtypeStruct((N_TOK, 128), jnp.int32),
            jax.ShapeDtypeStruct((N_TOK, 128), F32),
            jax.ShapeDtypeStruct((N_TOK, 128), jnp.int32),
            jax.ShapeDtypeStruct((1, N_EXPERTS), F32),
        ],
        scratch_shapes=[pltpu.VMEM((1, N_EXPERTS), F32)],
        compiler_params=_params(),
        name="outproj_router",
    )(xp, xs, yp, ys, op, os_, ag, wos, woa, fg, wr, br, ltri)


def _scatter_kernel(dest_ref, pend_ref, h_ref, rows_ref, zero_ref, sem, zsem):
    i = pl.program_id(0)

    @pl.when(i == 0)
    def _():
        zero_ref[...] = jnp.zeros_like(zero_ref)

        block_tiles = _tiled(MOE_BM)[0]

        def zero_block(b):
            start = pl.multiple_of(b * block_tiles, block_tiles)
            return pltpu.make_async_copy(zero_ref, rows_ref.at[pl.ds(start, block_tiles)], zsem)

        def last_block(e):
            end = pend_ref[e]
            nonempty = end > (pend_ref[e - 1] if e > 0 else 0)
            return nonempty, zero_block(jnp.maximum(end // MOE_BM - 1, 0))

        for e in range(N_EXPERTS):
            nonempty, cp = last_block(e)
            pl.when(nonempty)(cp.start)
        for e in range(N_EXPERTS):
            nonempty, cp = last_block(e)
            pl.when(nonempty)(cp.wait)

        first_unused = pend_ref[N_EXPERTS - 1] // MOE_BM
        lax.fori_loop(first_unused, MOE_BLOCKS, lambda b, c: (zero_block(b).start(), c)[1], 0)
        lax.fori_loop(first_unused, MOE_BLOCKS, lambda b, c: (zero_block(b).wait(), c)[1], 0)

    def issue(r, carry):
        for k in range(TOP_K):
            d = dest_ref[r * TOP_K + k]
            pltpu.make_async_copy(_tile_of(h_ref, r), _tile_of(rows_ref, d), sem).start(priority=k % 2)
        return carry

    lax.fori_loop(0, TM, issue, 0, unroll=ISSUE_UNROLL)
    for _ in range(TOP_K):
        pltpu.make_async_copy(h_ref, rows_ref.at[pl.ds(0, _tiled(TM)[0])], sem).wait()


def _scatter_rows(dest_flat, pad_end, h):
    return pl.pallas_call(
        _scatter_kernel,
        grid=(N_TILES,),
        in_specs=[
            pl.BlockSpec((TM * TOP_K,), lambda i: (i,), memory_space=pltpu.SMEM),
            pl.BlockSpec((N_EXPERTS,), lambda i: (0,), memory_space=pltpu.SMEM),
            pl.BlockSpec(_tiled(TM), lambda i: (i, 0)),
        ],
        out_specs=pl.BlockSpec(memory_space=pl.ANY),
        out_shape=jax.ShapeDtypeStruct(_tiled(MOE_ROWS), U32),
        scratch_shapes=[pltpu.VMEM(_tiled(MOE_BM), U32), pltpu.SemaphoreType.DMA(()),
                        pltpu.SemaphoreType.DMA(())],
        compiler_params=_params(),
        name="moe_scatter",
    )(dest_flat, pad_end, h)


def _expert_kernel(be_ref, nu_ref, nxt_ref, x_ref, wgu_hbm, bgu_ref, wd_hbm, bd_ref, y_ref,
                   wgu_f, wd_f, wgu_s, wd_s, sem):
    i = pl.program_id(0)
    active = i < nu_ref[0]
    e = be_ref[i]

    def fetch(expert):
        return (pltpu.make_async_copy(wgu_hbm.at[expert], wgu_f, sem.at[0]),
                pltpu.make_async_copy(wd_hbm.at[expert], wd_f, sem.at[1]))

    @pl.when(active & (i == 0))
    def _():
        for cp in fetch(e):
            cp.start()

    @pl.when(active & ((i == 0) | (e != be_ref[jnp.maximum(i - 1, 0)])))
    def _():
        for cp in fetch(e):
            cp.wait()
        wgu_s[...] = wgu_f[...].astype(BF16)
        wd_s[...] = wd_f[...].astype(BF16)
        nxt = nxt_ref[e]

        @pl.when(nxt >= 0)
        def _():
            for cp in fetch(nxt):
                cp.start()

    @pl.when(active)
    def _():
        gu = _dot(_tiles_to_rows(x_ref, MOE_BM).astype(BF16), wgu_s[...]) + bgu_ref[0]
        gate = jnp.minimum(gu[:, :D_FF], SWIGLU_LIMIT)
        up = jnp.clip(gu[:, D_FF:], -SWIGLU_LIMIT, SWIGLU_LIMIT)
        act = (up + 1.0) * gate * _sigmoid(gate * SWIGLU_ALPHA)
        _rows_to_tiles(y_ref, _dot(act.astype(BF16), wd_s[...]) + bd_ref[0])

    @pl.when(jnp.logical_not(active))
    def _():
        y_ref[...] = jnp.zeros_like(y_ref)


def _experts(block_expert, n_used, next_expert, rows, wgu, bgu, wd, bd):
    grid_spec = pltpu.PrefetchScalarGridSpec(
        num_scalar_prefetch=3,
        grid=(MOE_BLOCKS,),
        in_specs=[
            pl.BlockSpec(_tiled(MOE_BM), lambda i, be, nu, nx: (jnp.minimum(i, nu[0] - 1), 0)),
            pl.BlockSpec(memory_space=pl.ANY),
            pl.BlockSpec((1, 1, 2 * D_FF), lambda i, be, nu, nx: (be[i], 0, 0)),
            pl.BlockSpec(memory_space=pl.ANY),
            pl.BlockSpec((1, 1, D_MODEL), lambda i, be, nu, nx: (be[i], 0, 0)),
        ],
        out_specs=pl.BlockSpec(_tiled(MOE_BM), lambda i, be, nu, nx: (i, 0)),
        scratch_shapes=[pltpu.VMEM((D_MODEL, 2 * D_FF), F32), pltpu.VMEM((D_FF, D_MODEL), F32),
                        pltpu.VMEM((D_MODEL, 2 * D_FF), BF16), pltpu.VMEM((D_FF, D_MODEL), BF16),
                        pltpu.SemaphoreType.DMA((2,))],
    )
    return pl.pallas_call(
        _expert_kernel,
        grid_spec=grid_spec,
        out_shape=jax.ShapeDtypeStruct(_tiled(MOE_ROWS), U32),
        compiler_params=_params(),
        name="moe_experts",
    )(block_expert, n_used, next_expert, rows, wgu, bgu, wd, bd)


def _combine_kernel(dest_ref, dest_next_ref, gate_ref, xmid_ref, g_ref, rows_ref, yp_ref, ys_ref, buf_ref, sem):
    i = pl.program_id(0)
    slot = i % 2

    def issue(idx_ref, s):
        def body(r, carry):
            for k in range(TOP_K):
                d = idx_ref[r * TOP_K + k]
                pltpu.make_async_copy(_tile_of(rows_ref, d), _tile_of(buf_ref.at[s], k * TM + r),
                                      sem.at[s]).start(priority=k % 2)
            return carry

        lax.fori_loop(0, TM, body, 0, unroll=ISSUE_UNROLL)

    @pl.when(i == 0)
    def _():
        issue(dest_ref, 0)

    @pl.when(i + 1 < N_TILES)
    def _():
        issue(dest_next_ref, 1 - slot)

    buf = buf_ref.at[slot]
    slot_tiles = _tiled(TM)[0]
    for k in range(TOP_K):
        pltpu.make_async_copy(rows_ref.at[pl.ds(0, slot_tiles)], buf.at[pl.ds(k * slot_tiles, slot_tiles)],
                              sem.at[slot]).wait()
    acc = xmid_ref[...]
    for k in range(TOP_K):
        acc = acc + _tiles_to_rows(buf, TM, first=k * TM) * gate_ref[:, k:k + 1]
    y = _rms(acc, g_ref[...])

    @pl.when(i < N_PROMPT_TILES)
    def _():
        yp_ref[...] = y

    @pl.when(i == N_PROMPT_TILES)
    def _():
        ys_ref[...] = y


def _combine(dest_flat, gates, xmid, g, y_rows):
    return pl.pallas_call(
        _combine_kernel,
        grid=(N_TILES,),
        in_specs=[
            pl.BlockSpec((TM * TOP_K,), lambda i: (i,), memory_space=pltpu.SMEM),
            pl.BlockSpec((TM * TOP_K,), lambda i: (jnp.minimum(i + 1, N_TILES - 1),), memory_space=pltpu.SMEM),
            pl.BlockSpec((TM, 128), lambda i: (i, 0)),
            pl.BlockSpec((TM, D_MODEL), lambda i: (i, 0)),
            _const_spec((1, D_MODEL)),
            pl.BlockSpec(memory_space=pl.ANY),
        ],
        out_specs=[
            pl.BlockSpec((TM, D_MODEL), lambda i: (jnp.minimum(i, N_PROMPT_TILES - 1), 0)),
            pl.BlockSpec((TM, D_MODEL), lambda i: (0, 0)),
        ],
        out_shape=[
            jax.ShapeDtypeStruct((N_PROMPT, D_MODEL), F32),
            jax.ShapeDtypeStruct((N_SAMPLE, D_MODEL), F32),
        ],
        scratch_shapes=[pltpu.VMEM((2,) + _tiled(TOP_K * TM), U32), pltpu.SemaphoreType.DMA((2,))],
        compiler_params=_params(),
        name="moe_combine",
    )(dest_flat, dest_flat, gates, xmid, g, y_rows)


def _band_bias(table):
    n_diag = BAND + CHUNK - 1
    idx = np.clip(ATT_LEFT + (CHUNK - 1) - np.arange(n_diag), -REL_CLIP, REL_CLIP) + REL_CLIP
    pick = (np.arange(2 * REL_CLIP + 1)[:, None] == idx[None, :]).astype(np.float32)
    diag = jnp.dot(table, jnp.asarray(pick), precision=lax.Precision.HIGHEST)
    return jnp.stack([diag[:, CHUNK - 1 - qi:CHUNK - 1 - qi + BAND] for qi in range(CHUNK)], axis=1)


def _pair_rows(v):
    return jnp.repeat(v.reshape(HEAD_PAIRS, 2), CHUNK, axis=1)


def _layer(l, xp, xs, cache_k, cache_v, state_conv, state_ssm,
           norm_mix_g, w_in, conv_w, conv_b, dt_bias, a_log, d_skip, ssm_norm_g,
           att_norm_g, rel_bias_table, w_out, norm_ffn_g, w_router, b_router,
           w_gate_up, b_gate_up, w_down, b_down, norm_final_g):
    wb = w_in[l].astype(BF16)
    c0 = D_SSM
    c1 = c0 + D_CONV
    c2 = c1 + SSM_HEADS
    c3 = c2 + D_ATT
    c4 = c3 + D_ATT
    z, xbc, dt, dtt, q, k, v, k_p, v_p, k_s, v_s, ctail = _inproj(
        xp, xs, norm_mix_g[l][None], wb[:, :c0], wb[:, c0:c1], wb[:, c1:c2], wb[:, c1:c2].T,
        wb[:, c2:c3], wb[:, c3:c4], wb[:, c4:])

    n_chunks = N_TOK // CHUNK
    dtp = dtt.reshape(HEAD_PAIRS, 2, n_chunks, CHUNK).transpose(2, 0, 1, 3).reshape(n_chunks, HEAD_PAIRS, 128)
    hp = jnp.arange(D_SSM) // SSM_HEAD_DIM
    expand = (hp[None, :] == jnp.arange(SSM_HEADS)[:, None]).astype(BF16)
    r128 = jnp.arange(128)
    triu2 = ((r128[:, None] // CHUNK == r128[None, :] // CHUNK) & (r128[:, None] <= r128[None, :])).astype(BF16)
    consts = (conv_w[l], conv_b[l][None], dt_bias[l][None], _pair_rows(dt_bias[l]),
              a_log[l][None], _pair_rows(a_log[l]),
              jnp.repeat(d_skip[l], SSM_HEAD_DIM)[None], ssm_norm_g[l][None],
              expand, triu2)
    y_ssm_p, ssm_p = _ssd_prompt(z, xbc, dt, dtp, consts)
    y_ssm_s, ssm_s = _ssd_sample(z, xbc, dt, dtp, state_conv[l],
                                 state_ssm[l].reshape(DEC_BATCH, D_SSM, D_STATE), consts)

    bias2 = _band_bias(rel_bias_table[l]).reshape(ATT_PAIRS, 2 * CHUNK, BAND)
    o_att_p = _attn_prompt(q, k, v, bias2)
    o_att_s = _attn_sample(q, k, v, cache_k[l].reshape(DEC_BATCH, ATT_LEFT, D_ATT),
                           cache_v[l].reshape(DEC_BATCH, ATT_LEFT, D_ATT), bias2)

    wo = w_out[l].astype(BF16)
    wr = w_router[l]
    wr1 = wr.astype(BF16)
    wr2 = (wr - wr1.astype(F32)).astype(BF16)
    ltri = jnp.tril(jnp.ones((TM, TM), BF16), -1)
    xmid, h, top_idx, gates, rank, counts = _outproj(
        xp, xs, y_ssm_p, y_ssm_s, o_att_p, o_att_s, att_norm_g[l][None], wo[:D_SSM], wo[D_SSM:], norm_ffn_g[l][None],
        jnp.stack([wr1, wr2]), b_router[l][None], ltri)

    counts = counts[0].astype(jnp.int32)
    padded = (counts + MOE_BM - 1) // MOE_BM * MOE_BM
    pad_end = jnp.cumsum(padded)
    pad_start = pad_end - padded
    experts = jnp.arange(N_EXPERTS, dtype=jnp.int32)
    start_of = jnp.sum(jnp.where(top_idx[:, :TOP_K, None] == experts, pad_start, 0), axis=-1)
    dest = (start_of + rank[:, :TOP_K]).reshape(-1).astype(jnp.int32)
    block_start = jnp.arange(MOE_BLOCKS, dtype=jnp.int32) * MOE_BM
    block_expert = jnp.minimum(jnp.sum((pad_end[None, :] <= block_start[:, None]).astype(jnp.int32), axis=1),
                               N_EXPERTS - 1).astype(jnp.int32)
    n_used = (pad_end[-1:] // MOE_BM).astype(jnp.int32)
    later_nonempty = (experts[None, :] > experts[:, None]) & (padded[None, :] > 0)
    next_expert = jnp.min(jnp.where(later_nonempty, experts[None, :], N_EXPERTS), axis=1)
    next_expert = jnp.where(next_expert < N_EXPERTS, next_expert, -1).astype(jnp.int32)

    rows = _scatter_rows(dest, pad_end.astype(jnp.int32), h)
    y_rows = _experts(block_expert, n_used, next_expert, rows, w_gate_up[l], b_gate_up[l][:, None, :],
                      w_down[l], b_down[l][:, None, :])
    y_p, y_s = _combine(dest, gates, xmid, norm_final_g[None], y_rows)

    keep = min(ATT_LEFT, SEQ)
    k_p = k_p[:, TM - keep:].reshape(BATCH, keep, ATT_HEADS, ATT_HEAD_DIM)
    v_p = v_p[:, TM - keep:].reshape(BATCH, keep, ATT_HEADS, ATT_HEAD_DIM)
    k_s = k_s.reshape(DEC_BATCH, DEC_SEQ, ATT_HEADS, ATT_HEAD_DIM)
    v_s = v_s.reshape(DEC_BATCH, DEC_SEQ, ATT_HEADS, ATT_HEAD_DIM)
    conv_p = ctail[:BATCH, -(CONV_W - 1):]
    conv_s = ctail[BATCH].reshape(DEC_BATCH, 8, D_CONV)[:, -(CONV_W - 1):]
    ssm_p = ssm_p.reshape(BATCH, SSM_HEADS, SSM_HEAD_DIM, D_STATE)
    ssm_s = ssm_s.reshape(DEC_BATCH, SSM_HEADS, SSM_HEAD_DIM, D_STATE)
    return (y_p.reshape(BATCH, SEQ, D_MODEL), y_s.reshape(DEC_BATCH, DEC_SEQ, D_MODEL),
            k_p, v_p, conv_p, ssm_p, k_s, v_s, conv_s, ssm_s)


def kernel(x_prompt, x_sample, cache_k, cache_v, state_conv, state_ssm, norm_mix_g, w_in, conv_w, conv_b,
           dt_bias, a_log, d_skip, ssm_norm_g, att_norm_g, rel_bias_table, w_out, norm_ffn_g, w_router,
           b_router, w_gate_up, b_gate_up, w_down, b_down, norm_final_g):
    assert w_in.shape[0] == 1, "single trunk layer"
    xp = x_prompt.reshape(N_PROMPT, D_MODEL)
    xs = x_sample.reshape(N_SAMPLE, D_MODEL)
    outs = _layer(0, xp, xs, cache_k, cache_v, state_conv, state_ssm,
                  norm_mix_g, w_in, conv_w, conv_b, dt_bias, a_log, d_skip, ssm_norm_g,
                  att_norm_g, rel_bias_table, w_out, norm_ffn_g, w_router, b_router,
                  w_gate_up, b_gate_up, w_down, b_down, norm_final_g)
    y_p, y_s, k_p, v_p, conv_p, ssm_p, k_s, v_s, conv_s, ssm_s = outs
    return (y_p, y_s, k_p[None], v_p[None], conv_p[None], ssm_p[None],
            k_s[None], v_s[None], conv_s[None], ssm_s[None])
```

```python
import jax
import jax.numpy as jnp
import numpy as np
from jax import lax
from jax.experimental import pallas as pl
from jax.experimental.pallas import tpu as pltpu

D_MODEL = 1024
BATCH = 8
SEQ = 2048
DEC_BATCH = 8
DEC_SEQ = 64
CHUNK = 64
SSM_HEADS = 16
SSM_HEAD_DIM = 64
D_SSM = SSM_HEADS * SSM_HEAD_DIM
SSM_GROUPS = 2
D_STATE = 128
CONV_W = 4
D_BC = SSM_GROUPS * D_STATE
D_CONV = D_SSM + 2 * D_BC
ATT_HEADS = 8
ATT_HEAD_DIM = 64
D_ATT = ATT_HEADS * ATT_HEAD_DIM
LEFT_CHUNKS = 8
ATT_LEFT = LEFT_CHUNKS * CHUNK
BAND = ATT_LEFT + CHUNK
REL_CLIP = 128
ATT_SCALE = ATT_HEAD_DIM ** -0.5
N_EXPERTS = 32
TOP_K = 4
D_FF = D_MODEL
SWIGLU_ALPHA = 1.702
SWIGLU_LIMIT = 7.0
EPS = 1e-5

F32 = jnp.float32
BF16 = jnp.bfloat16
U32 = jnp.uint32

N_PROMPT = BATCH * SEQ
N_SAMPLE = DEC_BATCH * DEC_SEQ
N_TOK = N_PROMPT + N_SAMPLE
TM = 512
N_PROMPT_TILES = N_PROMPT // TM
N_TILES = N_TOK // TM
TILES_PER_SEQ = SEQ // TM
CHUNKS_PER_TILE = TM // CHUNK
N_CHUNKS_SEQ = SEQ // CHUNK
HEAD_PAIRS = SSM_HEADS // 2
PAIRS_PER_GROUP = HEAD_PAIRS // SSM_GROUPS
ATT_PAIRS = ATT_HEADS // 2
SSD_BLOCK = 2
CONV_HEAD = 8
MOE_BM = 512
N_ASSIGN = N_TOK * TOP_K
MOE_BLOCKS = N_ASSIGN // MOE_BM + N_EXPERTS
MOE_ROWS = MOE_BLOCKS * MOE_BM
ISSUE_UNROLL = 8
ROW_TILE = (4, 128)
assert 2 * ROW_TILE[0] * ROW_TILE[1] == D_MODEL
VMEM_LIMIT = 56 * 1024 * 1024


def _dot(a, b):
    return jnp.dot(a, b, preferred_element_type=F32)


def _dot_nt(a, b):
    return lax.dot_general(a, b, (((1,), (1,)), ((), ())), preferred_element_type=F32)


def _dot_tn(a, b):
    return lax.dot_general(a, b, (((0,), (0,)), ((), ())), preferred_element_type=F32)


def _split3(x):
    x1 = x.astype(BF16)
    r1 = x - x1.astype(F32)
    x2 = r1.astype(BF16)
    r2 = r1 - x2.astype(F32)
    return x1, x2, r2.astype(BF16)


def _dot_exact_rhs(x, m):
    x1, x2, x3 = _split3(x)
    return _dot(x1, m) + _dot(x2, m) + _dot(x3, m)


def _dot_exact_lhs(m, x):
    x1, x2, x3 = _split3(x)
    return _dot(m, x1) + _dot(m, x2) + _dot(m, x3)


def _rms(x, g):
    return x * lax.rsqrt(jnp.mean(x * x, axis=-1, keepdims=True) + EPS) * g


def _sigmoid(x):
    return 1.0 / (1.0 + jnp.exp(-x))


def _softplus(x):
    return jnp.maximum(x, 0.0) + jnp.log(1.0 + jnp.exp(-jnp.abs(x)))


def _tiled(n):
    return (n * ROW_TILE[0], ROW_TILE[1])


def _tile_of(ref, row):
    return ref.at[pl.ds(pl.multiple_of(row * ROW_TILE[0], ROW_TILE[0]), ROW_TILE[0])]


def _rows_to_tiles(ref, x, first=0):
    sub, lanes = ROW_TILE
    half = D_MODEL // 2
    hi = lax.bitcast_convert_type(x[:, :half].astype(BF16).astype(F32), U32)
    lo = lax.bitcast_convert_type(x[:, half:].astype(BF16).astype(F32), U32)
    words = hi | (lo >> 16)
    for j in range(sub):
        ref[pl.ds(first * sub + j, x.shape[0], stride=sub), :] = words[:, j * lanes:(j + 1) * lanes]


def _tiles_to_rows(ref, n, first=0):
    sub = ROW_TILE[0]
    words = jnp.concatenate([ref[pl.ds(first * sub + j, n, stride=sub), :] for j in range(sub)], axis=1)
    hi = lax.bitcast_convert_type(words & jnp.uint32(0xFFFF0000), F32)
    lo = lax.bitcast_convert_type(words << 16, F32)
    return jnp.concatenate([hi, lo], axis=1)


def _const_spec(shape):
    nd = len(shape)
    return pl.BlockSpec(shape, lambda *_: (0,) * nd)


def _params(n_axes=1):
    return pltpu.CompilerParams(dimension_semantics=("arbitrary",) * n_axes,
                                vmem_limit_bytes=VMEM_LIMIT)


def _inproj_kernel(xp_ref, xs_ref, g_ref, wz_ref, wxbc_ref, wdt_ref, wdtt_ref, wq_ref, wk_ref, wv_ref,
                   z_ref, xbc_ref, dt_ref, dtt_ref, q_ref, k_ref, v_ref, kp_ref, vp_ref, ks_ref, vs_ref, ctail_ref):
    i = pl.program_id(0)
    x = jnp.where(i == N_PROMPT_TILES, xs_ref[...], xp_ref[...])
    h = _rms(x, g_ref[...]).astype(BF16)
    z_ref[...] = _dot(h, wz_ref[...]).astype(BF16)
    xbc = _dot(h, wxbc_ref[...])
    xbc_ref[...] = xbc.astype(BF16)
    for c in range(CHUNKS_PER_TILE):
        ctail_ref[0, c * 8:(c + 1) * 8, :] = xbc[c * CHUNK + CHUNK - 8:(c + 1) * CHUNK, :]
    dt_ref[...] = _dot(h, wdt_ref[...])
    dtt_ref[...] = _dot_nt(wdtt_ref[...], h)
    q_ref[...] = (_dot(h, wq_ref[...]) * ATT_SCALE).astype(BF16)
    k = _dot(h, wk_ref[...])
    v = _dot(h, wv_ref[...])
    k_ref[...] = k.astype(BF16)
    v_ref[...] = v.astype(BF16)

    @pl.when(i < N_PROMPT_TILES)
    def _():
        kp_ref[0] = k
        vp_ref[0] = v

    @pl.when(i == N_PROMPT_TILES)
    def _():
        ks_ref[...] = k
        vs_ref[...] = v


def _inproj(xp, xs, g, wz, wxbc, wdt, wdtt, wq, wk, wv):
    tok = lambda n: pl.BlockSpec((TM, n), lambda i: (i, 0))
    tail_idx = lambda i: (i // TILES_PER_SEQ, 0, 0)
    seq_idx = lambda i: (jnp.minimum(i // TILES_PER_SEQ, BATCH - 1), 0, 0)
    n_tail = BATCH + 1
    return pl.pallas_call(
        _inproj_kernel,
        grid=(N_TILES,),
        in_specs=[
            pl.BlockSpec((TM, D_MODEL), lambda i: (jnp.minimum(i, N_PROMPT_TILES - 1), 0)),
            pl.BlockSpec((TM, D_MODEL), lambda i: (0, 0)),
            _const_spec((1, D_MODEL)),
            _const_spec((D_MODEL, D_SSM)),
            _const_spec((D_MODEL, D_CONV)),
            _const_spec((D_MODEL, SSM_HEADS)),
            _const_spec((SSM_HEADS, D_MODEL)),
            _const_spec((D_MODEL, D_ATT)),
            _const_spec((D_MODEL, D_ATT)),
            _const_spec((D_MODEL, D_ATT)),
        ],
        out_specs=[
            tok(D_SSM), tok(D_CONV), tok(SSM_HEADS),
            pl.BlockSpec((SSM_HEADS, TM), lambda i: (0, i)),
            tok(D_ATT), tok(D_ATT), tok(D_ATT),
            pl.BlockSpec((1, TM, D_ATT), seq_idx),
            pl.BlockSpec((1, TM, D_ATT), seq_idx),
            pl.BlockSpec((TM, D_ATT), lambda i: (0, 0)),
            pl.BlockSpec((TM, D_ATT), lambda i: (0, 0)),
            pl.BlockSpec((1, CHUNKS_PER_TILE * 8, D_CONV), tail_idx),
        ],
        out_shape=[
            jax.ShapeDtypeStruct((N_TOK, D_SSM), BF16),
            jax.ShapeDtypeStruct((N_TOK, D_CONV), BF16),
            jax.ShapeDtypeStruct((N_TOK, SSM_HEADS), F32),
            jax.ShapeDtypeStruct((SSM_HEADS, N_TOK), F32),
            jax.ShapeDtypeStruct((N_TOK, D_ATT), BF16),
            jax.ShapeDtypeStruct((N_TOK, D_ATT), BF16),
            jax.ShapeDtypeStruct((N_TOK, D_ATT), BF16),
            jax.ShapeDtypeStruct((BATCH, TM, D_ATT), F32),
            jax.ShapeDtypeStruct((BATCH, TM, D_ATT), F32),
            jax.ShapeDtypeStruct((N_SAMPLE, D_ATT), F32),
            jax.ShapeDtypeStruct((N_SAMPLE, D_ATT), F32),
            jax.ShapeDtypeStruct((n_tail, CHUNKS_PER_TILE * 8, D_CONV), F32),
        ],
        compiler_params=_params(),
        name="inproj",
    )(xp, xs, g, wz, wxbc, wdt, wdtt, wq, wk, wv)


def _ssd_tile(n_chunks, z_ref, xbc_ref, dt_ref, dtp_ref, cw_ref, cb_ref, dtb_ref, dtbp_ref,
              alog_h_ref, alog_p_ref, dskip_e_ref, ng_ref, expand_ref, triu2_ref,
              y_ref, xw_ref, state_ref):
    nb = SSD_BLOCK if n_chunks % SSD_BLOCK == 0 else 1
    rb = nb * CHUNK
    half = D_SSM // SSM_GROUPS
    xw_ref[CONV_HEAD:CONV_HEAD + n_chunks * CHUNK, :] = xbc_ref[...].astype(F32)

    a_h = -jnp.exp(alog_h_ref[...])
    a_p = jnp.concatenate([-jnp.exp(alog_p_ref[...])] * nb, axis=0)
    dtb_p = jnp.concatenate([dtbp_ref[...]] * nb, axis=0)
    tr_r = lax.broadcasted_iota(jnp.int32, (rb, rb), 0)
    tr_c = lax.broadcasted_iota(jnp.int32, (rb, rb), 1)
    tril_b = jnp.where((tr_r // CHUNK == tr_c // CHUNK) & (tr_c <= tr_r), 1.0, 0.0).astype(BF16)
    row_i = lax.broadcasted_iota(jnp.int32, (CHUNK, 128), 0)
    col_i = lax.broadcasted_iota(jnp.int32, (CHUNK, 128), 1)
    causal2 = row_i >= (col_i % CHUNK)
    sel_r = lax.broadcasted_iota(jnp.int32, ((CONV_W - 1) * rb, CONV_HEAD + rb), 0)
    sel_c = lax.broadcasted_iota(jnp.int32, ((CONV_W - 1) * rb, CONV_HEAD + rb), 1)
    shift_sel = (sel_c == sel_r % rb + sel_r // rb + CONV_HEAD - (CONV_W - 1)).astype(F32)
    bd_r = lax.broadcasted_iota(jnp.int32, (128, 128), 0) // CHUNK
    bd_c = lax.broadcasted_iota(jnp.int32, (128, 128), 1) // CHUNK
    blockdiag = bd_r == bd_c

    def block(i, carry):
        r0 = pl.multiple_of(i * rb, rb)
        win = xw_ref[pl.ds(r0, CONV_HEAD + rb), :]
        shifted = _dot(shift_sel, win)
        acc = cb_ref[...] + cw_ref[CONV_W - 1:CONV_W, :] * win[CONV_HEAD:, :]
        for tap in range(CONV_W - 1):
            acc = acc + cw_ref[tap:tap + 1, :] * shifted[tap * rb:(tap + 1) * rb, :]
        xa = acc * _sigmoid(acc)
        xs = xa[:, 0:D_SSM]
        bm = xa[:, D_SSM:D_SSM + D_BC].astype(BF16)
        cm = xa[:, D_SSM + D_BC:D_CONV].astype(BF16)
        dt = _softplus(dt_ref[pl.ds(r0, rb), :] + dtb_ref[...])
        dt_e = _dot_exact_rhs(dt, expand_ref[...])
        acum = _dot_exact_rhs(_dot_exact_lhs(tril_b, dt * a_h), expand_ref[...])
        dtp = _softplus(dtp_ref[pl.ds(i * nb, nb)].reshape(nb * HEAD_PAIRS, 128) + dtb_p)
        acum_p = _dot_exact_rhs(dtp * a_p, triu2_ref[...])
        xdt = xs * dt_e
        a_last = [acum[(c + 1) * CHUNK - 1:(c + 1) * CHUNK, :] for c in range(nb)]
        a_end = jnp.concatenate([jnp.broadcast_to(a, (CHUNK, D_SSM)) for a in a_last], axis=0)
        xdt_end = (xdt * jnp.exp(a_end - acum)).astype(BF16)

        y_diag, new_s = [], []
        for c in range(nb):
            rs = slice(c * CHUNK, (c + 1) * CHUNK)
            y_parts = []
            for g in range(SSM_GROUPS):
                bg = bm[rs, g * D_STATE:(g + 1) * D_STATE]
                cg = cm[rs, g * D_STATE:(g + 1) * D_STATE]
                cb2 = _dot_nt(cg, jnp.concatenate([bg, bg], axis=0))
                for jj in range(PAIRS_PER_GROUP):
                    j = g * PAIRS_PER_GROUP + jj
                    seg = acum[rs, j * 128:(j + 1) * 128] - acum_p[c * HEAD_PAIRS + j:c * HEAD_PAIRS + j + 1, :]
                    decay = jnp.exp(jnp.where(causal2, seg, -jnp.inf))
                    x2 = xdt[rs, j * 128:(j + 1) * 128]
                    rhs = jnp.where(blockdiag, jnp.concatenate([x2, x2], axis=0), 0.0).astype(BF16)
                    y_parts.append(_dot((cb2 * decay).astype(BF16), rhs))
            y_diag.append(jnp.concatenate(y_parts, axis=1))
            new_s.append(jnp.concatenate(
                [_dot_tn(bm[rs, g * D_STATE:(g + 1) * D_STATE], xdt_end[rs, g * half:(g + 1) * half])
                 for g in range(SSM_GROUPS)], axis=1))

        y_off = []
        for c in range(nb):
            rs = slice(c * CHUNK, (c + 1) * CHUNK)
            state = state_ref[...]
            y_off.append(jnp.concatenate(
                [_dot(cm[rs, g * D_STATE:(g + 1) * D_STATE], state[:, g * half:(g + 1) * half].astype(BF16))
                 for g in range(SSM_GROUPS)], axis=1))
            state_ref[...] = state * jnp.exp(a_last[c]) + new_s[c]

        y = (jnp.concatenate(y_diag, axis=0) + jnp.concatenate(y_off, axis=0) * jnp.exp(acum)
             + dskip_e_ref[...] * xs)
        zc = z_ref[pl.ds(r0, rb), :].astype(F32)
        y = y * (zc * _sigmoid(zc))
        yn = jnp.concatenate(
            [y[:, g * half:(g + 1) * half]
             * lax.rsqrt(jnp.mean(jnp.square(y[:, g * half:(g + 1) * half]), axis=-1, keepdims=True) + EPS)
             for g in range(SSM_GROUPS)], axis=1)
        y_ref[pl.ds(r0, rb), :] = (yn * ng_ref[...]).astype(BF16)
        return carry

    lax.fori_loop(0, n_chunks // nb, block, 0)


def _state_store(state_ref, out_ref):
    for j in range(HEAD_PAIRS):
        out_ref[0, j * 128:(j + 1) * 128, :] = state_ref[:, j * 128:(j + 1) * 128].T


def _ssd_prompt_kernel(z_ref, xbc_ref, dt_ref, dtp_ref, cw_ref, cb_ref, dtb_ref, dtbp_ref,
                       alog_h_ref, alog_p_ref, dskip_e_ref, ng_ref, expand_ref, triu2_ref,
                       y_ref, ssm_ref, xw_ref, state_ref, tail_ref):
    t = pl.program_id(1)

    @pl.when(t == 0)
    def _():
        state_ref[...] = jnp.zeros_like(state_ref)
        xw_ref[0:CONV_HEAD, :] = jnp.zeros((CONV_HEAD, D_CONV), F32)

    @pl.when(t > 0)
    def _():
        xw_ref[0:CONV_HEAD, :] = tail_ref[...]

    _ssd_tile(CHUNKS_PER_TILE, z_ref, xbc_ref, dt_ref, dtp_ref, cw_ref, cb_ref, dtb_ref, dtbp_ref,
              alog_h_ref, alog_p_ref, dskip_e_ref, ng_ref, expand_ref, triu2_ref,
              y_ref, xw_ref, state_ref)
    tail_ref[...] = xw_ref[TM:TM + CONV_HEAD, :]

    @pl.when(t == TILES_PER_SEQ - 1)
    def _():
        _state_store(state_ref, ssm_ref)


def _ssd_sample_kernel(z_ref, xbc_ref, dt_ref, dtp_ref, cprev_ref, sprev_ref,
                       cw_ref, cb_ref, dtb_ref, dtbp_ref,
                       alog_h_ref, alog_p_ref, dskip_e_ref, ng_ref, expand_ref, triu2_ref,
                       y_ref, ssm_ref, xw_ref, state_ref):
    xw_ref[0:CONV_HEAD, :] = jnp.zeros((CONV_HEAD, D_CONV), F32)
    xw_ref[CONV_HEAD - (CONV_W - 1):CONV_HEAD, :] = cprev_ref[0]
    for j in range(HEAD_PAIRS):
        state_ref[:, j * 128:(j + 1) * 128] = sprev_ref[0, j * 128:(j + 1) * 128, :].T
    _ssd_tile(1, z_ref, xbc_ref, dt_ref, dtp_ref, cw_ref, cb_ref, dtb_ref, dtbp_ref,
              alog_h_ref, alog_p_ref, dskip_e_ref, ng_ref, expand_ref, triu2_ref,
              y_ref, xw_ref, state_ref)
    _state_store(state_ref, ssm_ref)


def _ssd_const_specs():
    return [
        _const_spec((CONV_W, D_CONV)), _const_spec((1, D_CONV)),
        _const_spec((1, SSM_HEADS)), _const_spec((HEAD_PAIRS, 128)),
        _const_spec((1, SSM_HEADS)), _const_spec((HEAD_PAIRS, 128)),
        _const_spec((1, D_SSM)), _const_spec((1, D_SSM)),
        _const_spec((SSM_HEADS, D_SSM)), _const_spec((128, 128)),
    ]


def _ssd_prompt(z, xbc, dt, dtp, consts):
    tile = lambda b, t: (b * TILES_PER_SEQ + t, 0)
    return pl.pallas_call(
        _ssd_prompt_kernel,
        grid=(BATCH, TILES_PER_SEQ),
        in_specs=[
            pl.BlockSpec((TM, D_SSM), tile),
            pl.BlockSpec((TM, D_CONV), tile),
            pl.BlockSpec((TM, SSM_HEADS), tile),
            pl.BlockSpec((CHUNKS_PER_TILE, HEAD_PAIRS, 128), lambda b, t: (b * TILES_PER_SEQ + t, 0, 0)),
        ] + _ssd_const_specs(),
        out_specs=[
            pl.BlockSpec((TM, D_SSM), tile),
            pl.BlockSpec((1, D_SSM, D_STATE), lambda b, t: (b, 0, 0)),
        ],
        out_shape=[
            jax.ShapeDtypeStruct((N_PROMPT, D_SSM), BF16),
            jax.ShapeDtypeStruct((BATCH, D_SSM, D_STATE), F32),
        ],
        scratch_shapes=[
            pltpu.VMEM((CONV_HEAD + TM, D_CONV), F32),
            pltpu.VMEM((D_STATE, D_SSM), F32),
            pltpu.VMEM((CONV_HEAD, D_CONV), F32),
        ],
        compiler_params=_params(2),
        name="ssd_prompt",
    )(z, xbc, dt, dtp, *consts)


def _ssd_sample(z, xbc, dt, dtp, conv_prev, ssm_prev, consts):
    first = N_PROMPT // CHUNK
    row = lambda b: (first + b, 0)
    return pl.pallas_call(
        _ssd_sample_kernel,
        grid=(DEC_BATCH,),
        in_specs=[
            pl.BlockSpec((CHUNK, D_SSM), row),
            pl.BlockSpec((CHUNK, D_CONV), row),
            pl.BlockSpec((CHUNK, SSM_HEADS), row),
            pl.BlockSpec((1, HEAD_PAIRS, 128), lambda b: (first + b, 0, 0)),
            pl.BlockSpec((1, CONV_W - 1, D_CONV), lambda b: (b, 0, 0)),
            pl.BlockSpec((1, D_SSM, D_STATE), lambda b: (b, 0, 0)),
        ] + _ssd_const_specs(),
        out_specs=[
            pl.BlockSpec((CHUNK, D_SSM), lambda b: (b, 0)),
            pl.BlockSpec((1, D_SSM, D_STATE), lambda b: (b, 0, 0)),
        ],
        out_shape=[
            jax.ShapeDtypeStruct((N_SAMPLE, D_SSM), BF16),
            jax.ShapeDtypeStruct((DEC_BATCH, D_SSM, D_STATE), F32),
        ],
        scratch_shapes=[
            pltpu.VMEM((CONV_HEAD + CHUNK, D_CONV), F32),
            pltpu.VMEM((D_STATE, D_SSM), F32),
        ],
        compiler_params=_params(),
        name="ssd_sample",
    )(z, xbc, dt, dtp, conv_prev, ssm_prev, *consts)


def _attn_chunks(n_chunks, first_chunk, q_ref, kpad_ref, vpad_ref, bias_ref, o_ref):
    lane = lax.broadcasted_iota(jnp.int32, (CHUNK, 128), 1)
    low = lane < ATT_HEAD_DIM
    kj = lax.broadcasted_iota(jnp.int32, (2 * CHUNK, BAND), 1)

    def chunk(c, carry, masked):
        r0 = pl.multiple_of(c * CHUNK, CHUNK)
        if masked:
            valid = kj >= (LEFT_CHUNKS - (first_chunk + c)) * CHUNK
        scores = []
        for j in range(ATT_PAIRS):
            qp = q_ref[pl.ds(r0, CHUNK), j * 128:(j + 1) * 128]
            zero = jnp.zeros_like(qp)
            q2 = jnp.concatenate([jnp.where(low, qp, zero), jnp.where(low, zero, qp)], axis=0)
            kb = kpad_ref[pl.ds(r0, BAND), j * 128:(j + 1) * 128]
            s = _dot_nt(q2, kb) + bias_ref[j]
            scores.append(jnp.where(valid, s, -jnp.inf) if masked else s)
        probs = []
        for s in scores:
            e = jnp.exp(s - jnp.max(s, axis=-1, keepdims=True))
            probs.append((e.astype(BF16), jnp.sum(e, axis=-1, keepdims=True)))
        outs = []
        for j, (e, denom) in enumerate(probs):
            vb = vpad_ref[pl.ds(r0, BAND), j * 128:(j + 1) * 128]
            r = _dot(e, vb) / denom
            outs.append(jnp.where(low, r[0:CHUNK], r[CHUNK:2 * CHUNK]))
        o_ref[pl.ds(r0, CHUNK), :] = jnp.concatenate(outs, axis=1).astype(BF16)
        return carry

    n_masked = min(max(LEFT_CHUNKS - first_chunk, 0), n_chunks)
    for lo, hi, masked in ((0, n_masked, True), (n_masked, n_chunks, False)):
        if hi > lo:
            lax.fori_loop(lo, hi, lambda c, carry, masked=masked: chunk(c, carry, masked), 0,
                          unroll=2 if (hi - lo) % 2 == 0 else 1)


def _attn_prompt_kernel(q_ref, k_ref, v_ref, bias_ref, o_ref, kpad_ref, vpad_ref):
    kpad_ref[0:ATT_LEFT, :] = jnp.zeros((ATT_LEFT, D_ATT), BF16)
    vpad_ref[0:ATT_LEFT, :] = jnp.zeros((ATT_LEFT, D_ATT), BF16)
    kpad_ref[ATT_LEFT:ATT_LEFT + SEQ, :] = k_ref[...]
    vpad_ref[ATT_LEFT:ATT_LEFT + SEQ, :] = v_ref[...]
    _attn_chunks(N_CHUNKS_SEQ, 0, q_ref, kpad_ref, vpad_ref, bias_ref, o_ref)


def _attn_sample_kernel(q_ref, k_ref, v_ref, ck_ref, cv_ref, bias_ref, o_ref, kpad_ref, vpad_ref):
    kpad_ref[0:ATT_LEFT, :] = ck_ref[0].astype(BF16)
    vpad_ref[0:ATT_LEFT, :] = cv_ref[0].astype(BF16)
    kpad_ref[ATT_LEFT:BAND, :] = k_ref[...]
    vpad_ref[ATT_LEFT:BAND, :] = v_ref[...]
    _attn_chunks(1, LEFT_CHUNKS, q_ref, kpad_ref, vpad_ref, bias_ref, o_ref)


def _attn_prompt(q, k, v, bias2):
    seq = pl.BlockSpec((SEQ, D_ATT), lambda b: (b, 0))
    return pl.pallas_call(
        _attn_prompt_kernel,
        grid=(BATCH,),
        in_specs=[seq, seq, seq, _const_spec((ATT_PAIRS, 2 * CHUNK, BAND))],
        out_specs=seq,
        out_shape=jax.ShapeDtypeStruct((N_PROMPT, D_ATT), BF16),
        scratch_shapes=[pltpu.VMEM((ATT_LEFT + SEQ, D_ATT), BF16),
                        pltpu.VMEM((ATT_LEFT + SEQ, D_ATT), BF16)],
        compiler_params=_params(),
        name="attn_prompt",
    )(q, k, v, bias2)


def _attn_sample(q, k, v, cache_k, cache_v, bias2):
    first = N_PROMPT // CHUNK
    row = pl.BlockSpec((CHUNK, D_ATT), lambda b: (first + b, 0))
    cache = pl.BlockSpec((1, ATT_LEFT, D_ATT), lambda b: (b, 0, 0))
    return pl.pallas_call(
        _attn_sample_kernel,
        grid=(DEC_BATCH,),
        in_specs=[row, row, row, cache, cache, _const_spec((ATT_PAIRS, 2 * CHUNK, BAND))],
        out_specs=pl.BlockSpec((CHUNK, D_ATT), lambda b: (b, 0)),
        out_shape=jax.ShapeDtypeStruct((N_SAMPLE, D_ATT), BF16),
        scratch_shapes=[pltpu.VMEM((BAND, D_ATT), BF16), pltpu.VMEM((BAND, D_ATT), BF16)],
        compiler_params=_params(),
        name="attn_sample",
    )(q, k, v, cache_k, cache_v, bias2)


def _outproj_kernel(xp_ref, xs_ref, yp_ref, ys_ref, op_ref, os_ref, ag_ref, wos_ref, woa_ref, fg_ref,
                    wr_ref, br_ref, earlier_ref,
                    xmid_ref, h_ref, idx_ref, gate_ref, rank_ref, cnt_ref, carry_ref):
    i = pl.program_id(0)

    @pl.when(i == 0)
    def _():
        carry_ref[...] = jnp.zeros_like(carry_ref)

    is_sample = i == N_PROMPT_TILES
    x = jnp.where(is_sample, xs_ref[...], xp_ref[...])
    y = jnp.where(is_sample, ys_ref[...], yp_ref[...])
    o = jnp.where(is_sample, os_ref[...], op_ref[...])
    o = _rms(o.astype(F32), ag_ref[...]).astype(BF16)
    xm = x + _dot(y, wos_ref[...]) + _dot(o, woa_ref[...])
    xmid_ref[...] = xm
    h = _rms(xm, fg_ref[...])
    _rows_to_tiles(h_ref, h)
    h1 = h.astype(BF16)
    h2 = (h - h1.astype(F32)).astype(BF16)
    both = _dot_nt(wr_ref[...], h1)
    logits = both[:N_EXPERTS] + (both[N_EXPERTS:] + _dot_nt(wr_ref[0:N_EXPERTS, :], h2)) + br_ref[...]
    eidx = lax.broadcasted_iota(jnp.int32, (N_EXPERTS, TM), 0)
    slot = lax.broadcasted_iota(jnp.int32, (8, TM), 0)
    work = logits
    vals, sels = [], []
    idx_out = jnp.zeros((8, TM), jnp.int32)
    for k in range(TOP_K):
        m = jnp.max(work, axis=0, keepdims=True)
        idx = jnp.min(jnp.where(work == m, eidx, N_EXPERTS), axis=0, keepdims=True)
        sel = eidx == idx
        vals.append(m)
        sels.append(sel)
        idx_out = jnp.where(slot == k, idx, idx_out)
        work = jnp.where(sel, -jnp.inf, work)
    es = [jnp.exp(v - vals[0]) for v in vals]
    tot = es[0] + es[1] + es[2] + es[3]
    gate_out = jnp.zeros((8, TM), F32)
    for k in range(TOP_K):
        gate_out = jnp.where(slot == k, es[k] / tot, gate_out)
    idx_ref[...] = idx_out
    gate_ref[...] = gate_out
    multi = jnp.zeros((N_EXPERTS, TM), F32)
    for sel in sels:
        multi = jnp.where(sel, 1.0, multi)
    before = _dot(multi.astype(BF16), earlier_ref[...]) + carry_ref[...]
    rank_out = jnp.zeros((8, TM), jnp.int32)
    for k in range(TOP_K):
        rk = jnp.sum(jnp.where(sels[k], before, 0.0), axis=0, keepdims=True).astype(jnp.int32)
        rank_out = jnp.where(slot == k, rk, rank_out)
    rank_ref[...] = rank_out
    carry_ref[...] = carry_ref[...] + jnp.sum(multi, axis=1, keepdims=True)
    cnt_ref[...] = carry_ref[...]


def _outproj(xp, xs, yp, ys, op, os_, ag, wos, woa, fg, wr, br, earlier):
    tok = lambda n: pl.BlockSpec((TM, n), lambda i: (i, 0))
    slots = pl.BlockSpec((8, TM), lambda i: (0, i))
    prompt = lambda n: pl.BlockSpec((TM, n), lambda i: (jnp.minimum(i, N_PROMPT_TILES - 1), 0))
    sample = lambda n: pl.BlockSpec((TM, n), lambda i: (0, 0))
    return pl.pallas_call(
        _outproj_kernel,
        grid=(N_TILES,),
        in_specs=[
            prompt(D_MODEL), sample(D_MODEL), prompt(D_SSM), sample(D_SSM), prompt(D_ATT), sample(D_ATT),
            _const_spec((1, D_ATT)),
            _const_spec((D_SSM, D_MODEL)), _const_spec((D_ATT, D_MODEL)),
            _const_spec((1, D_MODEL)),
            _const_spec((2 * N_EXPERTS, D_MODEL)), _const_spec((N_EXPERTS, 1)),
            _const_spec((TM, TM)),
        ],
        out_specs=[tok(D_MODEL), pl.BlockSpec(_tiled(TM), lambda i: (i, 0)), slots, slots, slots,
                   _const_spec((N_EXPERTS, 1))],
        out_shape=[
            jax.ShapeDtypeStruct((N_TOK, D_MODEL), F32),
            jax.ShapeDtypeStruct(_tiled(N_TOK), U32),
            jax.ShapeDtypeStruct((8, N_TOK), jnp.int32),
            jax.ShapeDtypeStruct((8, N_TOK), F32),
            jax.ShapeDtypeStruct((8, N_TOK), jnp.int32),
            jax.ShapeDtypeStruct((N_EXPERTS, 1), F32),
        ],
        scratch_shapes=[pltpu.VMEM((N_EXPERTS, 1), F32)],
        compiler_params=_params(),
        name="outproj_router",
    )(xp, xs, yp, ys, op, os_, ag, wos, woa, fg, wr, br, earlier)


def _scatter_kernel(dest_ref, pend_ref, h_ref, rows_ref, zero_ref, sem, zsem):
    i = pl.program_id(0)

    @pl.when(i == 0)
    def _():
        zero_ref[...] = jnp.zeros_like(zero_ref)

        block_tiles = _tiled(MOE_BM)[0]

        def zero_block(b):
            start = pl.multiple_of(b * block_tiles, block_tiles)
            return pltpu.make_async_copy(zero_ref, rows_ref.at[pl.ds(start, block_tiles)], zsem)

        def last_block(e):
            end = pend_ref[e]
            nonempty = end > (pend_ref[e - 1] if e > 0 else 0)
            return nonempty, zero_block(jnp.maximum(end // MOE_BM - 1, 0))

        for e in range(N_EXPERTS):
            nonempty, cp = last_block(e)
            pl.when(nonempty)(cp.start)
        for e in range(N_EXPERTS):
            nonempty, cp = last_block(e)
            pl.when(nonempty)(cp.wait)

        first_unused = pend_ref[N_EXPERTS - 1] // MOE_BM
        lax.fori_loop(first_unused, MOE_BLOCKS, lambda b, c: (zero_block(b).start(), c)[1], 0)
        lax.fori_loop(first_unused, MOE_BLOCKS, lambda b, c: (zero_block(b).wait(), c)[1], 0)

    def issue(r, carry):
        for k in range(TOP_K):
            d = dest_ref[r * TOP_K + k]
            pltpu.make_async_copy(_tile_of(h_ref, r), _tile_of(rows_ref, d), sem).start(priority=k % 2)
        return carry

    lax.fori_loop(0, TM, issue, 0, unroll=ISSUE_UNROLL)
    for _ in range(TOP_K):
        pltpu.make_async_copy(h_ref, rows_ref.at[pl.ds(0, _tiled(TM)[0])], sem).wait()


def _scatter_rows(dest_flat, pad_end, h):
    return pl.pallas_call(
        _scatter_kernel,
        grid=(N_TILES,),
        in_specs=[
            pl.BlockSpec((TM * TOP_K,), lambda i: (i,), memory_space=pltpu.SMEM),
            pl.BlockSpec((N_EXPERTS,), lambda i: (0,), memory_space=pltpu.SMEM),
            pl.BlockSpec(_tiled(TM), lambda i: (i, 0)),
        ],
        out_specs=pl.BlockSpec(memory_space=pl.ANY),
        out_shape=jax.ShapeDtypeStruct(_tiled(MOE_ROWS), U32),
        scratch_shapes=[pltpu.VMEM(_tiled(MOE_BM), U32), pltpu.SemaphoreType.DMA(()),
                        pltpu.SemaphoreType.DMA(())],
        compiler_params=_params(),
        name="moe_scatter",
    )(dest_flat, pad_end, h)


def _expert_kernel(be_ref, nu_ref, nxt_ref, x_ref, wgu_hbm, bgu_ref, wd_hbm, bd_ref, y_ref,
                   wgu_f, wd_f, wgu_s, wd_s, sem):
    i = pl.program_id(0)
    active = i < nu_ref[0]
    e = be_ref[i]

    def fetch(expert):
        return (pltpu.make_async_copy(wgu_hbm.at[expert], wgu_f, sem.at[0]),
                pltpu.make_async_copy(wd_hbm.at[expert], wd_f, sem.at[1]))

    @pl.when(active & (i == 0))
    def _():
        for cp in fetch(e):
            cp.start()

    @pl.when(active & ((i == 0) | (e != be_ref[jnp.maximum(i - 1, 0)])))
    def _():
        for cp in fetch(e):
            cp.wait()
        wgu_s[...] = wgu_f[...].astype(BF16)
        wd_s[...] = wd_f[...].astype(BF16)
        nxt = nxt_ref[e]

        @pl.when(nxt >= 0)
        def _():
            for cp in fetch(nxt):
                cp.start()

    @pl.when(active)
    def _():
        gu = _dot(_tiles_to_rows(x_ref, MOE_BM).astype(BF16), wgu_s[...]) + bgu_ref[0]
        gate = jnp.minimum(gu[:, :D_FF], SWIGLU_LIMIT)
        up = jnp.clip(gu[:, D_FF:], -SWIGLU_LIMIT, SWIGLU_LIMIT)
        act = (up + 1.0) * gate * _sigmoid(gate * SWIGLU_ALPHA)
        _rows_to_tiles(y_ref, _dot(act.astype(BF16), wd_s[...]) + bd_ref[0])

    @pl.when(jnp.logical_not(active))
    def _():
        y_ref[...] = jnp.zeros_like(y_ref)


def _experts(block_expert, n_used, next_expert, rows, wgu, bgu, wd, bd):
    grid_spec = pltpu.PrefetchScalarGridSpec(
        num_scalar_prefetch=3,
        grid=(MOE_BLOCKS,),
        in_specs=[
            pl.BlockSpec(_tiled(MOE_BM), lambda i, be, nu, nx: (jnp.minimum(i, nu[0] - 1), 0)),
            pl.BlockSpec(memory_space=pl.ANY),
            pl.BlockSpec((1, 1, 2 * D_FF), lambda i, be, nu, nx: (be[i], 0, 0)),
            pl.BlockSpec(memory_space=pl.ANY),
            pl.BlockSpec((1, 1, D_MODEL), lambda i, be, nu, nx: (be[i], 0, 0)),
        ],
        out_specs=pl.BlockSpec(_tiled(MOE_BM), lambda i, be, nu, nx: (i, 0)),
        scratch_shapes=[pltpu.VMEM((D_MODEL, 2 * D_FF), F32), pltpu.VMEM((D_FF, D_MODEL), F32),
                        pltpu.VMEM((D_MODEL, 2 * D_FF), BF16), pltpu.VMEM((D_FF, D_MODEL), BF16),
                        pltpu.SemaphoreType.DMA((2,))],
    )
    return pl.pallas_call(
        _expert_kernel,
        grid_spec=grid_spec,
        out_shape=jax.ShapeDtypeStruct(_tiled(MOE_ROWS), U32),
        compiler_params=_params(),
        name="moe_experts",
    )(block_expert, n_used, next_expert, rows, wgu, bgu, wd, bd)


def _combine_kernel(dest_ref, dest_next_ref, gate_ref, xmid_ref, g_ref, rows_ref, yp_ref, ys_ref, buf_ref, sem):
    i = pl.program_id(0)
    slot = i % 2

    def issue(idx_ref, s):
        def body(r, carry):
            for k in range(TOP_K):
                d = idx_ref[r * TOP_K + k]
                pltpu.make_async_copy(_tile_of(rows_ref, d), _tile_of(buf_ref.at[s], k * TM + r),
                                      sem.at[s]).start(priority=k % 2)
            return carry

        lax.fori_loop(0, TM, body, 0, unroll=ISSUE_UNROLL)

    @pl.when(i == 0)
    def _():
        issue(dest_ref, 0)

    @pl.when(i + 1 < N_TILES)
    def _():
        issue(dest_next_ref, 1 - slot)

    buf = buf_ref.at[slot]
    slot_tiles = _tiled(TM)[0]
    for k in range(TOP_K):
        pltpu.make_async_copy(rows_ref.at[pl.ds(0, slot_tiles)], buf.at[pl.ds(k * slot_tiles, slot_tiles)],
                              sem.at[slot]).wait()
    acc = xmid_ref[...]
    for k in range(TOP_K):
        acc = acc + _tiles_to_rows(buf, TM, first=k * TM) * gate_ref[:, k:k + 1]
    y = _rms(acc, g_ref[...])

    @pl.when(i < N_PROMPT_TILES)
    def _():
        yp_ref[...] = y

    @pl.when(i == N_PROMPT_TILES)
    def _():
        ys_ref[...] = y


def _combine(dest_flat, gates, xmid, g, y_rows):
    return pl.pallas_call(
        _combine_kernel,
        grid=(N_TILES,),
        in_specs=[
            pl.BlockSpec((TM * TOP_K,), lambda i: (i,), memory_space=pltpu.SMEM),
            pl.BlockSpec((TM * TOP_K,), lambda i: (jnp.minimum(i + 1, N_TILES - 1),), memory_space=pltpu.SMEM),
            pl.BlockSpec((TM, TOP_K), lambda i: (i, 0)),
            pl.BlockSpec((TM, D_MODEL), lambda i: (i, 0)),
            _const_spec((1, D_MODEL)),
            pl.BlockSpec(memory_space=pl.ANY),
        ],
        out_specs=[
            pl.BlockSpec((TM, D_MODEL), lambda i: (jnp.minimum(i, N_PROMPT_TILES - 1), 0)),
            pl.BlockSpec((TM, D_MODEL), lambda i: (0, 0)),
        ],
        out_shape=[
            jax.ShapeDtypeStruct((N_PROMPT, D_MODEL), F32),
            jax.ShapeDtypeStruct((N_SAMPLE, D_MODEL), F32),
        ],
        scratch_shapes=[pltpu.VMEM((2,) + _tiled(TOP_K * TM), U32), pltpu.SemaphoreType.DMA((2,))],
        compiler_params=_params(),
        name="moe_combine",
    )(dest_flat, dest_flat, gates, xmid, g, y_rows)


def _band_bias(table):
    n_diag = BAND + CHUNK - 1
    idx = np.clip(ATT_LEFT + (CHUNK - 1) - np.arange(n_diag), -REL_CLIP, REL_CLIP) + REL_CLIP
    pick = (np.arange(2 * REL_CLIP + 1)[:, None] == idx[None, :]).astype(np.float32)
    diag = jnp.dot(table, jnp.asarray(pick), precision=lax.Precision.HIGHEST)
    return jnp.stack([diag[:, CHUNK - 1 - qi:CHUNK - 1 - qi + BAND] for qi in range(CHUNK)], axis=1)


def _pair_rows(v):
    return jnp.repeat(v.reshape(HEAD_PAIRS, 2), CHUNK, axis=1)


def _layer(l, xp, xs, cache_k, cache_v, state_conv, state_ssm,
           norm_mix_g, w_in, conv_w, conv_b, dt_bias, a_log, d_skip, ssm_norm_g,
           att_norm_g, rel_bias_table, w_out, norm_ffn_g, w_router, b_router,
           w_gate_up, b_gate_up, w_down, b_down, norm_final_g):
    wb = w_in[l].astype(BF16)
    c0 = D_SSM
    c1 = c0 + D_CONV
    c2 = c1 + SSM_HEADS
    c3 = c2 + D_ATT
    c4 = c3 + D_ATT
    z, xbc, dt, dtt, q, k, v, k_p, v_p, k_s, v_s, ctail = _inproj(
        xp, xs, norm_mix_g[l][None], wb[:, :c0], wb[:, c0:c1], wb[:, c1:c2], wb[:, c1:c2].T,
        wb[:, c2:c3], wb[:, c3:c4], wb[:, c4:])

    n_chunks = N_TOK // CHUNK
    dtp = dtt.reshape(HEAD_PAIRS, 2, n_chunks, CHUNK).transpose(2, 0, 1, 3).reshape(n_chunks, HEAD_PAIRS, 128)
    hp = jnp.arange(D_SSM) // SSM_HEAD_DIM
    expand = (hp[None, :] == jnp.arange(SSM_HEADS)[:, None]).astype(BF16)
    r128 = jnp.arange(128)
    triu2 = ((r128[:, None] // CHUNK == r128[None, :] // CHUNK) & (r128[:, None] <= r128[None, :])).astype(BF16)
    consts = (conv_w[l], conv_b[l][None], dt_bias[l][None], _pair_rows(dt_bias[l]),
              a_log[l][None], _pair_rows(a_log[l]),
              jnp.repeat(d_skip[l], SSM_HEAD_DIM)[None], ssm_norm_g[l][None],
              expand, triu2)
    y_ssm_p, ssm_p = _ssd_prompt(z, xbc, dt, dtp, consts)
    y_ssm_s, ssm_s = _ssd_sample(z, xbc, dt, dtp, state_conv[l],
                                 state_ssm[l].reshape(DEC_BATCH, D_SSM, D_STATE), consts)

    bias2 = _band_bias(rel_bias_table[l]).reshape(ATT_PAIRS, 2 * CHUNK, BAND)
    o_att_p = _attn_prompt(q, k, v, bias2)
    o_att_s = _attn_sample(q, k, v, cache_k[l].reshape(DEC_BATCH, ATT_LEFT, D_ATT),
                           cache_v[l].reshape(DEC_BATCH, ATT_LEFT, D_ATT), bias2)

    wo = w_out[l].astype(BF16)
    wr = w_router[l].T
    wr1 = wr.astype(BF16)
    wr2 = (wr - wr1.astype(F32)).astype(BF16)
    earlier = jnp.triu(jnp.ones((TM, TM), BF16), 1)
    xmid, h, top_idx, gates, rank, counts = _outproj(
        xp, xs, y_ssm_p, y_ssm_s, o_att_p, o_att_s, att_norm_g[l][None], wo[:D_SSM], wo[D_SSM:], norm_ffn_g[l][None],
        jnp.concatenate([wr1, wr2], axis=0), b_router[l][:, None], earlier)
    gates = gates[:TOP_K].T

    counts = counts[:, 0].astype(jnp.int32)
    padded = (counts + MOE_BM - 1) // MOE_BM * MOE_BM
    pad_end = jnp.cumsum(padded)
    pad_start = pad_end - padded
    experts = jnp.arange(N_EXPERTS, dtype=jnp.int32)
    start_of = jnp.sum(jnp.where(top_idx[:TOP_K, :, None] == experts, pad_start, 0), axis=-1)
    dest = (start_of + rank[:TOP_K]).T.reshape(-1).astype(jnp.int32)
    block_start = jnp.arange(MOE_BLOCKS, dtype=jnp.int32) * MOE_BM
    block_expert = jnp.minimum(jnp.sum((pad_end[None, :] <= block_start[:, None]).astype(jnp.int32), axis=1),
                               N_EXPERTS - 1).astype(jnp.int32)
    n_used = (pad_end[-1:] // MOE_BM).astype(jnp.int32)
    later_nonempty = (experts[None, :] > experts[:, None]) & (padded[None, :] > 0)
    next_expert = jnp.min(jnp.where(later_nonempty, experts[None, :], N_EXPERTS), axis=1)
    next_expert = jnp.where(next_expert < N_EXPERTS, next_expert, -1).astype(jnp.int32)

    rows = _scatter_rows(dest, pad_end.astype(jnp.int32), h)
    y_rows = _experts(block_expert, n_used, next_expert, rows, w_gate_up[l], b_gate_up[l][:, None, :],
                      w_down[l], b_down[l][:, None, :])
    y_p, y_s = _combine(dest, gates, xmid, norm_final_g[None], y_rows)

    keep = min(ATT_LEFT, SEQ)
    k_p = k_p[:, TM - keep:].reshape(BATCH, keep, ATT_HEADS, ATT_HEAD_DIM)
    v_p = v_p[:, TM - keep:].reshape(BATCH, keep, ATT_HEADS, ATT_HEAD_DIM)
    k_s = k_s.reshape(DEC_BATCH, DEC_SEQ, ATT_HEADS, ATT_HEAD_DIM)
    v_s = v_s.reshape(DEC_BATCH, DEC_SEQ, ATT_HEADS, ATT_HEAD_DIM)
    conv_p = ctail[:BATCH, -(CONV_W - 1):]
    conv_s = ctail[BATCH].reshape(DEC_BATCH, 8, D_CONV)[:, -(CONV_W - 1):]
    ssm_p = ssm_p.reshape(BATCH, SSM_HEADS, SSM_HEAD_DIM, D_STATE)
    ssm_s = ssm_s.reshape(DEC_BATCH, SSM_HEADS, SSM_HEAD_DIM, D_STATE)
    return (y_p.reshape(BATCH, SEQ, D_MODEL), y_s.reshape(DEC_BATCH, DEC_SEQ, D_MODEL),
            k_p, v_p, conv_p, ssm_p, k_s, v_s, conv_s, ssm_s)


def kernel(x_prompt, x_sample, cache_k, cache_v, state_conv, state_ssm, norm_mix_g, w_in, conv_w, conv_b,
           dt_bias, a_log, d_skip, ssm_norm_g, att_norm_g, rel_bias_table, w_out, norm_ffn_g, w_router,
           b_router, w_gate_up, b_gate_up, w_down, b_down, norm_final_g):
    assert w_in.shape[0] == 1, "single trunk layer"
    xp = x_prompt.reshape(N_PROMPT, D_MODEL)
    xs = x_sample.reshape(N_SAMPLE, D_MODEL)
    outs = _layer(0, xp, xs, cache_k, cache_v, state_conv, state_ssm,
                  norm_mix_g, w_in, conv_w, conv_b, dt_bias, a_log, d_skip, ssm_norm_g,
                  att_norm_g, rel_bias_table, w_out, norm_ffn_g, w_router, b_router,
                  w_gate_up, b_gate_up, w_down, b_down, norm_final_g)
    y_p, y_s, k_p, v_p, conv_p, ssm_p, k_s, v_s, conv_s, ssm_s = outs
    return (y_p, y_s, k_p[None], v_p[None], conv_p[None], ssm_p[None],
            k_s[None], v_s[None], conv_s[None], ssm_s[None])
```

```python
import jax
import jax.numpy as jnp
import numpy as np
from jax import lax
from jax.experimental import pallas as pl
from jax.experimental.pallas import tpu as pltpu

D_MODEL = 1024
BATCH = 8
SEQ = 2048
DEC_BATCH = 8
DEC_SEQ = 64
CHUNK = 64
SSM_HEADS = 16
SSM_HEAD_DIM = 64
D_SSM = SSM_HEADS * SSM_HEAD_DIM
SSM_GROUPS = 2
D_STATE = 128
CONV_W = 4
D_BC = SSM_GROUPS * D_STATE
D_CONV = D_SSM + 2 * D_BC
ATT_HEADS = 8
ATT_HEAD_DIM = 64
D_ATT = ATT_HEADS * ATT_HEAD_DIM
LEFT_CHUNKS = 8
ATT_LEFT = LEFT_CHUNKS * CHUNK
BAND = ATT_LEFT + CHUNK
REL_CLIP = 128
ATT_SCALE = ATT_HEAD_DIM ** -0.5
N_EXPERTS = 32
TOP_K = 4
D_FF = D_MODEL
SWIGLU_ALPHA = 1.702
SWIGLU_LIMIT = 7.0
EPS = 1e-5

F32 = jnp.float32
BF16 = jnp.bfloat16
U32 = jnp.uint32

LANES = 128
SUBLANES = 8

N_PROMPT = BATCH * SEQ
N_SAMPLE = DEC_BATCH * DEC_SEQ
N_TOK = N_PROMPT + N_SAMPLE
TM = 512
N_PROMPT_TILES = N_PROMPT // TM
N_TILES = N_TOK // TM
TILES_PER_SEQ = SEQ // TM
CHUNKS_PER_TILE = TM // CHUNK
N_CHUNKS_SEQ = SEQ // CHUNK
assert 2 * SSM_HEAD_DIM == LANES and 2 * ATT_HEAD_DIM == LANES
HEAD_PAIRS = SSM_HEADS // 2
PAIRS_PER_GROUP = HEAD_PAIRS // SSM_GROUPS
ATT_PAIRS = ATT_HEADS // 2
SSD_BLOCK = 2
CONV_HEAD = SUBLANES
MOE_BM = 512
N_ASSIGN = N_TOK * TOP_K
MOE_BLOCKS = N_ASSIGN // MOE_BM + N_EXPERTS
MOE_ROWS = MOE_BLOCKS * MOE_BM
ISSUE_UNROLL = 8
ROW_TILE = (SUBLANES // 2, LANES)
assert 2 * ROW_TILE[0] * ROW_TILE[1] == D_MODEL
VMEM_BYTES = 64 * 1024 * 1024
VMEM_LIMIT = VMEM_BYTES * 7 // 8


def _dot(a, b):
    return jnp.dot(a, b, preferred_element_type=F32)


def _dot_nt(a, b):
    return lax.dot_general(a, b, (((1,), (1,)), ((), ())), preferred_element_type=F32)


def _dot_tn(a, b):
    return lax.dot_general(a, b, (((0,), (0,)), ((), ())), preferred_element_type=F32)


def _split3(x):
    x1 = x.astype(BF16)
    r1 = x - x1.astype(F32)
    x2 = r1.astype(BF16)
    r2 = r1 - x2.astype(F32)
    return x1, x2, r2.astype(BF16)


def _dot_exact_rhs(x, m):
    x1, x2, x3 = _split3(x)
    return _dot(x1, m) + _dot(x2, m) + _dot(x3, m)


def _dot_exact_lhs(m, x):
    x1, x2, x3 = _split3(x)
    return _dot(m, x1) + _dot(m, x2) + _dot(m, x3)


def _rms(x, g):
    return x * lax.rsqrt(jnp.mean(x * x, axis=-1, keepdims=True) + EPS) * g


def _sigmoid(x):
    return 1.0 / (1.0 + jnp.exp(-x))


def _softplus(x):
    return jnp.maximum(x, 0.0) + jnp.log(1.0 + jnp.exp(-jnp.abs(x)))


def _tiled(n):
    return (n * ROW_TILE[0], ROW_TILE[1])


def _tile_of(ref, row):
    return ref.at[pl.ds(pl.multiple_of(row * ROW_TILE[0], ROW_TILE[0]), ROW_TILE[0])]


def _rows_to_tiles(ref, x, first=0):
    sub, lanes = ROW_TILE
    half = D_MODEL // 2
    hi = lax.bitcast_convert_type(x[:, :half].astype(BF16).astype(F32), U32)
    lo = lax.bitcast_convert_type(x[:, half:].astype(BF16).astype(F32), U32)
    words = hi | (lo >> 16)
    for j in range(sub):
        ref[pl.ds(first * sub + j, x.shape[0], stride=sub), :] = words[:, j * lanes:(j + 1) * lanes]


def _tiles_to_rows(ref, n, first=0):
    sub = ROW_TILE[0]
    words = jnp.concatenate([ref[pl.ds(first * sub + j, n, stride=sub), :] for j in range(sub)], axis=1)
    hi = lax.bitcast_convert_type(words & jnp.uint32(0xFFFF0000), F32)
    lo = lax.bitcast_convert_type(words << 16, F32)
    return jnp.concatenate([hi, lo], axis=1)


def _const_spec(shape):
    nd = len(shape)
    return pl.BlockSpec(shape, lambda *_: (0,) * nd)


def _params(n_axes=1):
    return pltpu.CompilerParams(dimension_semantics=("arbitrary",) * n_axes,
                                vmem_limit_bytes=VMEM_LIMIT)


def _inproj_kernel(xp_ref, xs_ref, g_ref, wz_ref, wxbc_ref, wdt_ref, wdtt_ref, wq_ref, wk_ref, wv_ref,
                   z_ref, xbc_ref, dt_ref, dtt_ref, q_ref, k_ref, v_ref, kp_ref, vp_ref, ks_ref, vs_ref, ctail_ref):
    i = pl.program_id(0)
    x = jnp.where(i == N_PROMPT_TILES, xs_ref[...], xp_ref[...])
    h = _rms(x, g_ref[...]).astype(BF16)
    z_ref[...] = _dot(h, wz_ref[...]).astype(BF16)
    xbc = _dot(h, wxbc_ref[...])
    xbc_ref[...] = xbc.astype(BF16)
    for c in range(CHUNKS_PER_TILE):
        ctail_ref[0, c * SUBLANES:(c + 1) * SUBLANES, :] = xbc[(c + 1) * CHUNK - SUBLANES:(c + 1) * CHUNK, :]
    dt_ref[...] = _dot(h, wdt_ref[...])
    dtt_ref[...] = _dot_nt(wdtt_ref[...], h)
    q_ref[...] = (_dot(h, wq_ref[...]) * ATT_SCALE).astype(BF16)
    k = _dot(h, wk_ref[...])
    v = _dot(h, wv_ref[...])
    k_ref[...] = k.astype(BF16)
    v_ref[...] = v.astype(BF16)

    @pl.when(i < N_PROMPT_TILES)
    def _():
        kp_ref[0] = k
        vp_ref[0] = v

    @pl.when(i == N_PROMPT_TILES)
    def _():
        ks_ref[...] = k
        vs_ref[...] = v


def _inproj(xp, xs, g, wz, wxbc, wdt, wdtt, wq, wk, wv):
    tok = lambda n: pl.BlockSpec((TM, n), lambda i: (i, 0))
    tail_idx = lambda i: (i // TILES_PER_SEQ, 0, 0)
    seq_idx = lambda i: (jnp.minimum(i // TILES_PER_SEQ, BATCH - 1), 0, 0)
    n_tail = BATCH + 1
    return pl.pallas_call(
        _inproj_kernel,
        grid=(N_TILES,),
        in_specs=[
            pl.BlockSpec((TM, D_MODEL), lambda i: (jnp.minimum(i, N_PROMPT_TILES - 1), 0)),
            pl.BlockSpec((TM, D_MODEL), lambda i: (0, 0)),
            _const_spec((1, D_MODEL)),
            _const_spec((D_MODEL, D_SSM)),
            _const_spec((D_MODEL, D_CONV)),
            _const_spec((D_MODEL, SSM_HEADS)),
            _const_spec((SSM_HEADS, D_MODEL)),
            _const_spec((D_MODEL, D_ATT)),
            _const_spec((D_MODEL, D_ATT)),
            _const_spec((D_MODEL, D_ATT)),
        ],
        out_specs=[
            tok(D_SSM), tok(D_CONV), tok(SSM_HEADS),
            pl.BlockSpec((SSM_HEADS, TM), lambda i: (0, i)),
            tok(D_ATT), tok(D_ATT), tok(D_ATT),
            pl.BlockSpec((1, TM, D_ATT), seq_idx),
            pl.BlockSpec((1, TM, D_ATT), seq_idx),
            pl.BlockSpec((TM, D_ATT), lambda i: (0, 0)),
            pl.BlockSpec((TM, D_ATT), lambda i: (0, 0)),
            pl.BlockSpec((1, CHUNKS_PER_TILE * SUBLANES, D_CONV), tail_idx),
        ],
        out_shape=[
            jax.ShapeDtypeStruct((N_TOK, D_SSM), BF16),
            jax.ShapeDtypeStruct((N_TOK, D_CONV), BF16),
            jax.ShapeDtypeStruct((N_TOK, SSM_HEADS), F32),
            jax.ShapeDtypeStruct((SSM_HEADS, N_TOK), F32),
            jax.ShapeDtypeStruct((N_TOK, D_ATT), BF16),
            jax.ShapeDtypeStruct((N_TOK, D_ATT), BF16),
            jax.ShapeDtypeStruct((N_TOK, D_ATT), BF16),
            jax.ShapeDtypeStruct((BATCH, TM, D_ATT), F32),
            jax.ShapeDtypeStruct((BATCH, TM, D_ATT), F32),
            jax.ShapeDtypeStruct((N_SAMPLE, D_ATT), F32),
            jax.ShapeDtypeStruct((N_SAMPLE, D_ATT), F32),
            jax.ShapeDtypeStruct((n_tail, CHUNKS_PER_TILE * SUBLANES, D_CONV), F32),
        ],
        compiler_params=_params(),
        name="inproj",
    )(xp, xs, g, wz, wxbc, wdt, wdtt, wq, wk, wv)


def _ssd_tile(n_chunks, z_ref, xbc_ref, dt_ref, dtp_ref, cw_ref, cb_ref, dtb_ref, dtbp_ref,
              alog_h_ref, alog_p_ref, dskip_e_ref, ng_ref, expand_ref, triu2_ref,
              y_ref, xw_ref, state_ref):
    nb = SSD_BLOCK if n_chunks % SSD_BLOCK == 0 else 1
    rb = nb * CHUNK
    half = D_SSM // SSM_GROUPS
    xw_ref[CONV_HEAD:CONV_HEAD + n_chunks * CHUNK, :] = xbc_ref[...].astype(F32)

    a_h = -jnp.exp(alog_h_ref[...])
    a_p = jnp.concatenate([-jnp.exp(alog_p_ref[...])] * nb, axis=0)
    dtb_p = jnp.concatenate([dtbp_ref[...]] * nb, axis=0)
    tr_r = lax.broadcasted_iota(jnp.int32, (rb, rb), 0)
    tr_c = lax.broadcasted_iota(jnp.int32, (rb, rb), 1)
    tril_b = jnp.where((tr_r // CHUNK == tr_c // CHUNK) & (tr_c <= tr_r), 1.0, 0.0).astype(BF16)
    row_i = lax.broadcasted_iota(jnp.int32, (CHUNK, LANES), 0)
    col_i = lax.broadcasted_iota(jnp.int32, (CHUNK, LANES), 1)
    causal2 = row_i >= (col_i % CHUNK)
    sel_r = lax.broadcasted_iota(jnp.int32, ((CONV_W - 1) * rb, CONV_HEAD + rb), 0)
    sel_c = lax.broadcasted_iota(jnp.int32, ((CONV_W - 1) * rb, CONV_HEAD + rb), 1)
    shift_sel = (sel_c == sel_r % rb + sel_r // rb + CONV_HEAD - (CONV_W - 1)).astype(F32)
    bd_r = lax.broadcasted_iota(jnp.int32, (LANES, LANES), 0) // CHUNK
    bd_c = lax.broadcasted_iota(jnp.int32, (LANES, LANES), 1) // CHUNK
    blockdiag = bd_r == bd_c

    def block(i, carry):
        r0 = pl.multiple_of(i * rb, rb)
        win = xw_ref[pl.ds(r0, CONV_HEAD + rb), :]
        shifted = _dot(shift_sel, win)
        acc = cb_ref[...] + cw_ref[CONV_W - 1:CONV_W, :] * win[CONV_HEAD:, :]
        for tap in range(CONV_W - 1):
            acc = acc + cw_ref[tap:tap + 1, :] * shifted[tap * rb:(tap + 1) * rb, :]
        xa = acc * _sigmoid(acc)
        xs = xa[:, 0:D_SSM]
        bm = xa[:, D_SSM:D_SSM + D_BC].astype(BF16)
        cm = xa[:, D_SSM + D_BC:D_CONV].astype(BF16)
        dt = _softplus(dt_ref[pl.ds(r0, rb), :] + dtb_ref[...])
        dt_e = _dot_exact_rhs(dt, expand_ref[...])
        acum = _dot_exact_rhs(_dot_exact_lhs(tril_b, dt * a_h), expand_ref[...])
        dtp = _softplus(dtp_ref[pl.ds(i * nb, nb)].reshape(nb * HEAD_PAIRS, LANES) + dtb_p)
        acum_p = _dot_exact_rhs(dtp * a_p, triu2_ref[...])
        xdt = xs * dt_e
        a_last = [acum[(c + 1) * CHUNK - 1:(c + 1) * CHUNK, :] for c in range(nb)]
        a_end = jnp.concatenate([jnp.broadcast_to(a, (CHUNK, D_SSM)) for a in a_last], axis=0)
        xdt_end = (xdt * jnp.exp(a_end - acum)).astype(BF16)

        y_diag, new_s = [], []
        for c in range(nb):
            rs = slice(c * CHUNK, (c + 1) * CHUNK)
            y_parts = []
            for g in range(SSM_GROUPS):
                bg = bm[rs, g * D_STATE:(g + 1) * D_STATE]
                cg = cm[rs, g * D_STATE:(g + 1) * D_STATE]
                cb2 = _dot_nt(cg, jnp.concatenate([bg, bg], axis=0))
                for jj in range(PAIRS_PER_GROUP):
                    j = g * PAIRS_PER_GROUP + jj
                    seg = acum[rs, j * LANES:(j + 1) * LANES] - acum_p[c * HEAD_PAIRS + j:c * HEAD_PAIRS + j + 1, :]
                    decay = jnp.exp(jnp.where(causal2, seg, -jnp.inf))
                    x2 = xdt[rs, j * LANES:(j + 1) * LANES]
                    rhs = jnp.where(blockdiag, jnp.concatenate([x2, x2], axis=0), 0.0).astype(BF16)
                    y_parts.append(_dot((cb2 * decay).astype(BF16), rhs))
            y_diag.append(jnp.concatenate(y_parts, axis=1))
            new_s.append(jnp.concatenate(
                [_dot_tn(bm[rs, g * D_STATE:(g + 1) * D_STATE], xdt_end[rs, g * half:(g + 1) * half])
                 for g in range(SSM_GROUPS)], axis=1))

        y_off = []
        for c in range(nb):
            rs = slice(c * CHUNK, (c + 1) * CHUNK)
            state = state_ref[...]
            y_off.append(jnp.concatenate(
                [_dot(cm[rs, g * D_STATE:(g + 1) * D_STATE], state[:, g * half:(g + 1) * half].astype(BF16))
                 for g in range(SSM_GROUPS)], axis=1))
            state_ref[...] = state * jnp.exp(a_last[c]) + new_s[c]

        y = (jnp.concatenate(y_diag, axis=0) + jnp.concatenate(y_off, axis=0) * jnp.exp(acum)
             + dskip_e_ref[...] * xs)
        zc = z_ref[pl.ds(r0, rb), :].astype(F32)
        y = y * (zc * _sigmoid(zc))
        yn = jnp.concatenate(
            [y[:, g * half:(g + 1) * half]
             * lax.rsqrt(jnp.mean(jnp.square(y[:, g * half:(g + 1) * half]), axis=-1, keepdims=True) + EPS)
             for g in range(SSM_GROUPS)], axis=1)
        y_ref[pl.ds(r0, rb), :] = (yn * ng_ref[...]).astype(BF16)
        return carry

    lax.fori_loop(0, n_chunks // nb, block, 0)


def _state_store(state_ref, out_ref):
    for j in range(HEAD_PAIRS):
        out_ref[0, j * LANES:(j + 1) * LANES, :] = state_ref[:, j * LANES:(j + 1) * LANES].T


def _ssd_prompt_kernel(z_ref, xbc_ref, dt_ref, dtp_ref, cw_ref, cb_ref, dtb_ref, dtbp_ref,
                       alog_h_ref, alog_p_ref, dskip_e_ref, ng_ref, expand_ref, triu2_ref,
                       y_ref, ssm_ref, xw_ref, state_ref, tail_ref):
    t = pl.program_id(1)

    @pl.when(t == 0)
    def _():
        state_ref[...] = jnp.zeros_like(state_ref)
        xw_ref[0:CONV_HEAD, :] = jnp.zeros((CONV_HEAD, D_CONV), F32)

    @pl.when(t > 0)
    def _():
        xw_ref[0:CONV_HEAD, :] = tail_ref[...]

    _ssd_tile(CHUNKS_PER_TILE, z_ref, xbc_ref, dt_ref, dtp_ref, cw_ref, cb_ref, dtb_ref, dtbp_ref,
              alog_h_ref, alog_p_ref, dskip_e_ref, ng_ref, expand_ref, triu2_ref,
              y_ref, xw_ref, state_ref)
    tail_ref[...] = xw_ref[TM:TM + CONV_HEAD, :]

    @pl.when(t == TILES_PER_SEQ - 1)
    def _():
        _state_store(state_ref, ssm_ref)


def _ssd_sample_kernel(z_ref, xbc_ref, dt_ref, dtp_ref, cprev_ref, sprev_ref,
                       cw_ref, cb_ref, dtb_ref, dtbp_ref,
                       alog_h_ref, alog_p_ref, dskip_e_ref, ng_ref, expand_ref, triu2_ref,
                       y_ref, ssm_ref, xw_ref, state_ref):
    xw_ref[0:CONV_HEAD, :] = jnp.zeros((CONV_HEAD, D_CONV), F32)
    xw_ref[CONV_HEAD - (CONV_W - 1):CONV_HEAD, :] = cprev_ref[0]
    for j in range(HEAD_PAIRS):
        state_ref[:, j * LANES:(j + 1) * LANES] = sprev_ref[0, j * LANES:(j + 1) * LANES, :].T
    _ssd_tile(1, z_ref, xbc_ref, dt_ref, dtp_ref, cw_ref, cb_ref, dtb_ref, dtbp_ref,
              alog_h_ref, alog_p_ref, dskip_e_ref, ng_ref, expand_ref, triu2_ref,
              y_ref, xw_ref, state_ref)
    _state_store(state_ref, ssm_ref)


def _ssd_const_specs():
    return [
        _const_spec((CONV_W, D_CONV)), _const_spec((1, D_CONV)),
        _const_spec((1, SSM_HEADS)), _const_spec((HEAD_PAIRS, LANES)),
        _const_spec((1, SSM_HEADS)), _const_spec((HEAD_PAIRS, LANES)),
        _const_spec((1, D_SSM)), _const_spec((1, D_SSM)),
        _const_spec((SSM_HEADS, D_SSM)), _const_spec((LANES, LANES)),
    ]


def _ssd_prompt(z, xbc, dt, dtp, consts):
    tile = lambda b, t: (b * TILES_PER_SEQ + t, 0)
    return pl.pallas_call(
        _ssd_prompt_kernel,
        grid=(BATCH, TILES_PER_SEQ),
        in_specs=[
            pl.BlockSpec((TM, D_SSM), tile),
            pl.BlockSpec((TM, D_CONV), tile),
            pl.BlockSpec((TM, SSM_HEADS), tile),
            pl.BlockSpec((CHUNKS_PER_TILE, HEAD_PAIRS, LANES), lambda b, t: (b * TILES_PER_SEQ + t, 0, 0)),
        ] + _ssd_const_specs(),
        out_specs=[
            pl.BlockSpec((TM, D_SSM), tile),
            pl.BlockSpec((1, D_SSM, D_STATE), lambda b, t: (b, 0, 0)),
        ],
        out_shape=[
            jax.ShapeDtypeStruct((N_PROMPT, D_SSM), BF16),
            jax.ShapeDtypeStruct((BATCH, D_SSM, D_STATE), F32),
        ],
        scratch_shapes=[
            pltpu.VMEM((CONV_HEAD + TM, D_CONV), F32),
            pltpu.VMEM((D_STATE, D_SSM), F32),
            pltpu.VMEM((CONV_HEAD, D_CONV), F32),
        ],
        compiler_params=_params(2),
        name="ssd_prompt",
    )(z, xbc, dt, dtp, *consts)


def _ssd_sample(z, xbc, dt, dtp, conv_prev, ssm_prev, consts):
    first = N_PROMPT // CHUNK
    row = lambda b: (first + b, 0)
    return pl.pallas_call(
        _ssd_sample_kernel,
        grid=(DEC_BATCH,),
        in_specs=[
            pl.BlockSpec((CHUNK, D_SSM), row),
            pl.BlockSpec((CHUNK, D_CONV), row),
            pl.BlockSpec((CHUNK, SSM_HEADS), row),
            pl.BlockSpec((1, HEAD_PAIRS, LANES), lambda b: (first + b, 0, 0)),
            pl.BlockSpec((1, CONV_W - 1, D_CONV), lambda b: (b, 0, 0)),
            pl.BlockSpec((1, D_SSM, D_STATE), lambda b: (b, 0, 0)),
        ] + _ssd_const_specs(),
        out_specs=[
            pl.BlockSpec((CHUNK, D_SSM), lambda b: (b, 0)),
            pl.BlockSpec((1, D_SSM, D_STATE), lambda b: (b, 0, 0)),
        ],
        out_shape=[
            jax.ShapeDtypeStruct((N_SAMPLE, D_SSM), BF16),
            jax.ShapeDtypeStruct((DEC_BATCH, D_SSM, D_STATE), F32),
        ],
        scratch_shapes=[
            pltpu.VMEM((CONV_HEAD + CHUNK, D_CONV), F32),
            pltpu.VMEM((D_STATE, D_SSM), F32),
        ],
        compiler_params=_params(),
        name="ssd_sample",
    )(z, xbc, dt, dtp, conv_prev, ssm_prev, *consts)


def _attn_chunks(n_chunks, first_chunk, q_ref, kpad_ref, vpad_ref, bias_ref, o_ref):
    lane = lax.broadcasted_iota(jnp.int32, (CHUNK, LANES), 1)
    low = lane < ATT_HEAD_DIM
    kj = lax.broadcasted_iota(jnp.int32, (2 * CHUNK, BAND), 1)

    def chunk(c, carry, masked):
        r0 = pl.multiple_of(c * CHUNK, CHUNK)
        if masked:
            valid = kj >= (LEFT_CHUNKS - (first_chunk + c)) * CHUNK
        scores = []
        for j in range(ATT_PAIRS):
            qp = q_ref[pl.ds(r0, CHUNK), j * LANES:(j + 1) * LANES]
            zero = jnp.zeros_like(qp)
            q2 = jnp.concatenate([jnp.where(low, qp, zero), jnp.where(low, zero, qp)], axis=0)
            kb = kpad_ref[pl.ds(r0, BAND), j * LANES:(j + 1) * LANES]
            s = _dot_nt(q2, kb) + bias_ref[j]
            scores.append(jnp.where(valid, s, -jnp.inf) if masked else s)
        probs = []
        for s in scores:
            e = jnp.exp(s - jnp.max(s, axis=-1, keepdims=True))
            probs.append((e.astype(BF16), jnp.sum(e, axis=-1, keepdims=True)))
        outs = []
        for j, (e, denom) in enumerate(probs):
            vb = vpad_ref[pl.ds(r0, BAND), j * LANES:(j + 1) * LANES]
            r = _dot(e, vb) / denom
            outs.append(jnp.where(low, r[0:CHUNK], r[CHUNK:2 * CHUNK]))
        o_ref[pl.ds(r0, CHUNK), :] = jnp.concatenate(outs, axis=1).astype(BF16)
        return carry

    n_masked = min(max(LEFT_CHUNKS - first_chunk, 0), n_chunks)
    for lo, hi, masked in ((0, n_masked, True), (n_masked, n_chunks, False)):
        if hi > lo:
            lax.fori_loop(lo, hi, lambda c, carry, masked=masked: chunk(c, carry, masked), 0,
                          unroll=2 if (hi - lo) % 2 == 0 else 1)


def _attn_prompt_kernel(q_ref, k_ref, v_ref, bias_ref, o_ref, kpad_ref, vpad_ref):
    kpad_ref[0:ATT_LEFT, :] = jnp.zeros((ATT_LEFT, D_ATT), BF16)
    vpad_ref[0:ATT_LEFT, :] = jnp.zeros((ATT_LEFT, D_ATT), BF16)
    kpad_ref[ATT_LEFT:ATT_LEFT + SEQ, :] = k_ref[...]
    vpad_ref[ATT_LEFT:ATT_LEFT + SEQ, :] = v_ref[...]
    _attn_chunks(N_CHUNKS_SEQ, 0, q_ref, kpad_ref, vpad_ref, bias_ref, o_ref)


def _attn_sample_kernel(q_ref, k_ref, v_ref, ck_ref, cv_ref, bias_ref, o_ref, kpad_ref, vpad_ref):
    kpad_ref[0:ATT_LEFT, :] = ck_ref[0].astype(BF16)
    vpad_ref[0:ATT_LEFT, :] = cv_ref[0].astype(BF16)
    kpad_ref[ATT_LEFT:BAND, :] = k_ref[...]
    vpad_ref[ATT_LEFT:BAND, :] = v_ref[...]
    _attn_chunks(1, LEFT_CHUNKS, q_ref, kpad_ref, vpad_ref, bias_ref, o_ref)


def _attn_prompt(q, k, v, bias2):
    seq = pl.BlockSpec((SEQ, D_ATT), lambda b: (b, 0))
    return pl.pallas_call(
        _attn_prompt_kernel,
        grid=(BATCH,),
        in_specs=[seq, seq, seq, _const_spec((ATT_PAIRS, 2 * CHUNK, BAND))],
        out_specs=seq,
        out_shape=jax.ShapeDtypeStruct((N_PROMPT, D_ATT), BF16),
        scratch_shapes=[pltpu.VMEM((ATT_LEFT + SEQ, D_ATT), BF16),
                        pltpu.VMEM((ATT_LEFT + SEQ, D_ATT), BF16)],
        compiler_params=_params(),
        name="attn_prompt",
    )(q, k, v, bias2)


def _attn_sample(q, k, v, cache_k, cache_v, bias2):
    first = N_PROMPT // CHUNK
    row = pl.BlockSpec((CHUNK, D_ATT), lambda b: (first + b, 0))
    cache = pl.BlockSpec((1, ATT_LEFT, D_ATT), lambda b: (b, 0, 0))
    return pl.pallas_call(
        _attn_sample_kernel,
        grid=(DEC_BATCH,),
        in_specs=[row, row, row, cache, cache, _const_spec((ATT_PAIRS, 2 * CHUNK, BAND))],
        out_specs=pl.BlockSpec((CHUNK, D_ATT), lambda b: (b, 0)),
        out_shape=jax.ShapeDtypeStruct((N_SAMPLE, D_ATT), BF16),
        scratch_shapes=[pltpu.VMEM((BAND, D_ATT), BF16), pltpu.VMEM((BAND, D_ATT), BF16)],
        compiler_params=_params(),
        name="attn_sample",
    )(q, k, v, cache_k, cache_v, bias2)


def _outproj_kernel(xp_ref, xs_ref, yp_ref, ys_ref, op_ref, os_ref, ag_ref, wos_ref, woa_ref, fg_ref,
                    wr_ref, br_ref, earlier_ref,
                    xmid_ref, h_ref, idx_ref, gate_ref, rank_ref, cnt_ref, carry_ref):
    i = pl.program_id(0)

    @pl.when(i == 0)
    def _():
        carry_ref[...] = jnp.zeros_like(carry_ref)

    is_sample = i == N_PROMPT_TILES
    x = jnp.where(is_sample, xs_ref[...], xp_ref[...])
    y = jnp.where(is_sample, ys_ref[...], yp_ref[...])
    o = jnp.where(is_sample, os_ref[...], op_ref[...])
    o = _rms(o.astype(F32), ag_ref[...]).astype(BF16)
    xm = x + _dot(y, wos_ref[...]) + _dot(o, woa_ref[...])
    xmid_ref[...] = xm
    h = _rms(xm, fg_ref[...])
    _rows_to_tiles(h_ref, h)
    h1 = h.astype(BF16)
    h2 = (h - h1.astype(F32)).astype(BF16)
    both = _dot_nt(wr_ref[...], h1)
    logits = both[:N_EXPERTS] + (both[N_EXPERTS:] + _dot_nt(wr_ref[0:N_EXPERTS, :], h2)) + br_ref[...]
    eidx = lax.broadcasted_iota(jnp.int32, (N_EXPERTS, TM), 0)
    slot = lax.broadcasted_iota(jnp.int32, (SUBLANES, TM), 0)
    work = logits
    vals, sels = [], []
    idx_out = jnp.zeros((SUBLANES, TM), jnp.int32)
    for k in range(TOP_K):
        m = jnp.max(work, axis=0, keepdims=True)
        idx = jnp.min(jnp.where(work == m, eidx, N_EXPERTS), axis=0, keepdims=True)
        sel = eidx == idx
        vals.append(m)
        sels.append(sel)
        idx_out = jnp.where(slot == k, idx, idx_out)
        work = jnp.where(sel, -jnp.inf, work)
    es = [jnp.exp(v - vals[0]) for v in vals]
    tot = es[0] + es[1] + es[2] + es[3]
    gate_out = jnp.zeros((SUBLANES, TM), F32)
    for k in range(TOP_K):
        gate_out = jnp.where(slot == k, es[k] / tot, gate_out)
    idx_ref[...] = idx_out
    gate_ref[...] = gate_out
    multi = jnp.zeros((N_EXPERTS, TM), F32)
    for sel in sels:
        multi = jnp.where(sel, 1.0, multi)
    before = _dot(multi.astype(BF16), earlier_ref[...]) + carry_ref[...]
    rank_out = jnp.zeros((SUBLANES, TM), jnp.int32)
    for k in range(TOP_K):
        rk = jnp.sum(jnp.where(sels[k], before, 0.0), axis=0, keepdims=True).astype(jnp.int32)
        rank_out = jnp.where(slot == k, rk, rank_out)
    rank_ref[...] = rank_out
    carry_ref[...] = carry_ref[...] + jnp.sum(multi, axis=1, keepdims=True)
    cnt_ref[...] = carry_ref[...]


def _outproj(xp, xs, yp, ys, op, os_, ag, wos, woa, fg, wr, br, earlier):
    tok = lambda n: pl.BlockSpec((TM, n), lambda i: (i, 0))
    slots = pl.BlockSpec((SUBLANES, TM), lambda i: (0, i))
    prompt = lambda n: pl.BlockSpec((TM, n), lambda i: (jnp.minimum(i, N_PROMPT_TILES - 1), 0))
    sample = lambda n: pl.BlockSpec((TM, n), lambda i: (0, 0))
    return pl.pallas_call(
        _outproj_kernel,
        grid=(N_TILES,),
        in_specs=[
            prompt(D_MODEL), sample(D_MODEL), prompt(D_SSM), sample(D_SSM), prompt(D_ATT), sample(D_ATT),
            _const_spec((1, D_ATT)),
            _const_spec((D_SSM, D_MODEL)), _const_spec((D_ATT, D_MODEL)),
            _const_spec((1, D_MODEL)),
            _const_spec((2 * N_EXPERTS, D_MODEL)), _const_spec((N_EXPERTS, 1)),
            _const_spec((TM, TM)),
        ],
        out_specs=[tok(D_MODEL), pl.BlockSpec(_tiled(TM), lambda i: (i, 0)), slots, slots, slots,
                   _const_spec((N_EXPERTS, 1))],
        out_shape=[
            jax.ShapeDtypeStruct((N_TOK, D_MODEL), F32),
            jax.ShapeDtypeStruct(_tiled(N_TOK), U32),
            jax.ShapeDtypeStruct((SUBLANES, N_TOK), jnp.int32),
            jax.ShapeDtypeStruct((SUBLANES, N_TOK), F32),
            jax.ShapeDtypeStruct((SUBLANES, N_TOK), jnp.int32),
            jax.ShapeDtypeStruct((N_EXPERTS, 1), F32),
        ],
        scratch_shapes=[pltpu.VMEM((N_EXPERTS, 1), F32)],
        compiler_params=_params(),
        name="outproj_router",
    )(xp, xs, yp, ys, op, os_, ag, wos, woa, fg, wr, br, earlier)


def _scatter_kernel(dest_ref, pend_ref, h_ref, rows_ref, zero_ref, sem, zsem):
    i = pl.program_id(0)

    @pl.when(i == 0)
    def _():
        zero_ref[...] = jnp.zeros_like(zero_ref)

        block_tiles = _tiled(MOE_BM)[0]

        def zero_block(b):
            start = pl.multiple_of(b * block_tiles, block_tiles)
            return pltpu.make_async_copy(zero_ref, rows_ref.at[pl.ds(start, block_tiles)], zsem)

        def last_block(e):
            end = pend_ref[e]
            nonempty = end > (pend_ref[e - 1] if e > 0 else 0)
            return nonempty, zero_block(jnp.maximum(end // MOE_BM - 1, 0))

        for e in range(N_EXPERTS):
            nonempty, cp = last_block(e)
            pl.when(nonempty)(cp.start)
        for e in range(N_EXPERTS):
            nonempty, cp = last_block(e)
            pl.when(nonempty)(cp.wait)

        first_unused = pend_ref[N_EXPERTS - 1] // MOE_BM
        lax.fori_loop(first_unused, MOE_BLOCKS, lambda b, c: (zero_block(b).start(), c)[1], 0)
        lax.fori_loop(first_unused, MOE_BLOCKS, lambda b, c: (zero_block(b).wait(), c)[1], 0)

    def issue(r, carry):
        for k in range(TOP_K):
            d = dest_ref[r * TOP_K + k]
            pltpu.make_async_copy(_tile_of(h_ref, r), _tile_of(rows_ref, d), sem).start(priority=k % 2)
        return carry

    lax.fori_loop(0, TM, issue, 0, unroll=ISSUE_UNROLL)
    for _ in range(TOP_K):
        pltpu.make_async_copy(h_ref, rows_ref.at[pl.ds(0, _tiled(TM)[0])], sem).wait()


def _scatter_rows(dest_flat, pad_end, h):
    return pl.pallas_call(
        _scatter_kernel,
        grid=(N_TILES,),
        in_specs=[
            pl.BlockSpec((TM * TOP_K,), lambda i: (i,), memory_space=pltpu.SMEM),
            pl.BlockSpec((N_EXPERTS,), lambda i: (0,), memory_space=pltpu.SMEM),
            pl.BlockSpec(_tiled(TM), lambda i: (i, 0)),
        ],
        out_specs=pl.BlockSpec(memory_space=pl.ANY),
        out_shape=jax.ShapeDtypeStruct(_tiled(MOE_ROWS), U32),
        scratch_shapes=[pltpu.VMEM(_tiled(MOE_BM), U32), pltpu.SemaphoreType.DMA(()),
                        pltpu.SemaphoreType.DMA(())],
        compiler_params=_params(),
        name="moe_scatter",
    )(dest_flat, pad_end, h)


def _expert_kernel(be_ref, nu_ref, nxt_ref, x_ref, wgu_hbm, bgu_ref, wd_hbm, bd_ref, y_ref,
                   wgu_f, wd_f, wgu_s, wd_s, sem):
    i = pl.program_id(0)
    active = i < nu_ref[0]
    e = be_ref[i]

    def fetch(expert):
        return (pltpu.make_async_copy(wgu_hbm.at[expert], wgu_f, sem.at[0]),
                pltpu.make_async_copy(wd_hbm.at[expert], wd_f, sem.at[1]))

    @pl.when(active & (i == 0))
    def _():
        for cp in fetch(e):
            cp.start()

    @pl.when(active & ((i == 0) | (e != be_ref[jnp.maximum(i - 1, 0)])))
    def _():
        for cp in fetch(e):
            cp.wait()
        wgu_s[...] = wgu_f[...].astype(BF16)
        wd_s[...] = wd_f[...].astype(BF16)
        nxt = nxt_ref[e]

        @pl.when(nxt >= 0)
        def _():
            for cp in fetch(nxt):
                cp.start()

    @pl.when(active)
    def _():
        gu = _dot(_tiles_to_rows(x_ref, MOE_BM).astype(BF16), wgu_s[...]) + bgu_ref[0]
        gate = jnp.minimum(gu[:, :D_FF], SWIGLU_LIMIT)
        up = jnp.clip(gu[:, D_FF:], -SWIGLU_LIMIT, SWIGLU_LIMIT)
        act = (up + 1.0) * gate * _sigmoid(gate * SWIGLU_ALPHA)
        _rows_to_tiles(y_ref, _dot(act.astype(BF16), wd_s[...]) + bd_ref[0])

    @pl.when(jnp.logical_not(active))
    def _():
        y_ref[...] = jnp.zeros_like(y_ref)


def _experts(block_expert, n_used, next_expert, rows, wgu, bgu, wd, bd):
    grid_spec = pltpu.PrefetchScalarGridSpec(
        num_scalar_prefetch=3,
        grid=(MOE_BLOCKS,),
        in_specs=[
            pl.BlockSpec(_tiled(MOE_BM), lambda i, be, nu, nx: (jnp.minimum(i, nu[0] - 1), 0)),
            pl.BlockSpec(memory_space=pl.ANY),
            pl.BlockSpec((1, 1, 2 * D_FF), lambda i, be, nu, nx: (be[i], 0, 0)),
            pl.BlockSpec(memory_space=pl.ANY),
            pl.BlockSpec((1, 1, D_MODEL), lambda i, be, nu, nx: (be[i], 0, 0)),
        ],
        out_specs=pl.BlockSpec(_tiled(MOE_BM), lambda i, be, nu, nx: (i, 0)),
        scratch_shapes=[pltpu.VMEM((D_MODEL, 2 * D_FF), F32), pltpu.VMEM((D_FF, D_MODEL), F32),
                        pltpu.VMEM((D_MODEL, 2 * D_FF), BF16), pltpu.VMEM((D_FF, D_MODEL), BF16),
                        pltpu.SemaphoreType.DMA((2,))],
    )
    return pl.pallas_call(
        _expert_kernel,
        grid_spec=grid_spec,
        out_shape=jax.ShapeDtypeStruct(_tiled(MOE_ROWS), U32),
        compiler_params=_params(),
        name="moe_experts",
    )(block_expert, n_used, next_expert, rows, wgu, bgu, wd, bd)


def _combine_kernel(dest_ref, dest_next_ref, gate_ref, xmid_ref, g_ref, rows_ref, yp_ref, ys_ref, buf_ref, sem):
    i = pl.program_id(0)
    slot = i % 2

    def issue(idx_ref, s):
        def body(r, carry):
            for k in range(TOP_K):
                d = idx_ref[r * TOP_K + k]
                pltpu.make_async_copy(_tile_of(rows_ref, d), _tile_of(buf_ref.at[s], k * TM + r),
                                      sem.at[s]).start(priority=k % 2)
            return carry

        lax.fori_loop(0, TM, body, 0, unroll=ISSUE_UNROLL)

    @pl.when(i == 0)
    def _():
        issue(dest_ref, 0)

    @pl.when(i + 1 < N_TILES)
    def _():
        issue(dest_next_ref, 1 - slot)

    buf = buf_ref.at[slot]
    slot_tiles = _tiled(TM)[0]
    for k in range(TOP_K):
        pltpu.make_async_copy(rows_ref.at[pl.ds(0, slot_tiles)], buf.at[pl.ds(k * slot_tiles, slot_tiles)],
                              sem.at[slot]).wait()
    acc = xmid_ref[...]
    for k in range(TOP_K):
        acc = acc + _tiles_to_rows(buf, TM, first=k * TM) * gate_ref[:, k:k + 1]
    y = _rms(acc, g_ref[...])

    @pl.when(i < N_PROMPT_TILES)
    def _():
        yp_ref[...] = y

    @pl.when(i == N_PROMPT_TILES)
    def _():
        ys_ref[...] = y


def _combine(dest_flat, gates, xmid, g, y_rows):
    return pl.pallas_call(
        _combine_kernel,
        grid=(N_TILES,),
        in_specs=[
            pl.BlockSpec((TM * TOP_K,), lambda i: (i,), memory_space=pltpu.SMEM),
            pl.BlockSpec((TM * TOP_K,), lambda i: (jnp.minimum(i + 1, N_TILES - 1),), memory_space=pltpu.SMEM),
            pl.BlockSpec((TM, TOP_K), lambda i: (i, 0)),
            pl.BlockSpec((TM, D_MODEL), lambda i: (i, 0)),
            _const_spec((1, D_MODEL)),
            pl.BlockSpec(memory_space=pl.ANY),
        ],
        out_specs=[
            pl.BlockSpec((TM, D_MODEL), lambda i: (jnp.minimum(i, N_PROMPT_TILES - 1), 0)),
            pl.BlockSpec((TM, D_MODEL), lambda i: (0, 0)),
        ],
        out_shape=[
            jax.ShapeDtypeStruct((N_PROMPT, D_MODEL), F32),
            jax.ShapeDtypeStruct((N_SAMPLE, D_MODEL), F32),
        ],
        scratch_shapes=[pltpu.VMEM((2,) + _tiled(TOP_K * TM), U32), pltpu.SemaphoreType.DMA((2,))],
        compiler_params=_params(),
        name="moe_combine",
    )(dest_flat, dest_flat, gates, xmid, g, y_rows)


def _band_bias(table):
    n_diag = BAND + CHUNK - 1
    idx = np.clip(ATT_LEFT + (CHUNK - 1) - np.arange(n_diag), -REL_CLIP, REL_CLIP) + REL_CLIP
    pick = (np.arange(2 * REL_CLIP + 1)[:, None] == idx[None, :]).astype(np.float32)
    diag = jnp.dot(table, jnp.asarray(pick), precision=lax.Precision.HIGHEST)
    return jnp.stack([diag[:, CHUNK - 1 - qi:CHUNK - 1 - qi + BAND] for qi in range(CHUNK)], axis=1)


def _pair_rows(v):
    return jnp.repeat(v.reshape(HEAD_PAIRS, 2), CHUNK, axis=1)


def _layer(l, xp, xs, cache_k, cache_v, state_conv, state_ssm,
           norm_mix_g, w_in, conv_w, conv_b, dt_bias, a_log, d_skip, ssm_norm_g,
           att_norm_g, rel_bias_table, w_out, norm_ffn_g, w_router, b_router,
           w_gate_up, b_gate_up, w_down, b_down, norm_final_g):
    wb = w_in[l].astype(BF16)
    c0 = D_SSM
    c1 = c0 + D_CONV
    c2 = c1 + SSM_HEADS
    c3 = c2 + D_ATT
    c4 = c3 + D_ATT
    z, xbc, dt, dtt, q, k, v, k_p, v_p, k_s, v_s, ctail = _inproj(
        xp, xs, norm_mix_g[l][None], wb[:, :c0], wb[:, c0:c1], wb[:, c1:c2], wb[:, c1:c2].T,
        wb[:, c2:c3], wb[:, c3:c4], wb[:, c4:])

    n_chunks = N_TOK // CHUNK
    dtp = dtt.reshape(HEAD_PAIRS, 2, n_chunks, CHUNK).transpose(2, 0, 1, 3).reshape(n_chunks, HEAD_PAIRS, LANES)
    hp = jnp.arange(D_SSM) // SSM_HEAD_DIM
    expand = (hp[None, :] == jnp.arange(SSM_HEADS)[:, None]).astype(BF16)
    lane = jnp.arange(LANES)
    triu2 = ((lane[:, None] // CHUNK == lane[None, :] // CHUNK) & (lane[:, None] <= lane[None, :])).astype(BF16)
    consts = (conv_w[l], conv_b[l][None], dt_bias[l][None], _pair_rows(dt_bias[l]),
              a_log[l][None], _pair_rows(a_log[l]),
              jnp.repeat(d_skip[l], SSM_HEAD_DIM)[None], ssm_norm_g[l][None],
              expand, triu2)
    y_ssm_p, ssm_p = _ssd_prompt(z, xbc, dt, dtp, consts)
    y_ssm_s, ssm_s = _ssd_sample(z, xbc, dt, dtp, state_conv[l],
                                 state_ssm[l].reshape(DEC_BATCH, D_SSM, D_STATE), consts)

    bias2 = _band_bias(rel_bias_table[l]).reshape(ATT_PAIRS, 2 * CHUNK, BAND)
    o_att_p = _attn_prompt(q, k, v, bias2)
    o_att_s = _attn_sample(q, k, v, cache_k[l].reshape(DEC_BATCH, ATT_LEFT, D_ATT),
                           cache_v[l].reshape(DEC_BATCH, ATT_LEFT, D_ATT), bias2)

    wo = w_out[l].astype(BF16)
    wr = w_router[l].T
    wr1 = wr.astype(BF16)
    wr2 = (wr - wr1.astype(F32)).astype(BF16)
    earlier = jnp.triu(jnp.ones((TM, TM), BF16), 1)
    xmid, h, top_idx, gates, rank, counts = _outproj(
        xp, xs, y_ssm_p, y_ssm_s, o_att_p, o_att_s, att_norm_g[l][None], wo[:D_SSM], wo[D_SSM:], norm_ffn_g[l][None],
        jnp.concatenate([wr1, wr2], axis=0), b_router[l][:, None], earlier)
    gates = gates[:TOP_K].T

    counts = counts[:, 0].astype(jnp.int32)
    padded = (counts + MOE_BM - 1) // MOE_BM * MOE_BM
    pad_end = jnp.cumsum(padded)
    pad_start = pad_end - padded
    experts = jnp.arange(N_EXPERTS, dtype=jnp.int32)
    start_of = jnp.sum(jnp.where(top_idx[:TOP_K, :, None] == experts, pad_start, 0), axis=-1)
    dest = (start_of + rank[:TOP_K]).T.reshape(-1).astype(jnp.int32)
    block_start = jnp.arange(MOE_BLOCKS, dtype=jnp.int32) * MOE_BM
    block_expert = jnp.minimum(jnp.sum((pad_end[None, :] <= block_start[:, None]).astype(jnp.int32), axis=1),
                               N_EXPERTS - 1).astype(jnp.int32)
    n_used = (pad_end[-1:] // MOE_BM).astype(jnp.int32)
    later_nonempty = (experts[None, :] > experts[:, None]) & (padded[None, :] > 0)
    next_expert = jnp.min(jnp.where(later_nonempty, experts[None, :], N_EXPERTS), axis=1)
    next_expert = jnp.where(next_expert < N_EXPERTS, next_expert, -1).astype(jnp.int32)

    rows = _scatter_rows(dest, pad_end.astype(jnp.int32), h)
    y_rows = _experts(block_expert, n_used, next_expert, rows, w_gate_up[l], b_gate_up[l][:, None, :],
                      w_down[l], b_down[l][:, None, :])
    y_p, y_s = _combine(dest, gates, xmid, norm_final_g[None], y_rows)

    keep = min(ATT_LEFT, SEQ)
    k_p = k_p[:, TM - keep:].reshape(BATCH, keep, ATT_HEADS, ATT_HEAD_DIM)
    v_p = v_p[:, TM - keep:].reshape(BATCH, keep, ATT_HEADS, ATT_HEAD_DIM)
    k_s = k_s.reshape(DEC_BATCH, DEC_SEQ, ATT_HEADS, ATT_HEAD_DIM)
    v_s = v_s.reshape(DEC_BATCH, DEC_SEQ, ATT_HEADS, ATT_HEAD_DIM)
    conv_p = ctail[:BATCH, -(CONV_W - 1):]
    conv_s = ctail[BATCH].reshape(DEC_BATCH, SUBLANES, D_CONV)[:, -(CONV_W - 1):]
    ssm_p = ssm_p.reshape(BATCH, SSM_HEADS, SSM_HEAD_DIM, D_STATE)
    ssm_s = ssm_s.reshape(DEC_BATCH, SSM_HEADS, SSM_HEAD_DIM, D_STATE)
    return (y_p.reshape(BATCH, SEQ, D_MODEL), y_s.reshape(DEC_BATCH, DEC_SEQ, D_MODEL),
            k_p, v_p, conv_p, ssm_p, k_s, v_s, conv_s, ssm_s)


def kernel(x_prompt, x_sample, cache_k, cache_v, state_conv, state_ssm, norm_mix_g, w_in, conv_w, conv_b,
           dt_bias, a_log, d_skip, ssm_norm_g, att_norm_g, rel_bias_table, w_out, norm_ffn_g, w_router,
           b_router, w_gate_up, b_gate_up, w_down, b_down, norm_final_g):
    assert w_in.shape[0] == 1, "single trunk layer"
    xp = x_prompt.reshape(N_PROMPT, D_MODEL)
    xs = x_sample.reshape(N_SAMPLE, D_MODEL)
    outs = _layer(0, xp, xs, cache_k, cache_v, state_conv, state_ssm,
                  norm_mix_g, w_in, conv_w, conv_b, dt_bias, a_log, d_skip, ssm_norm_g,
                  att_norm_g, rel_bias_table, w_out, norm_ffn_g, w_router, b_router,
                  w_gate_up, b_gate_up, w_down, b_down, norm_final_g)
    y_p, y_s, k_p, v_p, conv_p, ssm_p, k_s, v_s, conv_s, ssm_s = outs
    return (y_p, y_s, k_p[None], v_p[None], conv_p[None], ssm_p[None],
            k_s[None], v_s[None], conv_s[None], ssm_s[None])
```

```python
import jax
import jax.numpy as jnp
import numpy as np
from jax import lax
from jax.experimental import pallas as pl
from jax.experimental.pallas import tpu as pltpu

D_MODEL = 1024
BATCH = 8
SEQ = 2048
DEC_BATCH = 8
DEC_SEQ = 64
CHUNK = 64
SSM_HEADS = 16
SSM_HEAD_DIM = 64
D_SSM = SSM_HEADS * SSM_HEAD_DIM
SSM_GROUPS = 2
D_STATE = 128
CONV_W = 4
D_BC = SSM_GROUPS * D_STATE
D_CONV = D_SSM + 2 * D_BC
ATT_HEADS = 8
ATT_HEAD_DIM = 64
D_ATT = ATT_HEADS * ATT_HEAD_DIM
LEFT_CHUNKS = 8
ATT_LEFT = LEFT_CHUNKS * CHUNK
BAND = ATT_LEFT + CHUNK
REL_CLIP = 128
ATT_SCALE = ATT_HEAD_DIM ** -0.5
N_EXPERTS = 32
TOP_K = 4
D_FF = D_MODEL
SWIGLU_ALPHA = 1.702
SWIGLU_LIMIT = 7.0
EPS = 1e-5

F32 = jnp.float32
BF16 = jnp.bfloat16
U32 = jnp.uint32

LANES = 128
SUBLANES = 8

N_PROMPT = BATCH * SEQ
N_SAMPLE = DEC_BATCH * DEC_SEQ
N_TOK = N_PROMPT + N_SAMPLE
TM = 512
N_PROMPT_TILES = N_PROMPT // TM
N_TILES = N_TOK // TM
TILES_PER_SEQ = SEQ // TM
CHUNKS_PER_TILE = TM // CHUNK
N_CHUNKS_SEQ = SEQ // CHUNK
assert 2 * SSM_HEAD_DIM == LANES and 2 * ATT_HEAD_DIM == LANES
HEAD_PAIRS = SSM_HEADS // 2
PAIRS_PER_GROUP = HEAD_PAIRS // SSM_GROUPS
ATT_PAIRS = ATT_HEADS // 2
SSD_BLOCK = 2
CONV_HEAD = SUBLANES
MOE_BM = 512
N_ASSIGN = N_TOK * TOP_K
MOE_BLOCKS = N_ASSIGN // MOE_BM + N_EXPERTS
MOE_ROWS = MOE_BLOCKS * MOE_BM
ISSUE_UNROLL = 8
ROW_TILE = (SUBLANES // 2, LANES)
assert 2 * ROW_TILE[0] * ROW_TILE[1] == D_MODEL
VMEM_BYTES = 64 * 1024 * 1024
VMEM_LIMIT = VMEM_BYTES * 7 // 8


def _dot(a, b):
    return jnp.dot(a, b, preferred_element_type=F32)


def _dot_nt(a, b):
    return lax.dot_general(a, b, (((1,), (1,)), ((), ())), preferred_element_type=F32)


def _dot_tn(a, b):
    return lax.dot_general(a, b, (((0,), (0,)), ((), ())), preferred_element_type=F32)


def _split3(x):
    x1 = x.astype(BF16)
    r1 = x - x1.astype(F32)
    x2 = r1.astype(BF16)
    r2 = r1 - x2.astype(F32)
    return x1, x2, r2.astype(BF16)


def _dot_exact_rhs(x, m):
    x1, x2, x3 = _split3(x)
    return _dot(x1, m) + _dot(x2, m) + _dot(x3, m)


def _dot_exact_lhs(m, x):
    x1, x2, x3 = _split3(x)
    return _dot(m, x1) + _dot(m, x2) + _dot(m, x3)


def _rms(x, g):
    return x * lax.rsqrt(jnp.mean(x * x, axis=-1, keepdims=True) + EPS) * g


def _sigmoid(x):
    return 1.0 / (1.0 + jnp.exp(-x))


def _softplus(x):
    return jnp.maximum(x, 0.0) + jnp.log(1.0 + jnp.exp(-jnp.abs(x)))


def _tiled(n):
    return (n * ROW_TILE[0], ROW_TILE[1])


def _tile_of(ref, row):
    return ref.at[pl.ds(pl.multiple_of(row * ROW_TILE[0], ROW_TILE[0]), ROW_TILE[0])]


def _rows_to_tiles(ref, x, first=0):
    sub, lanes = ROW_TILE
    half = D_MODEL // 2
    hi = lax.bitcast_convert_type(x[:, :half].astype(BF16).astype(F32), U32)
    lo = lax.bitcast_convert_type(x[:, half:].astype(BF16).astype(F32), U32)
    words = hi | (lo >> 16)
    for j in range(sub):
        ref[pl.ds(first * sub + j, x.shape[0], stride=sub), :] = words[:, j * lanes:(j + 1) * lanes]


def _tiles_to_rows(ref, n, first=0):
    sub = ROW_TILE[0]
    words = jnp.concatenate([ref[pl.ds(first * sub + j, n, stride=sub), :] for j in range(sub)], axis=1)
    hi = lax.bitcast_convert_type(words & jnp.uint32(0xFFFF0000), F32)
    lo = lax.bitcast_convert_type(words << 16, F32)
    return jnp.concatenate([hi, lo], axis=1)


def _const_spec(shape):
    nd = len(shape)
    return pl.BlockSpec(shape, lambda *_: (0,) * nd)


def _params(n_axes=1):
    return pltpu.CompilerParams(dimension_semantics=("arbitrary",) * n_axes,
                                vmem_limit_bytes=VMEM_LIMIT)


def _inproj_kernel(xp_ref, xs_ref, g_ref, wz_ref, wxbc_ref, wdt_ref, wdtt_ref, wq_ref, wk_ref, wv_ref,
                   z_ref, xbc_ref, dt_ref, dtt_ref, q_ref, k_ref, v_ref, kp_ref, vp_ref, ks_ref, vs_ref, ctail_ref):
    i = pl.program_id(0)
    x = jnp.where(i == N_PROMPT_TILES, xs_ref[...], xp_ref[...])
    h = _rms(x, g_ref[...]).astype(BF16)
    z_ref[...] = _dot(h, wz_ref[...]).astype(BF16)
    xbc = _dot(h, wxbc_ref[...])
    xbc_ref[...] = xbc.astype(BF16)
    for c in range(CHUNKS_PER_TILE):
        ctail_ref[0, c * SUBLANES:(c + 1) * SUBLANES, :] = xbc[(c + 1) * CHUNK - SUBLANES:(c + 1) * CHUNK, :]
    dt_ref[...] = _dot(h, wdt_ref[...])
    dtt_ref[...] = _dot_nt(wdtt_ref[...], h)
    q_ref[...] = (_dot(h, wq_ref[...]) * ATT_SCALE).astype(BF16)
    k = _dot(h, wk_ref[...])
    v = _dot(h, wv_ref[...])
    k_ref[...] = k.astype(BF16)
    v_ref[...] = v.astype(BF16)

    @pl.when(i < N_PROMPT_TILES)
    def _():
        kp_ref[0] = k
        vp_ref[0] = v

    @pl.when(i == N_PROMPT_TILES)
    def _():
        ks_ref[...] = k
        vs_ref[...] = v


def _inproj(xp, xs, g, wz, wxbc, wdt, wdtt, wq, wk, wv):
    tok = lambda n: pl.BlockSpec((TM, n), lambda i: (i, 0))
    tail_idx = lambda i: (i // TILES_PER_SEQ, 0, 0)
    seq_idx = lambda i: (jnp.minimum(i // TILES_PER_SEQ, BATCH - 1), 0, 0)
    n_tail = BATCH + 1
    return pl.pallas_call(
        _inproj_kernel,
        grid=(N_TILES,),
        in_specs=[
            pl.BlockSpec((TM, D_MODEL), lambda i: (jnp.minimum(i, N_PROMPT_TILES - 1), 0)),
            pl.BlockSpec((TM, D_MODEL), lambda i: (0, 0)),
            _const_spec((1, D_MODEL)),
            _const_spec((D_MODEL, D_SSM)),
            _const_spec((D_MODEL, D_CONV)),
            _const_spec((D_MODEL, SSM_HEADS)),
            _const_spec((SSM_HEADS, D_MODEL)),
            _const_spec((D_MODEL, D_ATT)),
            _const_spec((D_MODEL, D_ATT)),
            _const_spec((D_MODEL, D_ATT)),
        ],
        out_specs=[
            tok(D_SSM), tok(D_CONV), tok(SSM_HEADS),
            pl.BlockSpec((SSM_HEADS, TM), lambda i: (0, i)),
            tok(D_ATT), tok(D_ATT), tok(D_ATT),
            pl.BlockSpec((1, TM, D_ATT), seq_idx),
            pl.BlockSpec((1, TM, D_ATT), seq_idx),
            pl.BlockSpec((TM, D_ATT), lambda i: (0, 0)),
            pl.BlockSpec((TM, D_ATT), lambda i: (0, 0)),
            pl.BlockSpec((1, CHUNKS_PER_TILE * SUBLANES, D_CONV), tail_idx),
        ],
        out_shape=[
            jax.ShapeDtypeStruct((N_TOK, D_SSM), BF16),
            jax.ShapeDtypeStruct((N_TOK, D_CONV), BF16),
            jax.ShapeDtypeStruct((N_TOK, SSM_HEADS), F32),
            jax.ShapeDtypeStruct((SSM_HEADS, N_TOK), F32),
            jax.ShapeDtypeStruct((N_TOK, D_ATT), BF16),
            jax.ShapeDtypeStruct((N_TOK, D_ATT), BF16),
            jax.ShapeDtypeStruct((N_TOK, D_ATT), BF16),
            jax.ShapeDtypeStruct((BATCH, TM, D_ATT), F32),
            jax.ShapeDtypeStruct((BATCH, TM, D_ATT), F32),
            jax.ShapeDtypeStruct((N_SAMPLE, D_ATT), F32),
            jax.ShapeDtypeStruct((N_SAMPLE, D_ATT), F32),
            jax.ShapeDtypeStruct((n_tail, CHUNKS_PER_TILE * SUBLANES, D_CONV), F32),
        ],
        compiler_params=_params(),
        name="inproj",
    )(xp, xs, g, wz, wxbc, wdt, wdtt, wq, wk, wv)


def _ssd_tile(n_chunks, z_ref, xbc_ref, dt_ref, dtp_ref, cw_ref, cb_ref, dtb_ref, dtbp_ref,
              alog_h_ref, alog_p_ref, dskip_e_ref, ng_ref, expand_ref, triu2_ref,
              y_ref, xw_ref, state_ref):
    nb = SSD_BLOCK if n_chunks % SSD_BLOCK == 0 else 1
    rb = nb * CHUNK
    half = D_SSM // SSM_GROUPS
    xw_ref[CONV_HEAD:CONV_HEAD + n_chunks * CHUNK, :] = xbc_ref[...].astype(F32)

    a_h = -jnp.exp(alog_h_ref[...])
    a_p = jnp.concatenate([-jnp.exp(alog_p_ref[...])] * nb, axis=0)
    dtb_p = jnp.concatenate([dtbp_ref[...]] * nb, axis=0)
    tr_r = lax.broadcasted_iota(jnp.int32, (rb, rb), 0)
    tr_c = lax.broadcasted_iota(jnp.int32, (rb, rb), 1)
    tril_b = jnp.where((tr_r // CHUNK == tr_c // CHUNK) & (tr_c <= tr_r), 1.0, 0.0).astype(BF16)
    row_i = lax.broadcasted_iota(jnp.int32, (CHUNK, LANES), 0)
    col_i = lax.broadcasted_iota(jnp.int32, (CHUNK, LANES), 1)
    causal2 = row_i >= (col_i % CHUNK)
    sel_r = lax.broadcasted_iota(jnp.int32, ((CONV_W - 1) * rb, CONV_HEAD + rb), 0)
    sel_c = lax.broadcasted_iota(jnp.int32, ((CONV_W - 1) * rb, CONV_HEAD + rb), 1)
    shift_sel = (sel_c == sel_r % rb + sel_r // rb + CONV_HEAD - (CONV_W - 1)).astype(F32)
    bd_r = lax.broadcasted_iota(jnp.int32, (LANES, LANES), 0) // CHUNK
    bd_c = lax.broadcasted_iota(jnp.int32, (LANES, LANES), 1) // CHUNK
    blockdiag = bd_r == bd_c

    def block(i, carry):
        r0 = pl.multiple_of(i * rb, rb)
        win = xw_ref[pl.ds(r0, CONV_HEAD + rb), :]
        shifted = _dot(shift_sel, win)
        acc = cb_ref[...] + cw_ref[CONV_W - 1:CONV_W, :] * win[CONV_HEAD:, :]
        for tap in range(CONV_W - 1):
            acc = acc + cw_ref[tap:tap + 1, :] * shifted[tap * rb:(tap + 1) * rb, :]
        xa = acc * _sigmoid(acc)
        xs = xa[:, 0:D_SSM]
        bm = xa[:, D_SSM:D_SSM + D_BC].astype(BF16)
        cm = xa[:, D_SSM + D_BC:D_CONV].astype(BF16)
        dt = _softplus(dt_ref[pl.ds(r0, rb), :] + dtb_ref[...])
        dt_e = _dot_exact_rhs(dt, expand_ref[...])
        acum = _dot_exact_rhs(_dot_exact_lhs(tril_b, dt * a_h), expand_ref[...])
        dtp = _softplus(dtp_ref[pl.ds(i * nb, nb)].reshape(nb * HEAD_PAIRS, LANES) + dtb_p)
        acum_p = _dot_exact_rhs(dtp * a_p, triu2_ref[...])
        xdt = xs * dt_e
        a_last = [acum[(c + 1) * CHUNK - 1:(c + 1) * CHUNK, :] for c in range(nb)]
        a_end = jnp.concatenate([jnp.broadcast_to(a, (CHUNK, D_SSM)) for a in a_last], axis=0)
        xdt_end = (xdt * jnp.exp(a_end - acum)).astype(BF16)

        y_diag, new_s = [], []
        for c in range(nb):
            rs = slice(c * CHUNK, (c + 1) * CHUNK)
            y_parts = []
            for g in range(SSM_GROUPS):
                bg = bm[rs, g * D_STATE:(g + 1) * D_STATE]
                cg = cm[rs, g * D_STATE:(g + 1) * D_STATE]
                cb2 = _dot_nt(cg, jnp.concatenate([bg, bg], axis=0))
                for jj in range(PAIRS_PER_GROUP):
                    j = g * PAIRS_PER_GROUP + jj
                    seg = acum[rs, j * LANES:(j + 1) * LANES] - acum_p[c * HEAD_PAIRS + j:c * HEAD_PAIRS + j + 1, :]
                    decay = jnp.exp(jnp.where(causal2, seg, -jnp.inf))
                    x2 = xdt[rs, j * LANES:(j + 1) * LANES]
                    rhs = jnp.where(blockdiag, jnp.concatenate([x2, x2], axis=0), 0.0).astype(BF16)
                    y_parts.append(_dot((cb2 * decay).astype(BF16), rhs))
            y_diag.append(jnp.concatenate(y_parts, axis=1))
            new_s.append(jnp.concatenate(
                [_dot_tn(bm[rs, g * D_STATE:(g + 1) * D_STATE], xdt_end[rs, g * half:(g + 1) * half])
                 for g in range(SSM_GROUPS)], axis=1))

        y_off = []
        for c in range(nb):
            rs = slice(c * CHUNK, (c + 1) * CHUNK)
            state = state_ref[...]
            y_off.append(jnp.concatenate(
                [_dot(cm[rs, g * D_STATE:(g + 1) * D_STATE], state[:, g * half:(g + 1) * half].astype(BF16))
                 for g in range(SSM_GROUPS)], axis=1))
            state_ref[...] = state * jnp.exp(a_last[c]) + new_s[c]

        y = (jnp.concatenate(y_diag, axis=0) + jnp.concatenate(y_off, axis=0) * jnp.exp(acum)
             + dskip_e_ref[...] * xs)
        zc = z_ref[pl.ds(r0, rb), :].astype(F32)
        y = y * (zc * _sigmoid(zc))
        yn = jnp.concatenate(
            [y[:, g * half:(g + 1) * half]
             * lax.rsqrt(jnp.mean(jnp.square(y[:, g * half:(g + 1) * half]), axis=-1, keepdims=True) + EPS)
             for g in range(SSM_GROUPS)], axis=1)
        y_ref[pl.ds(r0, rb), :] = (yn * ng_ref[...]).astype(BF16)
        return carry

    lax.fori_loop(0, n_chunks // nb, block, 0)


def _state_store(state_ref, out_ref):
    for j in range(HEAD_PAIRS):
        out_ref[0, j * LANES:(j + 1) * LANES, :] = state_ref[:, j * LANES:(j + 1) * LANES].T


def _ssd_prompt_kernel(z_ref, xbc_ref, dt_ref, dtp_ref, cw_ref, cb_ref, dtb_ref, dtbp_ref,
                       alog_h_ref, alog_p_ref, dskip_e_ref, ng_ref, expand_ref, triu2_ref,
                       y_ref, ssm_ref, xw_ref, state_ref, tail_ref):
    t = pl.program_id(1)

    @pl.when(t == 0)
    def _():
        state_ref[...] = jnp.zeros_like(state_ref)
        xw_ref[0:CONV_HEAD, :] = jnp.zeros((CONV_HEAD, D_CONV), F32)

    @pl.when(t > 0)
    def _():
        xw_ref[0:CONV_HEAD, :] = tail_ref[...]

    _ssd_tile(CHUNKS_PER_TILE, z_ref, xbc_ref, dt_ref, dtp_ref, cw_ref, cb_ref, dtb_ref, dtbp_ref,
              alog_h_ref, alog_p_ref, dskip_e_ref, ng_ref, expand_ref, triu2_ref,
              y_ref, xw_ref, state_ref)
    tail_ref[...] = xw_ref[TM:TM + CONV_HEAD, :]

    @pl.when(t == TILES_PER_SEQ - 1)
    def _():
        _state_store(state_ref, ssm_ref)


def _ssd_sample_kernel(z_ref, xbc_ref, dt_ref, dtp_ref, cprev_ref, sprev_ref,
                       cw_ref, cb_ref, dtb_ref, dtbp_ref,
                       alog_h_ref, alog_p_ref, dskip_e_ref, ng_ref, expand_ref, triu2_ref,
                       y_ref, ssm_ref, xw_ref, state_ref):
    xw_ref[0:CONV_HEAD, :] = jnp.zeros((CONV_HEAD, D_CONV), F32)
    xw_ref[CONV_HEAD - (CONV_W - 1):CONV_HEAD, :] = cprev_ref[0]
    for j in range(HEAD_PAIRS):
        state_ref[:, j * LANES:(j + 1) * LANES] = sprev_ref[0, j * LANES:(j + 1) * LANES, :].T
    _ssd_tile(1, z_ref, xbc_ref, dt_ref, dtp_ref, cw_ref, cb_ref, dtb_ref, dtbp_ref,
              alog_h_ref, alog_p_ref, dskip_e_ref, ng_ref, expand_ref, triu2_ref,
              y_ref, xw_ref, state_ref)
    _state_store(state_ref, ssm_ref)


def _ssd_const_specs():
    return [
        _const_spec((CONV_W, D_CONV)), _const_spec((1, D_CONV)),
        _const_spec((1, SSM_HEADS)), _const_spec((HEAD_PAIRS, LANES)),
        _const_spec((1, SSM_HEADS)), _const_spec((HEAD_PAIRS, LANES)),
        _const_spec((1, D_SSM)), _const_spec((1, D_SSM)),
        _const_spec((SSM_HEADS, D_SSM)), _const_spec((LANES, LANES)),
    ]


def _ssd_prompt(z, xbc, dt, dtp, consts):
    tile = lambda b, t: (b * TILES_PER_SEQ + t, 0)
    return pl.pallas_call(
        _ssd_prompt_kernel,
        grid=(BATCH, TILES_PER_SEQ),
        in_specs=[
            pl.BlockSpec((TM, D_SSM), tile),
            pl.BlockSpec((TM, D_CONV), tile),
            pl.BlockSpec((TM, SSM_HEADS), tile),
            pl.BlockSpec((CHUNKS_PER_TILE, HEAD_PAIRS, LANES), lambda b, t: (b * TILES_PER_SEQ + t, 0, 0)),
        ] + _ssd_const_specs(),
        out_specs=[
            pl.BlockSpec((TM, D_SSM), tile),
            pl.BlockSpec((1, D_SSM, D_STATE), lambda b, t: (b, 0, 0)),
        ],
        out_shape=[
            jax.ShapeDtypeStruct((N_PROMPT, D_SSM), BF16),
            jax.ShapeDtypeStruct((BATCH, D_SSM, D_STATE), F32),
        ],
        scratch_shapes=[
            pltpu.VMEM((CONV_HEAD + TM, D_CONV), F32),
            pltpu.VMEM((D_STATE, D_SSM), F32),
            pltpu.VMEM((CONV_HEAD, D_CONV), F32),
        ],
        compiler_params=_params(2),
        name="ssd_prompt",
    )(z, xbc, dt, dtp, *consts)


def _ssd_sample(z, xbc, dt, dtp, conv_prev, ssm_prev, consts):
    first = N_PROMPT // CHUNK
    row = lambda b: (first + b, 0)
    return pl.pallas_call(
        _ssd_sample_kernel,
        grid=(DEC_BATCH,),
        in_specs=[
            pl.BlockSpec((CHUNK, D_SSM), row),
            pl.BlockSpec((CHUNK, D_CONV), row),
            pl.BlockSpec((CHUNK, SSM_HEADS), row),
            pl.BlockSpec((1, HEAD_PAIRS, LANES), lambda b: (first + b, 0, 0)),
            pl.BlockSpec((1, CONV_W - 1, D_CONV), lambda b: (b, 0, 0)),
            pl.BlockSpec((1, D_SSM, D_STATE), lambda b: (b, 0, 0)),
        ] + _ssd_const_specs(),
        out_specs=[
            pl.BlockSpec((CHUNK, D_SSM), lambda b: (b, 0)),
            pl.BlockSpec((1, D_SSM, D_STATE), lambda b: (b, 0, 0)),
        ],
        out_shape=[
            jax.ShapeDtypeStruct((N_SAMPLE, D_SSM), BF16),
            jax.ShapeDtypeStruct((DEC_BATCH, D_SSM, D_STATE), F32),
        ],
        scratch_shapes=[
            pltpu.VMEM((CONV_HEAD + CHUNK, D_CONV), F32),
            pltpu.VMEM((D_STATE, D_SSM), F32),
        ],
        compiler_params=_params(),
        name="ssd_sample",
    )(z, xbc, dt, dtp, conv_prev, ssm_prev, *consts)


def _attn_chunks(n_chunks, first_chunk, q_ref, kpad_ref, vpad_ref, bias_ref, o_ref):
    lane = lax.broadcasted_iota(jnp.int32, (CHUNK, LANES), 1)
    low = lane < ATT_HEAD_DIM
    kj = lax.broadcasted_iota(jnp.int32, (2 * CHUNK, BAND), 1)

    def chunk(c, carry, masked):
        r0 = pl.multiple_of(c * CHUNK, CHUNK)
        if masked:
            valid = kj >= (LEFT_CHUNKS - (first_chunk + c)) * CHUNK
        scores = []
        for j in range(ATT_PAIRS):
            qp = q_ref[pl.ds(r0, CHUNK), j * LANES:(j + 1) * LANES]
            zero = jnp.zeros_like(qp)
            q2 = jnp.concatenate([jnp.where(low, qp, zero), jnp.where(low, zero, qp)], axis=0)
            kb = kpad_ref[pl.ds(r0, BAND), j * LANES:(j + 1) * LANES]
            s = _dot_nt(q2, kb) + bias_ref[j]
            scores.append(jnp.where(valid, s, -jnp.inf) if masked else s)
        probs = []
        for s in scores:
            e = jnp.exp(s - jnp.max(s, axis=-1, keepdims=True))
            probs.append((e.astype(BF16), jnp.sum(e, axis=-1, keepdims=True)))
        outs = []
        for j, (e, denom) in enumerate(probs):
            vb = vpad_ref[pl.ds(r0, BAND), j * LANES:(j + 1) * LANES]
            r = _dot(e, vb) / denom
            outs.append(jnp.where(low, r[0:CHUNK], r[CHUNK:2 * CHUNK]))
        o_ref[pl.ds(r0, CHUNK), :] = jnp.concatenate(outs, axis=1).astype(BF16)
        return carry

    n_masked = min(max(LEFT_CHUNKS - first_chunk, 0), n_chunks)
    for lo, hi, masked in ((0, n_masked, True), (n_masked, n_chunks, False)):
        if hi > lo:
            lax.fori_loop(lo, hi, lambda c, carry, masked=masked: chunk(c, carry, masked), 0,
                          unroll=2 if (hi - lo) % 2 == 0 else 1)


def _attn_prompt_kernel(q_ref, k_ref, v_ref, bias_ref, o_ref, kpad_ref, vpad_ref):
    kpad_ref[0:ATT_LEFT, :] = jnp.zeros((ATT_LEFT, D_ATT), BF16)
    vpad_ref[0:ATT_LEFT, :] = jnp.zeros((ATT_LEFT, D_ATT), BF16)
    kpad_ref[ATT_LEFT:ATT_LEFT + SEQ, :] = k_ref[...]
    vpad_ref[ATT_LEFT:ATT_LEFT + SEQ, :] = v_ref[...]
    _attn_chunks(N_CHUNKS_SEQ, 0, q_ref, kpad_ref, vpad_ref, bias_ref, o_ref)


def _attn_sample_kernel(q_ref, k_ref, v_ref, ck_ref, cv_ref, bias_ref, o_ref, kpad_ref, vpad_ref):
    kpad_ref[0:ATT_LEFT, :] = ck_ref[0].astype(BF16)
    vpad_ref[0:ATT_LEFT, :] = cv_ref[0].astype(BF16)
    kpad_ref[ATT_LEFT:BAND, :] = k_ref[...]
    vpad_ref[ATT_LEFT:BAND, :] = v_ref[...]
    _attn_chunks(1, LEFT_CHUNKS, q_ref, kpad_ref, vpad_ref, bias_ref, o_ref)


def _attn_prompt(q, k, v, bias2):
    seq = pl.BlockSpec((SEQ, D_ATT), lambda b: (b, 0))
    return pl.pallas_call(
        _attn_prompt_kernel,
        grid=(BATCH,),
        in_specs=[seq, seq, seq, _const_spec((ATT_PAIRS, 2 * CHUNK, BAND))],
        out_specs=seq,
        out_shape=jax.ShapeDtypeStruct((N_PROMPT, D_ATT), BF16),
        scratch_shapes=[pltpu.VMEM((ATT_LEFT + SEQ, D_ATT), BF16),
                        pltpu.VMEM((ATT_LEFT + SEQ, D_ATT), BF16)],
        compiler_params=_params(),
        name="attn_prompt",
    )(q, k, v, bias2)


def _attn_sample(q, k, v, cache_k, cache_v, bias2):
    first = N_PROMPT // CHUNK
    row = pl.BlockSpec((CHUNK, D_ATT), lambda b: (first + b, 0))
    cache = pl.BlockSpec((1, ATT_LEFT, D_ATT), lambda b: (b, 0, 0))
    return pl.pallas_call(
        _attn_sample_kernel,
        grid=(DEC_BATCH,),
        in_specs=[row, row, row, cache, cache, _const_spec((ATT_PAIRS, 2 * CHUNK, BAND))],
        out_specs=pl.BlockSpec((CHUNK, D_ATT), lambda b: (b, 0)),
        out_shape=jax.ShapeDtypeStruct((N_SAMPLE, D_ATT), BF16),
        scratch_shapes=[pltpu.VMEM((BAND, D_ATT), BF16), pltpu.VMEM((BAND, D_ATT), BF16)],
        compiler_params=_params(),
        name="attn_sample",
    )(q, k, v, cache_k, cache_v, bias2)


def _outproj_kernel(xp_ref, xs_ref, yp_ref, ys_ref, op_ref, os_ref, ag_ref, wos_ref, woa_ref, fg_ref,
                    wr_ref, br_ref, earlier_ref,
                    xmid_ref, h_ref, idx_ref, gate_ref, rank_ref, cnt_ref, carry_ref):
    i = pl.program_id(0)

    @pl.when(i == 0)
    def _():
        carry_ref[...] = jnp.zeros_like(carry_ref)

    is_sample = i == N_PROMPT_TILES
    x = jnp.where(is_sample, xs_ref[...], xp_ref[...])
    y = jnp.where(is_sample, ys_ref[...], yp_ref[...])
    o = jnp.where(is_sample, os_ref[...], op_ref[...])
    o = _rms(o.astype(F32), ag_ref[...]).astype(BF16)
    xm = x + _dot(y, wos_ref[...]) + _dot(o, woa_ref[...])
    xmid_ref[...] = xm
    h = _rms(xm, fg_ref[...])
    _rows_to_tiles(h_ref, h)
    h1 = h.astype(BF16)
    h2 = (h - h1.astype(F32)).astype(BF16)
    both = _dot_nt(wr_ref[...], h1)
    logits = both[:N_EXPERTS] + (both[N_EXPERTS:] + _dot_nt(wr_ref[0:N_EXPERTS, :], h2)) + br_ref[...]
    eidx = lax.broadcasted_iota(jnp.int32, (N_EXPERTS, TM), 0)
    slot = lax.broadcasted_iota(jnp.int32, (SUBLANES, TM), 0)
    work = logits
    vals, sels = [], []
    idx_out = jnp.zeros((SUBLANES, TM), jnp.int32)
    for k in range(TOP_K):
        m = jnp.max(work, axis=0, keepdims=True)
        idx = jnp.min(jnp.where(work == m, eidx, N_EXPERTS), axis=0, keepdims=True)
        sel = eidx == idx
        vals.append(m)
        sels.append(sel)
        idx_out = jnp.where(slot == k, idx, idx_out)
        work = jnp.where(sel, -jnp.inf, work)
    es = [jnp.exp(v - vals[0]) for v in vals]
    tot = es[0] + es[1] + es[2] + es[3]
    gate_out = jnp.zeros((SUBLANES, TM), F32)
    for k in range(TOP_K):
        gate_out = jnp.where(slot == k, es[k] / tot, gate_out)
    idx_ref[...] = idx_out
    gate_ref[...] = gate_out
    multi = jnp.zeros((N_EXPERTS, TM), F32)
    for sel in sels:
        multi = jnp.where(sel, 1.0, multi)
    before = _dot(multi.astype(BF16), earlier_ref[...]) + carry_ref[...]
    rank_out = jnp.zeros((SUBLANES, TM), jnp.int32)
    for k in range(TOP_K):
        rk = jnp.sum(jnp.where(sels[k], before, 0.0), axis=0, keepdims=True).astype(jnp.int32)
        rank_out = jnp.where(slot == k, rk, rank_out)
    rank_ref[...] = rank_out
    carry_ref[...] = carry_ref[...] + jnp.sum(multi, axis=1, keepdims=True)
    cnt_ref[...] = carry_ref[...]


def _outproj(xp, xs, yp, ys, op, os_, ag, wos, woa, fg, wr, br, earlier):
    tok = lambda n: pl.BlockSpec((TM, n), lambda i: (i, 0))
    slots = pl.BlockSpec((SUBLANES, TM), lambda i: (0, i))
    prompt = lambda n: pl.BlockSpec((TM, n), lambda i: (jnp.minimum(i, N_PROMPT_TILES - 1), 0))
    sample = lambda n: pl.BlockSpec((TM, n), lambda i: (0, 0))
    return pl.pallas_call(
        _outproj_kernel,
        grid=(N_TILES,),
        in_specs=[
            prompt(D_MODEL), sample(D_MODEL), prompt(D_SSM), sample(D_SSM), prompt(D_ATT), sample(D_ATT),
            _const_spec((1, D_ATT)),
            _const_spec((D_SSM, D_MODEL)), _const_spec((D_ATT, D_MODEL)),
            _const_spec((1, D_MODEL)),
            _const_spec((2 * N_EXPERTS, D_MODEL)), _const_spec((N_EXPERTS, 1)),
            _const_spec((TM, TM)),
        ],
        out_specs=[tok(D_MODEL), pl.BlockSpec(_tiled(TM), lambda i: (i, 0)), slots, slots, slots,
                   _const_spec((N_EXPERTS, 1))],
        out_shape=[
            jax.ShapeDtypeStruct((N_TOK, D_MODEL), F32),
            jax.ShapeDtypeStruct(_tiled(N_TOK), U32),
            jax.ShapeDtypeStruct((SUBLANES, N_TOK), jnp.int32),
            jax.ShapeDtypeStruct((SUBLANES, N_TOK), F32),
            jax.ShapeDtypeStruct((SUBLANES, N_TOK), jnp.int32),
            jax.ShapeDtypeStruct((N_EXPERTS, 1), F32),
        ],
        scratch_shapes=[pltpu.VMEM((N_EXPERTS, 1), F32)],
        compiler_params=_params(),
        name="outproj_router",
    )(xp, xs, yp, ys, op, os_, ag, wos, woa, fg, wr, br, earlier)


def _scatter_kernel(dest_ref, pend_ref, h_ref, rows_ref, zero_ref, sem, zsem):
    i = pl.program_id(0)

    @pl.when(i == 0)
    def _():
        zero_ref[...] = jnp.zeros_like(zero_ref)

        block_tiles = _tiled(MOE_BM)[0]

        def zero_block(b):
            start = pl.multiple_of(b * block_tiles, block_tiles)
            return pltpu.make_async_copy(zero_ref, rows_ref.at[pl.ds(start, block_tiles)], zsem)

        def last_block(e):
            end = pend_ref[e]
            nonempty = end > (pend_ref[e - 1] if e > 0 else 0)
            return nonempty, zero_block(jnp.maximum(end // MOE_BM - 1, 0))

        for e in range(N_EXPERTS):
            nonempty, cp = last_block(e)
            pl.when(nonempty)(cp.start)
        for e in range(N_EXPERTS):
            nonempty, cp = last_block(e)
            pl.when(nonempty)(cp.wait)

        first_unused = pend_ref[N_EXPERTS - 1] // MOE_BM
        lax.fori_loop(first_unused, MOE_BLOCKS, lambda b, c: (zero_block(b).start(), c)[1], 0)
        lax.fori_loop(first_unused, MOE_BLOCKS, lambda b, c: (zero_block(b).wait(), c)[1], 0)

    def issue(r, carry):
        for k in range(TOP_K):
            d = dest_ref[k, r]
            pltpu.make_async_copy(_tile_of(h_ref, r), _tile_of(rows_ref, d), sem).start(priority=k % 2)
        return carry

    lax.fori_loop(0, TM, issue, 0, unroll=ISSUE_UNROLL)
    for _ in range(TOP_K):
        pltpu.make_async_copy(h_ref, rows_ref.at[pl.ds(0, _tiled(TM)[0])], sem).wait()


def _scatter_rows(dest, pad_end, h):
    return pl.pallas_call(
        _scatter_kernel,
        grid=(N_TILES,),
        in_specs=[
            pl.BlockSpec((SUBLANES, TM), lambda i: (0, i), memory_space=pltpu.SMEM),
            pl.BlockSpec((N_EXPERTS,), lambda i: (0,), memory_space=pltpu.SMEM),
            pl.BlockSpec(_tiled(TM), lambda i: (i, 0)),
        ],
        out_specs=pl.BlockSpec(memory_space=pl.ANY),
        out_shape=jax.ShapeDtypeStruct(_tiled(MOE_ROWS), U32),
        scratch_shapes=[pltpu.VMEM(_tiled(MOE_BM), U32), pltpu.SemaphoreType.DMA(()),
                        pltpu.SemaphoreType.DMA(())],
        compiler_params=_params(),
        name="moe_scatter",
    )(dest, pad_end, h)


def _expert_kernel(be_ref, nu_ref, nxt_ref, x_ref, wgu_hbm, bgu_ref, wd_hbm, bd_ref, y_ref,
                   wgu_f, wd_f, wgu_s, wd_s, sem):
    i = pl.program_id(0)
    active = i < nu_ref[0]
    e = be_ref[i]

    def fetch(expert):
        return (pltpu.make_async_copy(wgu_hbm.at[expert], wgu_f, sem.at[0]),
                pltpu.make_async_copy(wd_hbm.at[expert], wd_f, sem.at[1]))

    @pl.when(active & (i == 0))
    def _():
        for cp in fetch(e):
            cp.start()

    @pl.when(active & ((i == 0) | (e != be_ref[jnp.maximum(i - 1, 0)])))
    def _():
        for cp in fetch(e):
            cp.wait()
        wgu_s[...] = wgu_f[...].astype(BF16)
        wd_s[...] = wd_f[...].astype(BF16)
        nxt = nxt_ref[e]

        @pl.when(nxt >= 0)
        def _():
            for cp in fetch(nxt):
                cp.start()

    @pl.when(active)
    def _():
        gu = _dot(_tiles_to_rows(x_ref, MOE_BM).astype(BF16), wgu_s[...]) + bgu_ref[0]
        gate = jnp.minimum(gu[:, :D_FF], SWIGLU_LIMIT)
        up = jnp.clip(gu[:, D_FF:], -SWIGLU_LIMIT, SWIGLU_LIMIT)
        act = (up + 1.0) * gate * _sigmoid(gate * SWIGLU_ALPHA)
        _rows_to_tiles(y_ref, _dot(act.astype(BF16), wd_s[...]) + bd_ref[0])

    @pl.when(jnp.logical_not(active))
    def _():
        y_ref[...] = jnp.zeros_like(y_ref)


def _experts(block_expert, n_used, next_expert, rows, wgu, bgu, wd, bd):
    grid_spec = pltpu.PrefetchScalarGridSpec(
        num_scalar_prefetch=3,
        grid=(MOE_BLOCKS,),
        in_specs=[
            pl.BlockSpec(_tiled(MOE_BM), lambda i, be, nu, nx: (jnp.minimum(i, nu[0] - 1), 0)),
            pl.BlockSpec(memory_space=pl.ANY),
            pl.BlockSpec((1, 1, 2 * D_FF), lambda i, be, nu, nx: (be[i], 0, 0)),
            pl.BlockSpec(memory_space=pl.ANY),
            pl.BlockSpec((1, 1, D_MODEL), lambda i, be, nu, nx: (be[i], 0, 0)),
        ],
        out_specs=pl.BlockSpec(_tiled(MOE_BM), lambda i, be, nu, nx: (i, 0)),
        scratch_shapes=[pltpu.VMEM((D_MODEL, 2 * D_FF), F32), pltpu.VMEM((D_FF, D_MODEL), F32),
                        pltpu.VMEM((D_MODEL, 2 * D_FF), BF16), pltpu.VMEM((D_FF, D_MODEL), BF16),
                        pltpu.SemaphoreType.DMA((2,))],
    )
    return pl.pallas_call(
        _expert_kernel,
        grid_spec=grid_spec,
        out_shape=jax.ShapeDtypeStruct(_tiled(MOE_ROWS), U32),
        compiler_params=_params(),
        name="moe_experts",
    )(block_expert, n_used, next_expert, rows, wgu, bgu, wd, bd)


def _combine_kernel(dest_ref, dest_next_ref, gate_ref, xmid_ref, g_ref, rows_ref, yp_ref, ys_ref, buf_ref, sem):
    i = pl.program_id(0)
    slot = i % 2

    def issue(idx_ref, s):
        def body(r, carry):
            for k in range(TOP_K):
                d = idx_ref[k, r]
                pltpu.make_async_copy(_tile_of(rows_ref, d), _tile_of(buf_ref.at[s], k * TM + r),
                                      sem.at[s]).start(priority=k % 2)
            return carry

        lax.fori_loop(0, TM, body, 0, unroll=ISSUE_UNROLL)

    @pl.when(i == 0)
    def _():
        issue(dest_ref, 0)

    @pl.when(i + 1 < N_TILES)
    def _():
        issue(dest_next_ref, 1 - slot)

    buf = buf_ref.at[slot]
    slot_tiles = _tiled(TM)[0]
    for k in range(TOP_K):
        pltpu.make_async_copy(rows_ref.at[pl.ds(0, slot_tiles)], buf.at[pl.ds(k * slot_tiles, slot_tiles)],
                              sem.at[slot]).wait()
    gates = gate_ref[...].T
    acc = xmid_ref[...]
    for k in range(TOP_K):
        acc = acc + _tiles_to_rows(buf, TM, first=k * TM) * gates[:, k:k + 1]
    y = _rms(acc, g_ref[...])

    @pl.when(i < N_PROMPT_TILES)
    def _():
        yp_ref[...] = y

    @pl.when(i == N_PROMPT_TILES)
    def _():
        ys_ref[...] = y


def _combine(dest, gates, xmid, g, y_rows):
    return pl.pallas_call(
        _combine_kernel,
        grid=(N_TILES,),
        in_specs=[
            pl.BlockSpec((SUBLANES, TM), lambda i: (0, i), memory_space=pltpu.SMEM),
            pl.BlockSpec((SUBLANES, TM), lambda i: (0, jnp.minimum(i + 1, N_TILES - 1)), memory_space=pltpu.SMEM),
            pl.BlockSpec((SUBLANES, TM), lambda i: (0, i)),
            pl.BlockSpec((TM, D_MODEL), lambda i: (i, 0)),
            _const_spec((1, D_MODEL)),
            pl.BlockSpec(memory_space=pl.ANY),
        ],
        out_specs=[
            pl.BlockSpec((TM, D_MODEL), lambda i: (jnp.minimum(i, N_PROMPT_TILES - 1), 0)),
            pl.BlockSpec((TM, D_MODEL), lambda i: (0, 0)),
        ],
        out_shape=[
            jax.ShapeDtypeStruct((N_PROMPT, D_MODEL), F32),
            jax.ShapeDtypeStruct((N_SAMPLE, D_MODEL), F32),
        ],
        scratch_shapes=[pltpu.VMEM((2,) + _tiled(TOP_K * TM), U32), pltpu.SemaphoreType.DMA((2,))],
        compiler_params=_params(),
        name="moe_combine",
    )(dest, dest, gates, xmid, g, y_rows)


def _band_bias(table):
    n_diag = BAND + CHUNK - 1
    idx = np.clip(ATT_LEFT + (CHUNK - 1) - np.arange(n_diag), -REL_CLIP, REL_CLIP) + REL_CLIP
    pick = (np.arange(2 * REL_CLIP + 1)[:, None] == idx[None, :]).astype(np.float32)
    diag = jnp.dot(table, jnp.asarray(pick), precision=lax.Precision.HIGHEST)
    return jnp.stack([diag[:, CHUNK - 1 - qi:CHUNK - 1 - qi + BAND] for qi in range(CHUNK)], axis=1)


def _pair_rows(v):
    return jnp.repeat(v.reshape(HEAD_PAIRS, 2), CHUNK, axis=1)


def _layer(l, xp, xs, cache_k, cache_v, state_conv, state_ssm,
           norm_mix_g, w_in, conv_w, conv_b, dt_bias, a_log, d_skip, ssm_norm_g,
           att_norm_g, rel_bias_table, w_out, norm_ffn_g, w_router, b_router,
           w_gate_up, b_gate_up, w_down, b_down, norm_final_g):
    wb = w_in[l].astype(BF16)
    c0 = D_SSM
    c1 = c0 + D_CONV
    c2 = c1 + SSM_HEADS
    c3 = c2 + D_ATT
    c4 = c3 + D_ATT
    z, xbc, dt, dtt, q, k, v, k_p, v_p, k_s, v_s, ctail = _inproj(
        xp, xs, norm_mix_g[l][None], wb[:, :c0], wb[:, c0:c1], wb[:, c1:c2], wb[:, c1:c2].T,
        wb[:, c2:c3], wb[:, c3:c4], wb[:, c4:])

    n_chunks = N_TOK // CHUNK
    dtp = dtt.reshape(HEAD_PAIRS, 2, n_chunks, CHUNK).transpose(2, 0, 1, 3).reshape(n_chunks, HEAD_PAIRS, LANES)
    hp = jnp.arange(D_SSM) // SSM_HEAD_DIM
    expand = (hp[None, :] == jnp.arange(SSM_HEADS)[:, None]).astype(BF16)
    lane = jnp.arange(LANES)
    triu2 = ((lane[:, None] // CHUNK == lane[None, :] // CHUNK) & (lane[:, None] <= lane[None, :])).astype(BF16)
    consts = (conv_w[l], conv_b[l][None], dt_bias[l][None], _pair_rows(dt_bias[l]),
              a_log[l][None], _pair_rows(a_log[l]),
              jnp.repeat(d_skip[l], SSM_HEAD_DIM)[None], ssm_norm_g[l][None],
              expand, triu2)
    y_ssm_p, ssm_p = _ssd_prompt(z, xbc, dt, dtp, consts)
    y_ssm_s, ssm_s = _ssd_sample(z, xbc, dt, dtp, state_conv[l],
                                 state_ssm[l].reshape(DEC_BATCH, D_SSM, D_STATE), consts)

    bias2 = _band_bias(rel_bias_table[l]).reshape(ATT_PAIRS, 2 * CHUNK, BAND)
    o_att_p = _attn_prompt(q, k, v, bias2)
    o_att_s = _attn_sample(q, k, v, cache_k[l].reshape(DEC_BATCH, ATT_LEFT, D_ATT),
                           cache_v[l].reshape(DEC_BATCH, ATT_LEFT, D_ATT), bias2)

    wo = w_out[l].astype(BF16)
    wr = w_router[l].T
    wr1 = wr.astype(BF16)
    wr2 = (wr - wr1.astype(F32)).astype(BF16)
    earlier = jnp.triu(jnp.ones((TM, TM), BF16), 1)
    xmid, h, top_idx, gates, rank, counts = _outproj(
        xp, xs, y_ssm_p, y_ssm_s, o_att_p, o_att_s, att_norm_g[l][None], wo[:D_SSM], wo[D_SSM:], norm_ffn_g[l][None],
        jnp.concatenate([wr1, wr2], axis=0), b_router[l][:, None], earlier)

    counts = counts[:, 0].astype(jnp.int32)
    padded = (counts + MOE_BM - 1) // MOE_BM * MOE_BM
    pad_end = jnp.cumsum(padded)
    pad_start = pad_end - padded
    experts = jnp.arange(N_EXPERTS, dtype=jnp.int32)
    start_of = jnp.sum(jnp.where(top_idx[:, :, None] == experts, pad_start, 0), axis=-1)
    dest = (start_of + rank).astype(jnp.int32)
    block_start = jnp.arange(MOE_BLOCKS, dtype=jnp.int32) * MOE_BM
    block_expert = jnp.minimum(jnp.sum((pad_end[None, :] <= block_start[:, None]).astype(jnp.int32), axis=1),
                               N_EXPERTS - 1).astype(jnp.int32)
    n_used = (pad_end[-1:] // MOE_BM).astype(jnp.int32)
    later_nonempty = (experts[None, :] > experts[:, None]) & (padded[None, :] > 0)
    next_expert = jnp.min(jnp.where(later_nonempty, experts[None, :], N_EXPERTS), axis=1)
    next_expert = jnp.where(next_expert < N_EXPERTS, next_expert, -1).astype(jnp.int32)

    rows = _scatter_rows(dest, pad_end.astype(jnp.int32), h)
    y_rows = _experts(block_expert, n_used, next_expert, rows, w_gate_up[l], b_gate_up[l][:, None, :],
                      w_down[l], b_down[l][:, None, :])
    y_p, y_s = _combine(dest, gates, xmid, norm_final_g[None], y_rows)

    keep = min(ATT_LEFT, SEQ)
    k_p = k_p[:, TM - keep:].reshape(BATCH, keep, ATT_HEADS, ATT_HEAD_DIM)
    v_p = v_p[:, TM - keep:].reshape(BATCH, keep, ATT_HEADS, ATT_HEAD_DIM)
    k_s = k_s.reshape(DEC_BATCH, DEC_SEQ, ATT_HEADS, ATT_HEAD_DIM)
    v_s = v_s.reshape(DEC_BATCH, DEC_SEQ, ATT_HEADS, ATT_HEAD_DIM)
    conv_p = ctail[:BATCH, -(CONV_W - 1):]
    conv_s = ctail[BATCH].reshape(DEC_BATCH, SUBLANES, D_CONV)[:, -(CONV_W - 1):]
    ssm_p = ssm_p.reshape(BATCH, SSM_HEADS, SSM_HEAD_DIM, D_STATE)
    ssm_s = ssm_s.reshape(DEC_BATCH, SSM_HEADS, SSM_HEAD_DIM, D_STATE)
    return (y_p.reshape(BATCH, SEQ, D_MODEL), y_s.reshape(DEC_BATCH, DEC_SEQ, D_MODEL),
            k_p, v_p, conv_p, ssm_p, k_s, v_s, conv_s, ssm_s)


def kernel(x_prompt, x_sample, cache_k, cache_v, state_conv, state_ssm, norm_mix_g, w_in, conv_w, conv_b,
           dt_bias, a_log, d_skip, ssm_norm_g, att_norm_g, rel_bias_table, w_out, norm_ffn_g, w_router,
           b_router, w_gate_up, b_gate_up, w_down, b_down, norm_final_g):
    assert w_in.shape[0] == 1, "single trunk layer"
    xp = x_prompt.reshape(N_PROMPT, D_MODEL)
    xs = x_sample.reshape(N_SAMPLE, D_MODEL)
    outs = _layer(0, xp, xs, cache_k, cache_v, state_conv, state_ssm,
                  norm_mix_g, w_in, conv_w, conv_b, dt_bias, a_log, d_skip, ssm_norm_g,
                  att_norm_g, rel_bias_table, w_out, norm_ffn_g, w_router, b_router,
                  w_gate_up, b_gate_up, w_down, b_down, norm_final_g)
    y_p, y_s, k_p, v_p, conv_p, ssm_p, k_s, v_s, conv_s, ssm_s = outs
    return (y_p, y_s, k_p[None], v_p[None], conv_p[None], ssm_p[None],
            k_s[None], v_s[None], conv_s[None], ssm_s[None])
```

```python
import jax
import jax.numpy as jnp
import numpy as np
from jax import lax
from jax.experimental import pallas as pl
from jax.experimental.pallas import tpu as pltpu

D_MODEL = 1024
BATCH = 8
SEQ = 2048
DEC_BATCH = 8
DEC_SEQ = 64
CHUNK = 64
SSM_HEADS = 16
SSM_HEAD_DIM = 64
D_SSM = SSM_HEADS * SSM_HEAD_DIM
SSM_GROUPS = 2
D_STATE = 128
CONV_W = 4
D_BC = SSM_GROUPS * D_STATE
D_CONV = D_SSM + 2 * D_BC
ATT_HEADS = 8
ATT_HEAD_DIM = 64
D_ATT = ATT_HEADS * ATT_HEAD_DIM
LEFT_CHUNKS = 8
ATT_LEFT = LEFT_CHUNKS * CHUNK
BAND = ATT_LEFT + CHUNK
REL_CLIP = 128
ATT_SCALE = ATT_HEAD_DIM ** -0.5
N_EXPERTS = 32
TOP_K = 4
D_FF = D_MODEL
SWIGLU_ALPHA = 1.702
SWIGLU_LIMIT = 7.0
EPS = 1e-5

F32 = jnp.float32
BF16 = jnp.bfloat16
U32 = jnp.uint32

LANES = 128
SUBLANES = 8

N_PROMPT = BATCH * SEQ
N_SAMPLE = DEC_BATCH * DEC_SEQ
N_TOK = N_PROMPT + N_SAMPLE
TM = 512
N_PROMPT_TILES = N_PROMPT // TM
N_TILES = N_TOK // TM
TILES_PER_SEQ = SEQ // TM
CHUNKS_PER_TILE = TM // CHUNK
N_CHUNKS_SEQ = SEQ // CHUNK
assert 2 * SSM_HEAD_DIM == LANES and 2 * ATT_HEAD_DIM == LANES
HEAD_PAIRS = SSM_HEADS // 2
PAIRS_PER_GROUP = HEAD_PAIRS // SSM_GROUPS
ATT_PAIRS = ATT_HEADS // 2
SSD_BLOCK = 2
CONV_HEAD = SUBLANES
MOE_BM = 256
N_ASSIGN = N_TOK * TOP_K
MOE_BLOCKS = N_ASSIGN // MOE_BM + N_EXPERTS
MOE_ROWS = MOE_BLOCKS * MOE_BM
ISSUE_UNROLL = 8
ROW_TILE = (SUBLANES // 2, LANES)
assert 2 * ROW_TILE[0] * ROW_TILE[1] == D_MODEL
VMEM_BYTES = 64 * 1024 * 1024
VMEM_LIMIT = VMEM_BYTES * 7 // 8


def _dot(a, b):
    return jnp.dot(a, b, preferred_element_type=F32)


def _dot_nt(a, b):
    return lax.dot_general(a, b, (((1,), (1,)), ((), ())), preferred_element_type=F32)


def _dot_tn(a, b):
    return lax.dot_general(a, b, (((0,), (0,)), ((), ())), preferred_element_type=F32)


def _split3(x):
    x1 = x.astype(BF16)
    r1 = x - x1.astype(F32)
    x2 = r1.astype(BF16)
    r2 = r1 - x2.astype(F32)
    return x1, x2, r2.astype(BF16)


def _dot_exact_rhs(x, m):
    x1, x2, x3 = _split3(x)
    return _dot(x1, m) + _dot(x2, m) + _dot(x3, m)


def _dot_exact_lhs(m, x):
    x1, x2, x3 = _split3(x)
    return _dot(m, x1) + _dot(m, x2) + _dot(m, x3)


def _rms(x, g):
    return x * lax.rsqrt(jnp.mean(x * x, axis=-1, keepdims=True) + EPS) * g


def _sigmoid(x):
    return 1.0 / (1.0 + jnp.exp(-x))


def _softplus(x):
    return jnp.maximum(x, 0.0) + jnp.log(1.0 + jnp.exp(-jnp.abs(x)))


def _tiled(n):
    return (n * ROW_TILE[0], ROW_TILE[1])


def _tile_of(ref, row):
    return ref.at[pl.ds(pl.multiple_of(row * ROW_TILE[0], ROW_TILE[0]), ROW_TILE[0])]


def _rows_to_tiles(ref, x, first=0):
    sub, lanes = ROW_TILE
    half = D_MODEL // 2
    hi = lax.bitcast_convert_type(x[:, :half].astype(BF16).astype(F32), U32)
    lo = lax.bitcast_convert_type(x[:, half:].astype(BF16).astype(F32), U32)
    words = hi | (lo >> 16)
    for j in range(sub):
        ref[pl.ds(first * sub + j, x.shape[0], stride=sub), :] = words[:, j * lanes:(j + 1) * lanes]


def _tiles_to_rows(ref, n, first=0):
    sub = ROW_TILE[0]
    words = jnp.concatenate([ref[pl.ds(first * sub + j, n, stride=sub), :] for j in range(sub)], axis=1)
    hi = lax.bitcast_convert_type(words & jnp.uint32(0xFFFF0000), F32)
    lo = lax.bitcast_convert_type(words << 16, F32)
    return jnp.concatenate([hi, lo], axis=1)


def _const_spec(shape):
    nd = len(shape)
    return pl.BlockSpec(shape, lambda *_: (0,) * nd)


def _params(n_axes=1):
    return pltpu.CompilerParams(dimension_semantics=("arbitrary",) * n_axes,
                                vmem_limit_bytes=VMEM_LIMIT)


def _inproj_kernel(xp_ref, xs_ref, g_ref, wz_ref, wxbc_ref, wdt_ref, wdtt_ref, wq_ref, wk_ref, wv_ref,
                   z_ref, xbc_ref, dt_ref, dtt_ref, q_ref, k_ref, v_ref, kp_ref, vp_ref, ks_ref, vs_ref, ctail_ref):
    i = pl.program_id(0)
    x = jnp.where(i == N_PROMPT_TILES, xs_ref[...], xp_ref[...])
    h = _rms(x, g_ref[...]).astype(BF16)
    z_ref[...] = _dot(h, wz_ref[...]).astype(BF16)
    xbc = _dot(h, wxbc_ref[...])
    xbc_ref[...] = xbc.astype(BF16)
    for c in range(CHUNKS_PER_TILE):
        ctail_ref[0, c * SUBLANES:(c + 1) * SUBLANES, :] = xbc[(c + 1) * CHUNK - SUBLANES:(c + 1) * CHUNK, :]
    dt_ref[...] = _dot(h, wdt_ref[...])
    dtt_ref[...] = _dot_nt(wdtt_ref[...], h)
    q_ref[...] = (_dot(h, wq_ref[...]) * ATT_SCALE).astype(BF16)
    k = _dot(h, wk_ref[...])
    v = _dot(h, wv_ref[...])
    k_ref[...] = k.astype(BF16)
    v_ref[...] = v.astype(BF16)

    @pl.when(i < N_PROMPT_TILES)
    def _():
        kp_ref[0] = k
        vp_ref[0] = v

    @pl.when(i == N_PROMPT_TILES)
    def _():
        ks_ref[...] = k
        vs_ref[...] = v


def _inproj(xp, xs, g, wz, wxbc, wdt, wdtt, wq, wk, wv):
    tok = lambda n: pl.BlockSpec((TM, n), lambda i: (i, 0))
    tail_idx = lambda i: (i // TILES_PER_SEQ, 0, 0)
    seq_idx = lambda i: (jnp.minimum(i // TILES_PER_SEQ, BATCH - 1), 0, 0)
    n_tail = BATCH + 1
    return pl.pallas_call(
        _inproj_kernel,
        grid=(N_TILES,),
        in_specs=[
            pl.BlockSpec((TM, D_MODEL), lambda i: (jnp.minimum(i, N_PROMPT_TILES - 1), 0)),
            pl.BlockSpec((TM, D_MODEL), lambda i: (0, 0)),
            _const_spec((1, D_MODEL)),
            _const_spec((D_MODEL, D_SSM)),
            _const_spec((D_MODEL, D_CONV)),
            _const_spec((D_MODEL, SSM_HEADS)),
            _const_spec((SSM_HEADS, D_MODEL)),
            _const_spec((D_MODEL, D_ATT)),
            _const_spec((D_MODEL, D_ATT)),
            _const_spec((D_MODEL, D_ATT)),
        ],
        out_specs=[
            tok(D_SSM), tok(D_CONV), tok(SSM_HEADS),
            pl.BlockSpec((SSM_HEADS, TM), lambda i: (0, i)),
            tok(D_ATT), tok(D_ATT), tok(D_ATT),
            pl.BlockSpec((1, TM, D_ATT), seq_idx),
            pl.BlockSpec((1, TM, D_ATT), seq_idx),
            pl.BlockSpec((TM, D_ATT), lambda i: (0, 0)),
            pl.BlockSpec((TM, D_ATT), lambda i: (0, 0)),
            pl.BlockSpec((1, CHUNKS_PER_TILE * SUBLANES, D_CONV), tail_idx),
        ],
        out_shape=[
            jax.ShapeDtypeStruct((N_TOK, D_SSM), BF16),
            jax.ShapeDtypeStruct((N_TOK, D_CONV), BF16),
            jax.ShapeDtypeStruct((N_TOK, SSM_HEADS), F32),
            jax.ShapeDtypeStruct((SSM_HEADS, N_TOK), F32),
            jax.ShapeDtypeStruct((N_TOK, D_ATT), BF16),
            jax.ShapeDtypeStruct((N_TOK, D_ATT), BF16),
            jax.ShapeDtypeStruct((N_TOK, D_ATT), BF16),
            jax.ShapeDtypeStruct((BATCH, TM, D_ATT), F32),
            jax.ShapeDtypeStruct((BATCH, TM, D_ATT), F32),
            jax.ShapeDtypeStruct((N_SAMPLE, D_ATT), F32),
            jax.ShapeDtypeStruct((N_SAMPLE, D_ATT), F32),
            jax.ShapeDtypeStruct((n_tail, CHUNKS_PER_TILE * SUBLANES, D_CONV), F32),
        ],
        compiler_params=_params(),
        name="inproj",
    )(xp, xs, g, wz, wxbc, wdt, wdtt, wq, wk, wv)


def _ssd_tile(n_chunks, z_ref, xbc_ref, dt_ref, dtp_ref, cw_ref, cb_ref, dtb_ref, dtbp_ref,
              alog_h_ref, alog_p_ref, dskip_e_ref, ng_ref, expand_ref, triu2_ref,
              y_ref, xw_ref, state_ref):
    nb = SSD_BLOCK if n_chunks % SSD_BLOCK == 0 else 1
    rb = nb * CHUNK
    half = D_SSM // SSM_GROUPS
    xw_ref[CONV_HEAD:CONV_HEAD + n_chunks * CHUNK, :] = xbc_ref[...].astype(F32)

    a_h = -jnp.exp(alog_h_ref[...])
    a_p = jnp.concatenate([-jnp.exp(alog_p_ref[...])] * nb, axis=0)
    dtb_p = jnp.concatenate([dtbp_ref[...]] * nb, axis=0)
    tr_r = lax.broadcasted_iota(jnp.int32, (rb, rb), 0)
    tr_c = lax.broadcasted_iota(jnp.int32, (rb, rb), 1)
    tril_b = jnp.where((tr_r // CHUNK == tr_c // CHUNK) & (tr_c <= tr_r), 1.0, 0.0).astype(BF16)
    row_i = lax.broadcasted_iota(jnp.int32, (CHUNK, LANES), 0)
    col_i = lax.broadcasted_iota(jnp.int32, (CHUNK, LANES), 1)
    causal2 = row_i >= (col_i % CHUNK)
    sel_r = lax.broadcasted_iota(jnp.int32, ((CONV_W - 1) * rb, CONV_HEAD + rb), 0)
    sel_c = lax.broadcasted_iota(jnp.int32, ((CONV_W - 1) * rb, CONV_HEAD + rb), 1)
    shift_sel = (sel_c == sel_r % rb + sel_r // rb + CONV_HEAD - (CONV_W - 1)).astype(F32)
    bd_r = lax.broadcasted_iota(jnp.int32, (LANES, LANES), 0) // CHUNK
    bd_c = lax.broadcasted_iota(jnp.int32, (LANES, LANES), 1) // CHUNK
    blockdiag = bd_r == bd_c

    def block(i, carry):
        r0 = pl.multiple_of(i * rb, rb)
        win = xw_ref[pl.ds(r0, CONV_HEAD + rb), :]
        shifted = _dot(shift_sel, win)
        acc = cb_ref[...] + cw_ref[CONV_W - 1:CONV_W, :] * win[CONV_HEAD:, :]
        for tap in range(CONV_W - 1):
            acc = acc + cw_ref[tap:tap + 1, :] * shifted[tap * rb:(tap + 1) * rb, :]
        xa = acc * _sigmoid(acc)
        xs = xa[:, 0:D_SSM]
        bm = xa[:, D_SSM:D_SSM + D_BC].astype(BF16)
        cm = xa[:, D_SSM + D_BC:D_CONV].astype(BF16)
        dt = _softplus(dt_ref[pl.ds(r0, rb), :] + dtb_ref[...])
        dt_e = _dot_exact_rhs(dt, expand_ref[...])
        acum = _dot_exact_rhs(_dot_exact_lhs(tril_b, dt * a_h), expand_ref[...])
        dtp = _softplus(dtp_ref[pl.ds(i * nb, nb)].reshape(nb * HEAD_PAIRS, LANES) + dtb_p)
        acum_p = _dot_exact_rhs(dtp * a_p, triu2_ref[...])
        xdt = xs * dt_e
        a_last = [acum[(c + 1) * CHUNK - 1:(c + 1) * CHUNK, :] for c in range(nb)]
        a_end = jnp.concatenate([jnp.broadcast_to(a, (CHUNK, D_SSM)) for a in a_last], axis=0)
        xdt_end = (xdt * jnp.exp(a_end - acum)).astype(BF16)

        y_diag, new_s = [], []
        for c in range(nb):
            rs = slice(c * CHUNK, (c + 1) * CHUNK)
            y_parts = []
            for g in range(SSM_GROUPS):
                bg = bm[rs, g * D_STATE:(g + 1) * D_STATE]
                cg = cm[rs, g * D_STATE:(g + 1) * D_STATE]
                cb2 = _dot_nt(cg, jnp.concatenate([bg, bg], axis=0))
                for jj in range(PAIRS_PER_GROUP):
                    j = g * PAIRS_PER_GROUP + jj
                    seg = acum[rs, j * LANES:(j + 1) * LANES] - acum_p[c * HEAD_PAIRS + j:c * HEAD_PAIRS + j + 1, :]
                    decay = jnp.exp(jnp.where(causal2, seg, -jnp.inf))
                    x2 = xdt[rs, j * LANES:(j + 1) * LANES]
                    rhs = jnp.where(blockdiag, jnp.concatenate([x2, x2], axis=0), 0.0).astype(BF16)
                    y_parts.append(_dot((cb2 * decay).astype(BF16), rhs))
            y_diag.append(jnp.concatenate(y_parts, axis=1))
            new_s.append(jnp.concatenate(
                [_dot_tn(bm[rs, g * D_STATE:(g + 1) * D_STATE], xdt_end[rs, g * half:(g + 1) * half])
                 for g in range(SSM_GROUPS)], axis=1))

        y_off = []
        for c in range(nb):
            rs = slice(c * CHUNK, (c + 1) * CHUNK)
            state = state_ref[...]
            y_off.append(jnp.concatenate(
                [_dot(cm[rs, g * D_STATE:(g + 1) * D_STATE], state[:, g * half:(g + 1) * half].astype(BF16))
                 for g in range(SSM_GROUPS)], axis=1))
            state_ref[...] = state * jnp.exp(a_last[c]) + new_s[c]

        y = (jnp.concatenate(y_diag, axis=0) + jnp.concatenate(y_off, axis=0) * jnp.exp(acum)
             + dskip_e_ref[...] * xs)
        zc = z_ref[pl.ds(r0, rb), :].astype(F32)
        y = y * (zc * _sigmoid(zc))
        yn = jnp.concatenate(
            [y[:, g * half:(g + 1) * half]
             * lax.rsqrt(jnp.mean(jnp.square(y[:, g * half:(g + 1) * half]), axis=-1, keepdims=True) + EPS)
             for g in range(SSM_GROUPS)], axis=1)
        y_ref[pl.ds(r0, rb), :] = (yn * ng_ref[...]).astype(BF16)
        return carry

    lax.fori_loop(0, n_chunks // nb, block, 0)


def _state_store(state_ref, out_ref):
    for j in range(HEAD_PAIRS):
        out_ref[0, j * LANES:(j + 1) * LANES, :] = state_ref[:, j * LANES:(j + 1) * LANES].T


def _ssd_prompt_kernel(z_ref, xbc_ref, dt_ref, dtp_ref, cw_ref, cb_ref, dtb_ref, dtbp_ref,
                       alog_h_ref, alog_p_ref, dskip_e_ref, ng_ref, expand_ref, triu2_ref,
                       y_ref, ssm_ref, xw_ref, state_ref, tail_ref):
    t = pl.program_id(1)

    @pl.when(t == 0)
    def _():
        state_ref[...] = jnp.zeros_like(state_ref)
        xw_ref[0:CONV_HEAD, :] = jnp.zeros((CONV_HEAD, D_CONV), F32)

    @pl.when(t > 0)
    def _():
        xw_ref[0:CONV_HEAD, :] = tail_ref[...]

    _ssd_tile(CHUNKS_PER_TILE, z_ref, xbc_ref, dt_ref, dtp_ref, cw_ref, cb_ref, dtb_ref, dtbp_ref,
              alog_h_ref, alog_p_ref, dskip_e_ref, ng_ref, expand_ref, triu2_ref,
              y_ref, xw_ref, state_ref)
    tail_ref[...] = xw_ref[TM:TM + CONV_HEAD, :]

    @pl.when(t == TILES_PER_SEQ - 1)
    def _():
        _state_store(state_ref, ssm_ref)


def _ssd_sample_kernel(z_ref, xbc_ref, dt_ref, dtp_ref, cprev_ref, sprev_ref,
                       cw_ref, cb_ref, dtb_ref, dtbp_ref,
                       alog_h_ref, alog_p_ref, dskip_e_ref, ng_ref, expand_ref, triu2_ref,
                       y_ref, ssm_ref, xw_ref, state_ref):
    xw_ref[0:CONV_HEAD, :] = jnp.zeros((CONV_HEAD, D_CONV), F32)
    xw_ref[CONV_HEAD - (CONV_W - 1):CONV_HEAD, :] = cprev_ref[0]
    for j in range(HEAD_PAIRS):
        state_ref[:, j * LANES:(j + 1) * LANES] = sprev_ref[0, j * LANES:(j + 1) * LANES, :].T
    _ssd_tile(1, z_ref, xbc_ref, dt_ref, dtp_ref, cw_ref, cb_ref, dtb_ref, dtbp_ref,
              alog_h_ref, alog_p_ref, dskip_e_ref, ng_ref, expand_ref, triu2_ref,
              y_ref, xw_ref, state_ref)
    _state_store(state_ref, ssm_ref)


def _ssd_const_specs():
    return [
        _const_spec((CONV_W, D_CONV)), _const_spec((1, D_CONV)),
        _const_spec((1, SSM_HEADS)), _const_spec((HEAD_PAIRS, LANES)),
        _const_spec((1, SSM_HEADS)), _const_spec((HEAD_PAIRS, LANES)),
        _const_spec((1, D_SSM)), _const_spec((1, D_SSM)),
        _const_spec((SSM_HEADS, D_SSM)), _const_spec((LANES, LANES)),
    ]


def _ssd_prompt(z, xbc, dt, dtp, consts):
    tile = lambda b, t: (b * TILES_PER_SEQ + t, 0)
    return pl.pallas_call(
        _ssd_prompt_kernel,
        grid=(BATCH, TILES_PER_SEQ),
        in_specs=[
            pl.BlockSpec((TM, D_SSM), tile),
            pl.BlockSpec((TM, D_CONV), tile),
            pl.BlockSpec((TM, SSM_HEADS), tile),
            pl.BlockSpec((CHUNKS_PER_TILE, HEAD_PAIRS, LANES), lambda b, t: (b * TILES_PER_SEQ + t, 0, 0)),
        ] + _ssd_const_specs(),
        out_specs=[
            pl.BlockSpec((TM, D_SSM), tile),
            pl.BlockSpec((1, D_SSM, D_STATE), lambda b, t: (b, 0, 0)),
        ],
        out_shape=[
            jax.ShapeDtypeStruct((N_PROMPT, D_SSM), BF16),
            jax.ShapeDtypeStruct((BATCH, D_SSM, D_STATE), F32),
        ],
        scratch_shapes=[
            pltpu.VMEM((CONV_HEAD + TM, D_CONV), F32),
            pltpu.VMEM((D_STATE, D_SSM), F32),
            pltpu.VMEM((CONV_HEAD, D_CONV), F32),
        ],
        compiler_params=_params(2),
        name="ssd_prompt",
    )(z, xbc, dt, dtp, *consts)


def _ssd_sample(z, xbc, dt, dtp, conv_prev, ssm_prev, consts):
    first = N_PROMPT // CHUNK
    row = lambda b: (first + b, 0)
    return pl.pallas_call(
        _ssd_sample_kernel,
        grid=(DEC_BATCH,),
        in_specs=[
            pl.BlockSpec((CHUNK, D_SSM), row),
            pl.BlockSpec((CHUNK, D_CONV), row),
            pl.BlockSpec((CHUNK, SSM_HEADS), row),
            pl.BlockSpec((1, HEAD_PAIRS, LANES), lambda b: (first + b, 0, 0)),
            pl.BlockSpec((1, CONV_W - 1, D_CONV), lambda b: (b, 0, 0)),
            pl.BlockSpec((1, D_SSM, D_STATE), lambda b: (b, 0, 0)),
        ] + _ssd_const_specs(),
        out_specs=[
            pl.BlockSpec((CHUNK, D_SSM), lambda b: (b, 0)),
            pl.BlockSpec((1, D_SSM, D_STATE), lambda b: (b, 0, 0)),
        ],
        out_shape=[
            jax.ShapeDtypeStruct((N_SAMPLE, D_SSM), BF16),
            jax.ShapeDtypeStruct((DEC_BATCH, D_SSM, D_STATE), F32),
        ],
        scratch_shapes=[
            pltpu.VMEM((CONV_HEAD + CHUNK, D_CONV), F32),
            pltpu.VMEM((D_STATE, D_SSM), F32),
        ],
        compiler_params=_params(),
        name="ssd_sample",
    )(z, xbc, dt, dtp, conv_prev, ssm_prev, *consts)


def _attn_chunks(n_chunks, first_chunk, q_ref, kpad_ref, vpad_ref, bias_ref, o_ref):
    lane = lax.broadcasted_iota(jnp.int32, (CHUNK, LANES), 1)
    low = lane < ATT_HEAD_DIM
    kj = lax.broadcasted_iota(jnp.int32, (2 * CHUNK, BAND), 1)

    def chunk(c, carry, masked):
        r0 = pl.multiple_of(c * CHUNK, CHUNK)
        if masked:
            valid = kj >= (LEFT_CHUNKS - (first_chunk + c)) * CHUNK
        scores = []
        for j in range(ATT_PAIRS):
            qp = q_ref[pl.ds(r0, CHUNK), j * LANES:(j + 1) * LANES]
            zero = jnp.zeros_like(qp)
            q2 = jnp.concatenate([jnp.where(low, qp, zero), jnp.where(low, zero, qp)], axis=0)
            kb = kpad_ref[pl.ds(r0, BAND), j * LANES:(j + 1) * LANES]
            s = _dot_nt(q2, kb) + bias_ref[j]
            scores.append(jnp.where(valid, s, -jnp.inf) if masked else s)
        probs = []
        for s in scores:
            e = jnp.exp(s - jnp.max(s, axis=-1, keepdims=True))
            probs.append((e.astype(BF16), jnp.sum(e, axis=-1, keepdims=True)))
        outs = []
        for j, (e, denom) in enumerate(probs):
            vb = vpad_ref[pl.ds(r0, BAND), j * LANES:(j + 1) * LANES]
            r = _dot(e, vb) / denom
            outs.append(jnp.where(low, r[0:CHUNK], r[CHUNK:2 * CHUNK]))
        o_ref[pl.ds(r0, CHUNK), :] = jnp.concatenate(outs, axis=1).astype(BF16)
        return carry

    n_masked = min(max(LEFT_CHUNKS - first_chunk, 0), n_chunks)
    for lo, hi, masked in ((0, n_masked, True), (n_masked, n_chunks, False)):
        if hi > lo:
            lax.fori_loop(lo, hi, lambda c, carry, masked=masked: chunk(c, carry, masked), 0,
                          unroll=2 if (hi - lo) % 2 == 0 else 1)


def _attn_prompt_kernel(q_ref, k_ref, v_ref, bias_ref, o_ref, kpad_ref, vpad_ref):
    kpad_ref[0:ATT_LEFT, :] = jnp.zeros((ATT_LEFT, D_ATT), BF16)
    vpad_ref[0:ATT_LEFT, :] = jnp.zeros((ATT_LEFT, D_ATT), BF16)
    kpad_ref[ATT_LEFT:ATT_LEFT + SEQ, :] = k_ref[...]
    vpad_ref[ATT_LEFT:ATT_LEFT + SEQ, :] = v_ref[...]
    _attn_chunks(N_CHUNKS_SEQ, 0, q_ref, kpad_ref, vpad_ref, bias_ref, o_ref)


def _attn_sample_kernel(q_ref, k_ref, v_ref, ck_ref, cv_ref, bias_ref, o_ref, kpad_ref, vpad_ref):
    kpad_ref[0:ATT_LEFT, :] = ck_ref[0].astype(BF16)
    vpad_ref[0:ATT_LEFT, :] = cv_ref[0].astype(BF16)
    kpad_ref[ATT_LEFT:BAND, :] = k_ref[...]
    vpad_ref[ATT_LEFT:BAND, :] = v_ref[...]
    _attn_chunks(1, LEFT_CHUNKS, q_ref, kpad_ref, vpad_ref, bias_ref, o_ref)


def _attn_prompt(q, k, v, bias2):
    seq = pl.BlockSpec((SEQ, D_ATT), lambda b: (b, 0))
    return pl.pallas_call(
        _attn_prompt_kernel,
        grid=(BATCH,),
        in_specs=[seq, seq, seq, _const_spec((ATT_PAIRS, 2 * CHUNK, BAND))],
        out_specs=seq,
        out_shape=jax.ShapeDtypeStruct((N_PROMPT, D_ATT), BF16),
        scratch_shapes=[pltpu.VMEM((ATT_LEFT + SEQ, D_ATT), BF16),
                        pltpu.VMEM((ATT_LEFT + SEQ, D_ATT), BF16)],
        compiler_params=_params(),
        name="attn_prompt",
    )(q, k, v, bias2)


def _attn_sample(q, k, v, cache_k, cache_v, bias2):
    first = N_PROMPT // CHUNK
    row = pl.BlockSpec((CHUNK, D_ATT), lambda b: (first + b, 0))
    cache = pl.BlockSpec((1, ATT_LEFT, D_ATT), lambda b: (b, 0, 0))
    return pl.pallas_call(
        _attn_sample_kernel,
        grid=(DEC_BATCH,),
        in_specs=[row, row, row, cache, cache, _const_spec((ATT_PAIRS, 2 * CHUNK, BAND))],
        out_specs=pl.BlockSpec((CHUNK, D_ATT), lambda b: (b, 0)),
        out_shape=jax.ShapeDtypeStruct((N_SAMPLE, D_ATT), BF16),
        scratch_shapes=[pltpu.VMEM((BAND, D_ATT), BF16), pltpu.VMEM((BAND, D_ATT), BF16)],
        compiler_params=_params(),
        name="attn_sample",
    )(q, k, v, cache_k, cache_v, bias2)


def _outproj_kernel(xp_ref, xs_ref, yp_ref, ys_ref, op_ref, os_ref, ag_ref, wos_ref, woa_ref, fg_ref,
                    wr_ref, br_ref, earlier_ref,
                    xmid_ref, h_ref, idx_ref, gate_ref, rank_ref, cnt_ref, carry_ref):
    i = pl.program_id(0)

    @pl.when(i == 0)
    def _():
        carry_ref[...] = jnp.zeros_like(carry_ref)

    is_sample = i == N_PROMPT_TILES
    x = jnp.where(is_sample, xs_ref[...], xp_ref[...])
    y = jnp.where(is_sample, ys_ref[...], yp_ref[...])
    o = jnp.where(is_sample, os_ref[...], op_ref[...])
    o = _rms(o.astype(F32), ag_ref[...]).astype(BF16)
    xm = x + _dot(y, wos_ref[...]) + _dot(o, woa_ref[...])
    xmid_ref[...] = xm
    h = _rms(xm, fg_ref[...])
    _rows_to_tiles(h_ref, h)
    h1 = h.astype(BF16)
    h2 = (h - h1.astype(F32)).astype(BF16)
    both = _dot_nt(wr_ref[...], h1)
    logits = both[:N_EXPERTS] + (both[N_EXPERTS:] + _dot_nt(wr_ref[0:N_EXPERTS, :], h2)) + br_ref[...]
    eidx = lax.broadcasted_iota(jnp.int32, (N_EXPERTS, TM), 0)
    slot = lax.broadcasted_iota(jnp.int32, (SUBLANES, TM), 0)
    work = logits
    vals, sels = [], []
    idx_out = jnp.zeros((SUBLANES, TM), jnp.int32)
    for k in range(TOP_K):
        m = jnp.max(work, axis=0, keepdims=True)
        idx = jnp.min(jnp.where(work == m, eidx, N_EXPERTS), axis=0, keepdims=True)
        sel = eidx == idx
        vals.append(m)
        sels.append(sel)
        idx_out = jnp.where(slot == k, idx, idx_out)
        work = jnp.where(sel, -jnp.inf, work)
    es = [jnp.exp(v - vals[0]) for v in vals]
    tot = es[0] + es[1] + es[2] + es[3]
    gate_out = jnp.zeros((SUBLANES, TM), F32)
    for k in range(TOP_K):
        gate_out = jnp.where(slot == k, es[k] / tot, gate_out)
    idx_ref[...] = idx_out
    gate_ref[...] = gate_out
    multi = jnp.zeros((N_EXPERTS, TM), F32)
    for sel in sels:
        multi = jnp.where(sel, 1.0, multi)
    before = _dot(multi.astype(BF16), earlier_ref[...]) + carry_ref[...]
    rank_out = jnp.zeros((SUBLANES, TM), jnp.int32)
    for k in range(TOP_K):
        rk = jnp.sum(jnp.where(sels[k], before, 0.0), axis=0, keepdims=True).astype(jnp.int32)
        rank_out = jnp.where(slot == k, rk, rank_out)
    rank_ref[...] = rank_out
    carry_ref[...] = carry_ref[...] + jnp.sum(multi, axis=1, keepdims=True)
    cnt_ref[...] = carry_ref[...]


def _outproj(xp, xs, yp, ys, op, os_, ag, wos, woa, fg, wr, br, earlier):
    tok = lambda n: pl.BlockSpec((TM, n), lambda i: (i, 0))
    slots = pl.BlockSpec((SUBLANES, TM), lambda i: (0, i))
    prompt = lambda n: pl.BlockSpec((TM, n), lambda i: (jnp.minimum(i, N_PROMPT_TILES - 1), 0))
    sample = lambda n: pl.BlockSpec((TM, n), lambda i: (0, 0))
    return pl.pallas_call(
        _outproj_kernel,
        grid=(N_TILES,),
        in_specs=[
            prompt(D_MODEL), sample(D_MODEL), prompt(D_SSM), sample(D_SSM), prompt(D_ATT), sample(D_ATT),
            _const_spec((1, D_ATT)),
            _const_spec((D_SSM, D_MODEL)), _const_spec((D_ATT, D_MODEL)),
            _const_spec((1, D_MODEL)),
            _const_spec((2 * N_EXPERTS, D_MODEL)), _const_spec((N_EXPERTS, 1)),
            _const_spec((TM, TM)),
        ],
        out_specs=[tok(D_MODEL), pl.BlockSpec(_tiled(TM), lambda i: (i, 0)), slots, slots, slots,
                   _const_spec((N_EXPERTS, 1))],
        out_shape=[
            jax.ShapeDtypeStruct((N_TOK, D_MODEL), F32),
            jax.ShapeDtypeStruct(_tiled(N_TOK), U32),
            jax.ShapeDtypeStruct((SUBLANES, N_TOK), jnp.int32),
            jax.ShapeDtypeStruct((SUBLANES, N_TOK), F32),
            jax.ShapeDtypeStruct((SUBLANES, N_TOK), jnp.int32),
            jax.ShapeDtypeStruct((N_EXPERTS, 1), F32),
        ],
        scratch_shapes=[pltpu.VMEM((N_EXPERTS, 1), F32)],
        compiler_params=_params(),
        name="outproj_router",
    )(xp, xs, yp, ys, op, os_, ag, wos, woa, fg, wr, br, earlier)


def _scatter_kernel(dest_ref, pend_ref, h_ref, rows_ref, zero_ref, sem, zsem):
    i = pl.program_id(0)

    @pl.when(i == 0)
    def _():
        zero_ref[...] = jnp.zeros_like(zero_ref)

        block_tiles = _tiled(MOE_BM)[0]

        def zero_block(b):
            start = pl.multiple_of(b * block_tiles, block_tiles)
            return pltpu.make_async_copy(zero_ref, rows_ref.at[pl.ds(start, block_tiles)], zsem)

        def last_block(e):
            end = pend_ref[e]
            nonempty = end > (pend_ref[e - 1] if e > 0 else 0)
            return nonempty, zero_block(jnp.maximum(end // MOE_BM - 1, 0))

        for e in range(N_EXPERTS):
            nonempty, cp = last_block(e)
            pl.when(nonempty)(cp.start)
        for e in range(N_EXPERTS):
            nonempty, cp = last_block(e)
            pl.when(nonempty)(cp.wait)

        first_unused = pend_ref[N_EXPERTS - 1] // MOE_BM
        lax.fori_loop(first_unused, MOE_BLOCKS, lambda b, c: (zero_block(b).start(), c)[1], 0)
        lax.fori_loop(first_unused, MOE_BLOCKS, lambda b, c: (zero_block(b).wait(), c)[1], 0)

    def issue(r, carry):
        for k in range(TOP_K):
            d = dest_ref[r * TOP_K + k]
            pltpu.make_async_copy(_tile_of(h_ref, r), _tile_of(rows_ref, d), sem).start(priority=k % 2)
        return carry

    lax.fori_loop(0, TM, issue, 0, unroll=ISSUE_UNROLL)
    for _ in range(TOP_K):
        pltpu.make_async_copy(h_ref, rows_ref.at[pl.ds(0, _tiled(TM)[0])], sem).wait()


def _scatter_rows(dest_flat, pad_end, h):
    return pl.pallas_call(
        _scatter_kernel,
        grid=(N_TILES,),
        in_specs=[
            pl.BlockSpec((TM * TOP_K,), lambda i: (i,), memory_space=pltpu.SMEM),
            pl.BlockSpec((N_EXPERTS,), lambda i: (0,), memory_space=pltpu.SMEM),
            pl.BlockSpec(_tiled(TM), lambda i: (i, 0)),
        ],
        out_specs=pl.BlockSpec(memory_space=pl.ANY),
        out_shape=jax.ShapeDtypeStruct(_tiled(MOE_ROWS), U32),
        scratch_shapes=[pltpu.VMEM(_tiled(MOE_BM), U32), pltpu.SemaphoreType.DMA(()),
                        pltpu.SemaphoreType.DMA(())],
        compiler_params=_params(),
        name="moe_scatter",
    )(dest_flat, pad_end, h)


def _expert_kernel(first_ref, count_ref, nu_ref, rows_hbm, wgu_ref, bgu_ref, wd_ref, bd_ref, y_hbm,
                   wgu_s, wd_s, xbuf, ybuf, xsem, ysem):
    e = pl.program_id(0)
    n_used = nu_ref[0]
    block_tiles = _tiled(MOE_BM)[0]

    def block_of(ref, g):
        return ref.at[pl.ds(pl.multiple_of(g * block_tiles, block_tiles), block_tiles)]

    def x_copy(g, slot):
        return pltpu.make_async_copy(block_of(rows_hbm, g), xbuf.at[slot], xsem.at[slot])

    def y_copy(g, slot):
        return pltpu.make_async_copy(ybuf.at[slot], block_of(y_hbm, g), ysem.at[slot])

    @pl.when((e == 0) & (n_used > 0))
    def _():
        x_copy(0, 0).start()

    @pl.when(count_ref[e] > 0)
    def _():
        wgu_s[...] = wgu_ref[0].astype(BF16)
        wd_s[...] = wd_ref[0].astype(BF16)

    def block(j, carry):
        g = first_ref[e] + j
        slot = g % 2
        x_copy(g, slot).wait()

        @pl.when(g + 1 < n_used)
        def _():
            x_copy(g + 1, 1 - slot).start()

        @pl.when(g >= 2)
        def _():
            y_copy(g - 2, slot).wait()

        gu = _dot(_tiles_to_rows(xbuf.at[slot], MOE_BM).astype(BF16), wgu_s[...]) + bgu_ref[0]
        gate = jnp.minimum(gu[:, :D_FF], SWIGLU_LIMIT)
        up = jnp.clip(gu[:, D_FF:], -SWIGLU_LIMIT, SWIGLU_LIMIT)
        act = (up + 1.0) * gate * _sigmoid(gate * SWIGLU_ALPHA)
        _rows_to_tiles(ybuf.at[slot], _dot(act.astype(BF16), wd_s[...]) + bd_ref[0])
        y_copy(g, slot).start()
        return carry

    lax.fori_loop(0, count_ref[e], block, 0)

    @pl.when(e == N_EXPERTS - 1)
    def _():
        for back in (2, 1):
            @pl.when(n_used >= back)
            def _():
                y_copy(n_used - back, (n_used - back) % 2).wait()

        ybuf[0] = jnp.zeros_like(ybuf[0])
        lax.fori_loop(n_used, MOE_BLOCKS, lambda g, c: (y_copy(g, 0).start(), c)[1], 0)
        lax.fori_loop(n_used, MOE_BLOCKS, lambda g, c: (y_copy(g, 0).wait(), c)[1], 0)


def _experts(first_block, n_blocks, n_used, rows, wgu, bgu, wd, bd):
    by_expert = lambda shape: pl.BlockSpec((1,) + shape, lambda e, fb, nb, nu: (e, 0, 0))
    grid_spec = pltpu.PrefetchScalarGridSpec(
        num_scalar_prefetch=3,
        grid=(N_EXPERTS,),
        in_specs=[
            pl.BlockSpec(memory_space=pl.ANY),
            by_expert((D_MODEL, 2 * D_FF)), by_expert((1, 2 * D_FF)),
            by_expert((D_FF, D_MODEL)), by_expert((1, D_MODEL)),
        ],
        out_specs=pl.BlockSpec(memory_space=pl.ANY),
        scratch_shapes=[pltpu.VMEM((D_MODEL, 2 * D_FF), BF16), pltpu.VMEM((D_FF, D_MODEL), BF16),
                        pltpu.VMEM((2,) + _tiled(MOE_BM), U32), pltpu.VMEM((2,) + _tiled(MOE_BM), U32),
                        pltpu.SemaphoreType.DMA((2,)), pltpu.SemaphoreType.DMA((2,))],
    )
    return pl.pallas_call(
        _expert_kernel,
        grid_spec=grid_spec,
        out_shape=jax.ShapeDtypeStruct(_tiled(MOE_ROWS), U32),
        compiler_params=_params(),
        name="moe_experts",
    )(first_block, n_blocks, n_used, rows, wgu, bgu, wd, bd)


def _combine_kernel(dest_ref, dest_next_ref, gate_ref, xmid_ref, g_ref, rows_ref, yp_ref, ys_ref, buf_ref, sem):
    i = pl.program_id(0)
    slot = i % 2

    def issue(idx_ref, s):
        def body(r, carry):
            for k in range(TOP_K):
                d = idx_ref[r * TOP_K + k]
                pltpu.make_async_copy(_tile_of(rows_ref, d), _tile_of(buf_ref.at[s], k * TM + r),
                                      sem.at[s]).start(priority=k % 2)
            return carry

        lax.fori_loop(0, TM, body, 0, unroll=ISSUE_UNROLL)

    @pl.when(i == 0)
    def _():
        issue(dest_ref, 0)

    @pl.when(i + 1 < N_TILES)
    def _():
        issue(dest_next_ref, 1 - slot)

    buf = buf_ref.at[slot]
    slot_tiles = _tiled(TM)[0]
    for k in range(TOP_K):
        pltpu.make_async_copy(rows_ref.at[pl.ds(0, slot_tiles)], buf.at[pl.ds(k * slot_tiles, slot_tiles)],
                              sem.at[slot]).wait()
    acc = xmid_ref[...]
    for k in range(TOP_K):
        acc = acc + _tiles_to_rows(buf, TM, first=k * TM) * gate_ref[:, k:k + 1]
    y = _rms(acc, g_ref[...])

    @pl.when(i < N_PROMPT_TILES)
    def _():
        yp_ref[...] = y

    @pl.when(i == N_PROMPT_TILES)
    def _():
        ys_ref[...] = y


def _combine(dest_flat, gates, xmid, g, y_rows):
    return pl.pallas_call(
        _combine_kernel,
        grid=(N_TILES,),
        in_specs=[
            pl.BlockSpec((TM * TOP_K,), lambda i: (i,), memory_space=pltpu.SMEM),
            pl.BlockSpec((TM * TOP_K,), lambda i: (jnp.minimum(i + 1, N_TILES - 1),), memory_space=pltpu.SMEM),
            pl.BlockSpec((TM, TOP_K), lambda i: (i, 0)),
            pl.BlockSpec((TM, D_MODEL), lambda i: (i, 0)),
            _const_spec((1, D_MODEL)),
            pl.BlockSpec(memory_space=pl.ANY),
        ],
        out_specs=[
            pl.BlockSpec((TM, D_MODEL), lambda i: (jnp.minimum(i, N_PROMPT_TILES - 1), 0)),
            pl.BlockSpec((TM, D_MODEL), lambda i: (0, 0)),
        ],
        out_shape=[
            jax.ShapeDtypeStruct((N_PROMPT, D_MODEL), F32),
            jax.ShapeDtypeStruct((N_SAMPLE, D_MODEL), F32),
        ],
        scratch_shapes=[pltpu.VMEM((2,) + _tiled(TOP_K * TM), U32), pltpu.SemaphoreType.DMA((2,))],
        compiler_params=_params(),
        name="moe_combine",
    )(dest_flat, dest_flat, gates, xmid, g, y_rows)


def _band_bias(table):
    n_diag = BAND + CHUNK - 1
    idx = np.clip(ATT_LEFT + (CHUNK - 1) - np.arange(n_diag), -REL_CLIP, REL_CLIP) + REL_CLIP
    pick = (np.arange(2 * REL_CLIP + 1)[:, None] == idx[None, :]).astype(np.float32)
    diag = jnp.dot(table, jnp.asarray(pick), precision=lax.Precision.HIGHEST)
    return jnp.stack([diag[:, CHUNK - 1 - qi:CHUNK - 1 - qi + BAND] for qi in range(CHUNK)], axis=1)


def _pair_rows(v):
    return jnp.repeat(v.reshape(HEAD_PAIRS, 2), CHUNK, axis=1)


def _layer(l, xp, xs, cache_k, cache_v, state_conv, state_ssm,
           norm_mix_g, w_in, conv_w, conv_b, dt_bias, a_log, d_skip, ssm_norm_g,
           att_norm_g, rel_bias_table, w_out, norm_ffn_g, w_router, b_router,
           w_gate_up, b_gate_up, w_down, b_down, norm_final_g):
    wb = w_in[l].astype(BF16)
    c0 = D_SSM
    c1 = c0 + D_CONV
    c2 = c1 + SSM_HEADS
    c3 = c2 + D_ATT
    c4 = c3 + D_ATT
    z, xbc, dt, dtt, q, k, v, k_p, v_p, k_s, v_s, ctail = _inproj(
        xp, xs, norm_mix_g[l][None], wb[:, :c0], wb[:, c0:c1], wb[:, c1:c2], wb[:, c1:c2].T,
        wb[:, c2:c3], wb[:, c3:c4], wb[:, c4:])

    n_chunks = N_TOK // CHUNK
    dtp = dtt.reshape(HEAD_PAIRS, 2, n_chunks, CHUNK).transpose(2, 0, 1, 3).reshape(n_chunks, HEAD_PAIRS, LANES)
    hp = jnp.arange(D_SSM) // SSM_HEAD_DIM
    expand = (hp[None, :] == jnp.arange(SSM_HEADS)[:, None]).astype(BF16)
    lane = jnp.arange(LANES)
    triu2 = ((lane[:, None] // CHUNK == lane[None, :] // CHUNK) & (lane[:, None] <= lane[None, :])).astype(BF16)
    consts = (conv_w[l], conv_b[l][None], dt_bias[l][None], _pair_rows(dt_bias[l]),
              a_log[l][None], _pair_rows(a_log[l]),
              jnp.repeat(d_skip[l], SSM_HEAD_DIM)[None], ssm_norm_g[l][None],
              expand, triu2)
    y_ssm_p, ssm_p = _ssd_prompt(z, xbc, dt, dtp, consts)
    y_ssm_s, ssm_s = _ssd_sample(z, xbc, dt, dtp, state_conv[l],
                                 state_ssm[l].reshape(DEC_BATCH, D_SSM, D_STATE), consts)

    bias2 = _band_bias(rel_bias_table[l]).reshape(ATT_PAIRS, 2 * CHUNK, BAND)
    o_att_p = _attn_prompt(q, k, v, bias2)
    o_att_s = _attn_sample(q, k, v, cache_k[l].reshape(DEC_BATCH, ATT_LEFT, D_ATT),
                           cache_v[l].reshape(DEC_BATCH, ATT_LEFT, D_ATT), bias2)

    wo = w_out[l].astype(BF16)
    wr = w_router[l].T
    wr1 = wr.astype(BF16)
    wr2 = (wr - wr1.astype(F32)).astype(BF16)
    earlier = jnp.triu(jnp.ones((TM, TM), BF16), 1)
    xmid, h, top_idx, gates, rank, counts = _outproj(
        xp, xs, y_ssm_p, y_ssm_s, o_att_p, o_att_s, att_norm_g[l][None], wo[:D_SSM], wo[D_SSM:], norm_ffn_g[l][None],
        jnp.concatenate([wr1, wr2], axis=0), b_router[l][:, None], earlier)
    gates = gates[:TOP_K].T

    counts = counts[:, 0].astype(jnp.int32)
    padded = (counts + MOE_BM - 1) // MOE_BM * MOE_BM
    pad_end = jnp.cumsum(padded)
    pad_start = pad_end - padded
    experts = jnp.arange(N_EXPERTS, dtype=jnp.int32)
    start_of = jnp.sum(jnp.where(top_idx[:TOP_K, :, None] == experts, pad_start, 0), axis=-1)
    dest = (start_of + rank[:TOP_K]).T.reshape(-1).astype(jnp.int32)
    first_block = (pad_start // MOE_BM).astype(jnp.int32)
    n_blocks = (padded // MOE_BM).astype(jnp.int32)
    n_used = (pad_end[-1:] // MOE_BM).astype(jnp.int32)

    rows = _scatter_rows(dest, pad_end.astype(jnp.int32), h)
    y_rows = _experts(first_block, n_blocks, n_used, rows, w_gate_up[l], b_gate_up[l][:, None, :],
                      w_down[l], b_down[l][:, None, :])
    y_p, y_s = _combine(dest, gates, xmid, norm_final_g[None], y_rows)

    keep = min(ATT_LEFT, SEQ)
    k_p = k_p[:, TM - keep:].reshape(BATCH, keep, ATT_HEADS, ATT_HEAD_DIM)
    v_p = v_p[:, TM - keep:].reshape(BATCH, keep, ATT_HEADS, ATT_HEAD_DIM)
    k_s = k_s.reshape(DEC_BATCH, DEC_SEQ, ATT_HEADS, ATT_HEAD_DIM)
    v_s = v_s.reshape(DEC_BATCH, DEC_SEQ, ATT_HEADS, ATT_HEAD_DIM)
    conv_p = ctail[:BATCH, -(CONV_W - 1):]
    conv_s = ctail[BATCH].reshape(DEC_BATCH, SUBLANES, D_CONV)[:, -(CONV_W - 1):]
    ssm_p = ssm_p.reshape(BATCH, SSM_HEADS, SSM_HEAD_DIM, D_STATE)
    ssm_s = ssm_s.reshape(DEC_BATCH, SSM_HEADS, SSM_HEAD_DIM, D_STATE)
    return (y_p.reshape(BATCH, SEQ, D_MODEL), y_s.reshape(DEC_BATCH, DEC_SEQ, D_MODEL),
            k_p, v_p, conv_p, ssm_p, k_s, v_s, conv_s, ssm_s)


def kernel(x_prompt, x_sample, cache_k, cache_v, state_conv, state_ssm, norm_mix_g, w_in, conv_w, conv_b,
           dt_bias, a_log, d_skip, ssm_norm_g, att_norm_g, rel_bias_table, w_out, norm_ffn_g, w_router,
           b_router, w_gate_up, b_gate_up, w_down, b_down, norm_final_g):
    assert w_in.shape[0] == 1, "single trunk layer"
    xp = x_prompt.reshape(N_PROMPT, D_MODEL)
    xs = x_sample.reshape(N_SAMPLE, D_MODEL)
    outs = _layer(0, xp, xs, cache_k, cache_v, state_conv, state_ssm,
                  norm_mix_g, w_in, conv_w, conv_b, dt_bias, a_log, d_skip, ssm_norm_g,
                  att_norm_g, rel_bias_table, w_out, norm_ffn_g, w_router, b_router,
                  w_gate_up, b_gate_up, w_down, b_down, norm_final_g)
    y_p, y_s, k_p, v_p, conv_p, ssm_p, k_s, v_s, conv_s, ssm_s = outs
    return (y_p, y_s, k_p[None], v_p[None], conv_p[None], ssm_p[None],
            k_s[None], v_s[None], conv_s[None], ssm_s[None])
```

```python
import jax
import jax.numpy as jnp
import numpy as np
from jax import lax
from jax.experimental import pallas as pl
from jax.experimental.pallas import tpu as pltpu

D_MODEL = 1024
BATCH = 8
SEQ = 2048
DEC_BATCH = 8
DEC_SEQ = 64
CHUNK = 64
SSM_HEADS = 16
SSM_HEAD_DIM = 64
D_SSM = SSM_HEADS * SSM_HEAD_DIM
SSM_GROUPS = 2
D_STATE = 128
CONV_W = 4
D_BC = SSM_GROUPS * D_STATE
D_CONV = D_SSM + 2 * D_BC
ATT_HEADS = 8
ATT_HEAD_DIM = 64
D_ATT = ATT_HEADS * ATT_HEAD_DIM
LEFT_CHUNKS = 8
ATT_LEFT = LEFT_CHUNKS * CHUNK
BAND = ATT_LEFT + CHUNK
REL_CLIP = 128
ATT_SCALE = ATT_HEAD_DIM ** -0.5
N_EXPERTS = 32
TOP_K = 4
D_FF = D_MODEL
SWIGLU_ALPHA = 1.702
SWIGLU_LIMIT = 7.0
EPS = 1e-5

F32 = jnp.float32
BF16 = jnp.bfloat16
U32 = jnp.uint32

LANES = 128
SUBLANES = 8

N_PROMPT = BATCH * SEQ
N_SAMPLE = DEC_BATCH * DEC_SEQ
N_TOK = N_PROMPT + N_SAMPLE
TM = 512
N_PROMPT_TILES = N_PROMPT // TM
N_TILES = N_TOK // TM
TILES_PER_SEQ = SEQ // TM
CHUNKS_PER_TILE = TM // CHUNK
N_CHUNKS_SEQ = SEQ // CHUNK
assert 2 * SSM_HEAD_DIM == LANES and 2 * ATT_HEAD_DIM == LANES
HEAD_PAIRS = SSM_HEADS // 2
PAIRS_PER_GROUP = HEAD_PAIRS // SSM_GROUPS
ATT_PAIRS = ATT_HEADS // 2
SSD_BLOCK = 2
CONV_HEAD = SUBLANES
MOE_BM = 512
N_ASSIGN = N_TOK * TOP_K
MOE_BLOCKS = N_ASSIGN // MOE_BM + N_EXPERTS
MOE_ROWS = MOE_BLOCKS * MOE_BM
ISSUE_UNROLL = 8
ROW_TILE = (SUBLANES // 2, LANES)
assert 2 * ROW_TILE[0] * ROW_TILE[1] == D_MODEL
VMEM_BYTES = 64 * 1024 * 1024
VMEM_LIMIT = VMEM_BYTES * 7 // 8


def _dot(a, b):
    return jnp.dot(a, b, preferred_element_type=F32)


def _dot_nt(a, b):
    return lax.dot_general(a, b, (((1,), (1,)), ((), ())), preferred_element_type=F32)


def _dot_tn(a, b):
    return lax.dot_general(a, b, (((0,), (0,)), ((), ())), preferred_element_type=F32)


def _split3(x):
    x1 = x.astype(BF16)
    r1 = x - x1.astype(F32)
    x2 = r1.astype(BF16)
    r2 = r1 - x2.astype(F32)
    return x1, x2, r2.astype(BF16)


def _dot_exact_rhs(x, m):
    x1, x2, x3 = _split3(x)
    return _dot(x1, m) + _dot(x2, m) + _dot(x3, m)


def _dot_exact_lhs(m, x):
    x1, x2, x3 = _split3(x)
    return _dot(m, x1) + _dot(m, x2) + _dot(m, x3)


def _rms(x, g):
    return x * lax.rsqrt(jnp.mean(x * x, axis=-1, keepdims=True) + EPS) * g


def _sigmoid(x):
    return 1.0 / (1.0 + jnp.exp(-x))


def _softplus(x):
    return jnp.maximum(x, 0.0) + jnp.log(1.0 + jnp.exp(-jnp.abs(x)))


def _tiled(n):
    return (n * ROW_TILE[0], ROW_TILE[1])


def _tile_of(ref, row):
    return ref.at[pl.ds(pl.multiple_of(row * ROW_TILE[0], ROW_TILE[0]), ROW_TILE[0])]


def _rows_to_tiles(ref, x, first=0):
    sub, lanes = ROW_TILE
    half = D_MODEL // 2
    hi = lax.bitcast_convert_type(x[:, :half].astype(BF16).astype(F32), U32)
    lo = lax.bitcast_convert_type(x[:, half:].astype(BF16).astype(F32), U32)
    words = hi | (lo >> 16)
    for j in range(sub):
        ref[pl.ds(first * sub + j, x.shape[0], stride=sub), :] = words[:, j * lanes:(j + 1) * lanes]


def _tiles_to_rows(ref, n, first=0):
    sub = ROW_TILE[0]
    words = jnp.concatenate([ref[pl.ds(first * sub + j, n, stride=sub), :] for j in range(sub)], axis=1)
    hi = lax.bitcast_convert_type(words & jnp.uint32(0xFFFF0000), F32)
    lo = lax.bitcast_convert_type(words << 16, F32)
    return jnp.concatenate([hi, lo], axis=1)


def _const_spec(shape):
    nd = len(shape)
    return pl.BlockSpec(shape, lambda *_: (0,) * nd)


def _params(n_axes=1):
    return pltpu.CompilerParams(dimension_semantics=("arbitrary",) * n_axes,
                                vmem_limit_bytes=VMEM_LIMIT)


def _inproj_kernel(xp_ref, xs_ref, g_ref, wz_ref, wxbc_ref, wdt_ref, wdtt_ref, wq_ref, wk_ref, wv_ref,
                   z_ref, xbc_ref, dt_ref, dtt_ref, q_ref, k_ref, v_ref, kp_ref, vp_ref, ks_ref, vs_ref, ctail_ref):
    i = pl.program_id(0)
    x = jnp.where(i == N_PROMPT_TILES, xs_ref[...], xp_ref[...])
    h = _rms(x, g_ref[...]).astype(BF16)
    z_ref[...] = _dot(h, wz_ref[...]).astype(BF16)
    xbc = _dot(h, wxbc_ref[...])
    xbc_ref[...] = xbc.astype(BF16)
    for c in range(CHUNKS_PER_TILE):
        ctail_ref[0, c * SUBLANES:(c + 1) * SUBLANES, :] = xbc[(c + 1) * CHUNK - SUBLANES:(c + 1) * CHUNK, :]
    dt_ref[...] = _dot(h, wdt_ref[...])
    dtt_ref[...] = _dot_nt(wdtt_ref[...], h)
    q_ref[...] = (_dot(h, wq_ref[...]) * ATT_SCALE).astype(BF16)
    k = _dot(h, wk_ref[...])
    v = _dot(h, wv_ref[...])
    k_ref[...] = k.astype(BF16)
    v_ref[...] = v.astype(BF16)

    @pl.when(i < N_PROMPT_TILES)
    def _():
        kp_ref[0] = k
        vp_ref[0] = v

    @pl.when(i == N_PROMPT_TILES)
    def _():
        ks_ref[...] = k
        vs_ref[...] = v


def _inproj(xp, xs, g, wz, wxbc, wdt, wdtt, wq, wk, wv):
    tok = lambda n: pl.BlockSpec((TM, n), lambda i: (i, 0))
    tail_idx = lambda i: (i // TILES_PER_SEQ, 0, 0)
    seq_idx = lambda i: (jnp.minimum(i // TILES_PER_SEQ, BATCH - 1), 0, 0)
    n_tail = BATCH + 1
    return pl.pallas_call(
        _inproj_kernel,
        grid=(N_TILES,),
        in_specs=[
            pl.BlockSpec((TM, D_MODEL), lambda i: (jnp.minimum(i, N_PROMPT_TILES - 1), 0)),
            pl.BlockSpec((TM, D_MODEL), lambda i: (0, 0)),
            _const_spec((1, D_MODEL)),
            _const_spec((D_MODEL, D_SSM)),
            _const_spec((D_MODEL, D_CONV)),
            _const_spec((D_MODEL, SSM_HEADS)),
            _const_spec((SSM_HEADS, D_MODEL)),
            _const_spec((D_MODEL, D_ATT)),
            _const_spec((D_MODEL, D_ATT)),
            _const_spec((D_MODEL, D_ATT)),
        ],
        out_specs=[
            tok(D_SSM), tok(D_CONV), tok(SSM_HEADS),
            pl.BlockSpec((SSM_HEADS, TM), lambda i: (0, i)),
            tok(D_ATT), tok(D_ATT), tok(D_ATT),
            pl.BlockSpec((1, TM, D_ATT), seq_idx),
            pl.BlockSpec((1, TM, D_ATT), seq_idx),
            pl.BlockSpec((TM, D_ATT), lambda i: (0, 0)),
            pl.BlockSpec((TM, D_ATT), lambda i: (0, 0)),
            pl.BlockSpec((1, CHUNKS_PER_TILE * SUBLANES, D_CONV), tail_idx),
        ],
        out_shape=[
            jax.ShapeDtypeStruct((N_TOK, D_SSM), BF16),
            jax.ShapeDtypeStruct((N_TOK, D_CONV), BF16),
            jax.ShapeDtypeStruct((N_TOK, SSM_HEADS), F32),
            jax.ShapeDtypeStruct((SSM_HEADS, N_TOK), F32),
            jax.ShapeDtypeStruct((N_TOK, D_ATT), BF16),
            jax.ShapeDtypeStruct((N_TOK, D_ATT), BF16),
            jax.ShapeDtypeStruct((N_TOK, D_ATT), BF16),
            jax.ShapeDtypeStruct((BATCH, TM, D_ATT), F32),
            jax.ShapeDtypeStruct((BATCH, TM, D_ATT), F32),
            jax.ShapeDtypeStruct((N_SAMPLE, D_ATT), F32),
            jax.ShapeDtypeStruct((N_SAMPLE, D_ATT), F32),
            jax.ShapeDtypeStruct((n_tail, CHUNKS_PER_TILE * SUBLANES, D_CONV), F32),
        ],
        compiler_params=_params(),
        name="inproj",
    )(xp, xs, g, wz, wxbc, wdt, wdtt, wq, wk, wv)


def _ssd_tile(n_chunks, z_ref, xbc_ref, dt_ref, dtp_ref, cw_ref, cb_ref, dtb_ref, dtbp_ref,
              alog_h_ref, alog_p_ref, dskip_e_ref, ng_ref, expand_ref, triu2_ref,
              y_ref, xw_ref, state_ref):
    nb = SSD_BLOCK if n_chunks % SSD_BLOCK == 0 else 1
    rb = nb * CHUNK
    half = D_SSM // SSM_GROUPS
    xw_ref[CONV_HEAD:CONV_HEAD + n_chunks * CHUNK, :] = xbc_ref[...].astype(F32)

    a_h = -jnp.exp(alog_h_ref[...])
    a_p = jnp.concatenate([-jnp.exp(alog_p_ref[...])] * nb, axis=0)
    dtb_p = jnp.concatenate([dtbp_ref[...]] * nb, axis=0)
    tr_r = lax.broadcasted_iota(jnp.int32, (rb, rb), 0)
    tr_c = lax.broadcasted_iota(jnp.int32, (rb, rb), 1)
    tril_b = jnp.where((tr_r // CHUNK == tr_c // CHUNK) & (tr_c <= tr_r), 1.0, 0.0).astype(BF16)
    row_i = lax.broadcasted_iota(jnp.int32, (CHUNK, LANES), 0)
    col_i = lax.broadcasted_iota(jnp.int32, (CHUNK, LANES), 1)
    causal2 = row_i >= (col_i % CHUNK)
    sel_r = lax.broadcasted_iota(jnp.int32, ((CONV_W - 1) * rb, CONV_HEAD + rb), 0)
    sel_c = lax.broadcasted_iota(jnp.int32, ((CONV_W - 1) * rb, CONV_HEAD + rb), 1)
    shift_sel = (sel_c == sel_r % rb + sel_r // rb + CONV_HEAD - (CONV_W - 1)).astype(F32)
    bd_r = lax.broadcasted_iota(jnp.int32, (LANES, LANES), 0) // CHUNK
    bd_c = lax.broadcasted_iota(jnp.int32, (LANES, LANES), 1) // CHUNK
    blockdiag = bd_r == bd_c

    def block(i, carry):
        r0 = pl.multiple_of(i * rb, rb)
        win = xw_ref[pl.ds(r0, CONV_HEAD + rb), :]
        shifted = _dot(shift_sel, win)
        acc = cb_ref[...] + cw_ref[CONV_W - 1:CONV_W, :] * win[CONV_HEAD:, :]
        for tap in range(CONV_W - 1):
            acc = acc + cw_ref[tap:tap + 1, :] * shifted[tap * rb:(tap + 1) * rb, :]
        xa = acc * _sigmoid(acc)
        xs = xa[:, 0:D_SSM]
        bm = xa[:, D_SSM:D_SSM + D_BC].astype(BF16)
        cm = xa[:, D_SSM + D_BC:D_CONV].astype(BF16)
        dt = _softplus(dt_ref[pl.ds(r0, rb), :] + dtb_ref[...])
        dt_e = _dot_exact_rhs(dt, expand_ref[...])
        acum = _dot_exact_rhs(_dot_exact_lhs(tril_b, dt * a_h), expand_ref[...])
        dtp = _softplus(dtp_ref[pl.ds(i * nb, nb)].reshape(nb * HEAD_PAIRS, LANES) + dtb_p)
        acum_p = _dot_exact_rhs(dtp * a_p, triu2_ref[...])
        xdt = xs * dt_e
        a_last = [acum[(c + 1) * CHUNK - 1:(c + 1) * CHUNK, :] for c in range(nb)]
        a_end = jnp.concatenate([jnp.broadcast_to(a, (CHUNK, D_SSM)) for a in a_last], axis=0)
        xdt_end = (xdt * jnp.exp(a_end - acum)).astype(BF16)

        y_diag, new_s = [], []
        for c in range(nb):
            rs = slice(c * CHUNK, (c + 1) * CHUNK)
            y_parts = []
            for g in range(SSM_GROUPS):
                bg = bm[rs, g * D_STATE:(g + 1) * D_STATE]
                cg = cm[rs, g * D_STATE:(g + 1) * D_STATE]
                cb2 = _dot_nt(cg, jnp.concatenate([bg, bg], axis=0))
                for jj in range(PAIRS_PER_GROUP):
                    j = g * PAIRS_PER_GROUP + jj
                    seg = acum[rs, j * LANES:(j + 1) * LANES] - acum_p[c * HEAD_PAIRS + j:c * HEAD_PAIRS + j + 1, :]
                    decay = jnp.exp(jnp.where(causal2, seg, -jnp.inf))
                    x2 = xdt[rs, j * LANES:(j + 1) * LANES]
                    rhs = jnp.where(blockdiag, jnp.concatenate([x2, x2], axis=0), 0.0).astype(BF16)
                    y_parts.append(_dot((cb2 * decay).astype(BF16), rhs))
            y_diag.append(jnp.concatenate(y_parts, axis=1))
            new_s.append(jnp.concatenate(
                [_dot_tn(bm[rs, g * D_STATE:(g + 1) * D_STATE], xdt_end[rs, g * half:(g + 1) * half])
                 for g in range(SSM_GROUPS)], axis=1))

        y_off = []
        for c in range(nb):
            rs = slice(c * CHUNK, (c + 1) * CHUNK)
            state = state_ref[...]
            y_off.append(jnp.concatenate(
                [_dot(cm[rs, g * D_STATE:(g + 1) * D_STATE], state[:, g * half:(g + 1) * half].astype(BF16))
                 for g in range(SSM_GROUPS)], axis=1))
            state_ref[...] = state * jnp.exp(a_last[c]) + new_s[c]

        y = (jnp.concatenate(y_diag, axis=0) + jnp.concatenate(y_off, axis=0) * jnp.exp(acum)
             + dskip_e_ref[...] * xs)
        zc = z_ref[pl.ds(r0, rb), :].astype(F32)
        y = y * (zc * _sigmoid(zc))
        yn = jnp.concatenate(
            [y[:, g * half:(g + 1) * half]
             * lax.rsqrt(jnp.mean(jnp.square(y[:, g * half:(g + 1) * half]), axis=-1, keepdims=True) + EPS)
             for g in range(SSM_GROUPS)], axis=1)
        y_ref[pl.ds(r0, rb), :] = (yn * ng_ref[...]).astype(BF16)
        return carry

    lax.fori_loop(0, n_chunks // nb, block, 0)


def _state_store(state_ref, out_ref):
    for j in range(HEAD_PAIRS):
        out_ref[0, j * LANES:(j + 1) * LANES, :] = state_ref[:, j * LANES:(j + 1) * LANES].T


def _ssd_prompt_kernel(z_ref, xbc_ref, dt_ref, dtp_ref, cw_ref, cb_ref, dtb_ref, dtbp_ref,
                       alog_h_ref, alog_p_ref, dskip_e_ref, ng_ref, expand_ref, triu2_ref,
                       y_ref, ssm_ref, xw_ref, state_ref, tail_ref):
    t = pl.program_id(1)

    @pl.when(t == 0)
    def _():
        state_ref[...] = jnp.zeros_like(state_ref)
        xw_ref[0:CONV_HEAD, :] = jnp.zeros((CONV_HEAD, D_CONV), F32)

    @pl.when(t > 0)
    def _():
        xw_ref[0:CONV_HEAD, :] = tail_ref[...]

    _ssd_tile(CHUNKS_PER_TILE, z_ref, xbc_ref, dt_ref, dtp_ref, cw_ref, cb_ref, dtb_ref, dtbp_ref,
              alog_h_ref, alog_p_ref, dskip_e_ref, ng_ref, expand_ref, triu2_ref,
              y_ref, xw_ref, state_ref)
    tail_ref[...] = xw_ref[TM:TM + CONV_HEAD, :]

    @pl.when(t == TILES_PER_SEQ - 1)
    def _():
        _state_store(state_ref, ssm_ref)


def _ssd_sample_kernel(z_ref, xbc_ref, dt_ref, dtp_ref, cprev_ref, sprev_ref,
                       cw_ref, cb_ref, dtb_ref, dtbp_ref,
                       alog_h_ref, alog_p_ref, dskip_e_ref, ng_ref, expand_ref, triu2_ref,
                       y_ref, ssm_ref, xw_ref, state_ref):
    xw_ref[0:CONV_HEAD, :] = jnp.zeros((CONV_HEAD, D_CONV), F32)
    xw_ref[CONV_HEAD - (CONV_W - 1):CONV_HEAD, :] = cprev_ref[0]
    for j in range(HEAD_PAIRS):
        state_ref[:, j * LANES:(j + 1) * LANES] = sprev_ref[0, j * LANES:(j + 1) * LANES, :].T
    _ssd_tile(1, z_ref, xbc_ref, dt_ref, dtp_ref, cw_ref, cb_ref, dtb_ref, dtbp_ref,
              alog_h_ref, alog_p_ref, dskip_e_ref, ng_ref, expand_ref, triu2_ref,
              y_ref, xw_ref, state_ref)
    _state_store(state_ref, ssm_ref)


def _ssd_const_specs():
    return [
        _const_spec((CONV_W, D_CONV)), _const_spec((1, D_CONV)),
        _const_spec((1, SSM_HEADS)), _const_spec((HEAD_PAIRS, LANES)),
        _const_spec((1, SSM_HEADS)), _const_spec((HEAD_PAIRS, LANES)),
        _const_spec((1, D_SSM)), _const_spec((1, D_SSM)),
        _const_spec((SSM_HEADS, D_SSM)), _const_spec((LANES, LANES)),
    ]


def _ssd_prompt(z, xbc, dt, dtp, consts):
    tile = lambda b, t: (b * TILES_PER_SEQ + t, 0)
    return pl.pallas_call(
        _ssd_prompt_kernel,
        grid=(BATCH, TILES_PER_SEQ),
        in_specs=[
            pl.BlockSpec((TM, D_SSM), tile),
            pl.BlockSpec((TM, D_CONV), tile),
            pl.BlockSpec((TM, SSM_HEADS), tile),
            pl.BlockSpec((CHUNKS_PER_TILE, HEAD_PAIRS, LANES), lambda b, t: (b * TILES_PER_SEQ + t, 0, 0)),
        ] + _ssd_const_specs(),
        out_specs=[
            pl.BlockSpec((TM, D_SSM), tile),
            pl.BlockSpec((1, D_SSM, D_STATE), lambda b, t: (b, 0, 0)),
        ],
        out_shape=[
            jax.ShapeDtypeStruct((N_PROMPT, D_SSM), BF16),
            jax.ShapeDtypeStruct((BATCH, D_SSM, D_STATE), F32),
        ],
        scratch_shapes=[
            pltpu.VMEM((CONV_HEAD + TM, D_CONV), F32),
            pltpu.VMEM((D_STATE, D_SSM), F32),
            pltpu.VMEM((CONV_HEAD, D_CONV), F32),
        ],
        compiler_params=_params(2),
        name="ssd_prompt",
    )(z, xbc, dt, dtp, *consts)


def _ssd_sample(z, xbc, dt, dtp, conv_prev, ssm_prev, consts):
    first = N_PROMPT // CHUNK
    row = lambda b: (first + b, 0)
    return pl.pallas_call(
        _ssd_sample_kernel,
        grid=(DEC_BATCH,),
        in_specs=[
            pl.BlockSpec((CHUNK, D_SSM), row),
            pl.BlockSpec((CHUNK, D_CONV), row),
            pl.BlockSpec((CHUNK, SSM_HEADS), row),
            pl.BlockSpec((1, HEAD_PAIRS, LANES), lambda b: (first + b, 0, 0)),
            pl.BlockSpec((1, CONV_W - 1, D_CONV), lambda b: (b, 0, 0)),
            pl.BlockSpec((1, D_SSM, D_STATE), lambda b: (b, 0, 0)),
        ] + _ssd_const_specs(),
        out_specs=[
            pl.BlockSpec((CHUNK, D_SSM), lambda b: (b, 0)),
            pl.BlockSpec((1, D_SSM, D_STATE), lambda b: (b, 0, 0)),
        ],
        out_shape=[
            jax.ShapeDtypeStruct((N_SAMPLE, D_SSM), BF16),
            jax.ShapeDtypeStruct((DEC_BATCH, D_SSM, D_STATE), F32),
        ],
        scratch_shapes=[
            pltpu.VMEM((CONV_HEAD + CHUNK, D_CONV), F32),
            pltpu.VMEM((D_STATE, D_SSM), F32),
        ],
        compiler_params=_params(),
        name="ssd_sample",
    )(z, xbc, dt, dtp, conv_prev, ssm_prev, *consts)


def _attn_chunks(n_chunks, first_chunk, q_ref, kpad_ref, vpad_ref, bias_ref, o_ref):
    lane = lax.broadcasted_iota(jnp.int32, (CHUNK, LANES), 1)
    low = lane < ATT_HEAD_DIM
    kj = lax.broadcasted_iota(jnp.int32, (2 * CHUNK, BAND), 1)

    def chunk(c, carry, masked):
        r0 = pl.multiple_of(c * CHUNK, CHUNK)
        if masked:
            valid = kj >= (LEFT_CHUNKS - (first_chunk + c)) * CHUNK
        scores = []
        for j in range(ATT_PAIRS):
            qp = q_ref[pl.ds(r0, CHUNK), j * LANES:(j + 1) * LANES]
            zero = jnp.zeros_like(qp)
            q2 = jnp.concatenate([jnp.where(low, qp, zero), jnp.where(low, zero, qp)], axis=0)
            kb = kpad_ref[pl.ds(r0, BAND), j * LANES:(j + 1) * LANES]
            s = _dot_nt(q2, kb) + bias_ref[j]
            scores.append(jnp.where(valid, s, -jnp.inf) if masked else s)
        probs = []
        for s in scores:
            e = jnp.exp(s - jnp.max(s, axis=-1, keepdims=True))
            probs.append((e.astype(BF16), jnp.sum(e, axis=-1, keepdims=True)))
        outs = []
        for j, (e, denom) in enumerate(probs):
            vb = vpad_ref[pl.ds(r0, BAND), j * LANES:(j + 1) * LANES]
            r = _dot(e, vb) / denom
            outs.append(jnp.where(low, r[0:CHUNK], r[CHUNK:2 * CHUNK]))
        o_ref[pl.ds(r0, CHUNK), :] = jnp.concatenate(outs, axis=1).astype(BF16)
        return carry

    n_masked = min(max(LEFT_CHUNKS - first_chunk, 0), n_chunks)
    for lo, hi, masked in ((0, n_masked, True), (n_masked, n_chunks, False)):
        if hi > lo:
            lax.fori_loop(lo, hi, lambda c, carry, masked=masked: chunk(c, carry, masked), 0,
                          unroll=2 if (hi - lo) % 2 == 0 else 1)


def _attn_prompt_kernel(q_ref, k_ref, v_ref, bias_ref, o_ref, kpad_ref, vpad_ref):
    kpad_ref[0:ATT_LEFT, :] = jnp.zeros((ATT_LEFT, D_ATT), BF16)
    vpad_ref[0:ATT_LEFT, :] = jnp.zeros((ATT_LEFT, D_ATT), BF16)
    kpad_ref[ATT_LEFT:ATT_LEFT + SEQ, :] = k_ref[...]
    vpad_ref[ATT_LEFT:ATT_LEFT + SEQ, :] = v_ref[...]
    _attn_chunks(N_CHUNKS_SEQ, 0, q_ref, kpad_ref, vpad_ref, bias_ref, o_ref)


def _attn_sample_kernel(q_ref, k_ref, v_ref, ck_ref, cv_ref, bias_ref, o_ref, kpad_ref, vpad_ref):
    for pad_ref, cache_ref in ((kpad_ref, ck_ref), (vpad_ref, cv_ref)):
        for j in range(ATT_PAIRS):
            pair = jnp.concatenate([cache_ref[0, :, 2 * j, :], cache_ref[0, :, 2 * j + 1, :]], axis=1)
            pad_ref[0:ATT_LEFT, j * LANES:(j + 1) * LANES] = pair.astype(BF16)
    kpad_ref[ATT_LEFT:BAND, :] = k_ref[...]
    vpad_ref[ATT_LEFT:BAND, :] = v_ref[...]
    _attn_chunks(1, LEFT_CHUNKS, q_ref, kpad_ref, vpad_ref, bias_ref, o_ref)


def _attn_prompt(q, k, v, bias2):
    seq = pl.BlockSpec((SEQ, D_ATT), lambda b: (b, 0))
    return pl.pallas_call(
        _attn_prompt_kernel,
        grid=(BATCH,),
        in_specs=[seq, seq, seq, _const_spec((ATT_PAIRS, 2 * CHUNK, BAND))],
        out_specs=seq,
        out_shape=jax.ShapeDtypeStruct((N_PROMPT, D_ATT), BF16),
        scratch_shapes=[pltpu.VMEM((ATT_LEFT + SEQ, D_ATT), BF16),
                        pltpu.VMEM((ATT_LEFT + SEQ, D_ATT), BF16)],
        compiler_params=_params(),
        name="attn_prompt",
    )(q, k, v, bias2)


def _attn_sample(q, k, v, cache_k, cache_v, bias2):
    first = N_PROMPT // CHUNK
    row = pl.BlockSpec((CHUNK, D_ATT), lambda b: (first + b, 0))
    cache = pl.BlockSpec((1, ATT_LEFT, ATT_HEADS, ATT_HEAD_DIM), lambda b: (b, 0, 0, 0))
    return pl.pallas_call(
        _attn_sample_kernel,
        grid=(DEC_BATCH,),
        in_specs=[row, row, row, cache, cache, _const_spec((ATT_PAIRS, 2 * CHUNK, BAND))],
        out_specs=pl.BlockSpec((CHUNK, D_ATT), lambda b: (b, 0)),
        out_shape=jax.ShapeDtypeStruct((N_SAMPLE, D_ATT), BF16),
        scratch_shapes=[pltpu.VMEM((BAND, D_ATT), BF16), pltpu.VMEM((BAND, D_ATT), BF16)],
        compiler_params=_params(),
        name="attn_sample",
    )(q, k, v, cache_k, cache_v, bias2)


def _outproj_kernel(xp_ref, xs_ref, yp_ref, ys_ref, op_ref, os_ref, ag_ref, wos_ref, woa_ref, fg_ref,
                    wr_ref, br_ref, earlier_ref,
                    xmid_ref, h_ref, idx_ref, gate_ref, rank_ref, cnt_ref, carry_ref):
    i = pl.program_id(0)

    @pl.when(i == 0)
    def _():
        carry_ref[...] = jnp.zeros_like(carry_ref)

    is_sample = i == N_PROMPT_TILES
    x = jnp.where(is_sample, xs_ref[...], xp_ref[...])
    y = jnp.where(is_sample, ys_ref[...], yp_ref[...])
    o = jnp.where(is_sample, os_ref[...], op_ref[...])
    o = _rms(o.astype(F32), ag_ref[...]).astype(BF16)
    xm = x + _dot(y, wos_ref[...]) + _dot(o, woa_ref[...])
    xmid_ref[...] = xm
    h = _rms(xm, fg_ref[...])
    _rows_to_tiles(h_ref, h)
    h1 = h.astype(BF16)
    h2 = (h - h1.astype(F32)).astype(BF16)
    both = _dot_nt(wr_ref[...], h1)
    logits = both[:N_EXPERTS] + (both[N_EXPERTS:] + _dot_nt(wr_ref[0:N_EXPERTS, :], h2)) + br_ref[...]
    eidx = lax.broadcasted_iota(jnp.int32, (N_EXPERTS, TM), 0)
    slot = lax.broadcasted_iota(jnp.int32, (SUBLANES, TM), 0)
    work = logits
    vals, sels = [], []
    idx_out = jnp.zeros((SUBLANES, TM), jnp.int32)
    for k in range(TOP_K):
        m = jnp.max(work, axis=0, keepdims=True)
        idx = jnp.min(jnp.where(work == m, eidx, N_EXPERTS), axis=0, keepdims=True)
        sel = eidx == idx
        vals.append(m)
        sels.append(sel)
        idx_out = jnp.where(slot == k, idx, idx_out)
        work = jnp.where(sel, -jnp.inf, work)
    es = [jnp.exp(v - vals[0]) for v in vals]
    tot = es[0] + es[1] + es[2] + es[3]
    gate_out = jnp.zeros((SUBLANES, TM), F32)
    for k in range(TOP_K):
        gate_out = jnp.where(slot == k, es[k] / tot, gate_out)
    idx_ref[...] = idx_out
    gate_ref[...] = gate_out
    multi = jnp.zeros((N_EXPERTS, TM), F32)
    for sel in sels:
        multi = jnp.where(sel, 1.0, multi)
    before = _dot(multi.astype(BF16), earlier_ref[...]) + carry_ref[...]
    rank_out = jnp.zeros((SUBLANES, TM), jnp.int32)
    for k in range(TOP_K):
        rk = jnp.sum(jnp.where(sels[k], before, 0.0), axis=0, keepdims=True).astype(jnp.int32)
        rank_out = jnp.where(slot == k, rk, rank_out)
    rank_ref[...] = rank_out
    carry_ref[...] = carry_ref[...] + jnp.sum(multi, axis=1, keepdims=True)
    cnt_ref[...] = carry_ref[...]


def _outproj(xp, xs, yp, ys, op, os_, ag, wos, woa, fg, wr, br, earlier):
    tok = lambda n: pl.BlockSpec((TM, n), lambda i: (i, 0))
    slots = pl.BlockSpec((SUBLANES, TM), lambda i: (0, i))
    prompt = lambda n: pl.BlockSpec((TM, n), lambda i: (jnp.minimum(i, N_PROMPT_TILES - 1), 0))
    sample = lambda n: pl.BlockSpec((TM, n), lambda i: (0, 0))
    return pl.pallas_call(
        _outproj_kernel,
        grid=(N_TILES,),
        in_specs=[
            prompt(D_MODEL), sample(D_MODEL), prompt(D_SSM), sample(D_SSM), prompt(D_ATT), sample(D_ATT),
            _const_spec((1, D_ATT)),
            _const_spec((D_SSM, D_MODEL)), _const_spec((D_ATT, D_MODEL)),
            _const_spec((1, D_MODEL)),
            _const_spec((2 * N_EXPERTS, D_MODEL)), _const_spec((N_EXPERTS, 1)),
            _const_spec((TM, TM)),
        ],
        out_specs=[tok(D_MODEL), pl.BlockSpec(_tiled(TM), lambda i: (i, 0)), slots, slots, slots,
                   _const_spec((N_EXPERTS, 1))],
        out_shape=[
            jax.ShapeDtypeStruct((N_TOK, D_MODEL), F32),
            jax.ShapeDtypeStruct(_tiled(N_TOK), U32),
            jax.ShapeDtypeStruct((SUBLANES, N_TOK), jnp.int32),
            jax.ShapeDtypeStruct((SUBLANES, N_TOK), F32),
            jax.ShapeDtypeStruct((SUBLANES, N_TOK), jnp.int32),
            jax.ShapeDtypeStruct((N_EXPERTS, 1), F32),
        ],
        scratch_shapes=[pltpu.VMEM((N_EXPERTS, 1), F32)],
        compiler_params=_params(),
        name="outproj_router",
    )(xp, xs, yp, ys, op, os_, ag, wos, woa, fg, wr, br, earlier)


def _scatter_kernel(dest_ref, pend_ref, h_ref, rows_ref, zero_ref, sem, zsem):
    i = pl.program_id(0)

    @pl.when(i == 0)
    def _():
        zero_ref[...] = jnp.zeros_like(zero_ref)

        block_tiles = _tiled(MOE_BM)[0]

        def zero_block(b):
            start = pl.multiple_of(b * block_tiles, block_tiles)
            return pltpu.make_async_copy(zero_ref, rows_ref.at[pl.ds(start, block_tiles)], zsem)

        def last_block(e):
            end = pend_ref[e]
            nonempty = end > (pend_ref[e - 1] if e > 0 else 0)
            return nonempty, zero_block(jnp.maximum(end // MOE_BM - 1, 0))

        for e in range(N_EXPERTS):
            nonempty, cp = last_block(e)
            pl.when(nonempty)(cp.start)
        for e in range(N_EXPERTS):
            nonempty, cp = last_block(e)
            pl.when(nonempty)(cp.wait)

        first_unused = pend_ref[N_EXPERTS - 1] // MOE_BM
        lax.fori_loop(first_unused, MOE_BLOCKS, lambda b, c: (zero_block(b).start(), c)[1], 0)
        lax.fori_loop(first_unused, MOE_BLOCKS, lambda b, c: (zero_block(b).wait(), c)[1], 0)

    def issue(r, carry):
        for k in range(TOP_K):
            d = dest_ref[r * TOP_K + k]
            pltpu.make_async_copy(_tile_of(h_ref, r), _tile_of(rows_ref, d), sem).start(priority=k % 2)
        return carry

    lax.fori_loop(0, TM, issue, 0, unroll=ISSUE_UNROLL)
    for _ in range(TOP_K):
        pltpu.make_async_copy(h_ref, rows_ref.at[pl.ds(0, _tiled(TM)[0])], sem).wait()


def _scatter_rows(dest_flat, pad_end, h):
    return pl.pallas_call(
        _scatter_kernel,
        grid=(N_TILES,),
        in_specs=[
            pl.BlockSpec((TM * TOP_K,), lambda i: (i,), memory_space=pltpu.SMEM),
            pl.BlockSpec((N_EXPERTS,), lambda i: (0,), memory_space=pltpu.SMEM),
            pl.BlockSpec(_tiled(TM), lambda i: (i, 0)),
        ],
        out_specs=pl.BlockSpec(memory_space=pl.ANY),
        out_shape=jax.ShapeDtypeStruct(_tiled(MOE_ROWS), U32),
        scratch_shapes=[pltpu.VMEM(_tiled(MOE_BM), U32), pltpu.SemaphoreType.DMA(()),
                        pltpu.SemaphoreType.DMA(())],
        compiler_params=_params(),
        name="moe_scatter",
    )(dest_flat, pad_end, h)


def _expert_kernel(be_ref, nu_ref, nxt_ref, x_ref, wgu_hbm, bgu_ref, wd_hbm, bd_ref, y_ref,
                   wgu_f, wd_f, wgu_s, wd_s, sem):
    i = pl.program_id(0)
    active = i < nu_ref[0]
    e = be_ref[i]

    def fetch(expert):
        return (pltpu.make_async_copy(wgu_hbm.at[expert], wgu_f, sem.at[0]),
                pltpu.make_async_copy(wd_hbm.at[expert], wd_f, sem.at[1]))

    @pl.when(active & (i == 0))
    def _():
        for cp in fetch(e):
            cp.start()

    @pl.when(active & ((i == 0) | (e != be_ref[jnp.maximum(i - 1, 0)])))
    def _():
        for cp in fetch(e):
            cp.wait()
        wgu_s[...] = wgu_f[...].astype(BF16)
        wd_s[...] = wd_f[...].astype(BF16)
        nxt = nxt_ref[e]

        @pl.when(nxt >= 0)
        def _():
            for cp in fetch(nxt):
                cp.start()

    @pl.when(active)
    def _():
        gu = _dot(_tiles_to_rows(x_ref, MOE_BM).astype(BF16), wgu_s[...]) + bgu_ref[0]
        gate = jnp.minimum(gu[:, :D_FF], SWIGLU_LIMIT)
        up = jnp.clip(gu[:, D_FF:], -SWIGLU_LIMIT, SWIGLU_LIMIT)
        act = (up + 1.0) * gate * _sigmoid(gate * SWIGLU_ALPHA)
        _rows_to_tiles(y_ref, _dot(act.astype(BF16), wd_s[...]) + bd_ref[0])

    @pl.when(jnp.logical_not(active))
    def _():
        y_ref[...] = jnp.zeros_like(y_ref)


def _experts(block_expert, n_used, next_expert, rows, wgu, bgu, wd, bd):
    grid_spec = pltpu.PrefetchScalarGridSpec(
        num_scalar_prefetch=3,
        grid=(MOE_BLOCKS,),
        in_specs=[
            pl.BlockSpec(_tiled(MOE_BM), lambda i, be, nu, nx: (jnp.minimum(i, nu[0] - 1), 0)),
            pl.BlockSpec(memory_space=pl.ANY),
            pl.BlockSpec((1, 1, 2 * D_FF), lambda i, be, nu, nx: (be[i], 0, 0)),
            pl.BlockSpec(memory_space=pl.ANY),
            pl.BlockSpec((1, 1, D_MODEL), lambda i, be, nu, nx: (be[i], 0, 0)),
        ],
        out_specs=pl.BlockSpec(_tiled(MOE_BM), lambda i, be, nu, nx: (i, 0)),
        scratch_shapes=[pltpu.VMEM((D_MODEL, 2 * D_FF), F32), pltpu.VMEM((D_FF, D_MODEL), F32),
                        pltpu.VMEM((D_MODEL, 2 * D_FF), BF16), pltpu.VMEM((D_FF, D_MODEL), BF16),
                        pltpu.SemaphoreType.DMA((2,))],
    )
    return pl.pallas_call(
        _expert_kernel,
        grid_spec=grid_spec,
        out_shape=jax.ShapeDtypeStruct(_tiled(MOE_ROWS), U32),
        compiler_params=_params(),
        name="moe_experts",
    )(block_expert, n_used, next_expert, rows, wgu, bgu, wd, bd)


def _combine_kernel(dest_ref, dest_next_ref, gate_ref, xmid_ref, g_ref, rows_ref, yp_ref, ys_ref, buf_ref, sem):
    i = pl.program_id(0)
    slot = i % 2

    def issue(idx_ref, s):
        def body(r, carry):
            for k in range(TOP_K):
                d = idx_ref[r * TOP_K + k]
                pltpu.make_async_copy(_tile_of(rows_ref, d), _tile_of(buf_ref.at[s], k * TM + r),
                                      sem.at[s]).start(priority=k % 2)
            return carry

        lax.fori_loop(0, TM, body, 0, unroll=ISSUE_UNROLL)

    @pl.when(i == 0)
    def _():
        issue(dest_ref, 0)

    @pl.when(i + 1 < N_TILES)
    def _():
        issue(dest_next_ref, 1 - slot)

    buf = buf_ref.at[slot]
    slot_tiles = _tiled(TM)[0]
    for k in range(TOP_K):
        pltpu.make_async_copy(rows_ref.at[pl.ds(0, slot_tiles)], buf.at[pl.ds(k * slot_tiles, slot_tiles)],
                              sem.at[slot]).wait()
    acc = xmid_ref[...]
    for k in range(TOP_K):
        acc = acc + _tiles_to_rows(buf, TM, first=k * TM) * gate_ref[:, k:k + 1]
    y = _rms(acc, g_ref[...])

    @pl.when(i < N_PROMPT_TILES)
    def _():
        yp_ref[...] = y

    @pl.when(i == N_PROMPT_TILES)
    def _():
        ys_ref[...] = y


def _combine(dest_flat, gates, xmid, g, y_rows):
    return pl.pallas_call(
        _combine_kernel,
        grid=(N_TILES,),
        in_specs=[
            pl.BlockSpec((TM * TOP_K,), lambda i: (i,), memory_space=pltpu.SMEM),
            pl.BlockSpec((TM * TOP_K,), lambda i: (jnp.minimum(i + 1, N_TILES - 1),), memory_space=pltpu.SMEM),
            pl.BlockSpec((TM, TOP_K), lambda i: (i, 0)),
            pl.BlockSpec((TM, D_MODEL), lambda i: (i, 0)),
            _const_spec((1, D_MODEL)),
            pl.BlockSpec(memory_space=pl.ANY),
        ],
        out_specs=[
            pl.BlockSpec((TM, D_MODEL), lambda i: (jnp.minimum(i, N_PROMPT_TILES - 1), 0)),
            pl.BlockSpec((TM, D_MODEL), lambda i: (0, 0)),
        ],
        out_shape=[
            jax.ShapeDtypeStruct((N_PROMPT, D_MODEL), F32),
            jax.ShapeDtypeStruct((N_SAMPLE, D_MODEL), F32),
        ],
        scratch_shapes=[pltpu.VMEM((2,) + _tiled(TOP_K * TM), U32), pltpu.SemaphoreType.DMA((2,))],
        compiler_params=_params(),
        name="moe_combine",
    )(dest_flat, dest_flat, gates, xmid, g, y_rows)


def _band_bias(table):
    n_diag = BAND + CHUNK - 1
    idx = np.clip(ATT_LEFT + (CHUNK - 1) - np.arange(n_diag), -REL_CLIP, REL_CLIP) + REL_CLIP
    pick = (np.arange(2 * REL_CLIP + 1)[:, None] == idx[None, :]).astype(np.float32)
    diag = jnp.dot(table, jnp.asarray(pick), precision=lax.Precision.HIGHEST)
    return jnp.stack([diag[:, CHUNK - 1 - qi:CHUNK - 1 - qi + BAND] for qi in range(CHUNK)], axis=1)


def _pair_rows(v):
    return jnp.repeat(v.reshape(HEAD_PAIRS, 2), CHUNK, axis=1)


def _layer(l, xp, xs, cache_k, cache_v, state_conv, state_ssm,
           norm_mix_g, w_in, conv_w, conv_b, dt_bias, a_log, d_skip, ssm_norm_g,
           att_norm_g, rel_bias_table, w_out, norm_ffn_g, w_router, b_router,
           w_gate_up, b_gate_up, w_down, b_down, norm_final_g):
    wb = w_in[l].astype(BF16)
    c0 = D_SSM
    c1 = c0 + D_CONV
    c2 = c1 + SSM_HEADS
    c3 = c2 + D_ATT
    c4 = c3 + D_ATT
    z, xbc, dt, dtt, q, k, v, k_p, v_p, k_s, v_s, ctail = _inproj(
        xp, xs, norm_mix_g[l][None], wb[:, :c0], wb[:, c0:c1], wb[:, c1:c2], wb[:, c1:c2].T,
        wb[:, c2:c3], wb[:, c3:c4], wb[:, c4:])

    n_chunks = N_TOK // CHUNK
    dtp = dtt.reshape(HEAD_PAIRS, 2, n_chunks, CHUNK).transpose(2, 0, 1, 3).reshape(n_chunks, HEAD_PAIRS, LANES)
    hp = np.arange(D_SSM) // SSM_HEAD_DIM
    expand = jnp.asarray(hp[None, :] == np.arange(SSM_HEADS)[:, None], BF16)
    lane = np.arange(LANES)
    triu2 = jnp.asarray((lane[:, None] // CHUNK == lane[None, :] // CHUNK) & (lane[:, None] <= lane[None, :]), BF16)
    consts = (conv_w[l], conv_b[l][None], dt_bias[l][None], _pair_rows(dt_bias[l]),
              a_log[l][None], _pair_rows(a_log[l]),
              jnp.repeat(d_skip[l], SSM_HEAD_DIM)[None], ssm_norm_g[l][None],
              expand, triu2)
    y_ssm_p, ssm_p = _ssd_prompt(z, xbc, dt, dtp, consts)
    y_ssm_s, ssm_s = _ssd_sample(z, xbc, dt, dtp, state_conv[l],
                                 state_ssm[l].reshape(DEC_BATCH, D_SSM, D_STATE), consts)

    bias2 = _band_bias(rel_bias_table[l]).reshape(ATT_PAIRS, 2 * CHUNK, BAND)
    o_att_p = _attn_prompt(q, k, v, bias2)
    o_att_s = _attn_sample(q, k, v, cache_k[l], cache_v[l], bias2)

    wo = w_out[l].astype(BF16)
    wr = w_router[l].T
    wr1 = wr.astype(BF16)
    wr2 = (wr - wr1.astype(F32)).astype(BF16)
    earlier = jnp.asarray(np.triu(np.ones((TM, TM), np.float32), 1), BF16)
    xmid, h, top_idx, gates, rank, counts = _outproj(
        xp, xs, y_ssm_p, y_ssm_s, o_att_p, o_att_s, att_norm_g[l][None], wo[:D_SSM], wo[D_SSM:], norm_ffn_g[l][None],
        jnp.concatenate([wr1, wr2], axis=0), b_router[l][:, None], earlier)
    gates = gates[:TOP_K].T

    counts = counts[:, 0].astype(jnp.int32)
    padded = (counts + MOE_BM - 1) // MOE_BM * MOE_BM
    pad_end = jnp.cumsum(padded)
    pad_start = pad_end - padded
    experts = jnp.arange(N_EXPERTS, dtype=jnp.int32)
    start_of = jnp.sum(jnp.where(top_idx[:TOP_K, :, None] == experts, pad_start, 0), axis=-1)
    dest = (start_of + rank[:TOP_K]).T.reshape(-1).astype(jnp.int32)
    block_start = jnp.arange(MOE_BLOCKS, dtype=jnp.int32) * MOE_BM
    block_expert = jnp.minimum(jnp.sum((pad_end[None, :] <= block_start[:, None]).astype(jnp.int32), axis=1),
                               N_EXPERTS - 1).astype(jnp.int32)
    n_used = (pad_end[-1:] // MOE_BM).astype(jnp.int32)
    later_nonempty = (experts[None, :] > experts[:, None]) & (padded[None, :] > 0)
    next_expert = jnp.min(jnp.where(later_nonempty, experts[None, :], N_EXPERTS), axis=1)
    next_expert = jnp.where(next_expert < N_EXPERTS, next_expert, -1).astype(jnp.int32)

    rows = _scatter_rows(dest, pad_end.astype(jnp.int32), h)
    y_rows = _experts(block_expert, n_used, next_expert, rows, w_gate_up[l], b_gate_up[l][:, None, :],
                      w_down[l], b_down[l][:, None, :])
    y_p, y_s = _combine(dest, gates, xmid, norm_final_g[None], y_rows)

    keep = min(ATT_LEFT, SEQ)
    k_p = k_p[:, TM - keep:].reshape(BATCH, keep, ATT_HEADS, ATT_HEAD_DIM)
    v_p = v_p[:, TM - keep:].reshape(BATCH, keep, ATT_HEADS, ATT_HEAD_DIM)
    k_s = k_s.reshape(DEC_BATCH, DEC_SEQ, ATT_HEADS, ATT_HEAD_DIM)
    v_s = v_s.reshape(DEC_BATCH, DEC_SEQ, ATT_HEADS, ATT_HEAD_DIM)
    conv_p = ctail[:BATCH, -(CONV_W - 1):]
    conv_s = ctail[BATCH].reshape(DEC_BATCH, SUBLANES, D_CONV)[:, -(CONV_W - 1):]
    ssm_p = ssm_p.reshape(BATCH, SSM_HEADS, SSM_HEAD_DIM, D_STATE)
    ssm_s = ssm_s.reshape(DEC_BATCH, SSM_HEADS, SSM_HEAD_DIM, D_STATE)
    return (y_p.reshape(BATCH, SEQ, D_MODEL), y_s.reshape(DEC_BATCH, DEC_SEQ, D_MODEL),
            k_p, v_p, conv_p, ssm_p, k_s, v_s, conv_s, ssm_s)


def kernel(x_prompt, x_sample, cache_k, cache_v, state_conv, state_ssm, norm_mix_g, w_in, conv_w, conv_b,
           dt_bias, a_log, d_skip, ssm_norm_g, att_norm_g, rel_bias_table, w_out, norm_ffn_g, w_router,
           b_router, w_gate_up, b_gate_up, w_down, b_down, norm_final_g):
    assert w_in.shape[0] == 1, "single trunk layer"
    xp = x_prompt.reshape(N_PROMPT, D_MODEL)
    xs = x_sample.reshape(N_SAMPLE, D_MODEL)
    outs = _layer(0, xp, xs, cache_k, cache_v, state_conv, state_ssm,
                  norm_mix_g, w_in, conv_w, conv_b, dt_bias, a_log, d_skip, ssm_norm_g,
                  att_norm_g, rel_bias_table, w_out, norm_ffn_g, w_router, b_router,
                  w_gate_up, b_gate_up, w_down, b_down, norm_final_g)
    y_p, y_s, k_p, v_p, conv_p, ssm_p, k_s, v_s, conv_s, ssm_s = outs
    return (y_p, y_s, k_p[None], v_p[None], conv_p[None], ssm_p[None],
            k_s[None], v_s[None], conv_s[None], ssm_s[None])
```

```python
import jax
import jax.numpy as jnp
import numpy as np
from jax import lax
from jax.experimental import pallas as pl
from jax.experimental.pallas import tpu as pltpu

D_MODEL = 1024
BATCH = 8
SEQ = 2048
DEC_BATCH = 8
DEC_SEQ = 64
CHUNK = 64
SSM_HEADS = 16
SSM_HEAD_DIM = 64
D_SSM = SSM_HEADS * SSM_HEAD_DIM
SSM_GROUPS = 2
D_STATE = 128
CONV_W = 4
D_BC = SSM_GROUPS * D_STATE
D_CONV = D_SSM + 2 * D_BC
ATT_HEADS = 8
ATT_HEAD_DIM = 64
D_ATT = ATT_HEADS * ATT_HEAD_DIM
LEFT_CHUNKS = 8
ATT_LEFT = LEFT_CHUNKS * CHUNK
BAND = ATT_LEFT + CHUNK
REL_CLIP = 128
ATT_SCALE = ATT_HEAD_DIM ** -0.5
N_EXPERTS = 32
TOP_K = 4
D_FF = D_MODEL
SWIGLU_ALPHA = 1.702
SWIGLU_LIMIT = 7.0
EPS = 1e-5

F32 = jnp.float32
BF16 = jnp.bfloat16
U32 = jnp.uint32

LANES = 128
SUBLANES = 8

N_PROMPT = BATCH * SEQ
N_SAMPLE = DEC_BATCH * DEC_SEQ
N_TOK = N_PROMPT + N_SAMPLE
TM = 512
N_PROMPT_TILES = N_PROMPT // TM
N_TILES = N_TOK // TM
TILES_PER_SEQ = SEQ // TM
CHUNKS_PER_TILE = TM // CHUNK
N_CHUNKS_SEQ = SEQ // CHUNK
assert 2 * SSM_HEAD_DIM == LANES and 2 * ATT_HEAD_DIM == LANES
HEAD_PAIRS = SSM_HEADS // 2
PAIRS_PER_GROUP = HEAD_PAIRS // SSM_GROUPS
ATT_PAIRS = ATT_HEADS // 2
SSD_BLOCK = 2
CONV_HEAD = SUBLANES
MOE_BM = 1024
MOE_SUB = 256
N_ASSIGN = N_TOK * TOP_K
MOE_BLOCKS = N_ASSIGN // MOE_BM + N_EXPERTS
MOE_ROWS = MOE_BLOCKS * MOE_BM
ISSUE_UNROLL = 8
ROW_TILE = (SUBLANES // 2, LANES)
assert 2 * ROW_TILE[0] * ROW_TILE[1] == D_MODEL
VMEM_BYTES = 64 * 1024 * 1024
VMEM_LIMIT = VMEM_BYTES * 7 // 8


def _dot(a, b):
    return jnp.dot(a, b, preferred_element_type=F32)


def _dot_nt(a, b):
    return lax.dot_general(a, b, (((1,), (1,)), ((), ())), preferred_element_type=F32)


def _dot_tn(a, b):
    return lax.dot_general(a, b, (((0,), (0,)), ((), ())), preferred_element_type=F32)


def _split3(x):
    x1 = x.astype(BF16)
    r1 = x - x1.astype(F32)
    x2 = r1.astype(BF16)
    r2 = r1 - x2.astype(F32)
    return x1, x2, r2.astype(BF16)


def _dot_exact_rhs(x, m):
    x1, x2, x3 = _split3(x)
    return _dot(x1, m) + _dot(x2, m) + _dot(x3, m)


def _dot_exact_lhs(m, x):
    x1, x2, x3 = _split3(x)
    return _dot(m, x1) + _dot(m, x2) + _dot(m, x3)


def _rms(x, g):
    return x * lax.rsqrt(jnp.mean(x * x, axis=-1, keepdims=True) + EPS) * g


def _sigmoid(x):
    return 1.0 / (1.0 + jnp.exp(-x))


def _softplus(x):
    return jnp.maximum(x, 0.0) + jnp.log(1.0 + jnp.exp(-jnp.abs(x)))


def _tiled(n):
    return (n * ROW_TILE[0], ROW_TILE[1])


def _tile_of(ref, row):
    return ref.at[pl.ds(pl.multiple_of(row * ROW_TILE[0], ROW_TILE[0]), ROW_TILE[0])]


def _rows_to_tiles(ref, x, first=0):
    sub, lanes = ROW_TILE
    half = D_MODEL // 2
    hi = lax.bitcast_convert_type(x[:, :half].astype(BF16).astype(F32), U32)
    lo = lax.bitcast_convert_type(x[:, half:].astype(BF16).astype(F32), U32)
    words = hi | (lo >> 16)
    for j in range(sub):
        ref[pl.ds(first * sub + j, x.shape[0], stride=sub), :] = words[:, j * lanes:(j + 1) * lanes]


def _tiles_to_rows(ref, n, first=0):
    sub = ROW_TILE[0]
    words = jnp.concatenate([ref[pl.ds(first * sub + j, n, stride=sub), :] for j in range(sub)], axis=1)
    hi = lax.bitcast_convert_type(words & jnp.uint32(0xFFFF0000), F32)
    lo = lax.bitcast_convert_type(words << 16, F32)
    return jnp.concatenate([hi, lo], axis=1)


def _const_spec(shape):
    nd = len(shape)
    return pl.BlockSpec(shape, lambda *_: (0,) * nd)


def _params(n_axes=1):
    return pltpu.CompilerParams(dimension_semantics=("arbitrary",) * n_axes,
                                vmem_limit_bytes=VMEM_LIMIT)


def _inproj_kernel(xp_ref, xs_ref, g_ref, wz_ref, wxbc_ref, wdt_ref, wdtt_ref, wq_ref, wk_ref, wv_ref,
                   z_ref, xbc_ref, dt_ref, dtt_ref, q_ref, k_ref, v_ref, kp_ref, vp_ref, ks_ref, vs_ref, ctail_ref):
    i = pl.program_id(0)
    x = jnp.where(i == N_PROMPT_TILES, xs_ref[...], xp_ref[...])
    h = _rms(x, g_ref[...]).astype(BF16)
    z_ref[...] = _dot(h, wz_ref[...]).astype(BF16)
    xbc = _dot(h, wxbc_ref[...])
    xbc_ref[...] = xbc.astype(BF16)
    for c in range(CHUNKS_PER_TILE):
        ctail_ref[0, c * SUBLANES:(c + 1) * SUBLANES, :] = xbc[(c + 1) * CHUNK - SUBLANES:(c + 1) * CHUNK, :]
    dt_ref[...] = _dot(h, wdt_ref[...])
    dtt_ref[...] = _dot_nt(wdtt_ref[...], h)
    q_ref[...] = (_dot(h, wq_ref[...]) * ATT_SCALE).astype(BF16)
    k = _dot(h, wk_ref[...])
    v = _dot(h, wv_ref[...])
    k_ref[...] = k.astype(BF16)
    v_ref[...] = v.astype(BF16)

    @pl.when(i < N_PROMPT_TILES)
    def _():
        kp_ref[0] = k
        vp_ref[0] = v

    @pl.when(i == N_PROMPT_TILES)
    def _():
        ks_ref[...] = k
        vs_ref[...] = v


def _inproj(xp, xs, g, wz, wxbc, wdt, wdtt, wq, wk, wv):
    tok = lambda n: pl.BlockSpec((TM, n), lambda i: (i, 0))
    tail_idx = lambda i: (i // TILES_PER_SEQ, 0, 0)
    seq_idx = lambda i: (jnp.minimum(i // TILES_PER_SEQ, BATCH - 1), 0, 0)
    n_tail = BATCH + 1
    return pl.pallas_call(
        _inproj_kernel,
        grid=(N_TILES,),
        in_specs=[
            pl.BlockSpec((TM, D_MODEL), lambda i: (jnp.minimum(i, N_PROMPT_TILES - 1), 0)),
            pl.BlockSpec((TM, D_MODEL), lambda i: (0, 0)),
            _const_spec((1, D_MODEL)),
            _const_spec((D_MODEL, D_SSM)),
            _const_spec((D_MODEL, D_CONV)),
            _const_spec((D_MODEL, SSM_HEADS)),
            _const_spec((SSM_HEADS, D_MODEL)),
            _const_spec((D_MODEL, D_ATT)),
            _const_spec((D_MODEL, D_ATT)),
            _const_spec((D_MODEL, D_ATT)),
        ],
        out_specs=[
            tok(D_SSM), tok(D_CONV), tok(SSM_HEADS),
            pl.BlockSpec((SSM_HEADS, TM), lambda i: (0, i)),
            tok(D_ATT), tok(D_ATT), tok(D_ATT),
            pl.BlockSpec((1, TM, D_ATT), seq_idx),
            pl.BlockSpec((1, TM, D_ATT), seq_idx),
            pl.BlockSpec((TM, D_ATT), lambda i: (0, 0)),
            pl.BlockSpec((TM, D_ATT), lambda i: (0, 0)),
            pl.BlockSpec((1, CHUNKS_PER_TILE * SUBLANES, D_CONV), tail_idx),
        ],
        out_shape=[
            jax.ShapeDtypeStruct((N_TOK, D_SSM), BF16),
            jax.ShapeDtypeStruct((N_TOK, D_CONV), BF16),
            jax.ShapeDtypeStruct((N_TOK, SSM_HEADS), F32),
            jax.ShapeDtypeStruct((SSM_HEADS, N_TOK), F32),
            jax.ShapeDtypeStruct((N_TOK, D_ATT), BF16),
            jax.ShapeDtypeStruct((N_TOK, D_ATT), BF16),
            jax.ShapeDtypeStruct((N_TOK, D_ATT), BF16),
            jax.ShapeDtypeStruct((BATCH, TM, D_ATT), F32),
            jax.ShapeDtypeStruct((BATCH, TM, D_ATT), F32),
            jax.ShapeDtypeStruct((N_SAMPLE, D_ATT), F32),
            jax.ShapeDtypeStruct((N_SAMPLE, D_ATT), F32),
            jax.ShapeDtypeStruct((n_tail, CHUNKS_PER_TILE * SUBLANES, D_CONV), F32),
        ],
        compiler_params=_params(),
        name="inproj",
    )(xp, xs, g, wz, wxbc, wdt, wdtt, wq, wk, wv)


def _ssd_tile(n_chunks, z_ref, xbc_ref, dt_ref, dtp_ref, cw_ref, cb_ref, dtb_ref, dtbp_ref,
              alog_h_ref, alog_p_ref, dskip_e_ref, ng_ref, expand_ref, triu2_ref,
              y_ref, xw_ref, state_ref):
    nb = SSD_BLOCK if n_chunks % SSD_BLOCK == 0 else 1
    rb = nb * CHUNK
    half = D_SSM // SSM_GROUPS
    xw_ref[CONV_HEAD:CONV_HEAD + n_chunks * CHUNK, :] = xbc_ref[...].astype(F32)

    a_h = -jnp.exp(alog_h_ref[...])
    a_p = jnp.concatenate([-jnp.exp(alog_p_ref[...])] * nb, axis=0)
    dtb_p = jnp.concatenate([dtbp_ref[...]] * nb, axis=0)
    tr_r = lax.broadcasted_iota(jnp.int32, (rb, rb), 0)
    tr_c = lax.broadcasted_iota(jnp.int32, (rb, rb), 1)
    tril_b = jnp.where((tr_r // CHUNK == tr_c // CHUNK) & (tr_c <= tr_r), 1.0, 0.0).astype(BF16)
    row_i = lax.broadcasted_iota(jnp.int32, (CHUNK, LANES), 0)
    col_i = lax.broadcasted_iota(jnp.int32, (CHUNK, LANES), 1)
    causal2 = row_i >= (col_i % CHUNK)
    sel_r = lax.broadcasted_iota(jnp.int32, ((CONV_W - 1) * rb, CONV_HEAD + rb), 0)
    sel_c = lax.broadcasted_iota(jnp.int32, ((CONV_W - 1) * rb, CONV_HEAD + rb), 1)
    shift_sel = (sel_c == sel_r % rb + sel_r // rb + CONV_HEAD - (CONV_W - 1)).astype(F32)
    bd_r = lax.broadcasted_iota(jnp.int32, (LANES, LANES), 0) // CHUNK
    bd_c = lax.broadcasted_iota(jnp.int32, (LANES, LANES), 1) // CHUNK
    blockdiag = bd_r == bd_c

    def block(i, carry):
        r0 = pl.multiple_of(i * rb, rb)
        win = xw_ref[pl.ds(r0, CONV_HEAD + rb), :]
        shifted = _dot(shift_sel, win)
        acc = cb_ref[...] + cw_ref[CONV_W - 1:CONV_W, :] * win[CONV_HEAD:, :]
        for tap in range(CONV_W - 1):
            acc = acc + cw_ref[tap:tap + 1, :] * shifted[tap * rb:(tap + 1) * rb, :]
        xa = acc * _sigmoid(acc)
        xs = xa[:, 0:D_SSM]
        bm = xa[:, D_SSM:D_SSM + D_BC].astype(BF16)
        cm = xa[:, D_SSM + D_BC:D_CONV].astype(BF16)
        dt = _softplus(dt_ref[pl.ds(r0, rb), :] + dtb_ref[...])
        dt_e = _dot_exact_rhs(dt, expand_ref[...])
        acum = _dot_exact_rhs(_dot_exact_lhs(tril_b, dt * a_h), expand_ref[...])
        dtp = _softplus(dtp_ref[pl.ds(i * nb, nb)].reshape(nb * HEAD_PAIRS, LANES) + dtb_p)
        acum_p = _dot_exact_rhs(dtp * a_p, triu2_ref[...])
        xdt = xs * dt_e
        a_last = [acum[(c + 1) * CHUNK - 1:(c + 1) * CHUNK, :] for c in range(nb)]
        a_end = jnp.concatenate([jnp.broadcast_to(a, (CHUNK, D_SSM)) for a in a_last], axis=0)
        xdt_end = (xdt * jnp.exp(a_end - acum)).astype(BF16)

        y_diag, new_s = [], []
        for c in range(nb):
            rs = slice(c * CHUNK, (c + 1) * CHUNK)
            y_parts = []
            for g in range(SSM_GROUPS):
                bg = bm[rs, g * D_STATE:(g + 1) * D_STATE]
                cg = cm[rs, g * D_STATE:(g + 1) * D_STATE]
                cb2 = _dot_nt(cg, jnp.concatenate([bg, bg], axis=0))
                for jj in range(PAIRS_PER_GROUP):
                    j = g * PAIRS_PER_GROUP + jj
                    seg = acum[rs, j * LANES:(j + 1) * LANES] - acum_p[c * HEAD_PAIRS + j:c * HEAD_PAIRS + j + 1, :]
                    decay = jnp.exp(jnp.where(causal2, seg, -jnp.inf))
                    x2 = xdt[rs, j * LANES:(j + 1) * LANES]
                    rhs = jnp.where(blockdiag, jnp.concatenate([x2, x2], axis=0), 0.0).astype(BF16)
                    y_parts.append(_dot((cb2 * decay).astype(BF16), rhs))
            y_diag.append(jnp.concatenate(y_parts, axis=1))
            new_s.append(jnp.concatenate(
                [_dot_tn(bm[rs, g * D_STATE:(g + 1) * D_STATE], xdt_end[rs, g * half:(g + 1) * half])
                 for g in range(SSM_GROUPS)], axis=1))

        y_off = []
        for c in range(nb):
            rs = slice(c * CHUNK, (c + 1) * CHUNK)
            state = state_ref[...]
            y_off.append(jnp.concatenate(
                [_dot(cm[rs, g * D_STATE:(g + 1) * D_STATE], state[:, g * half:(g + 1) * half].astype(BF16))
                 for g in range(SSM_GROUPS)], axis=1))
            state_ref[...] = state * jnp.exp(a_last[c]) + new_s[c]

        y = (jnp.concatenate(y_diag, axis=0) + jnp.concatenate(y_off, axis=0) * jnp.exp(acum)
             + dskip_e_ref[...] * xs)
        zc = z_ref[pl.ds(r0, rb), :].astype(F32)
        y = y * (zc * _sigmoid(zc))
        yn = jnp.concatenate(
            [y[:, g * half:(g + 1) * half]
             * lax.rsqrt(jnp.mean(jnp.square(y[:, g * half:(g + 1) * half]), axis=-1, keepdims=True) + EPS)
             for g in range(SSM_GROUPS)], axis=1)
        y_ref[pl.ds(r0, rb), :] = (yn * ng_ref[...]).astype(BF16)
        return carry

    lax.fori_loop(0, n_chunks // nb, block, 0)


def _state_store(state_ref, out_ref):
    for j in range(HEAD_PAIRS):
        out_ref[0, j * LANES:(j + 1) * LANES, :] = state_ref[:, j * LANES:(j + 1) * LANES].T


def _ssd_prompt_kernel(z_ref, xbc_ref, dt_ref, dtp_ref, cw_ref, cb_ref, dtb_ref, dtbp_ref,
                       alog_h_ref, alog_p_ref, dskip_e_ref, ng_ref, expand_ref, triu2_ref,
                       y_ref, ssm_ref, xw_ref, state_ref, tail_ref):
    t = pl.program_id(1)

    @pl.when(t == 0)
    def _():
        state_ref[...] = jnp.zeros_like(state_ref)
        xw_ref[0:CONV_HEAD, :] = jnp.zeros((CONV_HEAD, D_CONV), F32)

    @pl.when(t > 0)
    def _():
        xw_ref[0:CONV_HEAD, :] = tail_ref[...]

    _ssd_tile(CHUNKS_PER_TILE, z_ref, xbc_ref, dt_ref, dtp_ref, cw_ref, cb_ref, dtb_ref, dtbp_ref,
              alog_h_ref, alog_p_ref, dskip_e_ref, ng_ref, expand_ref, triu2_ref,
              y_ref, xw_ref, state_ref)
    tail_ref[...] = xw_ref[TM:TM + CONV_HEAD, :]

    @pl.when(t == TILES_PER_SEQ - 1)
    def _():
        _state_store(state_ref, ssm_ref)


def _ssd_sample_kernel(z_ref, xbc_ref, dt_ref, dtp_ref, cprev_ref, sprev_ref,
                       cw_ref, cb_ref, dtb_ref, dtbp_ref,
                       alog_h_ref, alog_p_ref, dskip_e_ref, ng_ref, expand_ref, triu2_ref,
                       y_ref, ssm_ref, xw_ref, state_ref):
    xw_ref[0:CONV_HEAD, :] = jnp.zeros((CONV_HEAD, D_CONV), F32)
    xw_ref[CONV_HEAD - (CONV_W - 1):CONV_HEAD, :] = cprev_ref[0]
    for j in range(HEAD_PAIRS):
        state_ref[:, j * LANES:(j + 1) * LANES] = sprev_ref[0, j * LANES:(j + 1) * LANES, :].T
    _ssd_tile(1, z_ref, xbc_ref, dt_ref, dtp_ref, cw_ref, cb_ref, dtb_ref, dtbp_ref,
              alog_h_ref, alog_p_ref, dskip_e_ref, ng_ref, expand_ref, triu2_ref,
              y_ref, xw_ref, state_ref)
    _state_store(state_ref, ssm_ref)


def _ssd_const_specs():
    return [
        _const_spec((CONV_W, D_CONV)), _const_spec((1, D_CONV)),
        _const_spec((1, SSM_HEADS)), _const_spec((HEAD_PAIRS, LANES)),
        _const_spec((1, SSM_HEADS)), _const_spec((HEAD_PAIRS, LANES)),
        _const_spec((1, D_SSM)), _const_spec((1, D_SSM)),
        _const_spec((SSM_HEADS, D_SSM)), _const_spec((LANES, LANES)),
    ]


def _ssd_prompt(z, xbc, dt, dtp, consts):
    tile = lambda b, t: (b * TILES_PER_SEQ + t, 0)
    return pl.pallas_call(
        _ssd_prompt_kernel,
        grid=(BATCH, TILES_PER_SEQ),
        in_specs=[
            pl.BlockSpec((TM, D_SSM), tile),
            pl.BlockSpec((TM, D_CONV), tile),
            pl.BlockSpec((TM, SSM_HEADS), tile),
            pl.BlockSpec((CHUNKS_PER_TILE, HEAD_PAIRS, LANES), lambda b, t: (b * TILES_PER_SEQ + t, 0, 0)),
        ] + _ssd_const_specs(),
        out_specs=[
            pl.BlockSpec((TM, D_SSM), tile),
            pl.BlockSpec((1, D_SSM, D_STATE), lambda b, t: (b, 0, 0)),
        ],
        out_shape=[
            jax.ShapeDtypeStruct((N_PROMPT, D_SSM), BF16),
            jax.ShapeDtypeStruct((BATCH, D_SSM, D_STATE), F32),
        ],
        scratch_shapes=[
            pltpu.VMEM((CONV_HEAD + TM, D_CONV), F32),
            pltpu.VMEM((D_STATE, D_SSM), F32),
            pltpu.VMEM((CONV_HEAD, D_CONV), F32),
        ],
        compiler_params=_params(2),
        name="ssd_prompt",
    )(z, xbc, dt, dtp, *consts)


def _ssd_sample(z, xbc, dt, dtp, conv_prev, ssm_prev, consts):
    first = N_PROMPT // CHUNK
    row = lambda b: (first + b, 0)
    return pl.pallas_call(
        _ssd_sample_kernel,
        grid=(DEC_BATCH,),
        in_specs=[
            pl.BlockSpec((CHUNK, D_SSM), row),
            pl.BlockSpec((CHUNK, D_CONV), row),
            pl.BlockSpec((CHUNK, SSM_HEADS), row),
            pl.BlockSpec((1, HEAD_PAIRS, LANES), lambda b: (first + b, 0, 0)),
            pl.BlockSpec((1, CONV_W - 1, D_CONV), lambda b: (b, 0, 0)),
            pl.BlockSpec((1, D_SSM, D_STATE), lambda b: (b, 0, 0)),
        ] + _ssd_const_specs(),
        out_specs=[
            pl.BlockSpec((CHUNK, D_SSM), lambda b: (b, 0)),
            pl.BlockSpec((1, D_SSM, D_STATE), lambda b: (b, 0, 0)),
        ],
        out_shape=[
            jax.ShapeDtypeStruct((N_SAMPLE, D_SSM), BF16),
            jax.ShapeDtypeStruct((DEC_BATCH, D_SSM, D_STATE), F32),
        ],
        scratch_shapes=[
            pltpu.VMEM((CONV_HEAD + CHUNK, D_CONV), F32),
            pltpu.VMEM((D_STATE, D_SSM), F32),
        ],
        compiler_params=_params(),
        name="ssd_sample",
    )(z, xbc, dt, dtp, conv_prev, ssm_prev, *consts)


def _attn_chunks(n_chunks, first_chunk, q_ref, kpad_ref, vpad_ref, bias_ref, o_ref):
    lane = lax.broadcasted_iota(jnp.int32, (CHUNK, LANES), 1)
    low = lane < ATT_HEAD_DIM
    kj = lax.broadcasted_iota(jnp.int32, (2 * CHUNK, BAND), 1)

    def chunk(c, carry, masked):
        r0 = pl.multiple_of(c * CHUNK, CHUNK)
        if masked:
            valid = kj >= (LEFT_CHUNKS - (first_chunk + c)) * CHUNK
        scores = []
        for j in range(ATT_PAIRS):
            qp = q_ref[pl.ds(r0, CHUNK), j * LANES:(j + 1) * LANES]
            zero = jnp.zeros_like(qp)
            q2 = jnp.concatenate([jnp.where(low, qp, zero), jnp.where(low, zero, qp)], axis=0)
            kb = kpad_ref[pl.ds(r0, BAND), j * LANES:(j + 1) * LANES]
            s = _dot_nt(q2, kb) + bias_ref[j]
            scores.append(jnp.where(valid, s, -jnp.inf) if masked else s)
        probs = []
        for s in scores:
            e = jnp.exp(s - jnp.max(s, axis=-1, keepdims=True))
            probs.append((e.astype(BF16), jnp.sum(e, axis=-1, keepdims=True)))
        outs = []
        for j, (e, denom) in enumerate(probs):
            vb = vpad_ref[pl.ds(r0, BAND), j * LANES:(j + 1) * LANES]
            r = _dot(e, vb) / denom
            outs.append(jnp.where(low, r[0:CHUNK], r[CHUNK:2 * CHUNK]))
        o_ref[pl.ds(r0, CHUNK), :] = jnp.concatenate(outs, axis=1).astype(BF16)
        return carry

    n_masked = min(max(LEFT_CHUNKS - first_chunk, 0), n_chunks)
    for lo, hi, masked in ((0, n_masked, True), (n_masked, n_chunks, False)):
        if hi > lo:
            lax.fori_loop(lo, hi, lambda c, carry, masked=masked: chunk(c, carry, masked), 0,
                          unroll=2 if (hi - lo) % 2 == 0 else 1)


def _attn_prompt_kernel(q_ref, k_ref, v_ref, bias_ref, o_ref, kpad_ref, vpad_ref):
    kpad_ref[0:ATT_LEFT, :] = jnp.zeros((ATT_LEFT, D_ATT), BF16)
    vpad_ref[0:ATT_LEFT, :] = jnp.zeros((ATT_LEFT, D_ATT), BF16)
    kpad_ref[ATT_LEFT:ATT_LEFT + SEQ, :] = k_ref[...]
    vpad_ref[ATT_LEFT:ATT_LEFT + SEQ, :] = v_ref[...]
    _attn_chunks(N_CHUNKS_SEQ, 0, q_ref, kpad_ref, vpad_ref, bias_ref, o_ref)


def _attn_sample_kernel(q_ref, k_ref, v_ref, ck_ref, cv_ref, bias_ref, o_ref, kpad_ref, vpad_ref):
    kpad_ref[0:ATT_LEFT, :] = ck_ref[0].astype(BF16)
    vpad_ref[0:ATT_LEFT, :] = cv_ref[0].astype(BF16)
    kpad_ref[ATT_LEFT:BAND, :] = k_ref[...]
    vpad_ref[ATT_LEFT:BAND, :] = v_ref[...]
    _attn_chunks(1, LEFT_CHUNKS, q_ref, kpad_ref, vpad_ref, bias_ref, o_ref)


def _attn_prompt(q, k, v, bias2):
    seq = pl.BlockSpec((SEQ, D_ATT), lambda b: (b, 0))
    return pl.pallas_call(
        _attn_prompt_kernel,
        grid=(BATCH,),
        in_specs=[seq, seq, seq, _const_spec((ATT_PAIRS, 2 * CHUNK, BAND))],
        out_specs=seq,
        out_shape=jax.ShapeDtypeStruct((N_PROMPT, D_ATT), BF16),
        scratch_shapes=[pltpu.VMEM((ATT_LEFT + SEQ, D_ATT), BF16),
                        pltpu.VMEM((ATT_LEFT + SEQ, D_ATT), BF16)],
        compiler_params=_params(),
        name="attn_prompt",
    )(q, k, v, bias2)


def _attn_sample(q, k, v, cache_k, cache_v, bias2):
    first = N_PROMPT // CHUNK
    row = pl.BlockSpec((CHUNK, D_ATT), lambda b: (first + b, 0))
    cache = pl.BlockSpec((1, ATT_LEFT, D_ATT), lambda b: (b, 0, 0))
    return pl.pallas_call(
        _attn_sample_kernel,
        grid=(DEC_BATCH,),
        in_specs=[row, row, row, cache, cache, _const_spec((ATT_PAIRS, 2 * CHUNK, BAND))],
        out_specs=pl.BlockSpec((CHUNK, D_ATT), lambda b: (b, 0)),
        out_shape=jax.ShapeDtypeStruct((N_SAMPLE, D_ATT), BF16),
        scratch_shapes=[pltpu.VMEM((BAND, D_ATT), BF16), pltpu.VMEM((BAND, D_ATT), BF16)],
        compiler_params=_params(),
        name="attn_sample",
    )(q, k, v, cache_k, cache_v, bias2)


def _outproj_kernel(xp_ref, xs_ref, yp_ref, ys_ref, op_ref, os_ref, ag_ref, wos_ref, woa_ref, fg_ref,
                    wr_ref, br_ref, earlier_ref,
                    xmid_ref, h_ref, idx_ref, gate_ref, rank_ref, cnt_ref, carry_ref):
    i = pl.program_id(0)

    @pl.when(i == 0)
    def _():
        carry_ref[...] = jnp.zeros_like(carry_ref)

    is_sample = i == N_PROMPT_TILES
    x = jnp.where(is_sample, xs_ref[...], xp_ref[...])
    y = jnp.where(is_sample, ys_ref[...], yp_ref[...])
    o = jnp.where(is_sample, os_ref[...], op_ref[...])
    o = _rms(o.astype(F32), ag_ref[...]).astype(BF16)
    xm = x + _dot(y, wos_ref[...]) + _dot(o, woa_ref[...])
    xmid_ref[...] = xm
    h = _rms(xm, fg_ref[...])
    _rows_to_tiles(h_ref, h)
    h1 = h.astype(BF16)
    h2 = (h - h1.astype(F32)).astype(BF16)
    both = _dot_nt(wr_ref[...], h1)
    logits = both[:N_EXPERTS] + (both[N_EXPERTS:] + _dot_nt(wr_ref[0:N_EXPERTS, :], h2)) + br_ref[...]
    eidx = lax.broadcasted_iota(jnp.int32, (N_EXPERTS, TM), 0)
    slot = lax.broadcasted_iota(jnp.int32, (SUBLANES, TM), 0)
    work = logits
    vals, sels = [], []
    idx_out = jnp.zeros((SUBLANES, TM), jnp.int32)
    for k in range(TOP_K):
        m = jnp.max(work, axis=0, keepdims=True)
        idx = jnp.min(jnp.where(work == m, eidx, N_EXPERTS), axis=0, keepdims=True)
        sel = eidx == idx
        vals.append(m)
        sels.append(sel)
        idx_out = jnp.where(slot == k, idx, idx_out)
        work = jnp.where(sel, -jnp.inf, work)
    es = [jnp.exp(v - vals[0]) for v in vals]
    tot = es[0] + es[1] + es[2] + es[3]
    gate_out = jnp.zeros((SUBLANES, TM), F32)
    for k in range(TOP_K):
        gate_out = jnp.where(slot == k, es[k] / tot, gate_out)
    idx_ref[...] = idx_out
    gate_ref[...] = gate_out
    multi = jnp.zeros((N_EXPERTS, TM), F32)
    for sel in sels:
        multi = jnp.where(sel, 1.0, multi)
    before = _dot(multi.astype(BF16), earlier_ref[...]) + carry_ref[...]
    rank_out = jnp.zeros((SUBLANES, TM), jnp.int32)
    for k in range(TOP_K):
        rk = jnp.sum(jnp.where(sels[k], before, 0.0), axis=0, keepdims=True).astype(jnp.int32)
        rank_out = jnp.where(slot == k, rk, rank_out)
    rank_ref[...] = rank_out
    carry_ref[...] = carry_ref[...] + jnp.sum(multi, axis=1, keepdims=True)
    cnt_ref[...] = carry_ref[...]


def _outproj(xp, xs, yp, ys, op, os_, ag, wos, woa, fg, wr, br, earlier):
    tok = lambda n: pl.BlockSpec((TM, n), lambda i: (i, 0))
    slots = pl.BlockSpec((SUBLANES, TM), lambda i: (0, i))
    prompt = lambda n: pl.BlockSpec((TM, n), lambda i: (jnp.minimum(i, N_PROMPT_TILES - 1), 0))
    sample = lambda n: pl.BlockSpec((TM, n), lambda i: (0, 0))
    return pl.pallas_call(
        _outproj_kernel,
        grid=(N_TILES,),
        in_specs=[
            prompt(D_MODEL), sample(D_MODEL), prompt(D_SSM), sample(D_SSM), prompt(D_ATT), sample(D_ATT),
            _const_spec((1, D_ATT)),
            _const_spec((D_SSM, D_MODEL)), _const_spec((D_ATT, D_MODEL)),
            _const_spec((1, D_MODEL)),
            _const_spec((2 * N_EXPERTS, D_MODEL)), _const_spec((N_EXPERTS, 1)),
            _const_spec((TM, TM)),
        ],
        out_specs=[tok(D_MODEL), pl.BlockSpec(_tiled(TM), lambda i: (i, 0)), slots, slots, slots,
                   _const_spec((N_EXPERTS, 1))],
        out_shape=[
            jax.ShapeDtypeStruct((N_TOK, D_MODEL), F32),
            jax.ShapeDtypeStruct(_tiled(N_TOK), U32),
            jax.ShapeDtypeStruct((SUBLANES, N_TOK), jnp.int32),
            jax.ShapeDtypeStruct((SUBLANES, N_TOK), F32),
            jax.ShapeDtypeStruct((SUBLANES, N_TOK), jnp.int32),
            jax.ShapeDtypeStruct((N_EXPERTS, 1), F32),
        ],
        scratch_shapes=[pltpu.VMEM((N_EXPERTS, 1), F32)],
        compiler_params=_params(),
        name="outproj_router",
    )(xp, xs, yp, ys, op, os_, ag, wos, woa, fg, wr, br, earlier)


def _scatter_kernel(dest_ref, pend_ref, h_ref, rows_ref, zero_ref, sem, zsem):
    i = pl.program_id(0)

    @pl.when(i == 0)
    def _():
        zero_ref[...] = jnp.zeros_like(zero_ref)

        block_tiles = _tiled(MOE_BM)[0]

        def zero_block(b):
            start = pl.multiple_of(b * block_tiles, block_tiles)
            return pltpu.make_async_copy(zero_ref, rows_ref.at[pl.ds(start, block_tiles)], zsem)

        def last_block(e):
            end = pend_ref[e]
            nonempty = end > (pend_ref[e - 1] if e > 0 else 0)
            return nonempty, zero_block(jnp.maximum(end // MOE_BM - 1, 0))

        for e in range(N_EXPERTS):
            nonempty, cp = last_block(e)
            pl.when(nonempty)(cp.start)
        for e in range(N_EXPERTS):
            nonempty, cp = last_block(e)
            pl.when(nonempty)(cp.wait)

        first_unused = pend_ref[N_EXPERTS - 1] // MOE_BM
        lax.fori_loop(first_unused, MOE_BLOCKS, lambda b, c: (zero_block(b).start(), c)[1], 0)
        lax.fori_loop(first_unused, MOE_BLOCKS, lambda b, c: (zero_block(b).wait(), c)[1], 0)

    def issue(r, carry):
        for k in range(TOP_K):
            d = dest_ref[r * TOP_K + k]
            pltpu.make_async_copy(_tile_of(h_ref, r), _tile_of(rows_ref, d), sem).start(priority=k % 2)
        return carry

    lax.fori_loop(0, TM, issue, 0, unroll=ISSUE_UNROLL)
    for _ in range(TOP_K):
        pltpu.make_async_copy(h_ref, rows_ref.at[pl.ds(0, _tiled(TM)[0])], sem).wait()


def _scatter_rows(dest_flat, pad_end, h):
    return pl.pallas_call(
        _scatter_kernel,
        grid=(N_TILES,),
        in_specs=[
            pl.BlockSpec((TM * TOP_K,), lambda i: (i,), memory_space=pltpu.SMEM),
            pl.BlockSpec((N_EXPERTS,), lambda i: (0,), memory_space=pltpu.SMEM),
            pl.BlockSpec(_tiled(TM), lambda i: (i, 0)),
        ],
        out_specs=pl.BlockSpec(memory_space=pl.ANY),
        out_shape=jax.ShapeDtypeStruct(_tiled(MOE_ROWS), U32),
        scratch_shapes=[pltpu.VMEM(_tiled(MOE_BM), U32), pltpu.SemaphoreType.DMA(()),
                        pltpu.SemaphoreType.DMA(())],
        compiler_params=_params(),
        name="moe_scatter",
    )(dest_flat, pad_end, h)


def _expert_kernel(be_ref, nu_ref, nxt_ref, valid_ref, x_ref, wgu_hbm, bgu_ref, wd_hbm, bd_ref, y_ref,
                   wgu_f, wd_f, wgu_s, wd_s, sem):
    i = pl.program_id(0)
    active = i < nu_ref[0]
    e = be_ref[i]

    def fetch(expert):
        return (pltpu.make_async_copy(wgu_hbm.at[expert], wgu_f, sem.at[0]),
                pltpu.make_async_copy(wd_hbm.at[expert], wd_f, sem.at[1]))

    @pl.when(active & (i == 0))
    def _():
        for cp in fetch(e):
            cp.start()

    @pl.when(active & ((i == 0) | (e != be_ref[jnp.maximum(i - 1, 0)])))
    def _():
        for cp in fetch(e):
            cp.wait()
        wgu_s[...] = wgu_f[...].astype(BF16)
        wd_s[...] = wd_f[...].astype(BF16)
        nxt = nxt_ref[e]

        @pl.when(nxt >= 0)
        def _():
            for cp in fetch(nxt):
                cp.start()

    n_valid = jnp.where(active, valid_ref[i], 0)
    sub_tiles = _tiled(MOE_SUB)[0]
    for s in range(MOE_BM // MOE_SUB):
        @pl.when(n_valid > s * MOE_SUB)
        def _():
            x = _tiles_to_rows(x_ref, MOE_SUB, first=s * MOE_SUB).astype(BF16)
            gu = _dot(x, wgu_s[...]) + bgu_ref[0]
            gate = jnp.minimum(gu[:, :D_FF], SWIGLU_LIMIT)
            up = jnp.clip(gu[:, D_FF:], -SWIGLU_LIMIT, SWIGLU_LIMIT)
            act = (up + 1.0) * gate * _sigmoid(gate * SWIGLU_ALPHA)
            _rows_to_tiles(y_ref, _dot(act.astype(BF16), wd_s[...]) + bd_ref[0], first=s * MOE_SUB)

        @pl.when(n_valid <= s * MOE_SUB)
        def _():
            y_ref[s * sub_tiles:(s + 1) * sub_tiles, :] = jnp.zeros((sub_tiles, ROW_TILE[1]), U32)


def _experts(block_expert, n_used, next_expert, block_valid, rows, wgu, bgu, wd, bd):
    grid_spec = pltpu.PrefetchScalarGridSpec(
        num_scalar_prefetch=4,
        grid=(MOE_BLOCKS,),
        in_specs=[
            pl.BlockSpec(_tiled(MOE_BM), lambda i, be, nu, nx, nv: (jnp.minimum(i, nu[0] - 1), 0)),
            pl.BlockSpec(memory_space=pl.ANY),
            pl.BlockSpec((1, 1, 2 * D_FF), lambda i, be, nu, nx, nv: (be[i], 0, 0)),
            pl.BlockSpec(memory_space=pl.ANY),
            pl.BlockSpec((1, 1, D_MODEL), lambda i, be, nu, nx, nv: (be[i], 0, 0)),
        ],
        out_specs=pl.BlockSpec(_tiled(MOE_BM), lambda i, be, nu, nx, nv: (i, 0)),
        scratch_shapes=[pltpu.VMEM((D_MODEL, 2 * D_FF), F32), pltpu.VMEM((D_FF, D_MODEL), F32),
                        pltpu.VMEM((D_MODEL, 2 * D_FF), BF16), pltpu.VMEM((D_FF, D_MODEL), BF16),
                        pltpu.SemaphoreType.DMA((2,))],
    )
    return pl.pallas_call(
        _expert_kernel,
        grid_spec=grid_spec,
        out_shape=jax.ShapeDtypeStruct(_tiled(MOE_ROWS), U32),
        compiler_params=_params(),
        name="moe_experts",
    )(block_expert, n_used, next_expert, block_valid, rows, wgu, bgu, wd, bd)


def _combine_kernel(dest_ref, dest_next_ref, gate_ref, xmid_ref, g_ref, rows_ref, yp_ref, ys_ref, buf_ref, sem):
    i = pl.program_id(0)
    slot = i % 2

    def issue(idx_ref, s):
        def body(r, carry):
            for k in range(TOP_K):
                d = idx_ref[r * TOP_K + k]
                pltpu.make_async_copy(_tile_of(rows_ref, d), _tile_of(buf_ref.at[s], k * TM + r),
                                      sem.at[s]).start(priority=k % 2)
            return carry

        lax.fori_loop(0, TM, body, 0, unroll=ISSUE_UNROLL)

    @pl.when(i == 0)
    def _():
        issue(dest_ref, 0)

    @pl.when(i + 1 < N_TILES)
    def _():
        issue(dest_next_ref, 1 - slot)

    buf = buf_ref.at[slot]
    slot_tiles = _tiled(TM)[0]
    for k in range(TOP_K):
        pltpu.make_async_copy(rows_ref.at[pl.ds(0, slot_tiles)], buf.at[pl.ds(k * slot_tiles, slot_tiles)],
                              sem.at[slot]).wait()
    acc = xmid_ref[...]
    for k in range(TOP_K):
        acc = acc + _tiles_to_rows(buf, TM, first=k * TM) * gate_ref[:, k:k + 1]
    y = _rms(acc, g_ref[...])

    @pl.when(i < N_PROMPT_TILES)
    def _():
        yp_ref[...] = y

    @pl.when(i == N_PROMPT_TILES)
    def _():
        ys_ref[...] = y


def _combine(dest_flat, gates, xmid, g, y_rows):
    return pl.pallas_call(
        _combine_kernel,
        grid=(N_TILES,),
        in_specs=[
            pl.BlockSpec((TM * TOP_K,), lambda i: (i,), memory_space=pltpu.SMEM),
            pl.BlockSpec((TM * TOP_K,), lambda i: (jnp.minimum(i + 1, N_TILES - 1),), memory_space=pltpu.SMEM),
            pl.BlockSpec((TM, TOP_K), lambda i: (i, 0)),
            pl.BlockSpec((TM, D_MODEL), lambda i: (i, 0)),
            _const_spec((1, D_MODEL)),
            pl.BlockSpec(memory_space=pl.ANY),
        ],
        out_specs=[
            pl.BlockSpec((TM, D_MODEL), lambda i: (jnp.minimum(i, N_PROMPT_TILES - 1), 0)),
            pl.BlockSpec((TM, D_MODEL), lambda i: (0, 0)),
        ],
        out_shape=[
            jax.ShapeDtypeStruct((N_PROMPT, D_MODEL), F32),
            jax.ShapeDtypeStruct((N_SAMPLE, D_MODEL), F32),
        ],
        scratch_shapes=[pltpu.VMEM((2,) + _tiled(TOP_K * TM), U32), pltpu.SemaphoreType.DMA((2,))],
        compiler_params=_params(),
        name="moe_combine",
    )(dest_flat, dest_flat, gates, xmid, g, y_rows)


def _band_bias(table):
    n_diag = BAND + CHUNK - 1
    idx = np.clip(ATT_LEFT + (CHUNK - 1) - np.arange(n_diag), -REL_CLIP, REL_CLIP) + REL_CLIP
    pick = (np.arange(2 * REL_CLIP + 1)[:, None] == idx[None, :]).astype(np.float32)
    diag = jnp.dot(table, jnp.asarray(pick), precision=lax.Precision.HIGHEST)
    return jnp.stack([diag[:, CHUNK - 1 - qi:CHUNK - 1 - qi + BAND] for qi in range(CHUNK)], axis=1)


def _pair_rows(v):
    return jnp.repeat(v.reshape(HEAD_PAIRS, 2), CHUNK, axis=1)


def _layer(l, xp, xs, cache_k, cache_v, state_conv, state_ssm,
           norm_mix_g, w_in, conv_w, conv_b, dt_bias, a_log, d_skip, ssm_norm_g,
           att_norm_g, rel_bias_table, w_out, norm_ffn_g, w_router, b_router,
           w_gate_up, b_gate_up, w_down, b_down, norm_final_g):
    wb = w_in[l].astype(BF16)
    c0 = D_SSM
    c1 = c0 + D_CONV
    c2 = c1 + SSM_HEADS
    c3 = c2 + D_ATT
    c4 = c3 + D_ATT
    z, xbc, dt, dtt, q, k, v, k_p, v_p, k_s, v_s, ctail = _inproj(
        xp, xs, norm_mix_g[l][None], wb[:, :c0], wb[:, c0:c1], wb[:, c1:c2], wb[:, c1:c2].T,
        wb[:, c2:c3], wb[:, c3:c4], wb[:, c4:])

    n_chunks = N_TOK // CHUNK
    dtp = dtt.reshape(HEAD_PAIRS, 2, n_chunks, CHUNK).transpose(2, 0, 1, 3).reshape(n_chunks, HEAD_PAIRS, LANES)
    hp = jnp.arange(D_SSM) // SSM_HEAD_DIM
    expand = (hp[None, :] == jnp.arange(SSM_HEADS)[:, None]).astype(BF16)
    lane = jnp.arange(LANES)
    triu2 = ((lane[:, None] // CHUNK == lane[None, :] // CHUNK) & (lane[:, None] <= lane[None, :])).astype(BF16)
    consts = (conv_w[l], conv_b[l][None], dt_bias[l][None], _pair_rows(dt_bias[l]),
              a_log[l][None], _pair_rows(a_log[l]),
              jnp.repeat(d_skip[l], SSM_HEAD_DIM)[None], ssm_norm_g[l][None],
              expand, triu2)
    y_ssm_p, ssm_p = _ssd_prompt(z, xbc, dt, dtp, consts)
    y_ssm_s, ssm_s = _ssd_sample(z, xbc, dt, dtp, state_conv[l],
                                 state_ssm[l].reshape(DEC_BATCH, D_SSM, D_STATE), consts)

    bias2 = _band_bias(rel_bias_table[l]).reshape(ATT_PAIRS, 2 * CHUNK, BAND)
    o_att_p = _attn_prompt(q, k, v, bias2)
    o_att_s = _attn_sample(q, k, v, cache_k[l].reshape(DEC_BATCH, ATT_LEFT, D_ATT),
                           cache_v[l].reshape(DEC_BATCH, ATT_LEFT, D_ATT), bias2)

    wo = w_out[l].astype(BF16)
    wr = w_router[l].T
    wr1 = wr.astype(BF16)
    wr2 = (wr - wr1.astype(F32)).astype(BF16)
    earlier = jnp.triu(jnp.ones((TM, TM), BF16), 1)
    xmid, h, top_idx, gates, rank, counts = _outproj(
        xp, xs, y_ssm_p, y_ssm_s, o_att_p, o_att_s, att_norm_g[l][None], wo[:D_SSM], wo[D_SSM:], norm_ffn_g[l][None],
        jnp.concatenate([wr1, wr2], axis=0), b_router[l][:, None], earlier)
    gates = gates[:TOP_K].T

    counts = counts[:, 0].astype(jnp.int32)
    padded = (counts + MOE_BM - 1) // MOE_BM * MOE_BM
    pad_end = jnp.cumsum(padded)
    pad_start = pad_end - padded
    experts = jnp.arange(N_EXPERTS, dtype=jnp.int32)
    start_of = jnp.sum(jnp.where(top_idx[:TOP_K, :, None] == experts, pad_start, 0), axis=-1)
    dest = (start_of + rank[:TOP_K]).T.reshape(-1).astype(jnp.int32)
    block_start = jnp.arange(MOE_BLOCKS, dtype=jnp.int32) * MOE_BM
    block_expert = jnp.minimum(jnp.sum((pad_end[None, :] <= block_start[:, None]).astype(jnp.int32), axis=1),
                               N_EXPERTS - 1).astype(jnp.int32)
    n_used = (pad_end[-1:] // MOE_BM).astype(jnp.int32)
    block_valid = jnp.clip((pad_start + counts)[block_expert] - block_start, 0, MOE_BM).astype(jnp.int32)
    later_nonempty = (experts[None, :] > experts[:, None]) & (padded[None, :] > 0)
    next_expert = jnp.min(jnp.where(later_nonempty, experts[None, :], N_EXPERTS), axis=1)
    next_expert = jnp.where(next_expert < N_EXPERTS, next_expert, -1).astype(jnp.int32)

    rows = _scatter_rows(dest, pad_end.astype(jnp.int32), h)
    y_rows = _experts(block_expert, n_used, next_expert, block_valid, rows, w_gate_up[l], b_gate_up[l][:, None, :],
                      w_down[l], b_down[l][:, None, :])
    y_p, y_s = _combine(dest, gates, xmid, norm_final_g[None], y_rows)

    keep = min(ATT_LEFT, SEQ)
    k_p = k_p[:, TM - keep:].reshape(BATCH, keep, ATT_HEADS, ATT_HEAD_DIM)
    v_p = v_p[:, TM - keep:].reshape(BATCH, keep, ATT_HEADS, ATT_HEAD_DIM)
    k_s = k_s.reshape(DEC_BATCH, DEC_SEQ, ATT_HEADS, ATT_HEAD_DIM)
    v_s = v_s.reshape(DEC_BATCH, DEC_SEQ, ATT_HEADS, ATT_HEAD_DIM)
    conv_p = ctail[:BATCH, -(CONV_W - 1):]
    conv_s = ctail[BATCH].reshape(DEC_BATCH, SUBLANES, D_CONV)[:, -(CONV_W - 1):]
    ssm_p = ssm_p.reshape(BATCH, SSM_HEADS, SSM_HEAD_DIM, D_STATE)
    ssm_s = ssm_s.reshape(DEC_BATCH, SSM_HEADS, SSM_HEAD_DIM, D_STATE)
    return (y_p.reshape(BATCH, SEQ, D_MODEL), y_s.reshape(DEC_BATCH, DEC_SEQ, D_MODEL),
            k_p, v_p, conv_p, ssm_p, k_s, v_s, conv_s, ssm_s)


def kernel(x_prompt, x_sample, cache_k, cache_v, state_conv, state_ssm, norm_mix_g, w_in, conv_w, conv_b,
           dt_bias, a_log, d_skip, ssm_norm_g, att_norm_g, rel_bias_table, w_out, norm_ffn_g, w_router,
           b_router, w_gate_up, b_gate_up, w_down, b_down, norm_final_g):
    assert w_in.shape[0] == 1, "single trunk layer"
    xp = x_prompt.reshape(N_PROMPT, D_MODEL)
    xs = x_sample.reshape(N_SAMPLE, D_MODEL)
    outs = _layer(0, xp, xs, cache_k, cache_v, state_conv, state_ssm,
                  norm_mix_g, w_in, conv_w, conv_b, dt_bias, a_log, d_skip, ssm_norm_g,
                  att_norm_g, rel_bias_table, w_out, norm_ffn_g, w_router, b_router,
                  w_gate_up, b_gate_up, w_down, b_down, norm_final_g)
    y_p, y_s, k_p, v_p, conv_p, ssm_p, k_s, v_s, conv_s, ssm_s = outs
    return (y_p, y_s, k_p[None], v_p[None], conv_p[None], ssm_p[None],
            k_s[None], v_s[None], conv_s[None], ssm_s[None])
```

```python
import jax
import jax.numpy as jnp
import numpy as np
from jax import lax
from jax.experimental import pallas as pl
from jax.experimental.pallas import tpu as pltpu

D_MODEL = 1024
BATCH = 8
SEQ = 2048
DEC_BATCH = 8
DEC_SEQ = 64
CHUNK = 64
SSM_HEADS = 16
SSM_HEAD_DIM = 64
D_SSM = SSM_HEADS * SSM_HEAD_DIM
SSM_GROUPS = 2
D_STATE = 128
CONV_W = 4
D_BC = SSM_GROUPS * D_STATE
D_CONV = D_SSM + 2 * D_BC
ATT_HEADS = 8
ATT_HEAD_DIM = 64
D_ATT = ATT_HEADS * ATT_HEAD_DIM
LEFT_CHUNKS = 8
ATT_LEFT = LEFT_CHUNKS * CHUNK
BAND = ATT_LEFT + CHUNK
REL_CLIP = 128
ATT_SCALE = ATT_HEAD_DIM ** -0.5
N_EXPERTS = 32
TOP_K = 4
D_FF = D_MODEL
SWIGLU_ALPHA = 1.702
SWIGLU_LIMIT = 7.0
EPS = 1e-5

F32 = jnp.float32
BF16 = jnp.bfloat16
U32 = jnp.uint32

LANES = 128
SUBLANES = 8

N_PROMPT = BATCH * SEQ
N_SAMPLE = DEC_BATCH * DEC_SEQ
N_TOK = N_PROMPT + N_SAMPLE
TM = 512
N_PROMPT_TILES = N_PROMPT // TM
N_TILES = N_TOK // TM
TILES_PER_SEQ = SEQ // TM
CHUNKS_PER_TILE = TM // CHUNK
N_CHUNKS_SEQ = SEQ // CHUNK
assert 2 * SSM_HEAD_DIM == LANES and 2 * ATT_HEAD_DIM == LANES
HEAD_PAIRS = SSM_HEADS // 2
PAIRS_PER_GROUP = HEAD_PAIRS // SSM_GROUPS
ATT_PAIRS = ATT_HEADS // 2
SSD_BLOCK = 2
CONV_HEAD = SUBLANES
MOE_BM = 512
WEIGHT_COPIES = 4
N_ASSIGN = N_TOK * TOP_K
MOE_BLOCKS = N_ASSIGN // MOE_BM + N_EXPERTS
MOE_ROWS = MOE_BLOCKS * MOE_BM
ISSUE_UNROLL = 8
ROW_TILE = (SUBLANES // 2, LANES)
assert 2 * ROW_TILE[0] * ROW_TILE[1] == D_MODEL
VMEM_BYTES = 64 * 1024 * 1024
VMEM_LIMIT = VMEM_BYTES * 7 // 8


def _dot(a, b):
    return jnp.dot(a, b, preferred_element_type=F32)


def _dot_nt(a, b):
    return lax.dot_general(a, b, (((1,), (1,)), ((), ())), preferred_element_type=F32)


def _dot_tn(a, b):
    return lax.dot_general(a, b, (((0,), (0,)), ((), ())), preferred_element_type=F32)


def _split3(x):
    x1 = x.astype(BF16)
    r1 = x - x1.astype(F32)
    x2 = r1.astype(BF16)
    r2 = r1 - x2.astype(F32)
    return x1, x2, r2.astype(BF16)


def _dot_exact_rhs(x, m):
    x1, x2, x3 = _split3(x)
    return _dot(x1, m) + _dot(x2, m) + _dot(x3, m)


def _dot_exact_lhs(m, x):
    x1, x2, x3 = _split3(x)
    return _dot(m, x1) + _dot(m, x2) + _dot(m, x3)


def _rms(x, g):
    return x * lax.rsqrt(jnp.mean(x * x, axis=-1, keepdims=True) + EPS) * g


def _sigmoid(x):
    return 1.0 / (1.0 + jnp.exp(-x))


def _softplus(x):
    return jnp.maximum(x, 0.0) + jnp.log(1.0 + jnp.exp(-jnp.abs(x)))


def _tiled(n):
    return (n * ROW_TILE[0], ROW_TILE[1])


def _tile_of(ref, row):
    return ref.at[pl.ds(pl.multiple_of(row * ROW_TILE[0], ROW_TILE[0]), ROW_TILE[0])]


def _rows_to_tiles(ref, x, first=0):
    sub, lanes = ROW_TILE
    half = D_MODEL // 2
    hi = lax.bitcast_convert_type(x[:, :half].astype(BF16).astype(F32), U32)
    lo = lax.bitcast_convert_type(x[:, half:].astype(BF16).astype(F32), U32)
    words = hi | (lo >> 16)
    for j in range(sub):
        ref[pl.ds(first * sub + j, x.shape[0], stride=sub), :] = words[:, j * lanes:(j + 1) * lanes]


def _tiles_to_rows(ref, n, first=0):
    sub = ROW_TILE[0]
    words = jnp.concatenate([ref[pl.ds(first * sub + j, n, stride=sub), :] for j in range(sub)], axis=1)
    hi = lax.bitcast_convert_type(words & jnp.uint32(0xFFFF0000), F32)
    lo = lax.bitcast_convert_type(words << 16, F32)
    return jnp.concatenate([hi, lo], axis=1)


def _const_spec(shape):
    nd = len(shape)
    return pl.BlockSpec(shape, lambda *_: (0,) * nd)


def _params(n_axes=1):
    return pltpu.CompilerParams(dimension_semantics=("arbitrary",) * n_axes,
                                vmem_limit_bytes=VMEM_LIMIT)


def _inproj_kernel(xp_ref, xs_ref, g_ref, wz_ref, wxbc_ref, wdt_ref, wdtt_ref, wq_ref, wk_ref, wv_ref,
                   z_ref, xbc_ref, dt_ref, dtt_ref, q_ref, k_ref, v_ref, kp_ref, vp_ref, ks_ref, vs_ref, ctail_ref):
    i = pl.program_id(0)
    x = jnp.where(i == N_PROMPT_TILES, xs_ref[...], xp_ref[...])
    h = _rms(x, g_ref[...]).astype(BF16)
    z_ref[...] = _dot(h, wz_ref[...]).astype(BF16)
    xbc = _dot(h, wxbc_ref[...])
    xbc_ref[...] = xbc.astype(BF16)
    for c in range(CHUNKS_PER_TILE):
        ctail_ref[0, c * SUBLANES:(c + 1) * SUBLANES, :] = xbc[(c + 1) * CHUNK - SUBLANES:(c + 1) * CHUNK, :]
    dt_ref[...] = _dot(h, wdt_ref[...])
    dtt_ref[...] = _dot_nt(wdtt_ref[...], h)
    q_ref[...] = (_dot(h, wq_ref[...]) * ATT_SCALE).astype(BF16)
    k = _dot(h, wk_ref[...])
    v = _dot(h, wv_ref[...])
    k_ref[...] = k.astype(BF16)
    v_ref[...] = v.astype(BF16)

    @pl.when(i < N_PROMPT_TILES)
    def _():
        kp_ref[0] = k
        vp_ref[0] = v

    @pl.when(i == N_PROMPT_TILES)
    def _():
        ks_ref[...] = k
        vs_ref[...] = v


def _inproj(xp, xs, g, wz, wxbc, wdt, wdtt, wq, wk, wv):
    tok = lambda n: pl.BlockSpec((TM, n), lambda i: (i, 0))
    tail_idx = lambda i: (i // TILES_PER_SEQ, 0, 0)
    seq_idx = lambda i: (jnp.minimum(i // TILES_PER_SEQ, BATCH - 1), 0, 0)
    n_tail = BATCH + 1
    return pl.pallas_call(
        _inproj_kernel,
        grid=(N_TILES,),
        in_specs=[
            pl.BlockSpec((TM, D_MODEL), lambda i: (jnp.minimum(i, N_PROMPT_TILES - 1), 0)),
            pl.BlockSpec((TM, D_MODEL), lambda i: (0, 0)),
            _const_spec((1, D_MODEL)),
            _const_spec((D_MODEL, D_SSM)),
            _const_spec((D_MODEL, D_CONV)),
            _const_spec((D_MODEL, SSM_HEADS)),
            _const_spec((SSM_HEADS, D_MODEL)),
            _const_spec((D_MODEL, D_ATT)),
            _const_spec((D_MODEL, D_ATT)),
            _const_spec((D_MODEL, D_ATT)),
        ],
        out_specs=[
            tok(D_SSM), tok(D_CONV), tok(SSM_HEADS),
            pl.BlockSpec((SSM_HEADS, TM), lambda i: (0, i)),
            tok(D_ATT), tok(D_ATT), tok(D_ATT),
            pl.BlockSpec((1, TM, D_ATT), seq_idx),
            pl.BlockSpec((1, TM, D_ATT), seq_idx),
            pl.BlockSpec((TM, D_ATT), lambda i: (0, 0)),
            pl.BlockSpec((TM, D_ATT), lambda i: (0, 0)),
            pl.BlockSpec((1, CHUNKS_PER_TILE * SUBLANES, D_CONV), tail_idx),
        ],
        out_shape=[
            jax.ShapeDtypeStruct((N_TOK, D_SSM), BF16),
            jax.ShapeDtypeStruct((N_TOK, D_CONV), BF16),
            jax.ShapeDtypeStruct((N_TOK, SSM_HEADS), F32),
            jax.ShapeDtypeStruct((SSM_HEADS, N_TOK), F32),
            jax.ShapeDtypeStruct((N_TOK, D_ATT), BF16),
            jax.ShapeDtypeStruct((N_TOK, D_ATT), BF16),
            jax.ShapeDtypeStruct((N_TOK, D_ATT), BF16),
            jax.ShapeDtypeStruct((BATCH, TM, D_ATT), F32),
            jax.ShapeDtypeStruct((BATCH, TM, D_ATT), F32),
            jax.ShapeDtypeStruct((N_SAMPLE, D_ATT), F32),
            jax.ShapeDtypeStruct((N_SAMPLE, D_ATT), F32),
            jax.ShapeDtypeStruct((n_tail, CHUNKS_PER_TILE * SUBLANES, D_CONV), F32),
        ],
        compiler_params=_params(),
        name="inproj",
    )(xp, xs, g, wz, wxbc, wdt, wdtt, wq, wk, wv)


def _ssd_tile(n_chunks, z_ref, xbc_ref, dt_ref, dtp_ref, cw_ref, cb_ref, dtb_ref, dtbp_ref,
              alog_h_ref, alog_p_ref, dskip_e_ref, ng_ref, expand_ref, triu2_ref,
              y_ref, xw_ref, state_ref):
    nb = SSD_BLOCK if n_chunks % SSD_BLOCK == 0 else 1
    rb = nb * CHUNK
    half = D_SSM // SSM_GROUPS
    xw_ref[CONV_HEAD:CONV_HEAD + n_chunks * CHUNK, :] = xbc_ref[...].astype(F32)

    a_h = -jnp.exp(alog_h_ref[...])
    a_p = jnp.concatenate([-jnp.exp(alog_p_ref[...])] * nb, axis=0)
    dtb_p = jnp.concatenate([dtbp_ref[...]] * nb, axis=0)
    tr_r = lax.broadcasted_iota(jnp.int32, (rb, rb), 0)
    tr_c = lax.broadcasted_iota(jnp.int32, (rb, rb), 1)
    tril_b = jnp.where((tr_r // CHUNK == tr_c // CHUNK) & (tr_c <= tr_r), 1.0, 0.0).astype(BF16)
    row_i = lax.broadcasted_iota(jnp.int32, (CHUNK, LANES), 0)
    col_i = lax.broadcasted_iota(jnp.int32, (CHUNK, LANES), 1)
    causal2 = row_i >= (col_i % CHUNK)
    sel_r = lax.broadcasted_iota(jnp.int32, ((CONV_W - 1) * rb, CONV_HEAD + rb), 0)
    sel_c = lax.broadcasted_iota(jnp.int32, ((CONV_W - 1) * rb, CONV_HEAD + rb), 1)
    shift_sel = (sel_c == sel_r % rb + sel_r // rb + CONV_HEAD - (CONV_W - 1)).astype(F32)
    bd_r = lax.broadcasted_iota(jnp.int32, (LANES, LANES), 0) // CHUNK
    bd_c = lax.broadcasted_iota(jnp.int32, (LANES, LANES), 1) // CHUNK
    blockdiag = bd_r == bd_c

    def block(i, carry):
        r0 = pl.multiple_of(i * rb, rb)
        win = xw_ref[pl.ds(r0, CONV_HEAD + rb), :]
        shifted = _dot(shift_sel, win)
        acc = cb_ref[...] + cw_ref[CONV_W - 1:CONV_W, :] * win[CONV_HEAD:, :]
        for tap in range(CONV_W - 1):
            acc = acc + cw_ref[tap:tap + 1, :] * shifted[tap * rb:(tap + 1) * rb, :]
        xa = acc * _sigmoid(acc)
        xs = xa[:, 0:D_SSM]
        bm = xa[:, D_SSM:D_SSM + D_BC].astype(BF16)
        cm = xa[:, D_SSM + D_BC:D_CONV].astype(BF16)
        dt = _softplus(dt_ref[pl.ds(r0, rb), :] + dtb_ref[...])
        dt_e = _dot_exact_rhs(dt, expand_ref[...])
        acum = _dot_exact_rhs(_dot_exact_lhs(tril_b, dt * a_h), expand_ref[...])
        dtp = _softplus(dtp_ref[pl.ds(i * nb, nb)].reshape(nb * HEAD_PAIRS, LANES) + dtb_p)
        acum_p = _dot_exact_rhs(dtp * a_p, triu2_ref[...])
        xdt = xs * dt_e
        a_last = [acum[(c + 1) * CHUNK - 1:(c + 1) * CHUNK, :] for c in range(nb)]
        a_end = jnp.concatenate([jnp.broadcast_to(a, (CHUNK, D_SSM)) for a in a_last], axis=0)
        xdt_end = (xdt * jnp.exp(a_end - acum)).astype(BF16)

        y_diag, new_s = [], []
        for c in range(nb):
            rs = slice(c * CHUNK, (c + 1) * CHUNK)
            y_parts = []
            for g in range(SSM_GROUPS):
                bg = bm[rs, g * D_STATE:(g + 1) * D_STATE]
                cg = cm[rs, g * D_STATE:(g + 1) * D_STATE]
                cb2 = _dot_nt(cg, jnp.concatenate([bg, bg], axis=0))
                for jj in range(PAIRS_PER_GROUP):
                    j = g * PAIRS_PER_GROUP + jj
                    seg = acum[rs, j * LANES:(j + 1) * LANES] - acum_p[c * HEAD_PAIRS + j:c * HEAD_PAIRS + j + 1, :]
                    decay = jnp.exp(jnp.where(causal2, seg, -jnp.inf))
                    x2 = xdt[rs, j * LANES:(j + 1) * LANES]
                    rhs = jnp.where(blockdiag, jnp.concatenate([x2, x2], axis=0), 0.0).astype(BF16)
                    y_parts.append(_dot((cb2 * decay).astype(BF16), rhs))
            y_diag.append(jnp.concatenate(y_parts, axis=1))
            new_s.append(jnp.concatenate(
                [_dot_tn(bm[rs, g * D_STATE:(g + 1) * D_STATE], xdt_end[rs, g * half:(g + 1) * half])
                 for g in range(SSM_GROUPS)], axis=1))

        y_off = []
        for c in range(nb):
            rs = slice(c * CHUNK, (c + 1) * CHUNK)
            state = state_ref[...]
            y_off.append(jnp.concatenate(
                [_dot(cm[rs, g * D_STATE:(g + 1) * D_STATE], state[:, g * half:(g + 1) * half].astype(BF16))
                 for g in range(SSM_GROUPS)], axis=1))
            state_ref[...] = state * jnp.exp(a_last[c]) + new_s[c]

        y = (jnp.concatenate(y_diag, axis=0) + jnp.concatenate(y_off, axis=0) * jnp.exp(acum)
             + dskip_e_ref[...] * xs)
        zc = z_ref[pl.ds(r0, rb), :].astype(F32)
        y = y * (zc * _sigmoid(zc))
        yn = jnp.concatenate(
            [y[:, g * half:(g + 1) * half]
             * lax.rsqrt(jnp.mean(jnp.square(y[:, g * half:(g + 1) * half]), axis=-1, keepdims=True) + EPS)
             for g in range(SSM_GROUPS)], axis=1)
        y_ref[pl.ds(r0, rb), :] = (yn * ng_ref[...]).astype(BF16)
        return carry

    lax.fori_loop(0, n_chunks // nb, block, 0)


def _state_store(state_ref, out_ref):
    for j in range(HEAD_PAIRS):
        out_ref[0, j * LANES:(j + 1) * LANES, :] = state_ref[:, j * LANES:(j + 1) * LANES].T


def _ssd_prompt_kernel(z_ref, xbc_ref, dt_ref, dtp_ref, cw_ref, cb_ref, dtb_ref, dtbp_ref,
                       alog_h_ref, alog_p_ref, dskip_e_ref, ng_ref, expand_ref, triu2_ref,
                       y_ref, ssm_ref, xw_ref, state_ref, tail_ref):
    t = pl.program_id(1)

    @pl.when(t == 0)
    def _():
        state_ref[...] = jnp.zeros_like(state_ref)
        xw_ref[0:CONV_HEAD, :] = jnp.zeros((CONV_HEAD, D_CONV), F32)

    @pl.when(t > 0)
    def _():
        xw_ref[0:CONV_HEAD, :] = tail_ref[...]

    _ssd_tile(CHUNKS_PER_TILE, z_ref, xbc_ref, dt_ref, dtp_ref, cw_ref, cb_ref, dtb_ref, dtbp_ref,
              alog_h_ref, alog_p_ref, dskip_e_ref, ng_ref, expand_ref, triu2_ref,
              y_ref, xw_ref, state_ref)
    tail_ref[...] = xw_ref[TM:TM + CONV_HEAD, :]

    @pl.when(t == TILES_PER_SEQ - 1)
    def _():
        _state_store(state_ref, ssm_ref)


def _ssd_sample_kernel(z_ref, xbc_ref, dt_ref, dtp_ref, cprev_ref, sprev_ref,
                       cw_ref, cb_ref, dtb_ref, dtbp_ref,
                       alog_h_ref, alog_p_ref, dskip_e_ref, ng_ref, expand_ref, triu2_ref,
                       y_ref, ssm_ref, xw_ref, state_ref):
    xw_ref[0:CONV_HEAD, :] = jnp.zeros((CONV_HEAD, D_CONV), F32)
    xw_ref[CONV_HEAD - (CONV_W - 1):CONV_HEAD, :] = cprev_ref[0]
    for j in range(HEAD_PAIRS):
        state_ref[:, j * LANES:(j + 1) * LANES] = sprev_ref[0, j * LANES:(j + 1) * LANES, :].T
    _ssd_tile(1, z_ref, xbc_ref, dt_ref, dtp_ref, cw_ref, cb_ref, dtb_ref, dtbp_ref,
              alog_h_ref, alog_p_ref, dskip_e_ref, ng_ref, expand_ref, triu2_ref,
              y_ref, xw_ref, state_ref)
    _state_store(state_ref, ssm_ref)


def _ssd_const_specs():
    return [
        _const_spec((CONV_W, D_CONV)), _const_spec((1, D_CONV)),
        _const_spec((1, SSM_HEADS)), _const_spec((HEAD_PAIRS, LANES)),
        _const_spec((1, SSM_HEADS)), _const_spec((HEAD_PAIRS, LANES)),
        _const_spec((1, D_SSM)), _const_spec((1, D_SSM)),
        _const_spec((SSM_HEADS, D_SSM)), _const_spec((LANES, LANES)),
    ]


def _ssd_prompt(z, xbc, dt, dtp, consts):
    tile = lambda b, t: (b * TILES_PER_SEQ + t, 0)
    return pl.pallas_call(
        _ssd_prompt_kernel,
        grid=(BATCH, TILES_PER_SEQ),
        in_specs=[
            pl.BlockSpec((TM, D_SSM), tile),
            pl.BlockSpec((TM, D_CONV), tile),
            pl.BlockSpec((TM, SSM_HEADS), tile),
            pl.BlockSpec((CHUNKS_PER_TILE, HEAD_PAIRS, LANES), lambda b, t: (b * TILES_PER_SEQ + t, 0, 0)),
        ] + _ssd_const_specs(),
        out_specs=[
            pl.BlockSpec((TM, D_SSM), tile),
            pl.BlockSpec((1, D_SSM, D_STATE), lambda b, t: (b, 0, 0)),
        ],
        out_shape=[
            jax.ShapeDtypeStruct((N_PROMPT, D_SSM), BF16),
            jax.ShapeDtypeStruct((BATCH, D_SSM, D_STATE), F32),
        ],
        scratch_shapes=[
            pltpu.VMEM((CONV_HEAD + TM, D_CONV), F32),
            pltpu.VMEM((D_STATE, D_SSM), F32),
            pltpu.VMEM((CONV_HEAD, D_CONV), F32),
        ],
        compiler_params=_params(2),
        name="ssd_prompt",
    )(z, xbc, dt, dtp, *consts)


def _ssd_sample(z, xbc, dt, dtp, conv_prev, ssm_prev, consts):
    first = N_PROMPT // CHUNK
    row = lambda b: (first + b, 0)
    return pl.pallas_call(
        _ssd_sample_kernel,
        grid=(DEC_BATCH,),
        in_specs=[
            pl.BlockSpec((CHUNK, D_SSM), row),
            pl.BlockSpec((CHUNK, D_CONV), row),
            pl.BlockSpec((CHUNK, SSM_HEADS), row),
            pl.BlockSpec((1, HEAD_PAIRS, LANES), lambda b: (first + b, 0, 0)),
            pl.BlockSpec((1, CONV_W - 1, D_CONV), lambda b: (b, 0, 0)),
            pl.BlockSpec((1, D_SSM, D_STATE), lambda b: (b, 0, 0)),
        ] + _ssd_const_specs(),
        out_specs=[
            pl.BlockSpec((CHUNK, D_SSM), lambda b: (b, 0)),
            pl.BlockSpec((1, D_SSM, D_STATE), lambda b: (b, 0, 0)),
        ],
        out_shape=[
            jax.ShapeDtypeStruct((N_SAMPLE, D_SSM), BF16),
            jax.ShapeDtypeStruct((DEC_BATCH, D_SSM, D_STATE), F32),
        ],
        scratch_shapes=[
            pltpu.VMEM((CONV_HEAD + CHUNK, D_CONV), F32),
            pltpu.VMEM((D_STATE, D_SSM), F32),
        ],
        compiler_params=_params(),
        name="ssd_sample",
    )(z, xbc, dt, dtp, conv_prev, ssm_prev, *consts)


def _attn_chunks(n_chunks, first_chunk, q_ref, kpad_ref, vpad_ref, bias_ref, o_ref):
    lane = lax.broadcasted_iota(jnp.int32, (CHUNK, LANES), 1)
    low = lane < ATT_HEAD_DIM
    kj = lax.broadcasted_iota(jnp.int32, (2 * CHUNK, BAND), 1)

    def chunk(c, carry, masked):
        r0 = pl.multiple_of(c * CHUNK, CHUNK)
        if masked:
            valid = kj >= (LEFT_CHUNKS - (first_chunk + c)) * CHUNK
        scores = []
        for j in range(ATT_PAIRS):
            qp = q_ref[pl.ds(r0, CHUNK), j * LANES:(j + 1) * LANES]
            zero = jnp.zeros_like(qp)
            q2 = jnp.concatenate([jnp.where(low, qp, zero), jnp.where(low, zero, qp)], axis=0)
            kb = kpad_ref[pl.ds(r0, BAND), j * LANES:(j + 1) * LANES]
            s = _dot_nt(q2, kb) + bias_ref[j]
            scores.append(jnp.where(valid, s, -jnp.inf) if masked else s)
        probs = []
        for s in scores:
            e = jnp.exp(s - jnp.max(s, axis=-1, keepdims=True))
            probs.append((e.astype(BF16), jnp.sum(e, axis=-1, keepdims=True)))
        outs = []
        for j, (e, denom) in enumerate(probs):
            vb = vpad_ref[pl.ds(r0, BAND), j * LANES:(j + 1) * LANES]
            r = _dot(e, vb) / denom
            outs.append(jnp.where(low, r[0:CHUNK], r[CHUNK:2 * CHUNK]))
        o_ref[pl.ds(r0, CHUNK), :] = jnp.concatenate(outs, axis=1).astype(BF16)
        return carry

    n_masked = min(max(LEFT_CHUNKS - first_chunk, 0), n_chunks)
    for lo, hi, masked in ((0, n_masked, True), (n_masked, n_chunks, False)):
        if hi > lo:
            lax.fori_loop(lo, hi, lambda c, carry, masked=masked: chunk(c, carry, masked), 0,
                          unroll=2 if (hi - lo) % 2 == 0 else 1)


def _attn_prompt_kernel(q_ref, k_ref, v_ref, bias_ref, o_ref, kpad_ref, vpad_ref):
    kpad_ref[0:ATT_LEFT, :] = jnp.zeros((ATT_LEFT, D_ATT), BF16)
    vpad_ref[0:ATT_LEFT, :] = jnp.zeros((ATT_LEFT, D_ATT), BF16)
    kpad_ref[ATT_LEFT:ATT_LEFT + SEQ, :] = k_ref[...]
    vpad_ref[ATT_LEFT:ATT_LEFT + SEQ, :] = v_ref[...]
    _attn_chunks(N_CHUNKS_SEQ, 0, q_ref, kpad_ref, vpad_ref, bias_ref, o_ref)


def _attn_sample_kernel(q_ref, k_ref, v_ref, ck_ref, cv_ref, bias_ref, o_ref, kpad_ref, vpad_ref):
    kpad_ref[0:ATT_LEFT, :] = ck_ref[0].astype(BF16)
    vpad_ref[0:ATT_LEFT, :] = cv_ref[0].astype(BF16)
    kpad_ref[ATT_LEFT:BAND, :] = k_ref[...]
    vpad_ref[ATT_LEFT:BAND, :] = v_ref[...]
    _attn_chunks(1, LEFT_CHUNKS, q_ref, kpad_ref, vpad_ref, bias_ref, o_ref)


def _attn_prompt(q, k, v, bias2):
    seq = pl.BlockSpec((SEQ, D_ATT), lambda b: (b, 0))
    return pl.pallas_call(
        _attn_prompt_kernel,
        grid=(BATCH,),
        in_specs=[seq, seq, seq, _const_spec((ATT_PAIRS, 2 * CHUNK, BAND))],
        out_specs=seq,
        out_shape=jax.ShapeDtypeStruct((N_PROMPT, D_ATT), BF16),
        scratch_shapes=[pltpu.VMEM((ATT_LEFT + SEQ, D_ATT), BF16),
                        pltpu.VMEM((ATT_LEFT + SEQ, D_ATT), BF16)],
        compiler_params=_params(),
        name="attn_prompt",
    )(q, k, v, bias2)


def _attn_sample(q, k, v, cache_k, cache_v, bias2):
    first = N_PROMPT // CHUNK
    row = pl.BlockSpec((CHUNK, D_ATT), lambda b: (first + b, 0))
    cache = pl.BlockSpec((1, ATT_LEFT, D_ATT), lambda b: (b, 0, 0))
    return pl.pallas_call(
        _attn_sample_kernel,
        grid=(DEC_BATCH,),
        in_specs=[row, row, row, cache, cache, _const_spec((ATT_PAIRS, 2 * CHUNK, BAND))],
        out_specs=pl.BlockSpec((CHUNK, D_ATT), lambda b: (b, 0)),
        out_shape=jax.ShapeDtypeStruct((N_SAMPLE, D_ATT), BF16),
        scratch_shapes=[pltpu.VMEM((BAND, D_ATT), BF16), pltpu.VMEM((BAND, D_ATT), BF16)],
        compiler_params=_params(),
        name="attn_sample",
    )(q, k, v, cache_k, cache_v, bias2)


def _outproj_kernel(xp_ref, xs_ref, yp_ref, ys_ref, op_ref, os_ref, ag_ref, wos_ref, woa_ref, fg_ref,
                    wr_ref, br_ref, earlier_ref,
                    xmid_ref, h_ref, idx_ref, gate_ref, rank_ref, cnt_ref, carry_ref):
    i = pl.program_id(0)

    @pl.when(i == 0)
    def _():
        carry_ref[...] = jnp.zeros_like(carry_ref)

    is_sample = i == N_PROMPT_TILES
    x = jnp.where(is_sample, xs_ref[...], xp_ref[...])
    y = jnp.where(is_sample, ys_ref[...], yp_ref[...])
    o = jnp.where(is_sample, os_ref[...], op_ref[...])
    o = _rms(o.astype(F32), ag_ref[...]).astype(BF16)
    xm = x + _dot(y, wos_ref[...]) + _dot(o, woa_ref[...])
    xmid_ref[...] = xm
    h = _rms(xm, fg_ref[...])
    _rows_to_tiles(h_ref, h)
    h1 = h.astype(BF16)
    h2 = (h - h1.astype(F32)).astype(BF16)
    both = _dot_nt(wr_ref[...], h1)
    logits = both[:N_EXPERTS] + (both[N_EXPERTS:] + _dot_nt(wr_ref[0:N_EXPERTS, :], h2)) + br_ref[...]
    eidx = lax.broadcasted_iota(jnp.int32, (N_EXPERTS, TM), 0)
    slot = lax.broadcasted_iota(jnp.int32, (SUBLANES, TM), 0)
    work = logits
    vals, sels = [], []
    idx_out = jnp.zeros((SUBLANES, TM), jnp.int32)
    for k in range(TOP_K):
        m = jnp.max(work, axis=0, keepdims=True)
        idx = jnp.min(jnp.where(work == m, eidx, N_EXPERTS), axis=0, keepdims=True)
        sel = eidx == idx
        vals.append(m)
        sels.append(sel)
        idx_out = jnp.where(slot == k, idx, idx_out)
        work = jnp.where(sel, -jnp.inf, work)
    es = [jnp.exp(v - vals[0]) for v in vals]
    tot = es[0] + es[1] + es[2] + es[3]
    gate_out = jnp.zeros((SUBLANES, TM), F32)
    for k in range(TOP_K):
        gate_out = jnp.where(slot == k, es[k] / tot, gate_out)
    idx_ref[...] = idx_out
    gate_ref[...] = gate_out
    multi = jnp.zeros((N_EXPERTS, TM), F32)
    for sel in sels:
        multi = jnp.where(sel, 1.0, multi)
    before = _dot(multi.astype(BF16), earlier_ref[...]) + carry_ref[...]
    rank_out = jnp.zeros((SUBLANES, TM), jnp.int32)
    for k in range(TOP_K):
        rk = jnp.sum(jnp.where(sels[k], before, 0.0), axis=0, keepdims=True).astype(jnp.int32)
        rank_out = jnp.where(slot == k, rk, rank_out)
    rank_ref[...] = rank_out
    carry_ref[...] = carry_ref[...] + jnp.sum(multi, axis=1, keepdims=True)
    cnt_ref[...] = carry_ref[...]


def _outproj(xp, xs, yp, ys, op, os_, ag, wos, woa, fg, wr, br, earlier):
    tok = lambda n: pl.BlockSpec((TM, n), lambda i: (i, 0))
    slots = pl.BlockSpec((SUBLANES, TM), lambda i: (0, i))
    prompt = lambda n: pl.BlockSpec((TM, n), lambda i: (jnp.minimum(i, N_PROMPT_TILES - 1), 0))
    sample = lambda n: pl.BlockSpec((TM, n), lambda i: (0, 0))
    return pl.pallas_call(
        _outproj_kernel,
        grid=(N_TILES,),
        in_specs=[
            prompt(D_MODEL), sample(D_MODEL), prompt(D_SSM), sample(D_SSM), prompt(D_ATT), sample(D_ATT),
            _const_spec((1, D_ATT)),
            _const_spec((D_SSM, D_MODEL)), _const_spec((D_ATT, D_MODEL)),
            _const_spec((1, D_MODEL)),
            _const_spec((2 * N_EXPERTS, D_MODEL)), _const_spec((N_EXPERTS, 1)),
            _const_spec((TM, TM)),
        ],
        out_specs=[tok(D_MODEL), pl.BlockSpec(_tiled(TM), lambda i: (i, 0)), slots, slots, slots,
                   _const_spec((N_EXPERTS, 1))],
        out_shape=[
            jax.ShapeDtypeStruct((N_TOK, D_MODEL), F32),
            jax.ShapeDtypeStruct(_tiled(N_TOK), U32),
            jax.ShapeDtypeStruct((SUBLANES, N_TOK), jnp.int32),
            jax.ShapeDtypeStruct((SUBLANES, N_TOK), F32),
            jax.ShapeDtypeStruct((SUBLANES, N_TOK), jnp.int32),
            jax.ShapeDtypeStruct((N_EXPERTS, 1), F32),
        ],
        scratch_shapes=[pltpu.VMEM((N_EXPERTS, 1), F32)],
        compiler_params=_params(),
        name="outproj_router",
    )(xp, xs, yp, ys, op, os_, ag, wos, woa, fg, wr, br, earlier)


def _scatter_kernel(dest_ref, pend_ref, h_ref, rows_ref, zero_ref, sem, zsem):
    i = pl.program_id(0)

    @pl.when(i == 0)
    def _():
        zero_ref[...] = jnp.zeros_like(zero_ref)

        block_tiles = _tiled(MOE_BM)[0]

        def zero_block(b):
            start = pl.multiple_of(b * block_tiles, block_tiles)
            return pltpu.make_async_copy(zero_ref, rows_ref.at[pl.ds(start, block_tiles)], zsem)

        def last_block(e):
            end = pend_ref[e]
            nonempty = end > (pend_ref[e - 1] if e > 0 else 0)
            return nonempty, zero_block(jnp.maximum(end // MOE_BM - 1, 0))

        for e in range(N_EXPERTS):
            nonempty, cp = last_block(e)
            pl.when(nonempty)(cp.start)
        for e in range(N_EXPERTS):
            nonempty, cp = last_block(e)
            pl.when(nonempty)(cp.wait)

        first_unused = pend_ref[N_EXPERTS - 1] // MOE_BM
        lax.fori_loop(first_unused, MOE_BLOCKS, lambda b, c: (zero_block(b).start(), c)[1], 0)
        lax.fori_loop(first_unused, MOE_BLOCKS, lambda b, c: (zero_block(b).wait(), c)[1], 0)

    def issue(r, carry):
        for k in range(TOP_K):
            d = dest_ref[r * TOP_K + k]
            pltpu.make_async_copy(_tile_of(h_ref, r), _tile_of(rows_ref, d), sem).start(priority=k % 2)
        return carry

    lax.fori_loop(0, TM, issue, 0, unroll=ISSUE_UNROLL)
    for _ in range(TOP_K):
        pltpu.make_async_copy(h_ref, rows_ref.at[pl.ds(0, _tiled(TM)[0])], sem).wait()


def _scatter_rows(dest_flat, pad_end, h):
    return pl.pallas_call(
        _scatter_kernel,
        grid=(N_TILES,),
        in_specs=[
            pl.BlockSpec((TM * TOP_K,), lambda i: (i,), memory_space=pltpu.SMEM),
            pl.BlockSpec((N_EXPERTS,), lambda i: (0,), memory_space=pltpu.SMEM),
            pl.BlockSpec(_tiled(TM), lambda i: (i, 0)),
        ],
        out_specs=pl.BlockSpec(memory_space=pl.ANY),
        out_shape=jax.ShapeDtypeStruct(_tiled(MOE_ROWS), U32),
        scratch_shapes=[pltpu.VMEM(_tiled(MOE_BM), U32), pltpu.SemaphoreType.DMA(()),
                        pltpu.SemaphoreType.DMA(())],
        compiler_params=_params(),
        name="moe_scatter",
    )(dest_flat, pad_end, h)


def _expert_kernel(be_ref, nu_ref, nxt_ref, x_ref, wgu_hbm, bgu_ref, wd_hbm, bd_ref, y_ref,
                   wgu_f, wd_f, wgu_s, wd_s, sem):
    i = pl.program_id(0)
    active = i < nu_ref[0]
    e = be_ref[i]

    def fetch(expert):
        copies = []
        for hbm, buf, s in ((wgu_hbm, wgu_f, sem.at[0]), (wd_hbm, wd_f, sem.at[1])):
            rows = buf.shape[0] // WEIGHT_COPIES
            for c in range(WEIGHT_COPIES):
                part = pl.ds(c * rows, rows)
                copies.append(pltpu.make_async_copy(hbm.at[expert, part], buf.at[part], s))
        return copies

    def start(copies):
        for n, cp in enumerate(copies):
            cp.start(priority=n % 2)

    @pl.when(active & (i == 0))
    def _():
        start(fetch(e))

    @pl.when(active & ((i == 0) | (e != be_ref[jnp.maximum(i - 1, 0)])))
    def _():
        for cp in fetch(e):
            cp.wait()
        wgu_s[...] = wgu_f[...].astype(BF16)
        wd_s[...] = wd_f[...].astype(BF16)
        nxt = nxt_ref[e]

        @pl.when(nxt >= 0)
        def _():
            start(fetch(nxt))

    @pl.when(active)
    def _():
        gu = _dot(_tiles_to_rows(x_ref, MOE_BM).astype(BF16), wgu_s[...]) + bgu_ref[0]
        gate = jnp.minimum(gu[:, :D_FF], SWIGLU_LIMIT)
        up = jnp.clip(gu[:, D_FF:], -SWIGLU_LIMIT, SWIGLU_LIMIT)
        act = (up + 1.0) * gate * _sigmoid(gate * SWIGLU_ALPHA)
        _rows_to_tiles(y_ref, _dot(act.astype(BF16), wd_s[...]) + bd_ref[0])

    @pl.when(jnp.logical_not(active))
    def _():
        y_ref[...] = jnp.zeros_like(y_ref)


def _experts(block_expert, n_used, next_expert, rows, wgu, bgu, wd, bd):
    grid_spec = pltpu.PrefetchScalarGridSpec(
        num_scalar_prefetch=3,
        grid=(MOE_BLOCKS,),
        in_specs=[
            pl.BlockSpec(_tiled(MOE_BM), lambda i, be, nu, nx: (jnp.minimum(i, nu[0] - 1), 0)),
            pl.BlockSpec(memory_space=pl.ANY),
            pl.BlockSpec((1, 1, 2 * D_FF), lambda i, be, nu, nx: (be[i], 0, 0)),
            pl.BlockSpec(memory_space=pl.ANY),
            pl.BlockSpec((1, 1, D_MODEL), lambda i, be, nu, nx: (be[i], 0, 0)),
        ],
        out_specs=pl.BlockSpec(_tiled(MOE_BM), lambda i, be, nu, nx: (i, 0)),
        scratch_shapes=[pltpu.VMEM((D_MODEL, 2 * D_FF), F32), pltpu.VMEM((D_FF, D_MODEL), F32),
                        pltpu.VMEM((D_MODEL, 2 * D_FF), BF16), pltpu.VMEM((D_FF, D_MODEL), BF16),
                        pltpu.SemaphoreType.DMA((2,))],
    )
    return pl.pallas_call(
        _expert_kernel,
        grid_spec=grid_spec,
        out_shape=jax.ShapeDtypeStruct(_tiled(MOE_ROWS), U32),
        compiler_params=_params(),
        name="moe_experts",
    )(block_expert, n_used, next_expert, rows, wgu, bgu, wd, bd)


def _combine_kernel(dest_ref, dest_next_ref, gate_ref, xmid_ref, g_ref, rows_ref, yp_ref, ys_ref, buf_ref, sem):
    i = pl.program_id(0)
    slot = i % 2

    def issue(idx_ref, s):
        def body(r, carry):
            for k in range(TOP_K):
                d = idx_ref[r * TOP_K + k]
                pltpu.make_async_copy(_tile_of(rows_ref, d), _tile_of(buf_ref.at[s], k * TM + r),
                                      sem.at[s]).start(priority=k % 2)
            return carry

        lax.fori_loop(0, TM, body, 0, unroll=ISSUE_UNROLL)

    @pl.when(i == 0)
    def _():
        issue(dest_ref, 0)

    @pl.when(i + 1 < N_TILES)
    def _():
        issue(dest_next_ref, 1 - slot)

    buf = buf_ref.at[slot]
    slot_tiles = _tiled(TM)[0]
    for k in range(TOP_K):
        pltpu.make_async_copy(rows_ref.at[pl.ds(0, slot_tiles)], buf.at[pl.ds(k * slot_tiles, slot_tiles)],
                              sem.at[slot]).wait()
    acc = xmid_ref[...]
    for k in range(TOP_K):
        acc = acc + _tiles_to_rows(buf, TM, first=k * TM) * gate_ref[:, k:k + 1]
    y = _rms(acc, g_ref[...])

    @pl.when(i < N_PROMPT_TILES)
    def _():
        yp_ref[...] = y

    @pl.when(i == N_PROMPT_TILES)
    def _():
        ys_ref[...] = y


def _combine(dest_flat, gates, xmid, g, y_rows):
    return pl.pallas_call(
        _combine_kernel,
        grid=(N_TILES,),
        in_specs=[
            pl.BlockSpec((TM * TOP_K,), lambda i: (i,), memory_space=pltpu.SMEM),
            pl.BlockSpec((TM * TOP_K,), lambda i: (jnp.minimum(i + 1, N_TILES - 1),), memory_space=pltpu.SMEM),
            pl.BlockSpec((TM, TOP_K), lambda i: (i, 0)),
            pl.BlockSpec((TM, D_MODEL), lambda i: (i, 0)),
            _const_spec((1, D_MODEL)),
            pl.BlockSpec(memory_space=pl.ANY),
        ],
        out_specs=[
            pl.BlockSpec((TM, D_MODEL), lambda i: (jnp.minimum(i, N_PROMPT_TILES - 1), 0)),
            pl.BlockSpec((TM, D_MODEL), lambda i: (0, 0)),
        ],
        out_shape=[
            jax.ShapeDtypeStruct((N_PROMPT, D_MODEL), F32),
            jax.ShapeDtypeStruct((N_SAMPLE, D_MODEL), F32),
        ],
        scratch_shapes=[pltpu.VMEM((2,) + _tiled(TOP_K * TM), U32), pltpu.SemaphoreType.DMA((2,))],
        compiler_params=_params(),
        name="moe_combine",
    )(dest_flat, dest_flat, gates, xmid, g, y_rows)


def _band_bias(table):
    n_diag = BAND + CHUNK - 1
    idx = np.clip(ATT_LEFT + (CHUNK - 1) - np.arange(n_diag), -REL_CLIP, REL_CLIP) + REL_CLIP
    pick = (np.arange(2 * REL_CLIP + 1)[:, None] == idx[None, :]).astype(np.float32)
    diag = jnp.dot(table, jnp.asarray(pick), precision=lax.Precision.HIGHEST)
    return jnp.stack([diag[:, CHUNK - 1 - qi:CHUNK - 1 - qi + BAND] for qi in range(CHUNK)], axis=1)


def _pair_rows(v):
    return jnp.repeat(v.reshape(HEAD_PAIRS, 2), CHUNK, axis=1)


def _layer(l, xp, xs, cache_k, cache_v, state_conv, state_ssm,
           norm_mix_g, w_in, conv_w, conv_b, dt_bias, a_log, d_skip, ssm_norm_g,
           att_norm_g, rel_bias_table, w_out, norm_ffn_g, w_router, b_router,
           w_gate_up, b_gate_up, w_down, b_down, norm_final_g):
    wb = w_in[l].astype(BF16)
    c0 = D_SSM
    c1 = c0 + D_CONV
    c2 = c1 + SSM_HEADS
    c3 = c2 + D_ATT
    c4 = c3 + D_ATT
    z, xbc, dt, dtt, q, k, v, k_p, v_p, k_s, v_s, ctail = _inproj(
        xp, xs, norm_mix_g[l][None], wb[:, :c0], wb[:, c0:c1], wb[:, c1:c2], wb[:, c1:c2].T,
        wb[:, c2:c3], wb[:, c3:c4], wb[:, c4:])

    n_chunks = N_TOK // CHUNK
    dtp = dtt.reshape(HEAD_PAIRS, 2, n_chunks, CHUNK).transpose(2, 0, 1, 3).reshape(n_chunks, HEAD_PAIRS, LANES)
    hp = jnp.arange(D_SSM) // SSM_HEAD_DIM
    expand = (hp[None, :] == jnp.arange(SSM_HEADS)[:, None]).astype(BF16)
    lane = jnp.arange(LANES)
    triu2 = ((lane[:, None] // CHUNK == lane[None, :] // CHUNK) & (lane[:, None] <= lane[None, :])).astype(BF16)
    consts = (conv_w[l], conv_b[l][None], dt_bias[l][None], _pair_rows(dt_bias[l]),
              a_log[l][None], _pair_rows(a_log[l]),
              jnp.repeat(d_skip[l], SSM_HEAD_DIM)[None], ssm_norm_g[l][None],
              expand, triu2)
    y_ssm_p, ssm_p = _ssd_prompt(z, xbc, dt, dtp, consts)
    y_ssm_s, ssm_s = _ssd_sample(z, xbc, dt, dtp, state_conv[l],
                                 state_ssm[l].reshape(DEC_BATCH, D_SSM, D_STATE), consts)

    bias2 = _band_bias(rel_bias_table[l]).reshape(ATT_PAIRS, 2 * CHUNK, BAND)
    o_att_p = _attn_prompt(q, k, v, bias2)
    o_att_s = _attn_sample(q, k, v, cache_k[l].reshape(DEC_BATCH, ATT_LEFT, D_ATT),
                           cache_v[l].reshape(DEC_BATCH, ATT_LEFT, D_ATT), bias2)

    wo = w_out[l].astype(BF16)
    wr = w_router[l].T
    wr1 = wr.astype(BF16)
    wr2 = (wr - wr1.astype(F32)).astype(BF16)
    earlier = jnp.triu(jnp.ones((TM, TM), BF16), 1)
    xmid, h, top_idx, gates, rank, counts = _outproj(
        xp, xs, y_ssm_p, y_ssm_s, o_att_p, o_att_s, att_norm_g[l][None], wo[:D_SSM], wo[D_SSM:], norm_ffn_g[l][None],
        jnp.concatenate([wr1, wr2], axis=0), b_router[l][:, None], earlier)
    gates = gates[:TOP_K].T

    counts = counts[:, 0].astype(jnp.int32)
    padded = (counts + MOE_BM - 1) // MOE_BM * MOE_BM
    pad_end = jnp.cumsum(padded)
    pad_start = pad_end - padded
    experts = jnp.arange(N_EXPERTS, dtype=jnp.int32)
    start_of = jnp.sum(jnp.where(top_idx[:TOP_K, :, None] == experts, pad_start, 0), axis=-1)
    dest = (start_of + rank[:TOP_K]).T.reshape(-1).astype(jnp.int32)
    block_start = jnp.arange(MOE_BLOCKS, dtype=jnp.int32) * MOE_BM
    block_expert = jnp.minimum(jnp.sum((pad_end[None, :] <= block_start[:, None]).astype(jnp.int32), axis=1),
                               N_EXPERTS - 1).astype(jnp.int32)
    n_used = (pad_end[-1:] // MOE_BM).astype(jnp.int32)
    later_nonempty = (experts[None, :] > experts[:, None]) & (padded[None, :] > 0)
    next_expert = jnp.min(jnp.where(later_nonempty, experts[None, :], N_EXPERTS), axis=1)
    next_expert = jnp.where(next_expert < N_EXPERTS, next_expert, -1).astype(jnp.int32)

    rows = _scatter_rows(dest, pad_end.astype(jnp.int32), h)
    y_rows = _experts(block_expert, n_used, next_expert, rows, w_gate_up[l], b_gate_up[l][:, None, :],
                      w_down[l], b_down[l][:, None, :])
    y_p, y_s = _combine(dest, gates, xmid, norm_final_g[None], y_rows)

    keep = min(ATT_LEFT, SEQ)
    k_p = k_p[:, TM - keep:].reshape(BATCH, keep, ATT_HEADS, ATT_HEAD_DIM)
    v_p = v_p[:, TM - keep:].reshape(BATCH, keep, ATT_HEADS, ATT_HEAD_DIM)
    k_s = k_s.reshape(DEC_BATCH, DEC_SEQ, ATT_HEADS, ATT_HEAD_DIM)
    v_s = v_s.reshape(DEC_BATCH, DEC_SEQ, ATT_HEADS, ATT_HEAD_DIM)
    conv_p = ctail[:BATCH, -(CONV_W - 1):]
    conv_s = ctail[BATCH].reshape(DEC_BATCH, SUBLANES, D_CONV)[:, -(CONV_W - 1):]
    ssm_p = ssm_p.reshape(BATCH, SSM_HEADS, SSM_HEAD_DIM, D_STATE)
    ssm_s = ssm_s.reshape(DEC_BATCH, SSM_HEADS, SSM_HEAD_DIM, D_STATE)
    return (y_p.reshape(BATCH, SEQ, D_MODEL), y_s.reshape(DEC_BATCH, DEC_SEQ, D_MODEL),
            k_p, v_p, conv_p, ssm_p, k_s, v_s, conv_s, ssm_s)


def kernel(x_prompt, x_sample, cache_k, cache_v, state_conv, state_ssm, norm_mix_g, w_in, conv_w, conv_b,
           dt_bias, a_log, d_skip, ssm_norm_g, att_norm_g, rel_bias_table, w_out, norm_ffn_g, w_router,
           b_router, w_gate_up, b_gate_up, w_down, b_down, norm_final_g):
    assert w_in.shape[0] == 1, "single trunk layer"
    xp = x_prompt.reshape(N_PROMPT, D_MODEL)
    xs = x_sample.reshape(N_SAMPLE, D_MODEL)
    outs = _layer(0, xp, xs, cache_k, cache_v, state_conv, state_ssm,
                  norm_mix_g, w_in, conv_w, conv_b, dt_bias, a_log, d_skip, ssm_norm_g,
                  att_norm_g, rel_bias_table, w_out, norm_ffn_g, w_router, b_router,
                  w_gate_up, b_gate_up, w_down, b_down, norm_final_g)
    y_p, y_s, k_p, v_p, conv_p, ssm_p, k_s, v_s, conv_s, ssm_s = outs
    return (y_p, y_s, k_p[None], v_p[None], conv_p[None], ssm_p[None],
            k_s[None], v_s[None], conv_s[None], ssm_s[None])
```

```python
import jax
import jax.numpy as jnp
import numpy as np
from jax import lax
from jax.experimental import pallas as pl
from jax.experimental.pallas import tpu as pltpu

D_MODEL = 1024
BATCH = 8
SEQ = 2048
DEC_BATCH = 8
DEC_SEQ = 64
CHUNK = 64
SSM_HEADS = 16
SSM_HEAD_DIM = 64
D_SSM = SSM_HEADS * SSM_HEAD_DIM
SSM_GROUPS = 2
D_STATE = 128
CONV_W = 4
D_BC = SSM_GROUPS * D_STATE
D_CONV = D_SSM + 2 * D_BC
ATT_HEADS = 8
ATT_HEAD_DIM = 64
D_ATT = ATT_HEADS * ATT_HEAD_DIM
LEFT_CHUNKS = 8
ATT_LEFT = LEFT_CHUNKS * CHUNK
BAND = ATT_LEFT + CHUNK
REL_CLIP = 128
ATT_SCALE = ATT_HEAD_DIM ** -0.5
N_EXPERTS = 32
TOP_K = 4
D_FF = D_MODEL
SWIGLU_ALPHA = 1.702
SWIGLU_LIMIT = 7.0
EPS = 1e-5

F32 = jnp.float32
BF16 = jnp.bfloat16
U32 = jnp.uint32

LANES = 128
SUBLANES = 8

N_PROMPT = BATCH * SEQ
N_SAMPLE = DEC_BATCH * DEC_SEQ
N_TOK = N_PROMPT + N_SAMPLE
TM = 512
N_PROMPT_TILES = N_PROMPT // TM
N_TILES = N_TOK // TM
TILES_PER_SEQ = SEQ // TM
CHUNKS_PER_TILE = TM // CHUNK
N_CHUNKS_SEQ = SEQ // CHUNK
assert 2 * SSM_HEAD_DIM == LANES and 2 * ATT_HEAD_DIM == LANES
HEAD_PAIRS = SSM_HEADS // 2
PAIRS_PER_GROUP = HEAD_PAIRS // SSM_GROUPS
ATT_PAIRS = ATT_HEADS // 2
SSD_BLOCK = 2
CONV_HEAD = SUBLANES
MOE_BM = 512
N_ASSIGN = N_TOK * TOP_K
MOE_BLOCKS = N_ASSIGN // MOE_BM + N_EXPERTS
MOE_ROWS = MOE_BLOCKS * MOE_BM
ISSUE_UNROLL = 8
ROW_TILE = (SUBLANES // 2, LANES)
assert 2 * ROW_TILE[0] * ROW_TILE[1] == D_MODEL
VMEM_BYTES = 64 * 1024 * 1024
VMEM_LIMIT = VMEM_BYTES * 7 // 8


def _dot(a, b):
    return jnp.dot(a, b, preferred_element_type=F32)


def _dot_nt(a, b):
    return lax.dot_general(a, b, (((1,), (1,)), ((), ())), preferred_element_type=F32)


def _dot_tn(a, b):
    return lax.dot_general(a, b, (((0,), (0,)), ((), ())), preferred_element_type=F32)


def _split3(x):
    x1 = x.astype(BF16)
    r1 = x - x1.astype(F32)
    x2 = r1.astype(BF16)
    r2 = r1 - x2.astype(F32)
    return x1, x2, r2.astype(BF16)


def _dot_exact_rhs(x, m):
    x1, x2, x3 = _split3(x)
    return _dot(x1, m) + _dot(x2, m) + _dot(x3, m)


def _dot_exact_lhs(m, x):
    x1, x2, x3 = _split3(x)
    return _dot(m, x1) + _dot(m, x2) + _dot(m, x3)


def _rms(x, g):
    return x * lax.rsqrt(jnp.mean(x * x, axis=-1, keepdims=True) + EPS) * g


def _sigmoid(x):
    return 1.0 / (1.0 + jnp.exp(-x))


def _softplus(x):
    return jnp.maximum(x, 0.0) + jnp.log(1.0 + jnp.exp(-jnp.abs(x)))


def _tiled(n):
    return (n * ROW_TILE[0], ROW_TILE[1])


def _tile_of(ref, row):
    return ref.at[pl.ds(pl.multiple_of(row * ROW_TILE[0], ROW_TILE[0]), ROW_TILE[0])]


def _rows_to_tiles(ref, x, first=0):
    sub, lanes = ROW_TILE
    half = D_MODEL // 2
    hi = lax.bitcast_convert_type(x[:, :half].astype(BF16).astype(F32), U32)
    lo = lax.bitcast_convert_type(x[:, half:].astype(BF16).astype(F32), U32)
    words = hi | (lo >> 16)
    for j in range(sub):
        ref[pl.ds(first * sub + j, x.shape[0], stride=sub), :] = words[:, j * lanes:(j + 1) * lanes]


def _tiles_to_rows(ref, n, first=0):
    sub = ROW_TILE[0]
    words = jnp.concatenate([ref[pl.ds(first * sub + j, n, stride=sub), :] for j in range(sub)], axis=1)
    hi = lax.bitcast_convert_type(words & jnp.uint32(0xFFFF0000), F32)
    lo = lax.bitcast_convert_type(words << 16, F32)
    return jnp.concatenate([hi, lo], axis=1)


def _const_spec(shape):
    nd = len(shape)
    return pl.BlockSpec(shape, lambda *_: (0,) * nd)


def _params(n_axes=1):
    return pltpu.CompilerParams(dimension_semantics=("arbitrary",) * n_axes,
                                vmem_limit_bytes=VMEM_LIMIT)


def _inproj_kernel(xp_ref, xs_ref, g_ref, wz_ref, wxbc_ref, wdt_ref, wdtt_ref, wq_ref, wk_ref, wv_ref,
                   z_ref, xbc_ref, dt_ref, dtt_ref, q_ref, k_ref, v_ref, kp_ref, vp_ref, ks_ref, vs_ref, ctail_ref):
    i = pl.program_id(0)
    x = jnp.where(i == N_PROMPT_TILES, xs_ref[...], xp_ref[...])
    h = _rms(x, g_ref[...]).astype(BF16)
    z_ref[...] = _dot(h, wz_ref[...]).astype(BF16)
    xbc = _dot(h, wxbc_ref[...])
    xbc_ref[...] = xbc.astype(BF16)
    for c in range(CHUNKS_PER_TILE):
        ctail_ref[0, c * SUBLANES:(c + 1) * SUBLANES, :] = xbc[(c + 1) * CHUNK - SUBLANES:(c + 1) * CHUNK, :]
    dt_ref[...] = _dot(h, wdt_ref[...])
    dtt_ref[...] = _dot_nt(wdtt_ref[...], h)
    q_ref[...] = (_dot(h, wq_ref[...]) * ATT_SCALE).astype(BF16)
    k = _dot(h, wk_ref[...])
    v = _dot(h, wv_ref[...])
    k_ref[...] = k.astype(BF16)
    v_ref[...] = v.astype(BF16)

    @pl.when(i < N_PROMPT_TILES)
    def _():
        kp_ref[0] = k
        vp_ref[0] = v

    @pl.when(i == N_PROMPT_TILES)
    def _():
        ks_ref[...] = k
        vs_ref[...] = v


def _inproj(xp, xs, g, wz, wxbc, wdt, wdtt, wq, wk, wv):
    tok = lambda n: pl.BlockSpec((TM, n), lambda i: (i, 0))
    tail_idx = lambda i: (i // TILES_PER_SEQ, 0, 0)
    seq_idx = lambda i: (jnp.minimum(i // TILES_PER_SEQ, BATCH - 1), 0, 0)
    n_tail = BATCH + 1
    return pl.pallas_call(
        _inproj_kernel,
        grid=(N_TILES,),
        in_specs=[
            pl.BlockSpec((TM, D_MODEL), lambda i: (jnp.minimum(i, N_PROMPT_TILES - 1), 0)),
            pl.BlockSpec((TM, D_MODEL), lambda i: (0, 0)),
            _const_spec((1, D_MODEL)),
            _const_spec((D_MODEL, D_SSM)),
            _const_spec((D_MODEL, D_CONV)),
            _const_spec((D_MODEL, SSM_HEADS)),
            _const_spec((SSM_HEADS, D_MODEL)),
            _const_spec((D_MODEL, D_ATT)),
            _const_spec((D_MODEL, D_ATT)),
            _const_spec((D_MODEL, D_ATT)),
        ],
        out_specs=[
            tok(D_SSM), tok(D_CONV), tok(SSM_HEADS),
            pl.BlockSpec((SSM_HEADS, TM), lambda i: (0, i)),
            tok(D_ATT), tok(D_ATT), tok(D_ATT),
            pl.BlockSpec((1, TM, D_ATT), seq_idx),
            pl.BlockSpec((1, TM, D_ATT), seq_idx),
            pl.BlockSpec((TM, D_ATT), lambda i: (0, 0)),
            pl.BlockSpec((TM, D_ATT), lambda i: (0, 0)),
            pl.BlockSpec((1, CHUNKS_PER_TILE * SUBLANES, D_CONV), tail_idx),
        ],
        out_shape=[
            jax.ShapeDtypeStruct((N_TOK, D_SSM), BF16),
            jax.ShapeDtypeStruct((N_TOK, D_CONV), BF16),
            jax.ShapeDtypeStruct((N_TOK, SSM_HEADS), F32),
            jax.ShapeDtypeStruct((SSM_HEADS, N_TOK), F32),
            jax.ShapeDtypeStruct((N_TOK, D_ATT), BF16),
            jax.ShapeDtypeStruct((N_TOK, D_ATT), BF16),
            jax.ShapeDtypeStruct((N_TOK, D_ATT), BF16),
            jax.ShapeDtypeStruct((BATCH, TM, D_ATT), F32),
            jax.ShapeDtypeStruct((BATCH, TM, D_ATT), F32),
            jax.ShapeDtypeStruct((N_SAMPLE, D_ATT), F32),
            jax.ShapeDtypeStruct((N_SAMPLE, D_ATT), F32),
            jax.ShapeDtypeStruct((n_tail, CHUNKS_PER_TILE * SUBLANES, D_CONV), F32),
        ],
        compiler_params=_params(),
        name="inproj",
    )(xp, xs, g, wz, wxbc, wdt, wdtt, wq, wk, wv)


def _ssd_tile(n_chunks, z_ref, xbc_ref, dt_ref, dtp_ref, cw_ref, cb_ref, dtb_ref, dtbp_ref,
              alog_h_ref, alog_p_ref, dskip_e_ref, ng_ref, expand_ref, triu2_ref,
              y_ref, xw_ref, state_ref):
    nb = SSD_BLOCK if n_chunks % SSD_BLOCK == 0 else 1
    rb = nb * CHUNK
    half = D_SSM // SSM_GROUPS
    xw_ref[CONV_HEAD:CONV_HEAD + n_chunks * CHUNK, :] = xbc_ref[...].astype(F32)

    a_h = -jnp.exp(alog_h_ref[...])
    a_p = jnp.concatenate([-jnp.exp(alog_p_ref[...])] * nb, axis=0)
    dtb_p = jnp.concatenate([dtbp_ref[...]] * nb, axis=0)
    tr_r = lax.broadcasted_iota(jnp.int32, (rb, rb), 0)
    tr_c = lax.broadcasted_iota(jnp.int32, (rb, rb), 1)
    tril_b = jnp.where((tr_r // CHUNK == tr_c // CHUNK) & (tr_c <= tr_r), 1.0, 0.0).astype(BF16)
    row_i = lax.broadcasted_iota(jnp.int32, (CHUNK, LANES), 0)
    col_i = lax.broadcasted_iota(jnp.int32, (CHUNK, LANES), 1)
    causal2 = row_i >= (col_i % CHUNK)
    sel_r = lax.broadcasted_iota(jnp.int32, ((CONV_W - 1) * rb, CONV_HEAD + rb), 0)
    sel_c = lax.broadcasted_iota(jnp.int32, ((CONV_W - 1) * rb, CONV_HEAD + rb), 1)
    shift_sel = (sel_c == sel_r % rb + sel_r // rb + CONV_HEAD - (CONV_W - 1)).astype(F32)
    bd_r = lax.broadcasted_iota(jnp.int32, (LANES, LANES), 0) // CHUNK
    bd_c = lax.broadcasted_iota(jnp.int32, (LANES, LANES), 1) // CHUNK
    blockdiag = bd_r == bd_c

    def block(i, carry):
        r0 = pl.multiple_of(i * rb, rb)
        win = xw_ref[pl.ds(r0, CONV_HEAD + rb), :]
        shifted = _dot(shift_sel, win)
        acc = cb_ref[...] + cw_ref[CONV_W - 1:CONV_W, :] * win[CONV_HEAD:, :]
        for tap in range(CONV_W - 1):
            acc = acc + cw_ref[tap:tap + 1, :] * shifted[tap * rb:(tap + 1) * rb, :]
        xa = acc * _sigmoid(acc)
        xs = xa[:, 0:D_SSM]
        bm = xa[:, D_SSM:D_SSM + D_BC].astype(BF16)
        cm = xa[:, D_SSM + D_BC:D_CONV].astype(BF16)
        dt = _softplus(dt_ref[pl.ds(r0, rb), :] + dtb_ref[...])
        dt_e = _dot_exact_rhs(dt, expand_ref[...])
        acum = _dot_exact_rhs(_dot_exact_lhs(tril_b, dt * a_h), expand_ref[...])
        dtp = _softplus(dtp_ref[pl.ds(i * nb, nb)].reshape(nb * HEAD_PAIRS, LANES) + dtb_p)
        acum_p = _dot_exact_rhs(dtp * a_p, triu2_ref[...])
        xdt = xs * dt_e
        a_last = [acum[(c + 1) * CHUNK - 1:(c + 1) * CHUNK, :] for c in range(nb)]
        a_end = jnp.concatenate([jnp.broadcast_to(a, (CHUNK, D_SSM)) for a in a_last], axis=0)
        xdt_end = (xdt * jnp.exp(a_end - acum)).astype(BF16)

        y_diag, new_s = [], []
        for c in range(nb):
            rs = slice(c * CHUNK, (c + 1) * CHUNK)
            y_parts = []
            for g in range(SSM_GROUPS):
                bg = bm[rs, g * D_STATE:(g + 1) * D_STATE]
                cg = cm[rs, g * D_STATE:(g + 1) * D_STATE]
                cb2 = _dot_nt(cg, jnp.concatenate([bg, bg], axis=0))
                for jj in range(PAIRS_PER_GROUP):
                    j = g * PAIRS_PER_GROUP + jj
                    seg = acum[rs, j * LANES:(j + 1) * LANES] - acum_p[c * HEAD_PAIRS + j:c * HEAD_PAIRS + j + 1, :]
                    decay = jnp.exp(jnp.where(causal2, seg, -jnp.inf))
                    x2 = xdt[rs, j * LANES:(j + 1) * LANES]
                    rhs = jnp.where(blockdiag, jnp.concatenate([x2, x2], axis=0), 0.0).astype(BF16)
                    y_parts.append(_dot((cb2 * decay).astype(BF16), rhs))
            y_diag.append(jnp.concatenate(y_parts, axis=1))
            new_s.append(jnp.concatenate(
                [_dot_tn(bm[rs, g * D_STATE:(g + 1) * D_STATE], xdt_end[rs, g * half:(g + 1) * half])
                 for g in range(SSM_GROUPS)], axis=1))

        y_off = []
        for c in range(nb):
            rs = slice(c * CHUNK, (c + 1) * CHUNK)
            state = state_ref[...]
            y_off.append(jnp.concatenate(
                [_dot(cm[rs, g * D_STATE:(g + 1) * D_STATE], state[:, g * half:(g + 1) * half].astype(BF16))
                 for g in range(SSM_GROUPS)], axis=1))
            state_ref[...] = state * jnp.exp(a_last[c]) + new_s[c]

        y = (jnp.concatenate(y_diag, axis=0) + jnp.concatenate(y_off, axis=0) * jnp.exp(acum)
             + dskip_e_ref[...] * xs)
        zc = z_ref[pl.ds(r0, rb), :].astype(F32)
        y = y * (zc * _sigmoid(zc))
        yn = jnp.concatenate(
            [y[:, g * half:(g + 1) * half]
             * lax.rsqrt(jnp.mean(jnp.square(y[:, g * half:(g + 1) * half]), axis=-1, keepdims=True) + EPS)
             for g in range(SSM_GROUPS)], axis=1)
        y_ref[pl.ds(r0, rb), :] = (yn * ng_ref[...]).astype(BF16)
        return carry

    lax.fori_loop(0, n_chunks // nb, block, 0, unroll=True)


def _state_store(state_ref, out_ref):
    for j in range(HEAD_PAIRS):
        out_ref[0, j * LANES:(j + 1) * LANES, :] = state_ref[:, j * LANES:(j + 1) * LANES].T


def _ssd_prompt_kernel(z_ref, xbc_ref, dt_ref, dtp_ref, cw_ref, cb_ref, dtb_ref, dtbp_ref,
                       alog_h_ref, alog_p_ref, dskip_e_ref, ng_ref, expand_ref, triu2_ref,
                       y_ref, ssm_ref, xw_ref, state_ref, tail_ref):
    t = pl.program_id(1)

    @pl.when(t == 0)
    def _():
        state_ref[...] = jnp.zeros_like(state_ref)
        xw_ref[0:CONV_HEAD, :] = jnp.zeros((CONV_HEAD, D_CONV), F32)

    @pl.when(t > 0)
    def _():
        xw_ref[0:CONV_HEAD, :] = tail_ref[...]

    _ssd_tile(CHUNKS_PER_TILE, z_ref, xbc_ref, dt_ref, dtp_ref, cw_ref, cb_ref, dtb_ref, dtbp_ref,
              alog_h_ref, alog_p_ref, dskip_e_ref, ng_ref, expand_ref, triu2_ref,
              y_ref, xw_ref, state_ref)
    tail_ref[...] = xw_ref[TM:TM + CONV_HEAD, :]

    @pl.when(t == TILES_PER_SEQ - 1)
    def _():
        _state_store(state_ref, ssm_ref)


def _ssd_sample_kernel(z_ref, xbc_ref, dt_ref, dtp_ref, cprev_ref, sprev_ref,
                       cw_ref, cb_ref, dtb_ref, dtbp_ref,
                       alog_h_ref, alog_p_ref, dskip_e_ref, ng_ref, expand_ref, triu2_ref,
                       y_ref, ssm_ref, xw_ref, state_ref):
    xw_ref[0:CONV_HEAD, :] = jnp.zeros((CONV_HEAD, D_CONV), F32)
    xw_ref[CONV_HEAD - (CONV_W - 1):CONV_HEAD, :] = cprev_ref[0]
    for j in range(HEAD_PAIRS):
        state_ref[:, j * LANES:(j + 1) * LANES] = sprev_ref[0, j * LANES:(j + 1) * LANES, :].T
    _ssd_tile(1, z_ref, xbc_ref, dt_ref, dtp_ref, cw_ref, cb_ref, dtb_ref, dtbp_ref,
              alog_h_ref, alog_p_ref, dskip_e_ref, ng_ref, expand_ref, triu2_ref,
              y_ref, xw_ref, state_ref)
    _state_store(state_ref, ssm_ref)


def _ssd_const_specs():
    return [
        _const_spec((CONV_W, D_CONV)), _const_spec((1, D_CONV)),
        _const_spec((1, SSM_HEADS)), _const_spec((HEAD_PAIRS, LANES)),
        _const_spec((1, SSM_HEADS)), _const_spec((HEAD_PAIRS, LANES)),
        _const_spec((1, D_SSM)), _const_spec((1, D_SSM)),
        _const_spec((SSM_HEADS, D_SSM)), _const_spec((LANES, LANES)),
    ]


def _ssd_prompt(z, xbc, dt, dtp, consts):
    tile = lambda b, t: (b * TILES_PER_SEQ + t, 0)
    return pl.pallas_call(
        _ssd_prompt_kernel,
        grid=(BATCH, TILES_PER_SEQ),
        in_specs=[
            pl.BlockSpec((TM, D_SSM), tile),
            pl.BlockSpec((TM, D_CONV), tile),
            pl.BlockSpec((TM, SSM_HEADS), tile),
            pl.BlockSpec((CHUNKS_PER_TILE, HEAD_PAIRS, LANES), lambda b, t: (b * TILES_PER_SEQ + t, 0, 0)),
        ] + _ssd_const_specs(),
        out_specs=[
            pl.BlockSpec((TM, D_SSM), tile),
            pl.BlockSpec((1, D_SSM, D_STATE), lambda b, t: (b, 0, 0)),
        ],
        out_shape=[
            jax.ShapeDtypeStruct((N_PROMPT, D_SSM), BF16),
            jax.ShapeDtypeStruct((BATCH, D_SSM, D_STATE), F32),
        ],
        scratch_shapes=[
            pltpu.VMEM((CONV_HEAD + TM, D_CONV), F32),
            pltpu.VMEM((D_STATE, D_SSM), F32),
            pltpu.VMEM((CONV_HEAD, D_CONV), F32),
        ],
        compiler_params=_params(2),
        name="ssd_prompt",
    )(z, xbc, dt, dtp, *consts)


def _ssd_sample(z, xbc, dt, dtp, conv_prev, ssm_prev, consts):
    first = N_PROMPT // CHUNK
    row = lambda b: (first + b, 0)
    return pl.pallas_call(
        _ssd_sample_kernel,
        grid=(DEC_BATCH,),
        in_specs=[
            pl.BlockSpec((CHUNK, D_SSM), row),
            pl.BlockSpec((CHUNK, D_CONV), row),
            pl.BlockSpec((CHUNK, SSM_HEADS), row),
            pl.BlockSpec((1, HEAD_PAIRS, LANES), lambda b: (first + b, 0, 0)),
            pl.BlockSpec((1, CONV_W - 1, D_CONV), lambda b: (b, 0, 0)),
            pl.BlockSpec((1, D_SSM, D_STATE), lambda b: (b, 0, 0)),
        ] + _ssd_const_specs(),
        out_specs=[
            pl.BlockSpec((CHUNK, D_SSM), lambda b: (b, 0)),
            pl.BlockSpec((1, D_SSM, D_STATE), lambda b: (b, 0, 0)),
        ],
        out_shape=[
            jax.ShapeDtypeStruct((N_SAMPLE, D_SSM), BF16),
            jax.ShapeDtypeStruct((DEC_BATCH, D_SSM, D_STATE), F32),
        ],
        scratch_shapes=[
            pltpu.VMEM((CONV_HEAD + CHUNK, D_CONV), F32),
            pltpu.VMEM((D_STATE, D_SSM), F32),
        ],
        compiler_params=_params(),
        name="ssd_sample",
    )(z, xbc, dt, dtp, conv_prev, ssm_prev, *consts)


def _attn_chunks(n_chunks, first_chunk, q_ref, kpad_ref, vpad_ref, bias_ref, o_ref):
    lane = lax.broadcasted_iota(jnp.int32, (CHUNK, LANES), 1)
    low = lane < ATT_HEAD_DIM
    kj = lax.broadcasted_iota(jnp.int32, (2 * CHUNK, BAND), 1)

    def chunk(c, carry, masked):
        r0 = pl.multiple_of(c * CHUNK, CHUNK)
        if masked:
            valid = kj >= (LEFT_CHUNKS - (first_chunk + c)) * CHUNK
        scores = []
        for j in range(ATT_PAIRS):
            qp = q_ref[pl.ds(r0, CHUNK), j * LANES:(j + 1) * LANES]
            zero = jnp.zeros_like(qp)
            q2 = jnp.concatenate([jnp.where(low, qp, zero), jnp.where(low, zero, qp)], axis=0)
            kb = kpad_ref[pl.ds(r0, BAND), j * LANES:(j + 1) * LANES]
            s = _dot_nt(q2, kb) + bias_ref[j]
            scores.append(jnp.where(valid, s, -jnp.inf) if masked else s)
        probs = []
        for s in scores:
            e = jnp.exp(s - jnp.max(s, axis=-1, keepdims=True))
            probs.append((e.astype(BF16), jnp.sum(e, axis=-1, keepdims=True)))
        outs = []
        for j, (e, denom) in enumerate(probs):
            vb = vpad_ref[pl.ds(r0, BAND), j * LANES:(j + 1) * LANES]
            r = _dot(e, vb) / denom
            outs.append(jnp.where(low, r[0:CHUNK], r[CHUNK:2 * CHUNK]))
        o_ref[pl.ds(r0, CHUNK), :] = jnp.concatenate(outs, axis=1).astype(BF16)
        return carry

    n_masked = min(max(LEFT_CHUNKS - first_chunk, 0), n_chunks)
    for lo, hi, masked in ((0, n_masked, True), (n_masked, n_chunks, False)):
        if hi > lo:
            lax.fori_loop(lo, hi, lambda c, carry, masked=masked: chunk(c, carry, masked), 0,
                          unroll=2 if (hi - lo) % 2 == 0 else 1)


def _attn_prompt_kernel(q_ref, k_ref, v_ref, bias_ref, o_ref, kpad_ref, vpad_ref):
    kpad_ref[0:ATT_LEFT, :] = jnp.zeros((ATT_LEFT, D_ATT), BF16)
    vpad_ref[0:ATT_LEFT, :] = jnp.zeros((ATT_LEFT, D_ATT), BF16)
    kpad_ref[ATT_LEFT:ATT_LEFT + SEQ, :] = k_ref[...]
    vpad_ref[ATT_LEFT:ATT_LEFT + SEQ, :] = v_ref[...]
    _attn_chunks(N_CHUNKS_SEQ, 0, q_ref, kpad_ref, vpad_ref, bias_ref, o_ref)


def _attn_sample_kernel(q_ref, k_ref, v_ref, ck_ref, cv_ref, bias_ref, o_ref, kpad_ref, vpad_ref):
    kpad_ref[0:ATT_LEFT, :] = ck_ref[0].astype(BF16)
    vpad_ref[0:ATT_LEFT, :] = cv_ref[0].astype(BF16)
    kpad_ref[ATT_LEFT:BAND, :] = k_ref[...]
    vpad_ref[ATT_LEFT:BAND, :] = v_ref[...]
    _attn_chunks(1, LEFT_CHUNKS, q_ref, kpad_ref, vpad_ref, bias_ref, o_ref)


def _attn_prompt(q, k, v, bias2):
    seq = pl.BlockSpec((SEQ, D_ATT), lambda b: (b, 0))
    return pl.pallas_call(
        _attn_prompt_kernel,
        grid=(BATCH,),
        in_specs=[seq, seq, seq, _const_spec((ATT_PAIRS, 2 * CHUNK, BAND))],
        out_specs=seq,
        out_shape=jax.ShapeDtypeStruct((N_PROMPT, D_ATT), BF16),
        scratch_shapes=[pltpu.VMEM((ATT_LEFT + SEQ, D_ATT), BF16),
                        pltpu.VMEM((ATT_LEFT + SEQ, D_ATT), BF16)],
        compiler_params=_params(),
        name="attn_prompt",
    )(q, k, v, bias2)


def _attn_sample(q, k, v, cache_k, cache_v, bias2):
    first = N_PROMPT // CHUNK
    row = pl.BlockSpec((CHUNK, D_ATT), lambda b: (first + b, 0))
    cache = pl.BlockSpec((1, ATT_LEFT, D_ATT), lambda b: (b, 0, 0))
    return pl.pallas_call(
        _attn_sample_kernel,
        grid=(DEC_BATCH,),
        in_specs=[row, row, row, cache, cache, _const_spec((ATT_PAIRS, 2 * CHUNK, BAND))],
        out_specs=pl.BlockSpec((CHUNK, D_ATT), lambda b: (b, 0)),
        out_shape=jax.ShapeDtypeStruct((N_SAMPLE, D_ATT), BF16),
        scratch_shapes=[pltpu.VMEM((BAND, D_ATT), BF16), pltpu.VMEM((BAND, D_ATT), BF16)],
        compiler_params=_params(),
        name="attn_sample",
    )(q, k, v, cache_k, cache_v, bias2)


def _outproj_kernel(xp_ref, xs_ref, yp_ref, ys_ref, op_ref, os_ref, ag_ref, wos_ref, woa_ref, fg_ref,
                    wr_ref, br_ref, earlier_ref,
                    xmid_ref, h_ref, idx_ref, gate_ref, rank_ref, cnt_ref, carry_ref):
    i = pl.program_id(0)

    @pl.when(i == 0)
    def _():
        carry_ref[...] = jnp.zeros_like(carry_ref)

    is_sample = i == N_PROMPT_TILES
    x = jnp.where(is_sample, xs_ref[...], xp_ref[...])
    y = jnp.where(is_sample, ys_ref[...], yp_ref[...])
    o = jnp.where(is_sample, os_ref[...], op_ref[...])
    o = _rms(o.astype(F32), ag_ref[...]).astype(BF16)
    xm = x + _dot(y, wos_ref[...]) + _dot(o, woa_ref[...])
    xmid_ref[...] = xm
    h = _rms(xm, fg_ref[...])
    _rows_to_tiles(h_ref, h)
    h1 = h.astype(BF16)
    h2 = (h - h1.astype(F32)).astype(BF16)
    both = _dot_nt(wr_ref[...], h1)
    logits = both[:N_EXPERTS] + (both[N_EXPERTS:] + _dot_nt(wr_ref[0:N_EXPERTS, :], h2)) + br_ref[...]
    eidx = lax.broadcasted_iota(jnp.int32, (N_EXPERTS, TM), 0)
    slot = lax.broadcasted_iota(jnp.int32, (SUBLANES, TM), 0)
    work = logits
    vals, sels = [], []
    idx_out = jnp.zeros((SUBLANES, TM), jnp.int32)
    for k in range(TOP_K):
        m = jnp.max(work, axis=0, keepdims=True)
        idx = jnp.min(jnp.where(work == m, eidx, N_EXPERTS), axis=0, keepdims=True)
        sel = eidx == idx
        vals.append(m)
        sels.append(sel)
        idx_out = jnp.where(slot == k, idx, idx_out)
        work = jnp.where(sel, -jnp.inf, work)
    es = [jnp.exp(v - vals[0]) for v in vals]
    tot = es[0] + es[1] + es[2] + es[3]
    gate_out = jnp.zeros((SUBLANES, TM), F32)
    for k in range(TOP_K):
        gate_out = jnp.where(slot == k, es[k] / tot, gate_out)
    idx_ref[...] = idx_out
    gate_ref[...] = gate_out
    multi = jnp.zeros((N_EXPERTS, TM), F32)
    for sel in sels:
        multi = jnp.where(sel, 1.0, multi)
    before = _dot(multi.astype(BF16), earlier_ref[...]) + carry_ref[...]
    rank_out = jnp.zeros((SUBLANES, TM), jnp.int32)
    for k in range(TOP_K):
        rk = jnp.sum(jnp.where(sels[k], before, 0.0), axis=0, keepdims=True).astype(jnp.int32)
        rank_out = jnp.where(slot == k, rk, rank_out)
    rank_ref[...] = rank_out
    carry_ref[...] = carry_ref[...] + jnp.sum(multi, axis=1, keepdims=True)
    cnt_ref[...] = carry_ref[...]


def _outproj(xp, xs, yp, ys, op, os_, ag, wos, woa, fg, wr, br, earlier):
    tok = lambda n: pl.BlockSpec((TM, n), lambda i: (i, 0))
    slots = pl.BlockSpec((SUBLANES, TM), lambda i: (0, i))
    prompt = lambda n: pl.BlockSpec((TM, n), lambda i: (jnp.minimum(i, N_PROMPT_TILES - 1), 0))
    sample = lambda n: pl.BlockSpec((TM, n), lambda i: (0, 0))
    return pl.pallas_call(
        _outproj_kernel,
        grid=(N_TILES,),
        in_specs=[
            prompt(D_MODEL), sample(D_MODEL), prompt(D_SSM), sample(D_SSM), prompt(D_ATT), sample(D_ATT),
            _const_spec((1, D_ATT)),
            _const_spec((D_SSM, D_MODEL)), _const_spec((D_ATT, D_MODEL)),
            _const_spec((1, D_MODEL)),
            _const_spec((2 * N_EXPERTS, D_MODEL)), _const_spec((N_EXPERTS, 1)),
            _const_spec((TM, TM)),
        ],
        out_specs=[tok(D_MODEL), pl.BlockSpec(_tiled(TM), lambda i: (i, 0)), slots, slots, slots,
                   _const_spec((N_EXPERTS, 1))],
        out_shape=[
            jax.ShapeDtypeStruct((N_TOK, D_MODEL), F32),
            jax.ShapeDtypeStruct(_tiled(N_TOK), U32),
            jax.ShapeDtypeStruct((SUBLANES, N_TOK), jnp.int32),
            jax.ShapeDtypeStruct((SUBLANES, N_TOK), F32),
            jax.ShapeDtypeStruct((SUBLANES, N_TOK), jnp.int32),
            jax.ShapeDtypeStruct((N_EXPERTS, 1), F32),
        ],
        scratch_shapes=[pltpu.VMEM((N_EXPERTS, 1), F32)],
        compiler_params=_params(),
        name="outproj_router",
    )(xp, xs, yp, ys, op, os_, ag, wos, woa, fg, wr, br, earlier)


def _scatter_kernel(dest_ref, pend_ref, h_ref, rows_ref, zero_ref, sem, zsem):
    i = pl.program_id(0)

    @pl.when(i == 0)
    def _():
        zero_ref[...] = jnp.zeros_like(zero_ref)

        block_tiles = _tiled(MOE_BM)[0]

        def zero_block(b):
            start = pl.multiple_of(b * block_tiles, block_tiles)
            return pltpu.make_async_copy(zero_ref, rows_ref.at[pl.ds(start, block_tiles)], zsem)

        def last_block(e):
            end = pend_ref[e]
            nonempty = end > (pend_ref[e - 1] if e > 0 else 0)
            return nonempty, zero_block(jnp.maximum(end // MOE_BM - 1, 0))

        for e in range(N_EXPERTS):
            nonempty, cp = last_block(e)
            pl.when(nonempty)(cp.start)
        for e in range(N_EXPERTS):
            nonempty, cp = last_block(e)
            pl.when(nonempty)(cp.wait)

        first_unused = pend_ref[N_EXPERTS - 1] // MOE_BM
        lax.fori_loop(first_unused, MOE_BLOCKS, lambda b, c: (zero_block(b).start(), c)[1], 0)
        lax.fori_loop(first_unused, MOE_BLOCKS, lambda b, c: (zero_block(b).wait(), c)[1], 0)

    def issue(r, carry):
        for k in range(TOP_K):
            d = dest_ref[r * TOP_K + k]
            pltpu.make_async_copy(_tile_of(h_ref, r), _tile_of(rows_ref, d), sem).start(priority=k % 2)
        return carry

    lax.fori_loop(0, TM, issue, 0, unroll=ISSUE_UNROLL)
    for _ in range(TOP_K):
        pltpu.make_async_copy(h_ref, rows_ref.at[pl.ds(0, _tiled(TM)[0])], sem).wait()


def _scatter_rows(dest_flat, pad_end, h):
    return pl.pallas_call(
        _scatter_kernel,
        grid=(N_TILES,),
        in_specs=[
            pl.BlockSpec((TM * TOP_K,), lambda i: (i,), memory_space=pltpu.SMEM),
            pl.BlockSpec((N_EXPERTS,), lambda i: (0,), memory_space=pltpu.SMEM),
            pl.BlockSpec(_tiled(TM), lambda i: (i, 0)),
        ],
        out_specs=pl.BlockSpec(memory_space=pl.ANY),
        out_shape=jax.ShapeDtypeStruct(_tiled(MOE_ROWS), U32),
        scratch_shapes=[pltpu.VMEM(_tiled(MOE_BM), U32), pltpu.SemaphoreType.DMA(()),
                        pltpu.SemaphoreType.DMA(())],
        compiler_params=_params(),
        name="moe_scatter",
    )(dest_flat, pad_end, h)


def _expert_kernel(be_ref, nu_ref, nxt_ref, x_ref, wgu_hbm, bgu_ref, wd_hbm, bd_ref, y_ref,
                   wgu_f, wd_f, wgu_s, wd_s, sem):
    i = pl.program_id(0)
    active = i < nu_ref[0]
    e = be_ref[i]

    def fetch(expert):
        return (pltpu.make_async_copy(wgu_hbm.at[expert], wgu_f, sem.at[0]),
                pltpu.make_async_copy(wd_hbm.at[expert], wd_f, sem.at[1]))

    @pl.when(active & (i == 0))
    def _():
        for cp in fetch(e):
            cp.start()

    @pl.when(active & ((i == 0) | (e != be_ref[jnp.maximum(i - 1, 0)])))
    def _():
        for cp in fetch(e):
            cp.wait()
        wgu_s[...] = wgu_f[...].astype(BF16)
        wd_s[...] = wd_f[...].astype(BF16)
        nxt = nxt_ref[e]

        @pl.when(nxt >= 0)
        def _():
            for cp in fetch(nxt):
                cp.start()

    @pl.when(active)
    def _():
        gu = _dot(_tiles_to_rows(x_ref, MOE_BM).astype(BF16), wgu_s[...]) + bgu_ref[0]
        gate = jnp.minimum(gu[:, :D_FF], SWIGLU_LIMIT)
        up = jnp.clip(gu[:, D_FF:], -SWIGLU_LIMIT, SWIGLU_LIMIT)
        act = (up + 1.0) * gate * _sigmoid(gate * SWIGLU_ALPHA)
        _rows_to_tiles(y_ref, _dot(act.astype(BF16), wd_s[...]) + bd_ref[0])

    @pl.when(jnp.logical_not(active))
    def _():
        y_ref[...] = jnp.zeros_like(y_ref)


def _experts(block_expert, n_used, next_expert, rows, wgu, bgu, wd, bd):
    grid_spec = pltpu.PrefetchScalarGridSpec(
        num_scalar_prefetch=3,
        grid=(MOE_BLOCKS,),
        in_specs=[
            pl.BlockSpec(_tiled(MOE_BM), lambda i, be, nu, nx: (jnp.minimum(i, nu[0] - 1), 0)),
            pl.BlockSpec(memory_space=pl.ANY),
            pl.BlockSpec((1, 1, 2 * D_FF), lambda i, be, nu, nx: (be[i], 0, 0)),
            pl.BlockSpec(memory_space=pl.ANY),
            pl.BlockSpec((1, 1, D_MODEL), lambda i, be, nu, nx: (be[i], 0, 0)),
        ],
        out_specs=pl.BlockSpec(_tiled(MOE_BM), lambda i, be, nu, nx: (i, 0)),
        scratch_shapes=[pltpu.VMEM((D_MODEL, 2 * D_FF), F32), pltpu.VMEM((D_FF, D_MODEL), F32),
                        pltpu.VMEM((D_MODEL, 2 * D_FF), BF16), pltpu.VMEM((D_FF, D_MODEL), BF16),
                        pltpu.SemaphoreType.DMA((2,))],
    )
    return pl.pallas_call(
        _expert_kernel,
        grid_spec=grid_spec,
        out_shape=jax.ShapeDtypeStruct(_tiled(MOE_ROWS), U32),
        compiler_params=_params(),
        name="moe_experts",
    )(block_expert, n_used, next_expert, rows, wgu, bgu, wd, bd)


def _combine_kernel(dest_ref, dest_next_ref, gate_ref, xmid_ref, g_ref, rows_ref, yp_ref, ys_ref, buf_ref, sem):
    i = pl.program_id(0)
    slot = i % 2

    def issue(idx_ref, s):
        def body(r, carry):
            for k in range(TOP_K):
                d = idx_ref[r * TOP_K + k]
                pltpu.make_async_copy(_tile_of(rows_ref, d), _tile_of(buf_ref.at[s], k * TM + r),
                                      sem.at[s]).start(priority=k % 2)
            return carry

        lax.fori_loop(0, TM, body, 0, unroll=ISSUE_UNROLL)

    @pl.when(i == 0)
    def _():
        issue(dest_ref, 0)

    @pl.when(i + 1 < N_TILES)
    def _():
        issue(dest_next_ref, 1 - slot)

    buf = buf_ref.at[slot]
    slot_tiles = _tiled(TM)[0]
    for k in range(TOP_K):
        pltpu.make_async_copy(rows_ref.at[pl.ds(0, slot_tiles)], buf.at[pl.ds(k * slot_tiles, slot_tiles)],
                              sem.at[slot]).wait()
    acc = xmid_ref[...]
    for k in range(TOP_K):
        acc = acc + _tiles_to_rows(buf, TM, first=k * TM) * gate_ref[:, k:k + 1]
    y = _rms(acc, g_ref[...])

    @pl.when(i < N_PROMPT_TILES)
    def _():
        yp_ref[...] = y

    @pl.when(i == N_PROMPT_TILES)
    def _():
        ys_ref[...] = y


def _combine(dest_flat, gates, xmid, g, y_rows):
    return pl.pallas_call(
        _combine_kernel,
        grid=(N_TILES,),
        in_specs=[
            pl.BlockSpec((TM * TOP_K,), lambda i: (i,), memory_space=pltpu.SMEM),
            pl.BlockSpec((TM * TOP_K,), lambda i: (jnp.minimum(i + 1, N_TILES - 1),), memory_space=pltpu.SMEM),
            pl.BlockSpec((TM, TOP_K), lambda i: (i, 0)),
            pl.BlockSpec((TM, D_MODEL), lambda i: (i, 0)),
            _const_spec((1, D_MODEL)),
            pl.BlockSpec(memory_space=pl.ANY),
        ],
        out_specs=[
            pl.BlockSpec((TM, D_MODEL), lambda i: (jnp.minimum(i, N_PROMPT_TILES - 1), 0)),
            pl.BlockSpec((TM, D_MODEL), lambda i: (0, 0)),
        ],
        out_shape=[
            jax.ShapeDtypeStruct((N_PROMPT, D_MODEL), F32),
            jax.ShapeDtypeStruct((N_SAMPLE, D_MODEL), F32),
        ],
        scratch_shapes=[pltpu.VMEM((2,) + _tiled(TOP_K * TM), U32), pltpu.SemaphoreType.DMA((2,))],
        compiler_params=_params(),
        name="moe_combine",
    )(dest_flat, dest_flat, gates, xmid, g, y_rows)


def _band_bias(table):
    n_diag = BAND + CHUNK - 1
    idx = np.clip(ATT_LEFT + (CHUNK - 1) - np.arange(n_diag), -REL_CLIP, REL_CLIP) + REL_CLIP
    pick = (np.arange(2 * REL_CLIP + 1)[:, None] == idx[None, :]).astype(np.float32)
    diag = jnp.dot(table, jnp.asarray(pick), precision=lax.Precision.HIGHEST)
    return jnp.stack([diag[:, CHUNK - 1 - qi:CHUNK - 1 - qi + BAND] for qi in range(CHUNK)], axis=1)


def _pair_rows(v):
    return jnp.repeat(v.reshape(HEAD_PAIRS, 2), CHUNK, axis=1)


def _layer(l, xp, xs, cache_k, cache_v, state_conv, state_ssm,
           norm_mix_g, w_in, conv_w, conv_b, dt_bias, a_log, d_skip, ssm_norm_g,
           att_norm_g, rel_bias_table, w_out, norm_ffn_g, w_router, b_router,
           w_gate_up, b_gate_up, w_down, b_down, norm_final_g):
    wb = w_in[l].astype(BF16)
    c0 = D_SSM
    c1 = c0 + D_CONV
    c2 = c1 + SSM_HEADS
    c3 = c2 + D_ATT
    c4 = c3 + D_ATT
    z, xbc, dt, dtt, q, k, v, k_p, v_p, k_s, v_s, ctail = _inproj(
        xp, xs, norm_mix_g[l][None], wb[:, :c0], wb[:, c0:c1], wb[:, c1:c2], wb[:, c1:c2].T,
        wb[:, c2:c3], wb[:, c3:c4], wb[:, c4:])

    n_chunks = N_TOK // CHUNK
    dtp = dtt.reshape(HEAD_PAIRS, 2, n_chunks, CHUNK).transpose(2, 0, 1, 3).reshape(n_chunks, HEAD_PAIRS, LANES)
    hp = jnp.arange(D_SSM) // SSM_HEAD_DIM
    expand = (hp[None, :] == jnp.arange(SSM_HEADS)[:, None]).astype(BF16)
    lane = jnp.arange(LANES)
    triu2 = ((lane[:, None] // CHUNK == lane[None, :] // CHUNK) & (lane[:, None] <= lane[None, :])).astype(BF16)
    consts = (conv_w[l], conv_b[l][None], dt_bias[l][None], _pair_rows(dt_bias[l]),
              a_log[l][None], _pair_rows(a_log[l]),
              jnp.repeat(d_skip[l], SSM_HEAD_DIM)[None], ssm_norm_g[l][None],
              expand, triu2)
    y_ssm_p, ssm_p = _ssd_prompt(z, xbc, dt, dtp, consts)
    y_ssm_s, ssm_s = _ssd_sample(z, xbc, dt, dtp, state_conv[l],
                                 state_ssm[l].reshape(DEC_BATCH, D_SSM, D_STATE), consts)

    bias2 = _band_bias(rel_bias_table[l]).reshape(ATT_PAIRS, 2 * CHUNK, BAND)
    o_att_p = _attn_prompt(q, k, v, bias2)
    o_att_s = _attn_sample(q, k, v, cache_k[l].reshape(DEC_BATCH, ATT_LEFT, D_ATT),
                           cache_v[l].reshape(DEC_BATCH, ATT_LEFT, D_ATT), bias2)

    wo = w_out[l].astype(BF16)
    wr = w_router[l].T
    wr1 = wr.astype(BF16)
    wr2 = (wr - wr1.astype(F32)).astype(BF16)
    earlier = jnp.triu(jnp.ones((TM, TM), BF16), 1)
    xmid, h, top_idx, gates, rank, counts = _outproj(
        xp, xs, y_ssm_p, y_ssm_s, o_att_p, o_att_s, att_norm_g[l][None], wo[:D_SSM], wo[D_SSM:], norm_ffn_g[l][None],
        jnp.concatenate([wr1, wr2], axis=0), b_router[l][:, None], earlier)
    gates = gates[:TOP_K].T

    counts = counts[:, 0].astype(jnp.int32)
    padded = (counts + MOE_BM - 1) // MOE_BM * MOE_BM
    pad_end = jnp.cumsum(padded)
    pad_start = pad_end - padded
    experts = jnp.arange(N_EXPERTS, dtype=jnp.int32)
    start_of = jnp.sum(jnp.where(top_idx[:TOP_K, :, None] == experts, pad_start, 0), axis=-1)
    dest = (start_of + rank[:TOP_K]).T.reshape(-1).astype(jnp.int32)
    block_start = jnp.arange(MOE_BLOCKS, dtype=jnp.int32) * MOE_BM
    block_expert = jnp.minimum(jnp.sum((pad_end[None, :] <= block_start[:, None]).astype(jnp.int32), axis=1),
                               N_EXPERTS - 1).astype(jnp.int32)
    n_used = (pad_end[-1:] // MOE_BM).astype(jnp.int32)
    later_nonempty = (experts[None, :] > experts[:, None]) & (padded[None, :] > 0)
    next_expert = jnp.min(jnp.where(later_nonempty, experts[None, :], N_EXPERTS), axis=1)
    next_expert = jnp.where(next_expert < N_EXPERTS, next_expert, -1).astype(jnp.int32)

    rows = _scatter_rows(dest, pad_end.astype(jnp.int32), h)
    y_rows = _experts(block_expert, n_used, next_expert, rows, w_gate_up[l], b_gate_up[l][:, None, :],
                      w_down[l], b_down[l][:, None, :])
    y_p, y_s = _combine(dest, gates, xmid, norm_final_g[None], y_rows)

    keep = min(ATT_LEFT, SEQ)
    k_p = k_p[:, TM - keep:].reshape(BATCH, keep, ATT_HEADS, ATT_HEAD_DIM)
    v_p = v_p[:, TM - keep:].reshape(BATCH, keep, ATT_HEADS, ATT_HEAD_DIM)
    k_s = k_s.reshape(DEC_BATCH, DEC_SEQ, ATT_HEADS, ATT_HEAD_DIM)
    v_s = v_s.reshape(DEC_BATCH, DEC_SEQ, ATT_HEADS, ATT_HEAD_DIM)
    conv_p = ctail[:BATCH, -(CONV_W - 1):]
    conv_s = ctail[BATCH].reshape(DEC_BATCH, SUBLANES, D_CONV)[:, -(CONV_W - 1):]
    ssm_p = ssm_p.reshape(BATCH, SSM_HEADS, SSM_HEAD_DIM, D_STATE)
    ssm_s = ssm_s.reshape(DEC_BATCH, SSM_HEADS, SSM_HEAD_DIM, D_STATE)
    return (y_p.reshape(BATCH, SEQ, D_MODEL), y_s.reshape(DEC_BATCH, DEC_SEQ, D_MODEL),
            k_p, v_p, conv_p, ssm_p, k_s, v_s, conv_s, ssm_s)


def kernel(x_prompt, x_sample, cache_k, cache_v, state_conv, state_ssm, norm_mix_g, w_in, conv_w, conv_b,
           dt_bias, a_log, d_skip, ssm_norm_g, att_norm_g, rel_bias_table, w_out, norm_ffn_g, w_router,
           b_router, w_gate_up, b_gate_up, w_down, b_down, norm_final_g):
    assert w_in.shape[0] == 1, "single trunk layer"
    xp = x_prompt.reshape(N_PROMPT, D_MODEL)
    xs = x_sample.reshape(N_SAMPLE, D_MODEL)
    outs = _layer(0, xp, xs, cache_k, cache_v, state_conv, state_ssm,
                  norm_mix_g, w_in, conv_w, conv_b, dt_bias, a_log, d_skip, ssm_norm_g,
                  att_norm_g, rel_bias_table, w_out, norm_ffn_g, w_router, b_router,
                  w_gate_up, b_gate_up, w_down, b_down, norm_final_g)
    y_p, y_s, k_p, v_p, conv_p, ssm_p, k_s, v_s, conv_s, ssm_s = outs
    return (y_p, y_s, k_p[None], v_p[None], conv_p[None], ssm_p[None],
            k_s[None], v_s[None], conv_s[None], ssm_s[None])
```

```python
import jax
import jax.numpy as jnp
import numpy as np
from jax import lax
from jax.experimental import pallas as pl
from jax.experimental.pallas import tpu as pltpu

D_MODEL = 1024
BATCH = 8
SEQ = 2048
DEC_BATCH = 8
DEC_SEQ = 64
CHUNK = 64
SSM_HEADS = 16
SSM_HEAD_DIM = 64
D_SSM = SSM_HEADS * SSM_HEAD_DIM
SSM_GROUPS = 2
D_STATE = 128
CONV_W = 4
D_BC = SSM_GROUPS * D_STATE
D_CONV = D_SSM + 2 * D_BC
ATT_HEADS = 8
ATT_HEAD_DIM = 64
D_ATT = ATT_HEADS * ATT_HEAD_DIM
LEFT_CHUNKS = 8
ATT_LEFT = LEFT_CHUNKS * CHUNK
BAND = ATT_LEFT + CHUNK
REL_CLIP = 128
ATT_SCALE = ATT_HEAD_DIM ** -0.5
N_EXPERTS = 32
TOP_K = 4
D_FF = D_MODEL
SWIGLU_ALPHA = 1.702
SWIGLU_LIMIT = 7.0
EPS = 1e-5

F32 = jnp.float32
BF16 = jnp.bfloat16
U32 = jnp.uint32

LANES = 128
SUBLANES = 8

N_PROMPT = BATCH * SEQ
N_SAMPLE = DEC_BATCH * DEC_SEQ
N_TOK = N_PROMPT + N_SAMPLE
TM = 512
N_PROMPT_TILES = N_PROMPT // TM
N_TILES = N_TOK // TM
TILES_PER_SEQ = SEQ // TM
CHUNKS_PER_TILE = TM // CHUNK
N_CHUNKS_SEQ = SEQ // CHUNK
assert 2 * SSM_HEAD_DIM == LANES and 2 * ATT_HEAD_DIM == LANES
HEAD_PAIRS = SSM_HEADS // 2
PAIRS_PER_GROUP = HEAD_PAIRS // SSM_GROUPS
ATT_PAIRS = ATT_HEADS // 2
ATT_UNROLL = 8
SSD_BLOCK = 2
CONV_HEAD = SUBLANES
MOE_BM = 512
N_ASSIGN = N_TOK * TOP_K
MOE_BLOCKS = N_ASSIGN // MOE_BM + N_EXPERTS
MOE_ROWS = MOE_BLOCKS * MOE_BM
ISSUE_UNROLL = 8
ROW_TILE = (SUBLANES // 2, LANES)
assert 2 * ROW_TILE[0] * ROW_TILE[1] == D_MODEL
VMEM_BYTES = 64 * 1024 * 1024
VMEM_LIMIT = VMEM_BYTES * 7 // 8


def _dot(a, b):
    return jnp.dot(a, b, preferred_element_type=F32)


def _dot_nt(a, b):
    return lax.dot_general(a, b, (((1,), (1,)), ((), ())), preferred_element_type=F32)


def _dot_tn(a, b):
    return lax.dot_general(a, b, (((0,), (0,)), ((), ())), preferred_element_type=F32)


def _split3(x):
    x1 = x.astype(BF16)
    r1 = x - x1.astype(F32)
    x2 = r1.astype(BF16)
    r2 = r1 - x2.astype(F32)
    return x1, x2, r2.astype(BF16)


def _dot_exact_rhs(x, m):
    x1, x2, x3 = _split3(x)
    return _dot(x1, m) + _dot(x2, m) + _dot(x3, m)


def _dot_exact_lhs(m, x):
    x1, x2, x3 = _split3(x)
    return _dot(m, x1) + _dot(m, x2) + _dot(m, x3)


def _rms(x, g):
    return x * lax.rsqrt(jnp.mean(x * x, axis=-1, keepdims=True) + EPS) * g


def _sigmoid(x):
    return 1.0 / (1.0 + jnp.exp(-x))


def _softplus(x):
    return jnp.maximum(x, 0.0) + jnp.log(1.0 + jnp.exp(-jnp.abs(x)))


def _tiled(n):
    return (n * ROW_TILE[0], ROW_TILE[1])


def _tile_of(ref, row):
    return ref.at[pl.ds(pl.multiple_of(row * ROW_TILE[0], ROW_TILE[0]), ROW_TILE[0])]


def _rows_to_tiles(ref, x, first=0):
    sub, lanes = ROW_TILE
    half = D_MODEL // 2
    hi = lax.bitcast_convert_type(x[:, :half].astype(BF16).astype(F32), U32)
    lo = lax.bitcast_convert_type(x[:, half:].astype(BF16).astype(F32), U32)
    words = hi | (lo >> 16)
    for j in range(sub):
        ref[pl.ds(first * sub + j, x.shape[0], stride=sub), :] = words[:, j * lanes:(j + 1) * lanes]


def _tiles_to_rows(ref, n, first=0):
    sub = ROW_TILE[0]
    words = jnp.concatenate([ref[pl.ds(first * sub + j, n, stride=sub), :] for j in range(sub)], axis=1)
    hi = lax.bitcast_convert_type(words & jnp.uint32(0xFFFF0000), F32)
    lo = lax.bitcast_convert_type(words << 16, F32)
    return jnp.concatenate([hi, lo], axis=1)


def _const_spec(shape):
    nd = len(shape)
    return pl.BlockSpec(shape, lambda *_: (0,) * nd)


def _params(n_axes=1):
    return pltpu.CompilerParams(dimension_semantics=("arbitrary",) * n_axes,
                                vmem_limit_bytes=VMEM_LIMIT)


def _inproj_kernel(xp_ref, xs_ref, g_ref, wz_ref, wxbc_ref, wdt_ref, wdtt_ref, wq_ref, wk_ref, wv_ref,
                   z_ref, xbc_ref, dt_ref, dtt_ref, q_ref, k_ref, v_ref, kp_ref, vp_ref, ks_ref, vs_ref, ctail_ref):
    i = pl.program_id(0)
    x = jnp.where(i == N_PROMPT_TILES, xs_ref[...], xp_ref[...])
    h = _rms(x, g_ref[...]).astype(BF16)
    z_ref[...] = _dot(h, wz_ref[...]).astype(BF16)
    xbc = _dot(h, wxbc_ref[...])
    xbc_ref[...] = xbc.astype(BF16)
    for c in range(CHUNKS_PER_TILE):
        ctail_ref[0, c * SUBLANES:(c + 1) * SUBLANES, :] = xbc[(c + 1) * CHUNK - SUBLANES:(c + 1) * CHUNK, :]
    dt_ref[...] = _dot(h, wdt_ref[...])
    dtt_ref[...] = _dot_nt(wdtt_ref[...], h)
    q_ref[...] = (_dot(h, wq_ref[...]) * ATT_SCALE).astype(BF16)
    k = _dot(h, wk_ref[...])
    v = _dot(h, wv_ref[...])
    k_ref[...] = k.astype(BF16)
    v_ref[...] = v.astype(BF16)

    @pl.when(i < N_PROMPT_TILES)
    def _():
        kp_ref[0] = k
        vp_ref[0] = v

    @pl.when(i == N_PROMPT_TILES)
    def _():
        ks_ref[...] = k
        vs_ref[...] = v


def _inproj(xp, xs, g, wz, wxbc, wdt, wdtt, wq, wk, wv):
    tok = lambda n: pl.BlockSpec((TM, n), lambda i: (i, 0))
    tail_idx = lambda i: (i // TILES_PER_SEQ, 0, 0)
    seq_idx = lambda i: (jnp.minimum(i // TILES_PER_SEQ, BATCH - 1), 0, 0)
    n_tail = BATCH + 1
    return pl.pallas_call(
        _inproj_kernel,
        grid=(N_TILES,),
        in_specs=[
            pl.BlockSpec((TM, D_MODEL), lambda i: (jnp.minimum(i, N_PROMPT_TILES - 1), 0)),
            pl.BlockSpec((TM, D_MODEL), lambda i: (0, 0)),
            _const_spec((1, D_MODEL)),
            _const_spec((D_MODEL, D_SSM)),
            _const_spec((D_MODEL, D_CONV)),
            _const_spec((D_MODEL, SSM_HEADS)),
            _const_spec((SSM_HEADS, D_MODEL)),
            _const_spec((D_MODEL, D_ATT)),
            _const_spec((D_MODEL, D_ATT)),
            _const_spec((D_MODEL, D_ATT)),
        ],
        out_specs=[
            tok(D_SSM), tok(D_CONV), tok(SSM_HEADS),
            pl.BlockSpec((SSM_HEADS, TM), lambda i: (0, i)),
            tok(D_ATT), tok(D_ATT), tok(D_ATT),
            pl.BlockSpec((1, TM, D_ATT), seq_idx),
            pl.BlockSpec((1, TM, D_ATT), seq_idx),
            pl.BlockSpec((TM, D_ATT), lambda i: (0, 0)),
            pl.BlockSpec((TM, D_ATT), lambda i: (0, 0)),
            pl.BlockSpec((1, CHUNKS_PER_TILE * SUBLANES, D_CONV), tail_idx),
        ],
        out_shape=[
            jax.ShapeDtypeStruct((N_TOK, D_SSM), BF16),
            jax.ShapeDtypeStruct((N_TOK, D_CONV), BF16),
            jax.ShapeDtypeStruct((N_TOK, SSM_HEADS), F32),
            jax.ShapeDtypeStruct((SSM_HEADS, N_TOK), F32),
            jax.ShapeDtypeStruct((N_TOK, D_ATT), BF16),
            jax.ShapeDtypeStruct((N_TOK, D_ATT), BF16),
            jax.ShapeDtypeStruct((N_TOK, D_ATT), BF16),
            jax.ShapeDtypeStruct((BATCH, TM, D_ATT), F32),
            jax.ShapeDtypeStruct((BATCH, TM, D_ATT), F32),
            jax.ShapeDtypeStruct((N_SAMPLE, D_ATT), F32),
            jax.ShapeDtypeStruct((N_SAMPLE, D_ATT), F32),
            jax.ShapeDtypeStruct((n_tail, CHUNKS_PER_TILE * SUBLANES, D_CONV), F32),
        ],
        compiler_params=_params(),
        name="inproj",
    )(xp, xs, g, wz, wxbc, wdt, wdtt, wq, wk, wv)


def _ssd_tile(n_chunks, z_ref, xbc_ref, dt_ref, dtp_ref, cw_ref, cb_ref, dtb_ref, dtbp_ref,
              alog_h_ref, alog_p_ref, dskip_e_ref, ng_ref, expand_ref, triu2_ref,
              y_ref, xw_ref, state_ref):
    nb = SSD_BLOCK if n_chunks % SSD_BLOCK == 0 else 1
    rb = nb * CHUNK
    half = D_SSM // SSM_GROUPS
    xw_ref[CONV_HEAD:CONV_HEAD + n_chunks * CHUNK, :] = xbc_ref[...].astype(F32)

    a_h = -jnp.exp(alog_h_ref[...])
    a_p = jnp.concatenate([-jnp.exp(alog_p_ref[...])] * nb, axis=0)
    dtb_p = jnp.concatenate([dtbp_ref[...]] * nb, axis=0)
    tr_r = lax.broadcasted_iota(jnp.int32, (rb, rb), 0)
    tr_c = lax.broadcasted_iota(jnp.int32, (rb, rb), 1)
    tril_b = jnp.where((tr_r // CHUNK == tr_c // CHUNK) & (tr_c <= tr_r), 1.0, 0.0).astype(BF16)
    row_i = lax.broadcasted_iota(jnp.int32, (CHUNK, LANES), 0)
    col_i = lax.broadcasted_iota(jnp.int32, (CHUNK, LANES), 1)
    causal2 = row_i >= (col_i % CHUNK)
    sel_r = lax.broadcasted_iota(jnp.int32, ((CONV_W - 1) * rb, CONV_HEAD + rb), 0)
    sel_c = lax.broadcasted_iota(jnp.int32, ((CONV_W - 1) * rb, CONV_HEAD + rb), 1)
    shift_sel = (sel_c == sel_r % rb + sel_r // rb + CONV_HEAD - (CONV_W - 1)).astype(F32)
    bd_r = lax.broadcasted_iota(jnp.int32, (LANES, LANES), 0) // CHUNK
    bd_c = lax.broadcasted_iota(jnp.int32, (LANES, LANES), 1) // CHUNK
    blockdiag = bd_r == bd_c

    def block(i, carry):
        r0 = pl.multiple_of(i * rb, rb)
        win = xw_ref[pl.ds(r0, CONV_HEAD + rb), :]
        shifted = _dot(shift_sel, win)
        acc = cb_ref[...] + cw_ref[CONV_W - 1:CONV_W, :] * win[CONV_HEAD:, :]
        for tap in range(CONV_W - 1):
            acc = acc + cw_ref[tap:tap + 1, :] * shifted[tap * rb:(tap + 1) * rb, :]
        xa = acc * _sigmoid(acc)
        xs = xa[:, 0:D_SSM]
        bm = xa[:, D_SSM:D_SSM + D_BC].astype(BF16)
        cm = xa[:, D_SSM + D_BC:D_CONV].astype(BF16)
        dt = _softplus(dt_ref[pl.ds(r0, rb), :] + dtb_ref[...])
        dt_e = _dot_exact_rhs(dt, expand_ref[...])
        acum = _dot_exact_rhs(_dot_exact_lhs(tril_b, dt * a_h), expand_ref[...])
        dtp = _softplus(dtp_ref[pl.ds(i * nb, nb)].reshape(nb * HEAD_PAIRS, LANES) + dtb_p)
        acum_p = _dot_exact_rhs(dtp * a_p, triu2_ref[...])
        xdt = xs * dt_e
        a_last = [acum[(c + 1) * CHUNK - 1:(c + 1) * CHUNK, :] for c in range(nb)]
        a_end = jnp.concatenate([jnp.broadcast_to(a, (CHUNK, D_SSM)) for a in a_last], axis=0)
        xdt_end = (xdt * jnp.exp(a_end - acum)).astype(BF16)

        y_diag, new_s = [], []
        for c in range(nb):
            rs = slice(c * CHUNK, (c + 1) * CHUNK)
            y_parts = []
            for g in range(SSM_GROUPS):
                bg = bm[rs, g * D_STATE:(g + 1) * D_STATE]
                cg = cm[rs, g * D_STATE:(g + 1) * D_STATE]
                cb2 = _dot_nt(cg, jnp.concatenate([bg, bg], axis=0))
                for jj in range(PAIRS_PER_GROUP):
                    j = g * PAIRS_PER_GROUP + jj
                    seg = acum[rs, j * LANES:(j + 1) * LANES] - acum_p[c * HEAD_PAIRS + j:c * HEAD_PAIRS + j + 1, :]
                    decay = jnp.exp(jnp.where(causal2, seg, -jnp.inf))
                    x2 = xdt[rs, j * LANES:(j + 1) * LANES]
                    rhs = jnp.where(blockdiag, jnp.concatenate([x2, x2], axis=0), 0.0).astype(BF16)
                    y_parts.append(_dot((cb2 * decay).astype(BF16), rhs))
            y_diag.append(jnp.concatenate(y_parts, axis=1))
            new_s.append(jnp.concatenate(
                [_dot_tn(bm[rs, g * D_STATE:(g + 1) * D_STATE], xdt_end[rs, g * half:(g + 1) * half])
                 for g in range(SSM_GROUPS)], axis=1))

        y_off = []
        for c in range(nb):
            rs = slice(c * CHUNK, (c + 1) * CHUNK)
            state = state_ref[...]
            y_off.append(jnp.concatenate(
                [_dot(cm[rs, g * D_STATE:(g + 1) * D_STATE], state[:, g * half:(g + 1) * half].astype(BF16))
                 for g in range(SSM_GROUPS)], axis=1))
            state_ref[...] = state * jnp.exp(a_last[c]) + new_s[c]

        y = (jnp.concatenate(y_diag, axis=0) + jnp.concatenate(y_off, axis=0) * jnp.exp(acum)
             + dskip_e_ref[...] * xs)
        zc = z_ref[pl.ds(r0, rb), :].astype(F32)
        y = y * (zc * _sigmoid(zc))
        yn = jnp.concatenate(
            [y[:, g * half:(g + 1) * half]
             * lax.rsqrt(jnp.mean(jnp.square(y[:, g * half:(g + 1) * half]), axis=-1, keepdims=True) + EPS)
             for g in range(SSM_GROUPS)], axis=1)
        y_ref[pl.ds(r0, rb), :] = (yn * ng_ref[...]).astype(BF16)
        return carry

    lax.fori_loop(0, n_chunks // nb, block, 0, unroll=True)


def _state_store(state_ref, out_ref):
    for j in range(HEAD_PAIRS):
        out_ref[0, j * LANES:(j + 1) * LANES, :] = state_ref[:, j * LANES:(j + 1) * LANES].T


def _ssd_prompt_kernel(z_ref, xbc_ref, dt_ref, dtp_ref, cw_ref, cb_ref, dtb_ref, dtbp_ref,
                       alog_h_ref, alog_p_ref, dskip_e_ref, ng_ref, expand_ref, triu2_ref,
                       y_ref, ssm_ref, xw_ref, state_ref, tail_ref):
    t = pl.program_id(1)

    @pl.when(t == 0)
    def _():
        state_ref[...] = jnp.zeros_like(state_ref)
        xw_ref[0:CONV_HEAD, :] = jnp.zeros((CONV_HEAD, D_CONV), F32)

    @pl.when(t > 0)
    def _():
        xw_ref[0:CONV_HEAD, :] = tail_ref[...]

    _ssd_tile(CHUNKS_PER_TILE, z_ref, xbc_ref, dt_ref, dtp_ref, cw_ref, cb_ref, dtb_ref, dtbp_ref,
              alog_h_ref, alog_p_ref, dskip_e_ref, ng_ref, expand_ref, triu2_ref,
              y_ref, xw_ref, state_ref)
    tail_ref[...] = xw_ref[TM:TM + CONV_HEAD, :]

    @pl.when(t == TILES_PER_SEQ - 1)
    def _():
        _state_store(state_ref, ssm_ref)


def _ssd_sample_kernel(z_ref, xbc_ref, dt_ref, dtp_ref, cprev_ref, sprev_ref,
                       cw_ref, cb_ref, dtb_ref, dtbp_ref,
                       alog_h_ref, alog_p_ref, dskip_e_ref, ng_ref, expand_ref, triu2_ref,
                       y_ref, ssm_ref, xw_ref, state_ref):
    xw_ref[0:CONV_HEAD, :] = jnp.zeros((CONV_HEAD, D_CONV), F32)
    xw_ref[CONV_HEAD - (CONV_W - 1):CONV_HEAD, :] = cprev_ref[0]
    for j in range(HEAD_PAIRS):
        state_ref[:, j * LANES:(j + 1) * LANES] = sprev_ref[0, j * LANES:(j + 1) * LANES, :].T
    _ssd_tile(1, z_ref, xbc_ref, dt_ref, dtp_ref, cw_ref, cb_ref, dtb_ref, dtbp_ref,
              alog_h_ref, alog_p_ref, dskip_e_ref, ng_ref, expand_ref, triu2_ref,
              y_ref, xw_ref, state_ref)
    _state_store(state_ref, ssm_ref)


def _ssd_const_specs():
    return [
        _const_spec((CONV_W, D_CONV)), _const_spec((1, D_CONV)),
        _const_spec((1, SSM_HEADS)), _const_spec((HEAD_PAIRS, LANES)),
        _const_spec((1, SSM_HEADS)), _const_spec((HEAD_PAIRS, LANES)),
        _const_spec((1, D_SSM)), _const_spec((1, D_SSM)),
        _const_spec((SSM_HEADS, D_SSM)), _const_spec((LANES, LANES)),
    ]


def _ssd_prompt(z, xbc, dt, dtp, consts):
    tile = lambda b, t: (b * TILES_PER_SEQ + t, 0)
    return pl.pallas_call(
        _ssd_prompt_kernel,
        grid=(BATCH, TILES_PER_SEQ),
        in_specs=[
            pl.BlockSpec((TM, D_SSM), tile),
            pl.BlockSpec((TM, D_CONV), tile),
            pl.BlockSpec((TM, SSM_HEADS), tile),
            pl.BlockSpec((CHUNKS_PER_TILE, HEAD_PAIRS, LANES), lambda b, t: (b * TILES_PER_SEQ + t, 0, 0)),
        ] + _ssd_const_specs(),
        out_specs=[
            pl.BlockSpec((TM, D_SSM), tile),
            pl.BlockSpec((1, D_SSM, D_STATE), lambda b, t: (b, 0, 0)),
        ],
        out_shape=[
            jax.ShapeDtypeStruct((N_PROMPT, D_SSM), BF16),
            jax.ShapeDtypeStruct((BATCH, D_SSM, D_STATE), F32),
        ],
        scratch_shapes=[
            pltpu.VMEM((CONV_HEAD + TM, D_CONV), F32),
            pltpu.VMEM((D_STATE, D_SSM), F32),
            pltpu.VMEM((CONV_HEAD, D_CONV), F32),
        ],
        compiler_params=_params(2),
        name="ssd_prompt",
    )(z, xbc, dt, dtp, *consts)


def _ssd_sample(z, xbc, dt, dtp, conv_prev, ssm_prev, consts):
    first = N_PROMPT // CHUNK
    row = lambda b: (first + b, 0)
    return pl.pallas_call(
        _ssd_sample_kernel,
        grid=(DEC_BATCH,),
        in_specs=[
            pl.BlockSpec((CHUNK, D_SSM), row),
            pl.BlockSpec((CHUNK, D_CONV), row),
            pl.BlockSpec((CHUNK, SSM_HEADS), row),
            pl.BlockSpec((1, HEAD_PAIRS, LANES), lambda b: (first + b, 0, 0)),
            pl.BlockSpec((1, CONV_W - 1, D_CONV), lambda b: (b, 0, 0)),
            pl.BlockSpec((1, D_SSM, D_STATE), lambda b: (b, 0, 0)),
        ] + _ssd_const_specs(),
        out_specs=[
            pl.BlockSpec((CHUNK, D_SSM), lambda b: (b, 0)),
            pl.BlockSpec((1, D_SSM, D_STATE), lambda b: (b, 0, 0)),
        ],
        out_shape=[
            jax.ShapeDtypeStruct((N_SAMPLE, D_SSM), BF16),
            jax.ShapeDtypeStruct((DEC_BATCH, D_SSM, D_STATE), F32),
        ],
        scratch_shapes=[
            pltpu.VMEM((CONV_HEAD + CHUNK, D_CONV), F32),
            pltpu.VMEM((D_STATE, D_SSM), F32),
        ],
        compiler_params=_params(),
        name="ssd_sample",
    )(z, xbc, dt, dtp, conv_prev, ssm_prev, *consts)


def _attn_chunks(n_chunks, first_chunk, q_ref, kpad_ref, vpad_ref, bias_ref, o_ref):
    lane = lax.broadcasted_iota(jnp.int32, (CHUNK, LANES), 1)
    low = lane < ATT_HEAD_DIM
    kj = lax.broadcasted_iota(jnp.int32, (2 * CHUNK, BAND), 1)

    def chunk(c, carry, masked):
        r0 = pl.multiple_of(c * CHUNK, CHUNK)
        if masked:
            valid = kj >= (LEFT_CHUNKS - (first_chunk + c)) * CHUNK
        scores = []
        for j in range(ATT_PAIRS):
            qp = q_ref[pl.ds(r0, CHUNK), j * LANES:(j + 1) * LANES]
            zero = jnp.zeros_like(qp)
            q2 = jnp.concatenate([jnp.where(low, qp, zero), jnp.where(low, zero, qp)], axis=0)
            kb = kpad_ref[pl.ds(r0, BAND), j * LANES:(j + 1) * LANES]
            s = _dot_nt(q2, kb) + bias_ref[j]
            scores.append(jnp.where(valid, s, -jnp.inf) if masked else s)
        probs = []
        for s in scores:
            e = jnp.exp(s - jnp.max(s, axis=-1, keepdims=True))
            probs.append((e.astype(BF16), jnp.sum(e, axis=-1, keepdims=True)))
        outs = []
        for j, (e, denom) in enumerate(probs):
            vb = vpad_ref[pl.ds(r0, BAND), j * LANES:(j + 1) * LANES]
            r = _dot(e, vb) / denom
            outs.append(jnp.where(low, r[0:CHUNK], r[CHUNK:2 * CHUNK]))
        o_ref[pl.ds(r0, CHUNK), :] = jnp.concatenate(outs, axis=1).astype(BF16)
        return carry

    n_masked = min(max(LEFT_CHUNKS - first_chunk, 0), n_chunks)
    for lo, hi, masked in ((0, n_masked, True), (n_masked, n_chunks, False)):
        if hi > lo:
            lax.fori_loop(lo, hi, lambda c, carry, masked=masked: chunk(c, carry, masked), 0,
                          unroll=ATT_UNROLL if (hi - lo) % ATT_UNROLL == 0 else 1)


def _attn_prompt_kernel(q_ref, k_ref, v_ref, bias_ref, o_ref, kpad_ref, vpad_ref):
    kpad_ref[0:ATT_LEFT, :] = jnp.zeros((ATT_LEFT, D_ATT), BF16)
    vpad_ref[0:ATT_LEFT, :] = jnp.zeros((ATT_LEFT, D_ATT), BF16)
    kpad_ref[ATT_LEFT:ATT_LEFT + SEQ, :] = k_ref[...]
    vpad_ref[ATT_LEFT:ATT_LEFT + SEQ, :] = v_ref[...]
    _attn_chunks(N_CHUNKS_SEQ, 0, q_ref, kpad_ref, vpad_ref, bias_ref, o_ref)


def _attn_sample_kernel(q_ref, k_ref, v_ref, ck_ref, cv_ref, bias_ref, o_ref, kpad_ref, vpad_ref):
    kpad_ref[0:ATT_LEFT, :] = ck_ref[0].astype(BF16)
    vpad_ref[0:ATT_LEFT, :] = cv_ref[0].astype(BF16)
    kpad_ref[ATT_LEFT:BAND, :] = k_ref[...]
    vpad_ref[ATT_LEFT:BAND, :] = v_ref[...]
    _attn_chunks(1, LEFT_CHUNKS, q_ref, kpad_ref, vpad_ref, bias_ref, o_ref)


def _attn_prompt(q, k, v, bias2):
    seq = pl.BlockSpec((SEQ, D_ATT), lambda b: (b, 0))
    return pl.pallas_call(
        _attn_prompt_kernel,
        grid=(BATCH,),
        in_specs=[seq, seq, seq, _const_spec((ATT_PAIRS, 2 * CHUNK, BAND))],
        out_specs=seq,
        out_shape=jax.ShapeDtypeStruct((N_PROMPT, D_ATT), BF16),
        scratch_shapes=[pltpu.VMEM((ATT_LEFT + SEQ, D_ATT), BF16),
                        pltpu.VMEM((ATT_LEFT + SEQ, D_ATT), BF16)],
        compiler_params=_params(),
        name="attn_prompt",
    )(q, k, v, bias2)


def _attn_sample(q, k, v, cache_k, cache_v, bias2):
    first = N_PROMPT // CHUNK
    row = pl.BlockSpec((CHUNK, D_ATT), lambda b: (first + b, 0))
    cache = pl.BlockSpec((1, ATT_LEFT, D_ATT), lambda b: (b, 0, 0))
    return pl.pallas_call(
        _attn_sample_kernel,
        grid=(DEC_BATCH,),
        in_specs=[row, row, row, cache, cache, _const_spec((ATT_PAIRS, 2 * CHUNK, BAND))],
        out_specs=pl.BlockSpec((CHUNK, D_ATT), lambda b: (b, 0)),
        out_shape=jax.ShapeDtypeStruct((N_SAMPLE, D_ATT), BF16),
        scratch_shapes=[pltpu.VMEM((BAND, D_ATT), BF16), pltpu.VMEM((BAND, D_ATT), BF16)],
        compiler_params=_params(),
        name="attn_sample",
    )(q, k, v, cache_k, cache_v, bias2)


def _outproj_kernel(xp_ref, xs_ref, yp_ref, ys_ref, op_ref, os_ref, ag_ref, wos_ref, woa_ref, fg_ref,
                    wr_ref, br_ref, earlier_ref,
                    xmid_ref, h_ref, idx_ref, gate_ref, rank_ref, cnt_ref, carry_ref):
    i = pl.program_id(0)

    @pl.when(i == 0)
    def _():
        carry_ref[...] = jnp.zeros_like(carry_ref)

    is_sample = i == N_PROMPT_TILES
    x = jnp.where(is_sample, xs_ref[...], xp_ref[...])
    y = jnp.where(is_sample, ys_ref[...], yp_ref[...])
    o = jnp.where(is_sample, os_ref[...], op_ref[...])
    o = _rms(o.astype(F32), ag_ref[...]).astype(BF16)
    xm = x + _dot(y, wos_ref[...]) + _dot(o, woa_ref[...])
    xmid_ref[...] = xm
    h = _rms(xm, fg_ref[...])
    _rows_to_tiles(h_ref, h)
    h1 = h.astype(BF16)
    h2 = (h - h1.astype(F32)).astype(BF16)
    both = _dot_nt(wr_ref[...], h1)
    logits = both[:N_EXPERTS] + (both[N_EXPERTS:] + _dot_nt(wr_ref[0:N_EXPERTS, :], h2)) + br_ref[...]
    eidx = lax.broadcasted_iota(jnp.int32, (N_EXPERTS, TM), 0)
    slot = lax.broadcasted_iota(jnp.int32, (SUBLANES, TM), 0)
    work = logits
    vals, sels = [], []
    idx_out = jnp.zeros((SUBLANES, TM), jnp.int32)
    for k in range(TOP_K):
        m = jnp.max(work, axis=0, keepdims=True)
        idx = jnp.min(jnp.where(work == m, eidx, N_EXPERTS), axis=0, keepdims=True)
        sel = eidx == idx
        vals.append(m)
        sels.append(sel)
        idx_out = jnp.where(slot == k, idx, idx_out)
        work = jnp.where(sel, -jnp.inf, work)
    es = [jnp.exp(v - vals[0]) for v in vals]
    tot = es[0] + es[1] + es[2] + es[3]
    gate_out = jnp.zeros((SUBLANES, TM), F32)
    for k in range(TOP_K):
        gate_out = jnp.where(slot == k, es[k] / tot, gate_out)
    idx_ref[...] = idx_out
    gate_ref[...] = gate_out
    multi = jnp.zeros((N_EXPERTS, TM), F32)
    for sel in sels:
        multi = jnp.where(sel, 1.0, multi)
    before = _dot(multi.astype(BF16), earlier_ref[...]) + carry_ref[...]
    rank_out = jnp.zeros((SUBLANES, TM), jnp.int32)
    for k in range(TOP_K):
        rk = jnp.sum(jnp.where(sels[k], before, 0.0), axis=0, keepdims=True).astype(jnp.int32)
        rank_out = jnp.where(slot == k, rk, rank_out)
    rank_ref[...] = rank_out
    carry_ref[...] = carry_ref[...] + jnp.sum(multi, axis=1, keepdims=True)
    cnt_ref[...] = carry_ref[...]


def _outproj(xp, xs, yp, ys, op, os_, ag, wos, woa, fg, wr, br, earlier):
    tok = lambda n: pl.BlockSpec((TM, n), lambda i: (i, 0))
    slots = pl.BlockSpec((SUBLANES, TM), lambda i: (0, i))
    prompt = lambda n: pl.BlockSpec((TM, n), lambda i: (jnp.minimum(i, N_PROMPT_TILES - 1), 0))
    sample = lambda n: pl.BlockSpec((TM, n), lambda i: (0, 0))
    return pl.pallas_call(
        _outproj_kernel,
        grid=(N_TILES,),
        in_specs=[
            prompt(D_MODEL), sample(D_MODEL), prompt(D_SSM), sample(D_SSM), prompt(D_ATT), sample(D_ATT),
            _const_spec((1, D_ATT)),
            _const_spec((D_SSM, D_MODEL)), _const_spec((D_ATT, D_MODEL)),
            _const_spec((1, D_MODEL)),
            _const_spec((2 * N_EXPERTS, D_MODEL)), _const_spec((N_EXPERTS, 1)),
            _const_spec((TM, TM)),
        ],
        out_specs=[tok(D_MODEL), pl.BlockSpec(_tiled(TM), lambda i: (i, 0)), slots, slots, slots,
                   _const_spec((N_EXPERTS, 1))],
        out_shape=[
            jax.ShapeDtypeStruct((N_TOK, D_MODEL), F32),
            jax.ShapeDtypeStruct(_tiled(N_TOK), U32),
            jax.ShapeDtypeStruct((SUBLANES, N_TOK), jnp.int32),
            jax.ShapeDtypeStruct((SUBLANES, N_TOK), F32),
            jax.ShapeDtypeStruct((SUBLANES, N_TOK), jnp.int32),
            jax.ShapeDtypeStruct((N_EXPERTS, 1), F32),
        ],
        scratch_shapes=[pltpu.VMEM((N_EXPERTS, 1), F32)],
        compiler_params=_params(),
        name="outproj_router",
    )(xp, xs, yp, ys, op, os_, ag, wos, woa, fg, wr, br, earlier)


def _scatter_kernel(dest_ref, pend_ref, h_ref, rows_ref, zero_ref, sem, zsem):
    i = pl.program_id(0)

    @pl.when(i == 0)
    def _():
        zero_ref[...] = jnp.zeros_like(zero_ref)

        block_tiles = _tiled(MOE_BM)[0]

        def zero_block(b):
            start = pl.multiple_of(b * block_tiles, block_tiles)
            return pltpu.make_async_copy(zero_ref, rows_ref.at[pl.ds(start, block_tiles)], zsem)

        def last_block(e):
            end = pend_ref[e]
            nonempty = end > (pend_ref[e - 1] if e > 0 else 0)
            return nonempty, zero_block(jnp.maximum(end // MOE_BM - 1, 0))

        for e in range(N_EXPERTS):
            nonempty, cp = last_block(e)
            pl.when(nonempty)(cp.start)
        for e in range(N_EXPERTS):
            nonempty, cp = last_block(e)
            pl.when(nonempty)(cp.wait)

        first_unused = pend_ref[N_EXPERTS - 1] // MOE_BM
        lax.fori_loop(first_unused, MOE_BLOCKS, lambda b, c: (zero_block(b).start(), c)[1], 0)
        lax.fori_loop(first_unused, MOE_BLOCKS, lambda b, c: (zero_block(b).wait(), c)[1], 0)

    def issue(r, carry):
        for k in range(TOP_K):
            d = dest_ref[r * TOP_K + k]
            pltpu.make_async_copy(_tile_of(h_ref, r), _tile_of(rows_ref, d), sem).start(priority=k % 2)
        return carry

    lax.fori_loop(0, TM, issue, 0, unroll=ISSUE_UNROLL)
    for _ in range(TOP_K):
        pltpu.make_async_copy(h_ref, rows_ref.at[pl.ds(0, _tiled(TM)[0])], sem).wait()


def _scatter_rows(dest_flat, pad_end, h):
    return pl.pallas_call(
        _scatter_kernel,
        grid=(N_TILES,),
        in_specs=[
            pl.BlockSpec((TM * TOP_K,), lambda i: (i,), memory_space=pltpu.SMEM),
            pl.BlockSpec((N_EXPERTS,), lambda i: (0,), memory_space=pltpu.SMEM),
            pl.BlockSpec(_tiled(TM), lambda i: (i, 0)),
        ],
        out_specs=pl.BlockSpec(memory_space=pl.ANY),
        out_shape=jax.ShapeDtypeStruct(_tiled(MOE_ROWS), U32),
        scratch_shapes=[pltpu.VMEM(_tiled(MOE_BM), U32), pltpu.SemaphoreType.DMA(()),
                        pltpu.SemaphoreType.DMA(())],
        compiler_params=_params(),
        name="moe_scatter",
    )(dest_flat, pad_end, h)


def _expert_kernel(be_ref, nu_ref, nxt_ref, x_ref, wgu_hbm, bgu_ref, wd_hbm, bd_ref, y_ref,
                   wgu_f, wd_f, wgu_s, wd_s, sem):
    i = pl.program_id(0)
    active = i < nu_ref[0]
    e = be_ref[i]

    def fetch(expert):
        return (pltpu.make_async_copy(wgu_hbm.at[expert], wgu_f, sem.at[0]),
                pltpu.make_async_copy(wd_hbm.at[expert], wd_f, sem.at[1]))

    @pl.when(active & (i == 0))
    def _():
        for cp in fetch(e):
            cp.start()

    @pl.when(active & ((i == 0) | (e != be_ref[jnp.maximum(i - 1, 0)])))
    def _():
        for cp in fetch(e):
            cp.wait()
        wgu_s[...] = wgu_f[...].astype(BF16)
        wd_s[...] = wd_f[...].astype(BF16)
        nxt = nxt_ref[e]

        @pl.when(nxt >= 0)
        def _():
            for cp in fetch(nxt):
                cp.start()

    @pl.when(active)
    def _():
        gu = _dot(_tiles_to_rows(x_ref, MOE_BM).astype(BF16), wgu_s[...]) + bgu_ref[0]
        gate = jnp.minimum(gu[:, :D_FF], SWIGLU_LIMIT)
        up = jnp.clip(gu[:, D_FF:], -SWIGLU_LIMIT, SWIGLU_LIMIT)
        act = (up + 1.0) * gate * _sigmoid(gate * SWIGLU_ALPHA)
        _rows_to_tiles(y_ref, _dot(act.astype(BF16), wd_s[...]) + bd_ref[0])

    @pl.when(jnp.logical_not(active))
    def _():
        y_ref[...] = jnp.zeros_like(y_ref)


def _experts(block_expert, n_used, next_expert, rows, wgu, bgu, wd, bd):
    grid_spec = pltpu.PrefetchScalarGridSpec(
        num_scalar_prefetch=3,
        grid=(MOE_BLOCKS,),
        in_specs=[
            pl.BlockSpec(_tiled(MOE_BM), lambda i, be, nu, nx: (jnp.minimum(i, nu[0] - 1), 0)),
            pl.BlockSpec(memory_space=pl.ANY),
            pl.BlockSpec((1, 1, 2 * D_FF), lambda i, be, nu, nx: (be[i], 0, 0)),
            pl.BlockSpec(memory_space=pl.ANY),
            pl.BlockSpec((1, 1, D_MODEL), lambda i, be, nu, nx: (be[i], 0, 0)),
        ],
        out_specs=pl.BlockSpec(_tiled(MOE_BM), lambda i, be, nu, nx: (i, 0)),
        scratch_shapes=[pltpu.VMEM((D_MODEL, 2 * D_FF), F32), pltpu.VMEM((D_FF, D_MODEL), F32),
                        pltpu.VMEM((D_MODEL, 2 * D_FF), BF16), pltpu.VMEM((D_FF, D_MODEL), BF16),
                        pltpu.SemaphoreType.DMA((2,))],
    )
    return pl.pallas_call(
        _expert_kernel,
        grid_spec=grid_spec,
        out_shape=jax.ShapeDtypeStruct(_tiled(MOE_ROWS), U32),
        compiler_params=_params(),
        name="moe_experts",
    )(block_expert, n_used, next_expert, rows, wgu, bgu, wd, bd)


def _combine_kernel(dest_ref, dest_next_ref, gate_ref, xmid_ref, g_ref, rows_ref, yp_ref, ys_ref, buf_ref, sem):
    i = pl.program_id(0)
    slot = i % 2

    def issue(idx_ref, s):
        def body(r, carry):
            for k in range(TOP_K):
                d = idx_ref[r * TOP_K + k]
                pltpu.make_async_copy(_tile_of(rows_ref, d), _tile_of(buf_ref.at[s], k * TM + r),
                                      sem.at[s]).start(priority=k % 2)
            return carry

        lax.fori_loop(0, TM, body, 0, unroll=ISSUE_UNROLL)

    @pl.when(i == 0)
    def _():
        issue(dest_ref, 0)

    @pl.when(i + 1 < N_TILES)
    def _():
        issue(dest_next_ref, 1 - slot)

    buf = buf_ref.at[slot]
    slot_tiles = _tiled(TM)[0]
    for k in range(TOP_K):
        pltpu.make_async_copy(rows_ref.at[pl.ds(0, slot_tiles)], buf.at[pl.ds(k * slot_tiles, slot_tiles)],
                              sem.at[slot]).wait()
    acc = xmid_ref[...]
    for k in range(TOP_K):
        acc = acc + _tiles_to_rows(buf, TM, first=k * TM) * gate_ref[:, k:k + 1]
    y = _rms(acc, g_ref[...])

    @pl.when(i < N_PROMPT_TILES)
    def _():
        yp_ref[...] = y

    @pl.when(i == N_PROMPT_TILES)
    def _():
        ys_ref[...] = y


def _combine(dest_flat, gates, xmid, g, y_rows):
    return pl.pallas_call(
        _combine_kernel,
        grid=(N_TILES,),
        in_specs=[
            pl.BlockSpec((TM * TOP_K,), lambda i: (i,), memory_space=pltpu.SMEM),
            pl.BlockSpec((TM * TOP_K,), lambda i: (jnp.minimum(i + 1, N_TILES - 1),), memory_space=pltpu.SMEM),
            pl.BlockSpec((TM, TOP_K), lambda i: (i, 0)),
            pl.BlockSpec((TM, D_MODEL), lambda i: (i, 0)),
            _const_spec((1, D_MODEL)),
            pl.BlockSpec(memory_space=pl.ANY),
        ],
        out_specs=[
            pl.BlockSpec((TM, D_MODEL), lambda i: (jnp.minimum(i, N_PROMPT_TILES - 1), 0)),
            pl.BlockSpec((TM, D_MODEL), lambda i: (0, 0)),
        ],
        out_shape=[
            jax.ShapeDtypeStruct((N_PROMPT, D_MODEL), F32),
            jax.ShapeDtypeStruct((N_SAMPLE, D_MODEL), F32),
        ],
        scratch_shapes=[pltpu.VMEM((2,) + _tiled(TOP_K * TM), U32), pltpu.SemaphoreType.DMA((2,))],
        compiler_params=_params(),
        name="moe_combine",
    )(dest_flat, dest_flat, gates, xmid, g, y_rows)


def _band_bias(table):
    n_diag = BAND + CHUNK - 1
    idx = np.clip(ATT_LEFT + (CHUNK - 1) - np.arange(n_diag), -REL_CLIP, REL_CLIP) + REL_CLIP
    pick = (np.arange(2 * REL_CLIP + 1)[:, None] == idx[None, :]).astype(np.float32)
    diag = jnp.dot(table, jnp.asarray(pick), precision=lax.Precision.HIGHEST)
    return jnp.stack([diag[:, CHUNK - 1 - qi:CHUNK - 1 - qi + BAND] for qi in range(CHUNK)], axis=1)


def _pair_rows(v):
    return jnp.repeat(v.reshape(HEAD_PAIRS, 2), CHUNK, axis=1)


def _layer(l, xp, xs, cache_k, cache_v, state_conv, state_ssm,
           norm_mix_g, w_in, conv_w, conv_b, dt_bias, a_log, d_skip, ssm_norm_g,
           att_norm_g, rel_bias_table, w_out, norm_ffn_g, w_router, b_router,
           w_gate_up, b_gate_up, w_down, b_down, norm_final_g):
    wb = w_in[l].astype(BF16)
    c0 = D_SSM
    c1 = c0 + D_CONV
    c2 = c1 + SSM_HEADS
    c3 = c2 + D_ATT
    c4 = c3 + D_ATT
    z, xbc, dt, dtt, q, k, v, k_p, v_p, k_s, v_s, ctail = _inproj(
        xp, xs, norm_mix_g[l][None], wb[:, :c0], wb[:, c0:c1], wb[:, c1:c2], wb[:, c1:c2].T,
        wb[:, c2:c3], wb[:, c3:c4], wb[:, c4:])

    n_chunks = N_TOK // CHUNK
    dtp = dtt.reshape(HEAD_PAIRS, 2, n_chunks, CHUNK).transpose(2, 0, 1, 3).reshape(n_chunks, HEAD_PAIRS, LANES)
    hp = jnp.arange(D_SSM) // SSM_HEAD_DIM
    expand = (hp[None, :] == jnp.arange(SSM_HEADS)[:, None]).astype(BF16)
    lane = jnp.arange(LANES)
    triu2 = ((lane[:, None] // CHUNK == lane[None, :] // CHUNK) & (lane[:, None] <= lane[None, :])).astype(BF16)
    consts = (conv_w[l], conv_b[l][None], dt_bias[l][None], _pair_rows(dt_bias[l]),
              a_log[l][None], _pair_rows(a_log[l]),
              jnp.repeat(d_skip[l], SSM_HEAD_DIM)[None], ssm_norm_g[l][None],
              expand, triu2)
    y_ssm_p, ssm_p = _ssd_prompt(z, xbc, dt, dtp, consts)
    y_ssm_s, ssm_s = _ssd_sample(z, xbc, dt, dtp, state_conv[l],
                                 state_ssm[l].reshape(DEC_BATCH, D_SSM, D_STATE), consts)

    bias2 = _band_bias(rel_bias_table[l]).reshape(ATT_PAIRS, 2 * CHUNK, BAND)
    o_att_p = _attn_prompt(q, k, v, bias2)
    o_att_s = _attn_sample(q, k, v, cache_k[l].reshape(DEC_BATCH, ATT_LEFT, D_ATT),
                           cache_v[l].reshape(DEC_BATCH, ATT_LEFT, D_ATT), bias2)

    wo = w_out[l].astype(BF16)
    wr = w_router[l].T
    wr1 = wr.astype(BF16)
    wr2 = (wr - wr1.astype(F32)).astype(BF16)
    earlier = jnp.triu(jnp.ones((TM, TM), BF16), 1)
    xmid, h, top_idx, gates, rank, counts = _outproj(
        xp, xs, y_ssm_p, y_ssm_s, o_att_p, o_att_s, att_norm_g[l][None], wo[:D_SSM], wo[D_SSM:], norm_ffn_g[l][None],
        jnp.concatenate([wr1, wr2], axis=0), b_router[l][:, None], earlier)
    gates = gates[:TOP_K].T

    counts = counts[:, 0].astype(jnp.int32)
    padded = (counts + MOE_BM - 1) // MOE_BM * MOE_BM
    pad_end = jnp.cumsum(padded)
    pad_start = pad_end - padded
    experts = jnp.arange(N_EXPERTS, dtype=jnp.int32)
    start_of = jnp.sum(jnp.where(top_idx[:TOP_K, :, None] == experts, pad_start, 0), axis=-1)
    dest = (start_of + rank[:TOP_K]).T.reshape(-1).astype(jnp.int32)
    block_start = jnp.arange(MOE_BLOCKS, dtype=jnp.int32) * MOE_BM
    block_expert = jnp.minimum(jnp.sum((pad_end[None, :] <= block_start[:, None]).astype(jnp.int32), axis=1),
                               N_EXPERTS - 1).astype(jnp.int32)
    n_used = (pad_end[-1:] // MOE_BM).astype(jnp.int32)
    later_nonempty = (experts[None, :] > experts[:, None]) & (padded[None, :] > 0)
    next_expert = jnp.min(jnp.where(later_nonempty, experts[None, :], N_EXPERTS), axis=1)
    next_expert = jnp.where(next_expert < N_EXPERTS, next_expert, -1).astype(jnp.int32)

    rows = _scatter_rows(dest, pad_end.astype(jnp.int32), h)
    y_rows = _experts(block_expert, n_used, next_expert, rows, w_gate_up[l], b_gate_up[l][:, None, :],
                      w_down[l], b_down[l][:, None, :])
    y_p, y_s = _combine(dest, gates, xmid, norm_final_g[None], y_rows)

    keep = min(ATT_LEFT, SEQ)
    k_p = k_p[:, TM - keep:].reshape(BATCH, keep, ATT_HEADS, ATT_HEAD_DIM)
    v_p = v_p[:, TM - keep:].reshape(BATCH, keep, ATT_HEADS, ATT_HEAD_DIM)
    k_s = k_s.reshape(DEC_BATCH, DEC_SEQ, ATT_HEADS, ATT_HEAD_DIM)
    v_s = v_s.reshape(DEC_BATCH, DEC_SEQ, ATT_HEADS, ATT_HEAD_DIM)
    conv_p = ctail[:BATCH, -(CONV_W - 1):]
    conv_s = ctail[BATCH].reshape(DEC_BATCH, SUBLANES, D_CONV)[:, -(CONV_W - 1):]
    ssm_p = ssm_p.reshape(BATCH, SSM_HEADS, SSM_HEAD_DIM, D_STATE)
    ssm_s = ssm_s.reshape(DEC_BATCH, SSM_HEADS, SSM_HEAD_DIM, D_STATE)
    return (y_p.reshape(BATCH, SEQ, D_MODEL), y_s.reshape(DEC_BATCH, DEC_SEQ, D_MODEL),
            k_p, v_p, conv_p, ssm_p, k_s, v_s, conv_s, ssm_s)


def kernel(x_prompt, x_sample, cache_k, cache_v, state_conv, state_ssm, norm_mix_g, w_in, conv_w, conv_b,
           dt_bias, a_log, d_skip, ssm_norm_g, att_norm_g, rel_bias_table, w_out, norm_ffn_g, w_router,
           b_router, w_gate_up, b_gate_up, w_down, b_down, norm_final_g):
    assert w_in.shape[0] == 1, "single trunk layer"
    xp = x_prompt.reshape(N_PROMPT, D_MODEL)
    xs = x_sample.reshape(N_SAMPLE, D_MODEL)
    outs = _layer(0, xp, xs, cache_k, cache_v, state_conv, state_ssm,
                  norm_mix_g, w_in, conv_w, conv_b, dt_bias, a_log, d_skip, ssm_norm_g,
                  att_norm_g, rel_bias_table, w_out, norm_ffn_g, w_router, b_router,
                  w_gate_up, b_gate_up, w_down, b_down, norm_final_g)
    y_p, y_s, k_p, v_p, conv_p, ssm_p, k_s, v_s, conv_s, ssm_s = outs
    return (y_p, y_s, k_p[None], v_p[None], conv_p[None], ssm_p[None],
            k_s[None], v_s[None], conv_s[None], ssm_s[None])
```
